```python
import jax, jax.numpy as jnp
from jax import lax
import numpy as np

D_MODEL = 1024
BATCH = 16
SEQ = 2048
DEPTH = 4

D_MIX = D_MODEL
POOL_WIDTH = D_MIX // 4
POOL_WINDOWS = (2, 4, 8, 16)
POOL_GROUP = POOL_WIDTH // len(POOL_WINDOWS)
QK_NOPE = 128
QK_ROPE = 64
V_HEAD = 128
ATTN_WIDTH = D_MIX - POOL_WIDTH
N_HEADS = ATTN_WIDTH // V_HEAD
Q_LORA = 384
KV_LORA = 256
IN_WIDTH = POOL_WIDTH + Q_LORA + KV_LORA + QK_ROPE
ROPE_THETA = 10000.0
Q_BLOCK = 128
N_EXPERTS = 64
TOP_K = 8
N_GROUPS = 8
TOPK_GROUPS = 4
D_EXPERT = D_MODEL // 4
ROUTED_SCALE = 2.5
DISPATCH_BLOCK = 256
DEEPNORM_ALPHA = (2.0 * DEPTH) ** 0.25
DEEPNORM_BETA = (8.0 * DEPTH) ** -0.25
LN_EPS = 1e-5
RMS_EPS = 1e-6

kernel_name = "hybrid_pool_mla_moe_deepnorm_adaln"


def layer_norm(x, g, b):
    xf = x.astype(jnp.float32)
    mu = xf.mean(-1, keepdims=True)
    var = jnp.square(xf - mu).mean(-1, keepdims=True)
    return ((xf - mu) * lax.rsqrt(var + LN_EPS) * g.astype(jnp.float32) + b.astype(jnp.float32)).astype(x.dtype)


def rms_norm(x, g):
    xf = x.astype(jnp.float32)
    return (xf * lax.rsqrt(jnp.mean(xf * xf, -1, keepdims=True) + RMS_EPS) * g.astype(jnp.float32)).astype(x.dtype)


def apply_rope(x, positions):
    half = QK_ROPE // 2
    inv_freq = ROPE_THETA ** (-jnp.arange(half, dtype=jnp.float32) / half)
    ang = positions.astype(jnp.float32)[..., None] * inv_freq
    cos, sin = jnp.cos(ang), jnp.sin(ang)
    if x.ndim == 4:
        cos, sin = cos[:, :, None, :], sin[:, :, None, :]
    xf = x.astype(jnp.float32)
    x1, x2 = xf[..., :half], xf[..., half:]
    return jnp.concatenate([x1 * cos - x2 * sin, x2 * cos + x1 * sin], axis=-1).astype(x.dtype)


def causal_pool_mixer(u, w_pool, pool_scale):
    S = u.shape[1]
    uf = u.astype(jnp.float32)
    t = jnp.arange(S)
    outs = []
    for gi, w in enumerate(POOL_WINDOWS):
        ug = uf[..., gi * POOL_GROUP:(gi + 1) * POOL_GROUP]
        cs = jnp.cumsum(ug, axis=1)
        lagged = jnp.pad(cs, ((0, 0), (w, 0), (0, 0)))[:, :S]
        count = jnp.minimum(t + 1, w).astype(jnp.float32)[None, :, None]
        token_mix = (cs - lagged) / count - ug
        outs.append(jnp.einsum('bsc,cd->bsd', token_mix.astype(u.dtype), w_pool[gi]))
    return jnp.concatenate(outs, axis=-1) * pool_scale


def mla_mixer(q_lat, kv_lat, k_rope, positions, q_norm_g, kv_norm_g, w_uq, w_ukv):
    B, S, _ = q_lat.shape
    q = (rms_norm(q_lat, q_norm_g) @ w_uq).reshape(B, S, N_HEADS, QK_NOPE + QK_ROPE)
    q_nope, q_rope = q[..., :QK_NOPE], apply_rope(q[..., QK_NOPE:], positions)
    kv = (rms_norm(kv_lat, kv_norm_g) @ w_ukv).reshape(B, S, N_HEADS, QK_NOPE + V_HEAD)
    k_nope, v = kv[..., :QK_NOPE], kv[..., QK_NOPE:]
    k_rope = apply_rope(k_rope, positions)
    scale = (QK_NOPE + QK_ROPE) ** -0.5
    outs = []
    for i in range(S // Q_BLOCK):
        q0, q1 = i * Q_BLOCK, (i + 1) * Q_BLOCK
        s = (jnp.einsum('bqhd,bkhd->bhqk', q_nope[:, q0:q1], k_nope[:, :q1])
             + jnp.einsum('bqhr,bkr->bhqk', q_rope[:, q0:q1], k_rope[:, :q1])).astype(jnp.float32) * scale
        causal = jnp.arange(q0, q1)[:, None] >= jnp.arange(q1)[None, :]
        p = jax.nn.softmax(jnp.where(causal, s, -jnp.inf), axis=-1).astype(v.dtype)
        outs.append(jnp.einsum('bhqk,bkhd->bqhd', p, v[:, :q1]))
    return jnp.concatenate(outs, axis=1).reshape(B, S, ATTN_WIDTH)


def swiglu(x, w_gate, w_up, w_down):
    return (jax.nn.silu(x @ w_gate) * (x @ w_up)) @ w_down


def route(ht, w_router, router_bias):
    T = ht.shape[0]
    scores = jax.nn.sigmoid(ht.astype(jnp.float32) @ w_router.astype(jnp.float32))
    choice = (scores + router_bias.astype(jnp.float32)).reshape(T, N_GROUPS, N_EXPERTS // N_GROUPS)
    group_score = lax.top_k(choice, 2)[0].sum(-1)
    _, top_groups = lax.top_k(group_score, TOPK_GROUPS)
    group_mask = jax.nn.one_hot(top_groups, N_GROUPS, dtype=jnp.float32).sum(1) > 0
    masked = jnp.where(group_mask[:, :, None], choice, -jnp.inf).reshape(T, N_EXPERTS)
    _, idx = lax.top_k(masked, TOP_K)
    w = jnp.take_along_axis(scores, idx, axis=-1)
    w = w / w.sum(-1, keepdims=True) * ROUTED_SCALE
    return idx, w


def routed_moe(h, w_router, router_bias, w_gate, w_up, w_down):
    B, S, D = h.shape
    T = B * S
    ht = h.reshape(T, D)
    idx, gates = route(ht, w_router, router_bias)
    A = T * TOP_K
    flat_e = idx.reshape(A)
    flat_tok = (jnp.arange(A) // TOP_K).astype(jnp.int32)
    flat_g = gates.reshape(A)
    order = jnp.argsort(flat_e)
    e_sorted = flat_e[order]
    counts = jnp.bincount(flat_e, length=N_EXPERTS)
    start = jnp.cumsum(counts) - counts
    padded = (counts + DISPATCH_BLOCK - 1) // DISPATCH_BLOCK * DISPATCH_BLOCK
    pad_end = jnp.cumsum(padded)
    pad_start = pad_end - padded
    dest = pad_start[e_sorted] + (jnp.arange(A) - start[e_sorted])
    P = (-(-A // DISPATCH_BLOCK) + N_EXPERTS) * DISPATCH_BLOCK
    NB = P // DISPATCH_BLOCK
    buf_tok = jnp.zeros((P,), jnp.int32).at[dest].set(flat_tok[order])
    buf_gate = jnp.zeros((P,), jnp.float32).at[dest].set(flat_g[order])
    blk_expert = jnp.minimum(jnp.searchsorted(pad_end, jnp.arange(NB) * DISPATCH_BLOCK, side='right'),
                             N_EXPERTS - 1)

    def expert_block(acc, xs):
        e, tok, g = xs
        y = swiglu(ht[tok], w_gate[e], w_up[e], w_down[e])
        return acc.at[tok].add(g[:, None] * y.astype(jnp.float32)), None

    acc, _ = lax.scan(expert_block, jnp.zeros((T, D), jnp.float32),
                      (blk_expert, buf_tok.reshape(NB, DISPATCH_BLOCK), buf_gate.reshape(NB, DISPATCH_BLOCK)))
    return acc.astype(h.dtype).reshape(B, S, D)


def setup_inputs(seed: int = 0) -> dict:
    key = jax.random.key(seed)
    ks = jax.random.split(key, 26)
    L, D, E, F = DEPTH, D_MODEL, N_EXPERTS, D_EXPERT

    def nrm(k, shape, scale):
        return jax.random.normal(k, shape, jnp.float32) * scale

    return {
        "x": nrm(ks[0], (BATCH, SEQ, D), 1.0),
        "c": nrm(ks[1], (BATCH, D), 1.0),
        "positions": (jnp.arange(SEQ, dtype=jnp.int32)[None, :]
                      + jax.random.randint(ks[2], (BATCH, 1), 0, 4096, dtype=jnp.int32)),
        "w_ada": nrm(ks[3], (L, D, 6 * D), D ** -0.5),
        "b_ada": nrm(ks[4], (L, 6 * D), 0.01),
        "w_in": nrm(ks[5], (L, D, IN_WIDTH), D ** -0.5),
        "q_norm_g": 1.0 + nrm(ks[6], (L, Q_LORA), 0.01),
        "kv_norm_g": 1.0 + nrm(ks[7], (L, KV_LORA), 0.01),
        "w_uq": nrm(ks[8], (L, Q_LORA, N_HEADS * (QK_NOPE + QK_ROPE)), Q_LORA ** -0.5),
        "w_ukv": nrm(ks[9], (L, KV_LORA, N_HEADS * (QK_NOPE + V_HEAD)), KV_LORA ** -0.5),
        "w_pool": nrm(ks[10], (L, len(POOL_WINDOWS), POOL_GROUP, POOL_GROUP), POOL_GROUP ** -0.5),
        "pool_scale": 1.0 + nrm(ks[11], (L, POOL_WIDTH), 0.1),
        "w_out": nrm(ks[12], (L, D_MIX, D), DEEPNORM_BETA * D_MIX ** -0.5),
        "ln1_g": 1.0 + nrm(ks[13], (L, D), 0.01),
        "ln1_b": nrm(ks[14], (L, D), 0.01),
        "w_router": nrm(ks[15], (L, D, E), D ** -0.5),
        "router_bias": nrm(ks[16], (L, E), 0.01),
        "w_exp_gate": nrm(ks[17], (L, E, D, F), D ** -0.5),
        "w_exp_up": nrm(ks[18], (L, E, D, F), D ** -0.5),
        "w_exp_down": nrm(ks[19], (L, E, F, D), DEEPNORM_BETA * F ** -0.5),
        "w_sh_gate": nrm(ks[20], (L, D, F), D ** -0.5),
        "w_sh_up": nrm(ks[21], (L, D, F), D ** -0.5),
        "w_sh_down": nrm(ks[22], (L, F, D), DEEPNORM_BETA * F ** -0.5),
        "ln2_g": 1.0 + nrm(ks[23], (L, D), 0.01),
        "ln2_b": nrm(ks[24], (L, D), 0.01),
    }


def reference(x, c, positions, w_ada, b_ada, w_in, q_norm_g, kv_norm_g, w_uq, w_ukv, w_pool, pool_scale,
              w_out, ln1_g, ln1_b, w_router, router_bias, w_exp_gate, w_exp_up, w_exp_down,
              w_sh_gate, w_sh_up, w_sh_down, ln2_g, ln2_b):
    cond = jax.nn.silu(c)
    split_cols = [POOL_WIDTH, POOL_WIDTH + Q_LORA, POOL_WIDTH + Q_LORA + KV_LORA]
    for l in range(DEPTH):
        mod = (cond @ w_ada[l] + b_ada[l])[:, None, :]
        sh1, sc1, g1, sh2, sc2, g2 = jnp.split(mod, 6, axis=-1)
        h = x * (1.0 + sc1) + sh1
        u, q_lat, kv_lat, k_rope = jnp.split(h @ w_in[l], split_cols, axis=-1)
        pooled = causal_pool_mixer(u, w_pool[l], pool_scale[l])
        attn = mla_mixer(q_lat, kv_lat, k_rope, positions, q_norm_g[l], kv_norm_g[l], w_uq[l], w_ukv[l])
        mixed = jnp.concatenate([pooled, attn], axis=-1) @ w_out[l]
        x = layer_norm(DEEPNORM_ALPHA * x + g1 * mixed, ln1_g[l], ln1_b[l])
        h = x * (1.0 + sc2) + sh2
        ffn = (swiglu(h, w_sh_gate[l], w_sh_up[l], w_sh_down[l])
               + routed_moe(h, w_router[l], router_bias[l], w_exp_gate[l], w_exp_up[l], w_exp_down[l]))
        x = layer_norm(DEEPNORM_ALPHA * x + g2 * ffn, ln2_g[l], ln2_b[l])
    return x
```

```python
import functools

import jax
import jax.numpy as jnp
from jax import lax
from jax.experimental import pallas as pl
from jax.experimental.pallas import tpu as pltpu

F32 = jnp.float32
BF16 = jnp.bfloat16

D_MODEL = 1024
DEPTH = 4
POOL_WIDTH = 256
POOL_WINDOWS = (2, 4, 8, 16)
POOL_GROUP = 64
POOL_HALO = 16
QK_NOPE = 128
QK_ROPE = 64
V_HEAD = 128
N_HEADS = 6
Q_LORA = 384
KV_LORA = 256
ROPE_THETA = 10000.0
N_EXPERTS = 64
TOP_K = 8
N_GROUPS = 8
GROUP_SIZE = N_EXPERTS // N_GROUPS
TOPK_GROUPS = 4
D_EXPERT = 256
ROUTED_SCALE = 2.5
DEEPNORM_ALPHA = (2.0 * DEPTH) ** 0.25
LN_EPS = 1e-5
RMS_EPS = 1e-6
ATTN_SCALE = (QK_NOPE + QK_ROPE) ** -0.5
LANES = 128
QK_PAD = 2 * LANES
MASK_VALUE = -1e30

TM_PROJ = 512
TQ = 512
TK = 512
ROUTE_ROWS = 8
TC_MOE = 1024
VMEM_LIMIT = 48 * 1024 * 1024


def _cparams(sem):
    return pltpu.CompilerParams(dimension_semantics=sem, vmem_limit_bytes=VMEM_LIMIT)


def _split_hi_lo(a):
    hi = a.astype(BF16)
    lo = (a - hi.astype(F32)).astype(BF16)
    return hi, lo


def _ada_kernel(c_ref, w_ref, b_ref, o_ref):
    c = c_ref[...]
    cond = c * jax.nn.sigmoid(c)
    c_hi, c_lo = _split_hi_lo(cond)
    w_hi, w_lo = _split_hi_lo(w_ref[0])
    acc = jnp.dot(c_hi, w_hi, preferred_element_type=F32)
    acc += jnp.dot(c_lo, w_hi, preferred_element_type=F32)
    acc += jnp.dot(c_hi, w_lo, preferred_element_type=F32)
    o_ref[0] = acc + b_ref[0]


def _ada_mod(c, w_ada, b_ada):
    L, D, N = w_ada.shape
    B = c.shape[0]
    tn = 1536
    return pl.pallas_call(
        _ada_kernel,
        grid=(L, N // tn),
        in_specs=[
            pl.BlockSpec((B, D), lambda l, j: (0, 0)),
            pl.BlockSpec((1, D, tn), lambda l, j: (l, 0, j)),
            pl.BlockSpec((1, 1, tn), lambda l, j: (l, 0, j)),
        ],
        out_specs=pl.BlockSpec((1, B, tn), lambda l, j: (l, 0, j)),
        out_shape=jax.ShapeDtypeStruct((L, B, N), F32),
        compiler_params=_cparams(("arbitrary", "arbitrary")),
        name="ada_mod",
    )(c, w_ada, b_ada.reshape(L, 1, N))


def _rope_table_kernel(pos_ref, freq_ref, o_ref):
    ang = pos_ref[0].astype(F32) * freq_ref[...]
    o_ref[0, :, 0:LANES] = jnp.cos(ang)
    o_ref[0, :, LANES:2 * LANES] = jnp.sin(ang)


def _rope_table(positions):
    B, S = positions.shape
    half = QK_ROPE // 2
    inv_freq = ROPE_THETA ** (-jnp.arange(half, dtype=F32) / half)
    freq = jnp.tile(inv_freq, LANES // half).reshape(1, LANES)
    return pl.pallas_call(
        _rope_table_kernel,
        grid=(B,),
        in_specs=[
            pl.BlockSpec((1, S, 1), lambda b: (b, 0, 0)),
            pl.BlockSpec((1, LANES), lambda b: (0, 0)),
        ],
        out_specs=pl.BlockSpec((1, S, 2 * LANES), lambda b: (b, 0, 0)),
        out_shape=jax.ShapeDtypeStruct((B, S, 2 * LANES), F32),
        compiler_params=_cparams(("arbitrary",)),
        name="rope_table",
    )(positions.reshape(B, S, 1), freq)


def _rms(x, g):
    return x * lax.rsqrt(jnp.mean(x * x, axis=-1, keepdims=True) + RMS_EPS) * g


def _in_proj_kernel(x_ref, mod_ref, cs_ref, w_in_ref, qg_ref, kvg_ref, w_uqa_ref, w_uqb_ref, w_ukv_ref,
                    u_ref, q_ref, k_ref, v_ref):
    x = x_ref[0]
    sh1 = mod_ref[0, 0:1, :]
    sc1 = mod_ref[0, 1:2, :]
    h = (x * (1.0 + sc1) + sh1).astype(BF16)
    proj = jnp.dot(h, w_in_ref[...], preferred_element_type=F32)
    u_ref[0] = proj[:, 0:POOL_WIDTH]
    o = POOL_WIDTH
    q_lat = proj[:, o:o + Q_LORA]
    o += Q_LORA
    kv_lat = proj[:, o:o + KV_LORA]
    o += KV_LORA
    k_a = proj[:, o:o + LANES]
    k_b = proj[:, o + LANES:o + 2 * LANES]
    cosv = cs_ref[0, :, 0:LANES]
    sinv = cs_ref[0, :, LANES:2 * LANES]
    k_rot = (k_a * cosv + k_b * sinv).astype(BF16)

    qn = _rms(q_lat, qg_ref[...]).astype(BF16)
    q_a = jnp.dot(qn, w_uqa_ref[...], preferred_element_type=F32)
    q_b = jnp.dot(qn, w_uqb_ref[...], preferred_element_type=F32)
    kvn = _rms(kv_lat, kvg_ref[...]).astype(BF16)
    kv = jnp.dot(kvn, w_ukv_ref[...], preferred_element_type=F32)
    for hd in range(N_HEADS):
        b0 = hd * QK_PAD
        q_ref[0, hd, :, 0:LANES] = (q_a[:, b0:b0 + LANES] * ATTN_SCALE).astype(BF16)
        q_rot = q_a[:, b0 + LANES:b0 + QK_PAD] * cosv + q_b[:, hd * LANES:(hd + 1) * LANES] * sinv
        q_ref[0, hd, :, LANES:QK_PAD] = (q_rot * ATTN_SCALE).astype(BF16)
        k_ref[0, hd, :, 0:LANES] = kv[:, b0:b0 + QK_NOPE].astype(BF16)
        k_ref[0, hd, :, LANES:QK_PAD] = k_rot
        v_ref[0, hd] = kv[:, b0 + QK_NOPE:b0 + QK_NOPE + V_HEAD].astype(BF16)


def _in_proj(x, mod, cs, w_in_ext, qg, kvg, w_uqa, w_uqb, w_ukv_bf):
    B, S, D = x.shape
    tm = TM_PROJ
    const = lambda b, i: (0, 0)
    return pl.pallas_call(
        _in_proj_kernel,
        grid=(B, S // tm),
        in_specs=[
            pl.BlockSpec((1, tm, D), lambda b, i: (b, i, 0)),
            pl.BlockSpec((1, 6, D), lambda b, i: (b, 0, 0)),
            pl.BlockSpec((1, tm, 2 * LANES), lambda b, i: (b, i, 0)),
            pl.BlockSpec(w_in_ext.shape, const),
            pl.BlockSpec(qg.shape, const),
            pl.BlockSpec(kvg.shape, const),
            pl.BlockSpec(w_uqa.shape, const),
            pl.BlockSpec(w_uqb.shape, const),
            pl.BlockSpec(w_ukv_bf.shape, const),
        ],
        out_specs=[
            pl.BlockSpec((1, tm, POOL_WIDTH), lambda b, i: (b, i, 0)),
            pl.BlockSpec((1, N_HEADS, tm, QK_PAD), lambda b, i: (b, 0, i, 0)),
            pl.BlockSpec((1, N_HEADS, tm, QK_PAD), lambda b, i: (b, 0, i, 0)),
            pl.BlockSpec((1, N_HEADS, tm, V_HEAD), lambda b, i: (b, 0, i, 0)),
        ],
        out_shape=[
            jax.ShapeDtypeStruct((B, S, POOL_WIDTH), F32),
            jax.ShapeDtypeStruct((B, N_HEADS, S, QK_PAD), BF16),
            jax.ShapeDtypeStruct((B, N_HEADS, S, QK_PAD), BF16),
            jax.ShapeDtypeStruct((B, N_HEADS, S, V_HEAD), BF16),
        ],
        compiler_params=_cparams(("arbitrary", "arbitrary")),
        name="in_proj",
    )(x, mod, cs, w_in_ext, qg, kvg, w_uqa, w_uqb, w_ukv_bf)


def _attn_kernel(q_ref, k_ref, v_ref, o_ref):
    i = pl.program_id(2)
    q = q_ref[0, 0]
    row = i * TQ + lax.broadcasted_iota(jnp.int32, (TQ, TK), 0)
    col0 = lax.broadcasted_iota(jnp.int32, (TQ, TK), 1)

    def body(j, carry):
        m, l, acc = carry
        k = k_ref[0, 0, pl.ds(j * TK, TK), :]
        v = v_ref[0, 0, pl.ds(j * TK, TK), :]
        s = lax.dot_general(q, k, (((1,), (1,)), ((), ())), preferred_element_type=F32)
        s = jnp.where(row >= col0 + j * TK, s, MASK_VALUE)
        m_new = jnp.maximum(m, jnp.max(s, axis=-1, keepdims=True))
        alpha = jnp.exp(m - m_new)
        p = jnp.exp(s - m_new)
        l_new = alpha * l + jnp.sum(p, axis=-1, keepdims=True)
        acc_new = alpha * acc + jnp.dot(p.astype(BF16), v, preferred_element_type=F32)
        return m_new, l_new, acc_new

    init = (jnp.full((TQ, 1), MASK_VALUE, F32), jnp.zeros((TQ, 1), F32), jnp.zeros((TQ, V_HEAD), F32))
    _, l, acc = lax.fori_loop(0, i + 1, body, init)
    o_ref[0] = (acc / l).astype(BF16)


def _attention(q, k, v):
    B, H, S, _ = q.shape
    return pl.pallas_call(
        _attn_kernel,
        grid=(B, H, S // TQ),
        in_specs=[
            pl.BlockSpec((1, 1, TQ, QK_PAD), lambda b, h, i: (b, h, i, 0)),
            pl.BlockSpec((1, 1, S, QK_PAD), lambda b, h, i: (b, h, 0, 0)),
            pl.BlockSpec((1, 1, S, V_HEAD), lambda b, h, i: (b, h, 0, 0)),
        ],
        out_specs=pl.BlockSpec((1, TQ, V_HEAD), lambda b, h, i: (b, i, h)),
        out_shape=jax.ShapeDtypeStruct((B, S, H * V_HEAD), BF16),
        compiler_params=_cparams(("arbitrary", "arbitrary", "arbitrary")),
        name="attention",
    )(q, k, v)


def _layer_norm(v, g, b):
    mu = jnp.mean(v, axis=-1, keepdims=True)
    d = v - mu
    var = jnp.mean(d * d, axis=-1, keepdims=True)
    return d * lax.rsqrt(var + LN_EPS) * g + b


def _post_kernel(x_ref, u_ref, halo_ref, attn_ref, mod_ref, wpool_ref, pscale_ref, wout_p_ref, wout_a_ref,
                 lng_ref, lnb_ref, wr_hi_ref, wr_lo_ref, x1_ref, h2_ref, lt_ref):
    i = pl.program_id(1)
    tm = u_ref.shape[1]
    u = u_ref[0]
    halo = jnp.where(i > 0, halo_ref[0], 0.0)
    ext = jnp.concatenate([halo, u], axis=0)
    s2 = ext + pltpu.roll(ext, 1, 0)
    s4 = s2 + pltpu.roll(s2, 2, 0)
    s8 = s4 + pltpu.roll(s4, 4, 0)
    s16 = s8 + pltpu.roll(s8, 8, 0)
    lane = lax.broadcasted_iota(jnp.int32, (1, POOL_WIDTH), 1)
    win = jnp.where(lane < POOL_GROUP, s2,
                    jnp.where(lane < 2 * POOL_GROUP, s4, jnp.where(lane < 3 * POOL_GROUP, s8, s16)))
    win = win[POOL_HALO:, :]
    width = jnp.where(lane < POOL_GROUP, POOL_WINDOWS[0],
                      jnp.where(lane < 2 * POOL_GROUP, POOL_WINDOWS[1],
                                jnp.where(lane < 3 * POOL_GROUP, POOL_WINDOWS[2], POOL_WINDOWS[3])))
    t = i * tm + lax.broadcasted_iota(jnp.int32, (tm, 1), 0)
    count = jnp.minimum(t + 1, width).astype(F32)
    token_mix = win / count - u
    pooled = jnp.dot(token_mix.astype(BF16), wpool_ref[...], preferred_element_type=F32) * pscale_ref[...]
    mixed = jnp.dot(pooled.astype(BF16), wout_p_ref[...], preferred_element_type=F32)
    mixed += jnp.dot(attn_ref[0], wout_a_ref[...], preferred_element_type=F32)

    g1 = mod_ref[0, 2:3, :]
    sh2 = mod_ref[0, 3:4, :]
    sc2 = mod_ref[0, 4:5, :]
    x1 = _layer_norm(DEEPNORM_ALPHA * x_ref[0] + g1 * mixed, lng_ref[...], lnb_ref[...])
    x1_ref[0] = x1
    h2 = x1 * (1.0 + sc2) + sh2
    h_hi, h_lo = _split_hi_lo(h2)
    h2_ref[0] = h_hi
    nt = (((1,), (1,)), ((), ()))
    lt = lax.dot_general(wr_hi_ref[...], h_hi, nt, preferred_element_type=F32)
    lt += lax.dot_general(wr_hi_ref[...], h_lo, nt, preferred_element_type=F32)
    lt += lax.dot_general(wr_lo_ref[...], h_hi, nt, preferred_element_type=F32)
    lt_ref[...] = lt


def _post_attn(x, u, attn, mod, wpool_bd, pscale, wout_p, wout_a, lng, lnb, wr_hi, wr_lo):
    B, S, D = x.shape
    tm = TM_PROJ
    nt = S // tm
    const = lambda b, i: (0, 0)
    halo_blocks = tm // POOL_HALO
    return pl.pallas_call(
        _post_kernel,
        grid=(B, nt),
        in_specs=[
            pl.BlockSpec((1, tm, D), lambda b, i: (b, i, 0)),
            pl.BlockSpec((1, tm, POOL_WIDTH), lambda b, i: (b, i, 0)),
            pl.BlockSpec((1, POOL_HALO, POOL_WIDTH), lambda b, i: (b, jnp.maximum(i * halo_blocks - 1, 0), 0)),
            pl.BlockSpec((1, tm, N_HEADS * V_HEAD), lambda b, i: (b, i, 0)),
            pl.BlockSpec((1, 6, D), lambda b, i: (b, 0, 0)),
            pl.BlockSpec(wpool_bd.shape, const),
            pl.BlockSpec(pscale.shape, const),
            pl.BlockSpec(wout_p.shape, const),
            pl.BlockSpec(wout_a.shape, const),
            pl.BlockSpec(lng.shape, const),
            pl.BlockSpec(lnb.shape, const),
            pl.BlockSpec(wr_hi.shape, const),
            pl.BlockSpec(wr_lo.shape, const),
        ],
        out_specs=[
            pl.BlockSpec((1, tm, D), lambda b, i: (b, i, 0)),
            pl.BlockSpec((1, tm, D), lambda b, i: (b, i, 0)),
            pl.BlockSpec((N_EXPERTS, tm), lambda b, i: (0, b * nt + i)),
        ],
        out_shape=[
            jax.ShapeDtypeStruct((B, S, D), F32),
            jax.ShapeDtypeStruct((B, S, D), BF16),
            jax.ShapeDtypeStruct((N_EXPERTS, B * S), F32),
        ],
        compiler_params=_cparams(("arbitrary", "arbitrary")),
        name="post_attn",
    )(x, u, u, attn, mod, wpool_bd, pscale, wout_p, wout_a, lng, lnb, wr_hi, wr_lo)


def _select_first_max(vals, n_rounds):
    work = list(vals)
    sel = [None] * len(vals)
    for _ in range(n_rounds):
        m = functools.reduce(jnp.maximum, work)
        taken = None
        for e in range(len(work)):
            hit = work[e] == m
            first = hit if taken is None else jnp.logical_and(hit, jnp.logical_not(taken))
            taken = hit if taken is None else jnp.logical_or(taken, hit)
            sel[e] = first if sel[e] is None else jnp.logical_or(sel[e], first)
            work[e] = jnp.where(first, -jnp.inf, work[e])
    return sel


def _route_kernel(bias_ref, lt_ref, g_ref):
    scores = [jax.nn.sigmoid(lt_ref[e]) for e in range(N_EXPERTS)]
    choice = [scores[e] + bias_ref[e] for e in range(N_EXPERTS)]
    group_score = []
    for g in range(N_GROUPS):
        vals = choice[g * GROUP_SIZE:(g + 1) * GROUP_SIZE]
        sel2 = _select_first_max(vals, 2)
        group_score.append(functools.reduce(
            jnp.add, [jnp.where(sel2[j], vals[j], 0.0) for j in range(GROUP_SIZE)]))
    group_sel = _select_first_max(group_score, TOPK_GROUPS)
    masked = [jnp.where(group_sel[e // GROUP_SIZE], choice[e], -jnp.inf) for e in range(N_EXPERTS)]
    sel = _select_first_max(masked, TOP_K)
    picked = [jnp.where(sel[e], scores[e], 0.0) for e in range(N_EXPERTS)]
    total = functools.reduce(jnp.add, picked)
    for e in range(N_EXPERTS):
        g_ref[e] = picked[e] / total * ROUTED_SCALE


def _route(lt3, bias):
    E, R, _ = lt3.shape
    return pl.pallas_call(
        _route_kernel,
        grid=(R // ROUTE_ROWS,),
        in_specs=[
            pl.BlockSpec(memory_space=pltpu.SMEM),
            pl.BlockSpec((E, ROUTE_ROWS, LANES), lambda r: (0, r, 0)),
        ],
        out_specs=pl.BlockSpec((E, ROUTE_ROWS, LANES), lambda r: (0, r, 0)),
        out_shape=jax.ShapeDtypeStruct((E, R, LANES), F32),
        compiler_params=_cparams(("arbitrary",)),
        name="route",
    )(bias, lt3)


def _swiglu(h, wg, wu):
    gate = jnp.dot(h, wg, preferred_element_type=F32)
    up = jnp.dot(h, wu, preferred_element_type=F32)
    return (gate * jax.nn.sigmoid(gate) * up).astype(BF16)


def _moe_kernel(h_ref, x1_ref, g_ref, mod_ref, wg_ref, wu_ref, wd_ref, wsg_ref, wsu_ref, wsd_ref,
                lng_ref, lnb_ref, o_ref, acc_ref):
    e = pl.program_id(1)
    h = h_ref[...]

    @pl.when(e == 0)
    def _():
        acc_ref[...] = jnp.dot(_swiglu(h, wsg_ref[...], wsu_ref[...]), wsd_ref[...], preferred_element_type=F32)

    lane = lax.broadcasted_iota(jnp.int32, (1, N_EXPERTS), 1)
    gcol = jnp.sum(jnp.where(lane == e, g_ref[...], 0.0), axis=1, keepdims=True)
    y = jnp.dot(_swiglu(h, wg_ref[0], wu_ref[0]), wd_ref[0], preferred_element_type=F32)
    acc_ref[...] += gcol * y

    @pl.when(e == N_EXPERTS - 1)
    def _():
        g2 = mod_ref[0, 5:6, :]
        o_ref[...] = _layer_norm(DEEPNORM_ALPHA * x1_ref[...] + g2 * acc_ref[...], lng_ref[...], lnb_ref[...])


def _moe(h2, x1, gates, mod, wg, wu, wd, wsg, wsu, wsd, lng, lnb, seq):
    T, D = h2.shape
    tc = TC_MOE
    per_seq = seq // tc
    const = lambda c, e: (0, 0)
    return pl.pallas_call(
        _moe_kernel,
        grid=(T // tc, N_EXPERTS),
        in_specs=[
            pl.BlockSpec((tc, D), lambda c, e: (c, 0)),
            pl.BlockSpec((tc, D), lambda c, e: (c, 0)),
            pl.BlockSpec((tc, N_EXPERTS), lambda c, e: (c, 0)),
            pl.BlockSpec((1, 6, D), lambda c, e: (c // per_seq, 0, 0)),
            pl.BlockSpec((1, D, D_EXPERT), lambda c, e: (e, 0, 0)),
            pl.BlockSpec((1, D, D_EXPERT), lambda c, e: (e, 0, 0)),
            pl.BlockSpec((1, D_EXPERT, D), lambda c, e: (e, 0, 0)),
            pl.BlockSpec(wsg.shape, const),
            pl.BlockSpec(wsu.shape, const),
            pl.BlockSpec(wsd.shape, const),
            pl.BlockSpec(lng.shape, const),
            pl.BlockSpec(lnb.shape, const),
        ],
        out_specs=pl.BlockSpec((tc, D), lambda c, e: (c, 0)),
        out_shape=jax.ShapeDtypeStruct((T, D), F32),
        scratch_shapes=[pltpu.VMEM((tc, D), F32)],
        compiler_params=_cparams(("arbitrary", "arbitrary")),
        name="moe",
    )(h2, x1, gates, mod, wg, wu, wd, wsg, wsu, wsd, lng, lnb)


def _rope_pair(w):
    half = QK_ROPE // 2
    return w, jnp.concatenate([-w[:, half:], w[:, :half]], axis=1)


def _prep_layer(w_in, w_uq, w_ukv, w_pool, w_out, w_router):
    D = w_in.shape[0]
    zpad = lambda r: jnp.zeros((r, LANES - QK_ROPE), F32)
    lat = POOL_WIDTH + Q_LORA + KV_LORA
    k_a, k_b = _rope_pair(w_in[:, lat:lat + QK_ROPE])
    w_in_ext = jnp.concatenate([w_in[:, :lat], k_a, zpad(D), k_b, zpad(D)], axis=1).astype(BF16)

    qa_cols, qb_cols = [], []
    per_head = QK_NOPE + QK_ROPE
    for hd in range(N_HEADS):
        w_h = w_uq[:, hd * per_head:(hd + 1) * per_head]
        r_a, r_b = _rope_pair(w_h[:, QK_NOPE:])
        qa_cols += [w_h[:, :QK_NOPE], r_a, zpad(Q_LORA)]
        qb_cols += [r_b, zpad(Q_LORA)]
    w_uqa = jnp.concatenate(qa_cols, axis=1).astype(BF16)
    w_uqb = jnp.concatenate(qb_cols, axis=1).astype(BF16)

    wpool_bd = jax.scipy.linalg.block_diag(*[w_pool[g] for g in range(len(POOL_WINDOWS))]).astype(BF16)
    wr_hi, wr_lo = _split_hi_lo(w_router.T)
    return dict(w_in_ext=w_in_ext, w_uqa=w_uqa, w_uqb=w_uqb, w_ukv=w_ukv.astype(BF16), wpool_bd=wpool_bd,
                wout_p=w_out[:POOL_WIDTH].astype(BF16), wout_a=w_out[POOL_WIDTH:].astype(BF16),
                wr_hi=wr_hi, wr_lo=wr_lo)


def kernel(x, c, positions, w_ada, b_ada, w_in, q_norm_g, kv_norm_g, w_uq, w_ukv, w_pool, pool_scale,
           w_out, ln1_g, ln1_b, w_router, router_bias, w_exp_gate, w_exp_up, w_exp_down,
           w_sh_gate, w_sh_up, w_sh_down, ln2_g, ln2_b):
    B, S, D = x.shape
    T = B * S
    L = w_in.shape[0]
    row = lambda v: v.reshape(1, -1)

    mod_all = _ada_mod(c, w_ada, b_ada).reshape(L, B, 6, D)
    cs = _rope_table(positions)
    for l in range(L):
        p = _prep_layer(w_in[l], w_uq[l], w_ukv[l], w_pool[l], w_out[l], w_router[l])
        mod = mod_all[l]
        u, q, k, v = _in_proj(x, mod, cs, p["w_in_ext"], row(q_norm_g[l]), row(kv_norm_g[l]),
                              p["w_uqa"], p["w_uqb"], p["w_ukv"])
        attn = _attention(q, k, v)
        x1, h2, lt = _post_attn(x, u, attn, mod, p["wpool_bd"], row(pool_scale[l]), p["wout_p"], p["wout_a"],
                                row(ln1_g[l]), row(ln1_b[l]), p["wr_hi"], p["wr_lo"])
        gates_t = _route(lt.reshape(N_EXPERTS, T // LANES, LANES), router_bias[l])
        gates = gates_t.reshape(N_EXPERTS, T).T
        x = _moe(h2.reshape(T, D), x1.reshape(T, D), gates, mod,
                 w_exp_gate[l].astype(BF16), w_exp_up[l].astype(BF16), w_exp_down[l].astype(BF16),
                 w_sh_gate[l].astype(BF16), w_sh_up[l].astype(BF16), w_sh_down[l].astype(BF16),
                 row(ln2_g[l]), row(ln2_b[l]), S).reshape(B, S, D)
    return x
```

```python
import functools

import jax
import jax.numpy as jnp
from jax import lax
from jax.experimental import pallas as pl
from jax.experimental.pallas import tpu as pltpu
from jax.experimental.pallas import tpu_sc as plsc

F32 = jnp.float32
BF16 = jnp.bfloat16

D_MODEL = 1024
DEPTH = 4
POOL_WIDTH = 256
POOL_WINDOWS = (2, 4, 8, 16)
POOL_GROUP = 64
POOL_HALO = 16
QK_NOPE = 128
QK_ROPE = 64
V_HEAD = 128
N_HEADS = 6
Q_LORA = 384
KV_LORA = 256
ROPE_THETA = 10000.0
N_EXPERTS = 64
TOP_K = 8
N_GROUPS = 8
GROUP_SIZE = N_EXPERTS // N_GROUPS
TOPK_GROUPS = 4
D_EXPERT = 256
ROUTED_SCALE = 2.5
DEEPNORM_ALPHA = (2.0 * DEPTH) ** 0.25
LN_EPS = 1e-5
RMS_EPS = 1e-6
ATTN_SCALE = (QK_NOPE + QK_ROPE) ** -0.5
LANES = 128
QK_PAD = 2 * LANES
MASK_VALUE = -1e30

TM_PROJ = 512
TQ = 512
TK = 512
ROUTE_ROWS = 8
DISPATCH_BLOCK = 512
TM_COMBINE = 512
SC_CORES = 2
SC_SUBCORES = 16
SC_WINDOW = 64
HALF = D_MODEL // 2
VMEM_LIMIT = 48 * 1024 * 1024


def _cparams(sem):
    return pltpu.CompilerParams(dimension_semantics=sem, vmem_limit_bytes=VMEM_LIMIT)


def _split_hi_lo(a):
    hi = a.astype(BF16)
    lo = (a - hi.astype(F32)).astype(BF16)
    return hi, lo


def _pack_bf16_pairs(a):
    bits = lax.bitcast_convert_type(a.astype(F32), jnp.uint32)
    half = a.shape[1] // 2
    word = (bits[:, :half] >> 16) | (bits[:, half:] & jnp.uint32(0xFFFF0000))
    return lax.bitcast_convert_type(word, jnp.int32)


def _unpack_bf16_pairs(w):
    bits = lax.bitcast_convert_type(w, jnp.uint32)
    lo = lax.bitcast_convert_type(bits << 16, F32)
    hi = lax.bitcast_convert_type(bits & jnp.uint32(0xFFFF0000), F32)
    return lo, hi


def _ada_kernel(c_ref, w_ref, b_ref, o_ref):
    c = c_ref[...]
    cond = c * jax.nn.sigmoid(c)
    c_hi, c_lo = _split_hi_lo(cond)
    w_hi, w_lo = _split_hi_lo(w_ref[0])
    acc = jnp.dot(c_hi, w_hi, preferred_element_type=F32)
    acc += jnp.dot(c_lo, w_hi, preferred_element_type=F32)
    acc += jnp.dot(c_hi, w_lo, preferred_element_type=F32)
    o_ref[0] = acc + b_ref[0]


def _ada_mod(c, w_ada, b_ada):
    L, D, N = w_ada.shape
    B = c.shape[0]
    tn = 1536
    return pl.pallas_call(
        _ada_kernel,
        grid=(L, N // tn),
        in_specs=[
            pl.BlockSpec((B, D), lambda l, j: (0, 0)),
            pl.BlockSpec((1, D, tn), lambda l, j: (l, 0, j)),
            pl.BlockSpec((1, 1, tn), lambda l, j: (l, 0, j)),
        ],
        out_specs=pl.BlockSpec((1, B, tn), lambda l, j: (l, 0, j)),
        out_shape=jax.ShapeDtypeStruct((L, B, N), F32),
        compiler_params=_cparams(("arbitrary", "arbitrary")),
        name="ada_mod",
    )(c, w_ada, b_ada.reshape(L, 1, N))


def _rope_table_kernel(pos_ref, freq_ref, o_ref):
    ang = pos_ref[0].astype(F32) * freq_ref[...]
    o_ref[0, :, 0:LANES] = jnp.cos(ang)
    o_ref[0, :, LANES:2 * LANES] = jnp.sin(ang)


def _rope_table(positions):
    B, S = positions.shape
    half = QK_ROPE // 2
    inv_freq = ROPE_THETA ** (-jnp.arange(half, dtype=F32) / half)
    freq = jnp.tile(inv_freq, LANES // half).reshape(1, LANES)
    return pl.pallas_call(
        _rope_table_kernel,
        grid=(B,),
        in_specs=[
            pl.BlockSpec((1, S, 1), lambda b: (b, 0, 0)),
            pl.BlockSpec((1, LANES), lambda b: (0, 0)),
        ],
        out_specs=pl.BlockSpec((1, S, 2 * LANES), lambda b: (b, 0, 0)),
        out_shape=jax.ShapeDtypeStruct((B, S, 2 * LANES), F32),
        compiler_params=_cparams(("arbitrary",)),
        name="rope_table",
    )(positions.reshape(B, S, 1), freq)


def _rms(x, g):
    return x * lax.rsqrt(jnp.mean(x * x, axis=-1, keepdims=True) + RMS_EPS) * g


def _in_proj_kernel(x_ref, mod_ref, cs_ref, w_in_ref, qg_ref, kvg_ref, w_uqa_ref, w_uqb_ref, w_ukv_ref,
                    u_ref, q_ref, k_ref, v_ref):
    x = x_ref[0]
    sh1 = mod_ref[0, 0:1, :]
    sc1 = mod_ref[0, 1:2, :]
    h = (x * (1.0 + sc1) + sh1).astype(BF16)
    proj = jnp.dot(h, w_in_ref[...], preferred_element_type=F32)
    u_ref[0] = proj[:, 0:POOL_WIDTH]
    o = POOL_WIDTH
    q_lat = proj[:, o:o + Q_LORA]
    o += Q_LORA
    kv_lat = proj[:, o:o + KV_LORA]
    o += KV_LORA
    k_a = proj[:, o:o + LANES]
    k_b = proj[:, o + LANES:o + 2 * LANES]
    cosv = cs_ref[0, :, 0:LANES]
    sinv = cs_ref[0, :, LANES:2 * LANES]
    k_rot = (k_a * cosv + k_b * sinv).astype(BF16)

    qn = _rms(q_lat, qg_ref[...]).astype(BF16)
    q_a = jnp.dot(qn, w_uqa_ref[...], preferred_element_type=F32)
    q_b = jnp.dot(qn, w_uqb_ref[...], preferred_element_type=F32)
    kvn = _rms(kv_lat, kvg_ref[...]).astype(BF16)
    kv = jnp.dot(kvn, w_ukv_ref[...], preferred_element_type=F32)
    for hd in range(N_HEADS):
        b0 = hd * QK_PAD
        q_ref[0, hd, :, 0:LANES] = (q_a[:, b0:b0 + LANES] * ATTN_SCALE).astype(BF16)
        q_rot = q_a[:, b0 + LANES:b0 + QK_PAD] * cosv + q_b[:, hd * LANES:(hd + 1) * LANES] * sinv
        q_ref[0, hd, :, LANES:QK_PAD] = (q_rot * ATTN_SCALE).astype(BF16)
        k_ref[0, hd, :, 0:LANES] = kv[:, b0:b0 + QK_NOPE].astype(BF16)
        k_ref[0, hd, :, LANES:QK_PAD] = k_rot
        v_ref[0, hd] = kv[:, b0 + QK_NOPE:b0 + QK_NOPE + V_HEAD].astype(BF16)


def _in_proj(x, mod, cs, w_in_ext, qg, kvg, w_uqa, w_uqb, w_ukv_bf):
    B, S, D = x.shape
    tm = TM_PROJ
    const = lambda b, i: (0, 0)
    return pl.pallas_call(
        _in_proj_kernel,
        grid=(B, S // tm),
        in_specs=[
            pl.BlockSpec((1, tm, D), lambda b, i: (b, i, 0)),
            pl.BlockSpec((1, 6, D), lambda b, i: (b, 0, 0)),
            pl.BlockSpec((1, tm, 2 * LANES), lambda b, i: (b, i, 0)),
            pl.BlockSpec(w_in_ext.shape, const),
            pl.BlockSpec(qg.shape, const),
            pl.BlockSpec(kvg.shape, const),
            pl.BlockSpec(w_uqa.shape, const),
            pl.BlockSpec(w_uqb.shape, const),
            pl.BlockSpec(w_ukv_bf.shape, const),
        ],
        out_specs=[
            pl.BlockSpec((1, tm, POOL_WIDTH), lambda b, i: (b, i, 0)),
            pl.BlockSpec((1, N_HEADS, tm, QK_PAD), lambda b, i: (b, 0, i, 0)),
            pl.BlockSpec((1, N_HEADS, tm, QK_PAD), lambda b, i: (b, 0, i, 0)),
            pl.BlockSpec((1, N_HEADS, tm, V_HEAD), lambda b, i: (b, 0, i, 0)),
        ],
        out_shape=[
            jax.ShapeDtypeStruct((B, S, POOL_WIDTH), F32),
            jax.ShapeDtypeStruct((B, N_HEADS, S, QK_PAD), BF16),
            jax.ShapeDtypeStruct((B, N_HEADS, S, QK_PAD), BF16),
            jax.ShapeDtypeStruct((B, N_HEADS, S, V_HEAD), BF16),
        ],
        compiler_params=_cparams(("arbitrary", "arbitrary")),
        name="in_proj",
    )(x, mod, cs, w_in_ext, qg, kvg, w_uqa, w_uqb, w_ukv_bf)


def _attn_kernel(q_ref, k_ref, v_ref, o_ref):
    i = pl.program_id(2)
    q = q_ref[0, 0]
    row = i * TQ + lax.broadcasted_iota(jnp.int32, (TQ, TK), 0)
    col0 = lax.broadcasted_iota(jnp.int32, (TQ, TK), 1)

    def body(j, carry):
        m, l, acc = carry
        k = k_ref[0, 0, pl.ds(j * TK, TK), :]
        v = v_ref[0, 0, pl.ds(j * TK, TK), :]
        s = lax.dot_general(q, k, (((1,), (1,)), ((), ())), preferred_element_type=F32)
        s = jnp.where(row >= col0 + j * TK, s, MASK_VALUE)
        m_new = jnp.maximum(m, jnp.max(s, axis=-1, keepdims=True))
        alpha = jnp.exp(m - m_new)
        p = jnp.exp(s - m_new)
        l_new = alpha * l + jnp.sum(p, axis=-1, keepdims=True)
        acc_new = alpha * acc + jnp.dot(p.astype(BF16), v, preferred_element_type=F32)
        return m_new, l_new, acc_new

    init = (jnp.full((TQ, 1), MASK_VALUE, F32), jnp.zeros((TQ, 1), F32), jnp.zeros((TQ, V_HEAD), F32))
    _, l, acc = lax.fori_loop(0, i + 1, body, init)
    o_ref[0] = (acc / l).astype(BF16)


def _attention(q, k, v):
    B, H, S, _ = q.shape
    return pl.pallas_call(
        _attn_kernel,
        grid=(B, H, S // TQ),
        in_specs=[
            pl.BlockSpec((1, 1, TQ, QK_PAD), lambda b, h, i: (b, h, i, 0)),
            pl.BlockSpec((1, 1, S, QK_PAD), lambda b, h, i: (b, h, 0, 0)),
            pl.BlockSpec((1, 1, S, V_HEAD), lambda b, h, i: (b, h, 0, 0)),
        ],
        out_specs=pl.BlockSpec((1, TQ, V_HEAD), lambda b, h, i: (b, i, h)),
        out_shape=jax.ShapeDtypeStruct((B, S, H * V_HEAD), BF16),
        compiler_params=_cparams(("arbitrary", "arbitrary", "arbitrary")),
        name="attention",
    )(q, k, v)


def _layer_norm(v, g, b):
    mu = jnp.mean(v, axis=-1, keepdims=True)
    d = v - mu
    var = jnp.mean(d * d, axis=-1, keepdims=True)
    return d * lax.rsqrt(var + LN_EPS) * g + b


def _post_kernel(x_ref, u_ref, halo_ref, attn_ref, mod_ref, wpool_ref, pscale_ref, wout_p_ref, wout_a_ref,
                 lng_ref, lnb_ref, wr_hi_ref, wr_lo_ref, x1_ref, h2_ref, lt_ref):
    i = pl.program_id(1)
    tm = u_ref.shape[1]
    u = u_ref[0]
    halo = jnp.where(i > 0, halo_ref[0], 0.0)
    ext = jnp.concatenate([halo, u], axis=0)
    s2 = ext + pltpu.roll(ext, 1, 0)
    s4 = s2 + pltpu.roll(s2, 2, 0)
    s8 = s4 + pltpu.roll(s4, 4, 0)
    s16 = s8 + pltpu.roll(s8, 8, 0)
    lane = lax.broadcasted_iota(jnp.int32, (1, POOL_WIDTH), 1)
    win = jnp.where(lane < POOL_GROUP, s2,
                    jnp.where(lane < 2 * POOL_GROUP, s4, jnp.where(lane < 3 * POOL_GROUP, s8, s16)))
    win = win[POOL_HALO:, :]
    width = jnp.where(lane < POOL_GROUP, POOL_WINDOWS[0],
                      jnp.where(lane < 2 * POOL_GROUP, POOL_WINDOWS[1],
                                jnp.where(lane < 3 * POOL_GROUP, POOL_WINDOWS[2], POOL_WINDOWS[3])))
    t = i * tm + lax.broadcasted_iota(jnp.int32, (tm, 1), 0)
    count = jnp.minimum(t + 1, width).astype(F32)
    token_mix = win / count - u
    pooled = jnp.dot(token_mix.astype(BF16), wpool_ref[...], preferred_element_type=F32) * pscale_ref[...]
    mixed = jnp.dot(pooled.astype(BF16), wout_p_ref[...], preferred_element_type=F32)
    mixed += jnp.dot(attn_ref[0], wout_a_ref[...], preferred_element_type=F32)

    g1 = mod_ref[0, 2:3, :]
    sh2 = mod_ref[0, 3:4, :]
    sc2 = mod_ref[0, 4:5, :]
    x1 = _layer_norm(DEEPNORM_ALPHA * x_ref[0] + g1 * mixed, lng_ref[...], lnb_ref[...])
    x1_ref[0] = x1
    h2 = x1 * (1.0 + sc2) + sh2
    h_hi, h_lo = _split_hi_lo(h2)
    h2_ref[0] = _pack_bf16_pairs(h_hi)
    nt = (((1,), (1,)), ((), ()))
    lt = lax.dot_general(wr_hi_ref[...], h_hi, nt, preferred_element_type=F32)
    lt += lax.dot_general(wr_hi_ref[...], h_lo, nt, preferred_element_type=F32)
    lt += lax.dot_general(wr_lo_ref[...], h_hi, nt, preferred_element_type=F32)
    lt_ref[...] = lt


def _post_attn(x, u, attn, mod, wpool_bd, pscale, wout_p, wout_a, lng, lnb, wr_hi, wr_lo):
    B, S, D = x.shape
    tm = TM_PROJ
    nt = S // tm
    const = lambda b, i: (0, 0)
    halo_blocks = tm // POOL_HALO
    return pl.pallas_call(
        _post_kernel,
        grid=(B, nt),
        in_specs=[
            pl.BlockSpec((1, tm, D), lambda b, i: (b, i, 0)),
            pl.BlockSpec((1, tm, POOL_WIDTH), lambda b, i: (b, i, 0)),
            pl.BlockSpec((1, POOL_HALO, POOL_WIDTH), lambda b, i: (b, jnp.maximum(i * halo_blocks - 1, 0), 0)),
            pl.BlockSpec((1, tm, N_HEADS * V_HEAD), lambda b, i: (b, i, 0)),
            pl.BlockSpec((1, 6, D), lambda b, i: (b, 0, 0)),
            pl.BlockSpec(wpool_bd.shape, const),
            pl.BlockSpec(pscale.shape, const),
            pl.BlockSpec(wout_p.shape, const),
            pl.BlockSpec(wout_a.shape, const),
            pl.BlockSpec(lng.shape, const),
            pl.BlockSpec(lnb.shape, const),
            pl.BlockSpec(wr_hi.shape, const),
            pl.BlockSpec(wr_lo.shape, const),
        ],
        out_specs=[
            pl.BlockSpec((1, tm, D), lambda b, i: (b, i, 0)),
            pl.BlockSpec((1, tm, HALF), lambda b, i: (b, i, 0)),
            pl.BlockSpec((N_EXPERTS, tm), lambda b, i: (0, b * nt + i)),
        ],
        out_shape=[
            jax.ShapeDtypeStruct((B, S, D), F32),
            jax.ShapeDtypeStruct((B, S, HALF), jnp.int32),
            jax.ShapeDtypeStruct((N_EXPERTS, B * S), F32),
        ],
        compiler_params=_cparams(("arbitrary", "arbitrary")),
        name="post_attn",
    )(x, u, u, attn, mod, wpool_bd, pscale, wout_p, wout_a, lng, lnb, wr_hi, wr_lo)


def _select_first_max(vals, n_rounds, payload=None):
    work = list(vals)
    sel = [None] * len(vals)
    rounds = []
    for _ in range(n_rounds):
        m = functools.reduce(jnp.maximum, work)
        taken = None
        win_idx = jnp.zeros(m.shape, jnp.int32)
        win_val = jnp.zeros(m.shape, F32)
        for e in range(len(work)):
            hit = work[e] == m
            first = hit if taken is None else jnp.logical_and(hit, jnp.logical_not(taken))
            taken = hit if taken is None else jnp.logical_or(taken, hit)
            sel[e] = first if sel[e] is None else jnp.logical_or(sel[e], first)
            work[e] = jnp.where(first, -jnp.inf, work[e])
            if payload is not None:
                win_idx = jnp.where(first, e, win_idx)
                win_val = jnp.where(first, payload[e], win_val)
        rounds.append((win_idx, win_val))
    return sel, rounds


def _route_kernel(bias_ref, lt_ref, idx_ref, gate_ref):
    scores = [jax.nn.sigmoid(lt_ref[e]) for e in range(N_EXPERTS)]
    choice = [scores[e] + bias_ref[e] for e in range(N_EXPERTS)]
    group_score = []
    for g in range(N_GROUPS):
        vals = choice[g * GROUP_SIZE:(g + 1) * GROUP_SIZE]
        sel2, _ = _select_first_max(vals, 2)
        group_score.append(functools.reduce(
            jnp.add, [jnp.where(sel2[j], vals[j], 0.0) for j in range(GROUP_SIZE)]))
    group_sel, _ = _select_first_max(group_score, TOPK_GROUPS)
    masked = [jnp.where(group_sel[e // GROUP_SIZE], choice[e], -jnp.inf) for e in range(N_EXPERTS)]
    _, rounds = _select_first_max(masked, TOP_K, payload=scores)
    total = functools.reduce(jnp.add, [w for _, w in rounds])
    for k, (e_k, w_k) in enumerate(rounds):
        idx_ref[k] = e_k
        gate_ref[k] = w_k / total * ROUTED_SCALE


def _route(lt3, bias):
    E, R, _ = lt3.shape
    out_spec = pl.BlockSpec((TOP_K, ROUTE_ROWS, LANES), lambda r: (0, r, 0))
    return pl.pallas_call(
        _route_kernel,
        grid=(R // ROUTE_ROWS,),
        in_specs=[
            pl.BlockSpec(memory_space=pltpu.SMEM),
            pl.BlockSpec((E, ROUTE_ROWS, LANES), lambda r: (0, r, 0)),
        ],
        out_specs=[out_spec, out_spec],
        out_shape=[jax.ShapeDtypeStruct((TOP_K, R, LANES), jnp.int32),
                   jax.ShapeDtypeStruct((TOP_K, R, LANES), F32)],
        compiler_params=_cparams(("arbitrary",)),
        name="route",
    )(bias, lt3)


def _sc_gather(table, idx):
    n = idx.shape[0]
    width = table.shape[1]
    workers = SC_CORES * SC_SUBCORES
    per_worker = n // workers
    n_win = per_worker // SC_WINDOW
    assert n == workers * n_win * SC_WINDOW
    mesh = plsc.VectorSubcoreMesh(core_axis_name="c", subcore_axis_name="s",
                                  num_cores=SC_CORES, num_subcores=SC_SUBCORES)

    @functools.partial(
        pl.kernel, mesh=mesh,
        out_type=jax.ShapeDtypeStruct((n, width), table.dtype),
        scratch_types=[
            pltpu.VMEM((n_win, SC_WINDOW), jnp.int32),
            pltpu.VMEM((SC_WINDOW, width), table.dtype),
            pltpu.SemaphoreType.DMA,
        ],
    )
    def gather(table_hbm, idx_hbm, out_hbm, idx_v, rows_v, sem):
        wid = lax.axis_index("s") * SC_CORES + lax.axis_index("c")
        pltpu.sync_copy(idx_hbm.at[wid], idx_v)

        @pl.loop(0, n_win)
        def _(j):
            pltpu.async_copy(table_hbm.at[idx_v.at[j]], rows_v, sem).wait()
            pltpu.sync_copy(rows_v, out_hbm.at[pl.ds(wid * per_worker + j * SC_WINDOW, SC_WINDOW)])

    return gather(table, idx.reshape(workers, n_win, SC_WINDOW))


def _swiglu(h_lo, h_hi, wg_ref, wu_ref):
    def proj(w_ref):
        return (jnp.dot(h_lo, w_ref[:HALF, :], preferred_element_type=F32)
                + jnp.dot(h_hi, w_ref[HALF:, :], preferred_element_type=F32))
    gate = proj(wg_ref)
    up = proj(wu_ref)
    return (gate * jax.nn.sigmoid(gate) * up).astype(BF16)


def _expert_kernel(blk_expert_ref, n_real_ref, xs_ref, wg_ref, wu_ref, wd_ref, ys_ref):
    nb = pl.program_id(0)

    @pl.when(nb < n_real_ref[0])
    def _():
        lo, hi = _unpack_bf16_pairs(xs_ref[...])
        a = _swiglu(lo.astype(BF16), hi.astype(BF16), wg_ref.at[0], wu_ref.at[0])
        y = jnp.dot(a, wd_ref[0], preferred_element_type=F32)
        ys_ref[...] = _pack_bf16_pairs(y.astype(BF16))

    @pl.when(nb >= n_real_ref[0])
    def _():
        ys_ref[...] = jnp.zeros(ys_ref.shape, ys_ref.dtype)


def _expert_mm(blk_expert, n_real, xs, wg, wu, wd):
    P = xs.shape[0]
    nblk = P // DISPATCH_BLOCK
    D = wg.shape[1]
    row_map = lambda nb, be, nr: (jnp.minimum(nb, nr[0] - 1), 0)
    w_map = lambda nb, be, nr: (be[jnp.minimum(nb, nr[0] - 1)], 0, 0)
    grid_spec = pltpu.PrefetchScalarGridSpec(
        num_scalar_prefetch=2,
        grid=(nblk,),
        in_specs=[
            pl.BlockSpec((DISPATCH_BLOCK, HALF), row_map),
            pl.BlockSpec((1, D, D_EXPERT), w_map),
            pl.BlockSpec((1, D, D_EXPERT), w_map),
            pl.BlockSpec((1, D_EXPERT, D), w_map),
        ],
        out_specs=pl.BlockSpec((DISPATCH_BLOCK, HALF), lambda nb, be, nr: (nb, 0)),
    )
    return pl.pallas_call(
        _expert_kernel,
        grid_spec=grid_spec,
        out_shape=jax.ShapeDtypeStruct((P, HALF), jnp.int32),
        compiler_params=_cparams(("arbitrary",)),
        name="expert_mm",
    )(blk_expert, n_real, xs, wg, wu, wd)


def _combine_kernel(h_ref, x1_ref, yk_ref, g_ref, mod_ref, wsg_ref, wsu_ref, wsd_ref, lng_ref, lnb_ref, o_ref):
    lo, hi = _unpack_bf16_pairs(h_ref[...])
    a = _swiglu(lo.astype(BF16), hi.astype(BF16), wsg_ref, wsu_ref)
    ffn = jnp.dot(a, wsd_ref[...], preferred_element_type=F32)
    r_lo = jnp.zeros((h_ref.shape[0], HALF), F32)
    r_hi = jnp.zeros((h_ref.shape[0], HALF), F32)
    for k in range(TOP_K):
        y_lo, y_hi = _unpack_bf16_pairs(yk_ref[:, k * HALF:(k + 1) * HALF])
        g = g_ref[:, k:k + 1]
        r_lo += g * y_lo
        r_hi += g * y_hi
    ffn += jnp.concatenate([r_lo, r_hi], axis=1)
    g2 = mod_ref[0, 5:6, :]
    o_ref[...] = _layer_norm(DEEPNORM_ALPHA * x1_ref[...] + g2 * ffn, lng_ref[...], lnb_ref[...])


def _combine(h2p, x1, yk, gates, mod, wsg, wsu, wsd, lng, lnb, seq):
    T, D = x1.shape
    tm = TM_COMBINE
    per_seq = seq // tm
    const = lambda i: (0, 0)
    return pl.pallas_call(
        _combine_kernel,
        grid=(T // tm,),
        in_specs=[
            pl.BlockSpec((tm, HALF), lambda i: (i, 0)),
            pl.BlockSpec((tm, D), lambda i: (i, 0)),
            pl.BlockSpec((tm, TOP_K * HALF), lambda i: (i, 0)),
            pl.BlockSpec((tm, TOP_K), lambda i: (i, 0)),
            pl.BlockSpec((1, 6, D), lambda i: (i // per_seq, 0, 0)),
            pl.BlockSpec(wsg.shape, const),
            pl.BlockSpec(wsu.shape, const),
            pl.BlockSpec(wsd.shape, const),
            pl.BlockSpec(lng.shape, const),
            pl.BlockSpec(lnb.shape, const),
        ],
        out_specs=pl.BlockSpec((tm, D), lambda i: (i, 0)),
        out_shape=jax.ShapeDtypeStruct((T, D), F32),
        compiler_params=_cparams(("arbitrary",)),
        name="combine",
    )(h2p, x1, yk, gates, mod, wsg, wsu, wsd, lng, lnb)


def _dispatch_plan(idx_tk):
    T = idx_tk.shape[0]
    blk = DISPATCH_BLOCK
    n_slots = (T * TOP_K // blk + N_EXPERTS) * blk
    mask = (idx_tk[:, :, None] == jnp.arange(N_EXPERTS, dtype=jnp.int32)).any(axis=1).astype(jnp.int32)
    csum = jnp.cumsum(mask, axis=0)
    rank = csum - mask
    counts = csum[-1]
    padded = (counts + blk - 1) // blk * blk
    pad_end = jnp.cumsum(padded)
    pad_start = pad_end - padded
    dest = pad_start[idx_tk] + jnp.take_along_axis(rank, idx_tk, axis=1)
    tok = jnp.broadcast_to(jnp.arange(T, dtype=jnp.int32)[:, None], dest.shape)
    slot_tok = (jnp.arange(n_slots, dtype=jnp.int32) % T).at[dest.reshape(-1)].set(tok.reshape(-1))
    blk_expert = jnp.minimum(
        jnp.searchsorted(pad_end, jnp.arange(n_slots // blk, dtype=jnp.int32) * blk, side="right"),
        N_EXPERTS - 1).astype(jnp.int32)
    n_real = (pad_end[-1] // blk).astype(jnp.int32).reshape(1)
    return dest.reshape(-1).astype(jnp.int32), slot_tok, blk_expert, n_real


def _rope_pair(w):
    half = QK_ROPE // 2
    return w, jnp.concatenate([-w[:, half:], w[:, :half]], axis=1)


def _prep_layer(w_in, w_uq, w_ukv, w_pool, w_out, w_router):
    D = w_in.shape[0]
    zpad = lambda r: jnp.zeros((r, LANES - QK_ROPE), F32)
    lat = POOL_WIDTH + Q_LORA + KV_LORA
    k_a, k_b = _rope_pair(w_in[:, lat:lat + QK_ROPE])
    w_in_ext = jnp.concatenate([w_in[:, :lat], k_a, zpad(D), k_b, zpad(D)], axis=1).astype(BF16)

    qa_cols, qb_cols = [], []
    per_head = QK_NOPE + QK_ROPE
    for hd in range(N_HEADS):
        w_h = w_uq[:, hd * per_head:(hd + 1) * per_head]
        r_a, r_b = _rope_pair(w_h[:, QK_NOPE:])
        qa_cols += [w_h[:, :QK_NOPE], r_a, zpad(Q_LORA)]
        qb_cols += [r_b, zpad(Q_LORA)]
    w_uqa = jnp.concatenate(qa_cols, axis=1).astype(BF16)
    w_uqb = jnp.concatenate(qb_cols, axis=1).astype(BF16)

    wpool_bd = jax.scipy.linalg.block_diag(*[w_pool[g] for g in range(len(POOL_WINDOWS))]).astype(BF16)
    wr_hi, wr_lo = _split_hi_lo(w_router.T)
    return dict(w_in_ext=w_in_ext, w_uqa=w_uqa, w_uqb=w_uqb, w_ukv=w_ukv.astype(BF16), wpool_bd=wpool_bd,
                wout_p=w_out[:POOL_WIDTH].astype(BF16), wout_a=w_out[POOL_WIDTH:].astype(BF16),
                wr_hi=wr_hi, wr_lo=wr_lo)


def kernel(x, c, positions, w_ada, b_ada, w_in, q_norm_g, kv_norm_g, w_uq, w_ukv, w_pool, pool_scale,
           w_out, ln1_g, ln1_b, w_router, router_bias, w_exp_gate, w_exp_up, w_exp_down,
           w_sh_gate, w_sh_up, w_sh_down, ln2_g, ln2_b):
    B, S, D = x.shape
    T = B * S
    L = w_in.shape[0]
    row = lambda v: v.reshape(1, -1)

    mod_all = _ada_mod(c, w_ada, b_ada).reshape(L, B, 6, D)
    cs = _rope_table(positions)
    for l in range(L):
        p = _prep_layer(w_in[l], w_uq[l], w_ukv[l], w_pool[l], w_out[l], w_router[l])
        mod = mod_all[l]
        u, q, k, v = _in_proj(x, mod, cs, p["w_in_ext"], row(q_norm_g[l]), row(kv_norm_g[l]),
                              p["w_uqa"], p["w_uqb"], p["w_ukv"])
        attn = _attention(q, k, v)
        x1, h2p, lt = _post_attn(x, u, attn, mod, p["wpool_bd"], row(pool_scale[l]), p["wout_p"], p["wout_a"],
                                row(ln1_g[l]), row(ln1_b[l]), p["wr_hi"], p["wr_lo"])
        idx_k, gate_k = _route(lt.reshape(N_EXPERTS, T // LANES, LANES), router_bias[l])
        idx_tk = idx_k.reshape(TOP_K, T).T
        gates = gate_k.reshape(TOP_K, T).T
        dest, slot_tok, blk_expert, n_real = _dispatch_plan(idx_tk)
        h2p = h2p.reshape(T, HALF)
        xs = _sc_gather(h2p, slot_tok)
        ys = _expert_mm(blk_expert, n_real, xs, w_exp_gate[l].astype(BF16), w_exp_up[l].astype(BF16),
                        w_exp_down[l].astype(BF16))
        yk = _sc_gather(ys, dest).reshape(T, TOP_K * HALF)
        x = _combine(h2p, x1.reshape(T, D), yk, gates, mod,
                     w_sh_gate[l].astype(BF16), w_sh_up[l].astype(BF16), w_sh_down[l].astype(BF16),
                     row(ln2_g[l]), row(ln2_b[l]), S).reshape(B, S, D)
    return x
```

```python
import functools

import jax
import jax.numpy as jnp
from jax import lax
from jax.experimental import pallas as pl
from jax.experimental.pallas import tpu as pltpu
from jax.experimental.pallas import tpu_sc as plsc

F32 = jnp.float32
BF16 = jnp.bfloat16

D_MODEL = 1024
DEPTH = 4
POOL_WIDTH = 256
POOL_WINDOWS = (2, 4, 8, 16)
POOL_GROUP = 64
POOL_HALO = 16
QK_NOPE = 128
QK_ROPE = 64
V_HEAD = 128
N_HEADS = 6
Q_LORA = 384
KV_LORA = 256
ROPE_THETA = 10000.0
N_EXPERTS = 64
TOP_K = 8
N_GROUPS = 8
GROUP_SIZE = N_EXPERTS // N_GROUPS
TOPK_GROUPS = 4
D_EXPERT = 256
ROUTED_SCALE = 2.5
DEEPNORM_ALPHA = (2.0 * DEPTH) ** 0.25
LN_EPS = 1e-5
RMS_EPS = 1e-6
ATTN_SCALE = (QK_NOPE + QK_ROPE) ** -0.5
LANES = 128
QK_PAD = 2 * LANES
MASK_VALUE = -1e30

TM_PROJ = 512
TQ = 512
TK = 512
ROUTE_ROWS = 8
DISPATCH_BLOCK = 512
TM_COMBINE = 512
SC_CORES = 2
SC_SUBCORES = 16
SC_WINDOW = 64
HALF = D_MODEL // 2
VMEM_LIMIT = 48 * 1024 * 1024


def _cparams(sem):
    return pltpu.CompilerParams(dimension_semantics=sem, vmem_limit_bytes=VMEM_LIMIT)


def _split_hi_lo(a):
    hi = a.astype(BF16)
    lo = (a - hi.astype(F32)).astype(BF16)
    return hi, lo


def _pack_bf16_pairs(a):
    bits = lax.bitcast_convert_type(a.astype(F32), jnp.uint32)
    half = a.shape[1] // 2
    word = (bits[:, :half] >> 16) | (bits[:, half:] & jnp.uint32(0xFFFF0000))
    return lax.bitcast_convert_type(word, jnp.int32)


def _unpack_bf16_pairs(w):
    bits = lax.bitcast_convert_type(w, jnp.uint32)
    lo = lax.bitcast_convert_type(bits << 16, F32)
    hi = lax.bitcast_convert_type(bits & jnp.uint32(0xFFFF0000), F32)
    return lo, hi


def _ada_kernel(c_ref, w_ref, b_ref, o_ref):
    c = c_ref[...]
    cond = c * jax.nn.sigmoid(c)
    c_hi, c_lo = _split_hi_lo(cond)
    w_hi, w_lo = _split_hi_lo(w_ref[0])
    acc = jnp.dot(c_hi, w_hi, preferred_element_type=F32)
    acc += jnp.dot(c_lo, w_hi, preferred_element_type=F32)
    acc += jnp.dot(c_hi, w_lo, preferred_element_type=F32)
    o_ref[0] = acc + b_ref[0]


def _ada_mod(c, w_ada, b_ada):
    L, D, N = w_ada.shape
    B = c.shape[0]
    tn = 1536
    return pl.pallas_call(
        _ada_kernel,
        grid=(L, N // tn),
        in_specs=[
            pl.BlockSpec((B, D), lambda l, j: (0, 0)),
            pl.BlockSpec((1, D, tn), lambda l, j: (l, 0, j)),
            pl.BlockSpec((1, 1, tn), lambda l, j: (l, 0, j)),
        ],
        out_specs=pl.BlockSpec((1, B, tn), lambda l, j: (l, 0, j)),
        out_shape=jax.ShapeDtypeStruct((L, B, N), F32),
        compiler_params=_cparams(("arbitrary", "arbitrary")),
        name="ada_mod",
    )(c, w_ada, b_ada.reshape(L, 1, N))


def _rope_table_kernel(pos_ref, freq_ref, o_ref):
    ang = pos_ref[0].astype(F32) * freq_ref[...]
    o_ref[0, :, 0:LANES] = jnp.cos(ang)
    o_ref[0, :, LANES:2 * LANES] = jnp.sin(ang)


def _rope_table(positions):
    B, S = positions.shape
    half = QK_ROPE // 2
    inv_freq = ROPE_THETA ** (-jnp.arange(half, dtype=F32) / half)
    freq = jnp.tile(inv_freq, LANES // half).reshape(1, LANES)
    return pl.pallas_call(
        _rope_table_kernel,
        grid=(B,),
        in_specs=[
            pl.BlockSpec((1, S, 1), lambda b: (b, 0, 0)),
            pl.BlockSpec((1, LANES), lambda b: (0, 0)),
        ],
        out_specs=pl.BlockSpec((1, S, 2 * LANES), lambda b: (b, 0, 0)),
        out_shape=jax.ShapeDtypeStruct((B, S, 2 * LANES), F32),
        compiler_params=_cparams(("arbitrary",)),
        name="rope_table",
    )(positions.reshape(B, S, 1), freq)


def _rms(x, g):
    return x * lax.rsqrt(jnp.mean(x * x, axis=-1, keepdims=True) + RMS_EPS) * g


def _in_proj_kernel(x_ref, mod_ref, cs_ref, w_in_ref, qg_ref, kvg_ref, w_uqa_ref, w_uqb_ref, w_ukv_ref,
                    u_ref, q_ref, k_ref, v_ref):
    x = x_ref[0]
    sh1 = mod_ref[0, 0:1, :]
    sc1 = mod_ref[0, 1:2, :]
    h = (x * (1.0 + sc1) + sh1).astype(BF16)
    proj = jnp.dot(h, w_in_ref[...], preferred_element_type=F32)
    u_ref[0] = proj[:, 0:POOL_WIDTH]
    o = POOL_WIDTH
    q_lat = proj[:, o:o + Q_LORA]
    o += Q_LORA
    kv_lat = proj[:, o:o + KV_LORA]
    o += KV_LORA
    k_a = proj[:, o:o + LANES]
    k_b = proj[:, o + LANES:o + 2 * LANES]
    cosv = cs_ref[0, :, 0:LANES]
    sinv = cs_ref[0, :, LANES:2 * LANES]
    k_rot = (k_a * cosv + k_b * sinv).astype(BF16)

    qn = _rms(q_lat, qg_ref[...]).astype(BF16)
    q_a = jnp.dot(qn, w_uqa_ref[...], preferred_element_type=F32)
    q_b = jnp.dot(qn, w_uqb_ref[...], preferred_element_type=F32)
    kvn = _rms(kv_lat, kvg_ref[...]).astype(BF16)
    kv = jnp.dot(kvn, w_ukv_ref[...], preferred_element_type=F32)
    for hd in range(N_HEADS):
        b0 = hd * QK_PAD
        q_ref[0, hd, :, 0:LANES] = (q_a[:, b0:b0 + LANES] * ATTN_SCALE).astype(BF16)
        q_rot = q_a[:, b0 + LANES:b0 + QK_PAD] * cosv + q_b[:, hd * LANES:(hd + 1) * LANES] * sinv
        q_ref[0, hd, :, LANES:QK_PAD] = (q_rot * ATTN_SCALE).astype(BF16)
        k_ref[0, hd, :, 0:LANES] = kv[:, b0:b0 + QK_NOPE].astype(BF16)
        k_ref[0, hd, :, LANES:QK_PAD] = k_rot
        v_ref[0, hd] = kv[:, b0 + QK_NOPE:b0 + QK_NOPE + V_HEAD].astype(BF16)


def _in_proj(x, mod, cs, w_in_ext, qg, kvg, w_uqa, w_uqb, w_ukv_bf):
    B, S, D = x.shape
    tm = TM_PROJ
    const = lambda b, i: (0, 0)
    return pl.pallas_call(
        _in_proj_kernel,
        grid=(B, S // tm),
        in_specs=[
            pl.BlockSpec((1, tm, D), lambda b, i: (b, i, 0)),
            pl.BlockSpec((1, 6, D), lambda b, i: (b, 0, 0)),
            pl.BlockSpec((1, tm, 2 * LANES), lambda b, i: (b, i, 0)),
            pl.BlockSpec(w_in_ext.shape, const),
            pl.BlockSpec(qg.shape, const),
            pl.BlockSpec(kvg.shape, const),
            pl.BlockSpec(w_uqa.shape, const),
            pl.BlockSpec(w_uqb.shape, const),
            pl.BlockSpec(w_ukv_bf.shape, const),
        ],
        out_specs=[
            pl.BlockSpec((1, tm, POOL_WIDTH), lambda b, i: (b, i, 0)),
            pl.BlockSpec((1, N_HEADS, tm, QK_PAD), lambda b, i: (b, 0, i, 0)),
            pl.BlockSpec((1, N_HEADS, tm, QK_PAD), lambda b, i: (b, 0, i, 0)),
            pl.BlockSpec((1, N_HEADS, tm, V_HEAD), lambda b, i: (b, 0, i, 0)),
        ],
        out_shape=[
            jax.ShapeDtypeStruct((B, S, POOL_WIDTH), F32),
            jax.ShapeDtypeStruct((B, N_HEADS, S, QK_PAD), BF16),
            jax.ShapeDtypeStruct((B, N_HEADS, S, QK_PAD), BF16),
            jax.ShapeDtypeStruct((B, N_HEADS, S, V_HEAD), BF16),
        ],
        compiler_params=_cparams(("arbitrary", "arbitrary")),
        name="in_proj",
    )(x, mod, cs, w_in_ext, qg, kvg, w_uqa, w_uqb, w_ukv_bf)


def _attn_kernel(q_ref, k_ref, v_ref, o_ref):
    i = pl.program_id(2)
    q = q_ref[0, 0]
    row = i * TQ + lax.broadcasted_iota(jnp.int32, (TQ, TK), 0)
    col0 = lax.broadcasted_iota(jnp.int32, (TQ, TK), 1)

    def body(j, carry):
        m, l, acc = carry
        k = k_ref[0, 0, pl.ds(j * TK, TK), :]
        v = v_ref[0, 0, pl.ds(j * TK, TK), :]
        s = lax.dot_general(q, k, (((1,), (1,)), ((), ())), preferred_element_type=F32)
        s = jnp.where(row >= col0 + j * TK, s, MASK_VALUE)
        m_new = jnp.maximum(m, jnp.max(s, axis=-1, keepdims=True))
        alpha = jnp.exp(m - m_new)
        p = jnp.exp(s - m_new)
        l_new = alpha * l + jnp.sum(p, axis=-1, keepdims=True)
        acc_new = alpha * acc + jnp.dot(p.astype(BF16), v, preferred_element_type=F32)
        return m_new, l_new, acc_new

    init = (jnp.full((TQ, 1), MASK_VALUE, F32), jnp.zeros((TQ, 1), F32), jnp.zeros((TQ, V_HEAD), F32))
    _, l, acc = lax.fori_loop(0, i + 1, body, init)
    o_ref[0] = (acc / l).astype(BF16)


def _attention(q, k, v):
    B, H, S, _ = q.shape
    return pl.pallas_call(
        _attn_kernel,
        grid=(B, H, S // TQ),
        in_specs=[
            pl.BlockSpec((1, 1, TQ, QK_PAD), lambda b, h, i: (b, h, i, 0)),
            pl.BlockSpec((1, 1, S, QK_PAD), lambda b, h, i: (b, h, 0, 0)),
            pl.BlockSpec((1, 1, S, V_HEAD), lambda b, h, i: (b, h, 0, 0)),
        ],
        out_specs=pl.BlockSpec((1, TQ, V_HEAD), lambda b, h, i: (b, i, h)),
        out_shape=jax.ShapeDtypeStruct((B, S, H * V_HEAD), BF16),
        compiler_params=_cparams(("arbitrary", "arbitrary", "arbitrary")),
        name="attention",
    )(q, k, v)


def _layer_norm(v, g, b):
    mu = jnp.mean(v, axis=-1, keepdims=True)
    d = v - mu
    var = jnp.mean(d * d, axis=-1, keepdims=True)
    return d * lax.rsqrt(var + LN_EPS) * g + b


def _post_kernel(x_ref, u_ref, halo_ref, attn_ref, mod_ref, wpool_ref, pscale_ref, wout_p_ref, wout_a_ref,
                 lng_ref, lnb_ref, wr_hi_ref, wr_lo_ref, x1_ref, h2_ref, lt_ref):
    i = pl.program_id(1)
    tm = u_ref.shape[1]
    u = u_ref[0]
    halo = jnp.where(i > 0, halo_ref[0], 0.0)
    ext = jnp.concatenate([halo, u], axis=0)
    s2 = ext + pltpu.roll(ext, 1, 0)
    s4 = s2 + pltpu.roll(s2, 2, 0)
    s8 = s4 + pltpu.roll(s4, 4, 0)
    s16 = s8 + pltpu.roll(s8, 8, 0)
    lane = lax.broadcasted_iota(jnp.int32, (1, POOL_WIDTH), 1)
    win = jnp.where(lane < POOL_GROUP, s2,
                    jnp.where(lane < 2 * POOL_GROUP, s4, jnp.where(lane < 3 * POOL_GROUP, s8, s16)))
    win = win[POOL_HALO:, :]
    width = jnp.where(lane < POOL_GROUP, POOL_WINDOWS[0],
                      jnp.where(lane < 2 * POOL_GROUP, POOL_WINDOWS[1],
                                jnp.where(lane < 3 * POOL_GROUP, POOL_WINDOWS[2], POOL_WINDOWS[3])))
    t = i * tm + lax.broadcasted_iota(jnp.int32, (tm, 1), 0)
    count = jnp.minimum(t + 1, width).astype(F32)
    token_mix = win / count - u
    pooled = jnp.dot(token_mix.astype(BF16), wpool_ref[...], preferred_element_type=F32) * pscale_ref[...]
    mixed = jnp.dot(pooled.astype(BF16), wout_p_ref[...], preferred_element_type=F32)
    mixed += jnp.dot(attn_ref[0], wout_a_ref[...], preferred_element_type=F32)

    g1 = mod_ref[0, 2:3, :]
    sh2 = mod_ref[0, 3:4, :]
    sc2 = mod_ref[0, 4:5, :]
    x1 = _layer_norm(DEEPNORM_ALPHA * x_ref[0] + g1 * mixed, lng_ref[...], lnb_ref[...])
    x1_ref[0] = x1
    h2 = x1 * (1.0 + sc2) + sh2
    h_hi, h_lo = _split_hi_lo(h2)
    h2_ref[0] = _pack_bf16_pairs(h_hi)
    nt = (((1,), (1,)), ((), ()))
    lt = lax.dot_general(wr_hi_ref[...], h_hi, nt, preferred_element_type=F32)
    lt += lax.dot_general(wr_hi_ref[...], h_lo, nt, preferred_element_type=F32)
    lt += lax.dot_general(wr_lo_ref[...], h_hi, nt, preferred_element_type=F32)
    lt_ref[...] = lt


def _post_attn(x, u, attn, mod, wpool_bd, pscale, wout_p, wout_a, lng, lnb, wr_hi, wr_lo):
    B, S, D = x.shape
    tm = TM_PROJ
    nt = S // tm
    const = lambda b, i: (0, 0)
    halo_blocks = tm // POOL_HALO
    return pl.pallas_call(
        _post_kernel,
        grid=(B, nt),
        in_specs=[
            pl.BlockSpec((1, tm, D), lambda b, i: (b, i, 0)),
            pl.BlockSpec((1, tm, POOL_WIDTH), lambda b, i: (b, i, 0)),
            pl.BlockSpec((1, POOL_HALO, POOL_WIDTH), lambda b, i: (b, jnp.maximum(i * halo_blocks - 1, 0), 0)),
            pl.BlockSpec((1, tm, N_HEADS * V_HEAD), lambda b, i: (b, i, 0)),
            pl.BlockSpec((1, 6, D), lambda b, i: (b, 0, 0)),
            pl.BlockSpec(wpool_bd.shape, const),
            pl.BlockSpec(pscale.shape, const),
            pl.BlockSpec(wout_p.shape, const),
            pl.BlockSpec(wout_a.shape, const),
            pl.BlockSpec(lng.shape, const),
            pl.BlockSpec(lnb.shape, const),
            pl.BlockSpec(wr_hi.shape, const),
            pl.BlockSpec(wr_lo.shape, const),
        ],
        out_specs=[
            pl.BlockSpec((1, tm, D), lambda b, i: (b, i, 0)),
            pl.BlockSpec((1, tm, HALF), lambda b, i: (b, i, 0)),
            pl.BlockSpec((N_EXPERTS, tm), lambda b, i: (0, b * nt + i)),
        ],
        out_shape=[
            jax.ShapeDtypeStruct((B, S, D), F32),
            jax.ShapeDtypeStruct((B, S, HALF), jnp.int32),
            jax.ShapeDtypeStruct((N_EXPERTS, B * S), F32),
        ],
        compiler_params=_cparams(("arbitrary", "arbitrary")),
        name="post_attn",
    )(x, u, u, attn, mod, wpool_bd, pscale, wout_p, wout_a, lng, lnb, wr_hi, wr_lo)


def _select_first_max(vals, n_rounds, payload=None):
    work = list(vals)
    sel = [None] * len(vals)
    rounds = []
    for _ in range(n_rounds):
        m = functools.reduce(jnp.maximum, work)
        taken = None
        win_idx = jnp.zeros(m.shape, jnp.int32)
        win_val = jnp.zeros(m.shape, F32)
        for e in range(len(work)):
            hit = work[e] == m
            first = hit if taken is None else jnp.logical_and(hit, jnp.logical_not(taken))
            taken = hit if taken is None else jnp.logical_or(taken, hit)
            sel[e] = first if sel[e] is None else jnp.logical_or(sel[e], first)
            work[e] = jnp.where(first, -jnp.inf, work[e])
            if payload is not None:
                win_idx = jnp.where(first, e, win_idx)
                win_val = jnp.where(first, payload[e], win_val)
        rounds.append((win_idx, win_val))
    return sel, rounds


def _route_kernel(bias_ref, lt_ref, idx_ref, gate_ref):
    scores = [jax.nn.sigmoid(lt_ref[e]) for e in range(N_EXPERTS)]
    choice = [scores[e] + bias_ref[e] for e in range(N_EXPERTS)]
    group_score = []
    for g in range(N_GROUPS):
        vals = choice[g * GROUP_SIZE:(g + 1) * GROUP_SIZE]
        sel2, _ = _select_first_max(vals, 2)
        group_score.append(functools.reduce(
            jnp.add, [jnp.where(sel2[j], vals[j], 0.0) for j in range(GROUP_SIZE)]))
    group_sel, _ = _select_first_max(group_score, TOPK_GROUPS)
    masked = [jnp.where(group_sel[e // GROUP_SIZE], choice[e], -jnp.inf) for e in range(N_EXPERTS)]
    _, rounds = _select_first_max(masked, TOP_K, payload=scores)
    total = functools.reduce(jnp.add, [w for _, w in rounds])
    for k, (e_k, w_k) in enumerate(rounds):
        idx_ref[k] = e_k
        gate_ref[k] = w_k / total * ROUTED_SCALE


def _route(lt3, bias):
    E, R, _ = lt3.shape
    out_spec = pl.BlockSpec((TOP_K, ROUTE_ROWS, LANES), lambda r: (0, r, 0))
    return pl.pallas_call(
        _route_kernel,
        grid=(R // ROUTE_ROWS,),
        in_specs=[
            pl.BlockSpec(memory_space=pltpu.SMEM),
            pl.BlockSpec((E, ROUTE_ROWS, LANES), lambda r: (0, r, 0)),
        ],
        out_specs=[out_spec, out_spec],
        out_shape=[jax.ShapeDtypeStruct((TOP_K, R, LANES), jnp.int32),
                   jax.ShapeDtypeStruct((TOP_K, R, LANES), F32)],
        compiler_params=_cparams(("arbitrary",)),
        name="route",
    )(bias, lt3)


def _sc_mesh():
    return plsc.VectorSubcoreMesh(core_axis_name="c", subcore_axis_name="s",
                                  num_cores=SC_CORES, num_subcores=SC_SUBCORES)


def _sc_worker_id():
    return lax.axis_index("s") * SC_CORES + lax.axis_index("c")


def _sc_gather(table, idx):
    n = idx.shape[0]
    width = table.shape[1]
    workers = SC_CORES * SC_SUBCORES
    per_worker = n // workers
    n_win = per_worker // SC_WINDOW
    assert n == workers * n_win * SC_WINDOW

    @functools.partial(
        pl.kernel, mesh=_sc_mesh(),
        out_type=jax.ShapeDtypeStruct((n, width), table.dtype),
        scratch_types=[
            pltpu.VMEM((n_win, SC_WINDOW), jnp.int32),
            pltpu.VMEM((SC_WINDOW, width), table.dtype),
            pltpu.SemaphoreType.DMA,
        ],
    )
    def gather(table_hbm, idx_hbm, out_hbm, idx_v, rows_v, sem):
        wid = _sc_worker_id()
        pltpu.sync_copy(idx_hbm.at[wid], idx_v)

        @pl.loop(0, n_win)
        def _(j):
            pltpu.async_copy(table_hbm.at[idx_v.at[j]], rows_v, sem).wait()
            pltpu.sync_copy(rows_v, out_hbm.at[pl.ds(wid * per_worker + j * SC_WINDOW, SC_WINDOW)])

    return gather(table, idx.reshape(workers, n_win, SC_WINDOW))


def _sc_dispatch(rows, dest_kt, n_slots):
    T, width = rows.shape
    K = dest_kt.shape[0]
    workers = SC_CORES * SC_SUBCORES
    per_worker = T // workers
    n_win = per_worker // SC_WINDOW
    assert T == workers * n_win * SC_WINDOW
    dest_w = dest_kt.reshape(K, workers, n_win, SC_WINDOW).transpose(1, 2, 0, 3)
    dest_w = dest_w.reshape(workers, n_win * K, SC_WINDOW)

    @functools.partial(
        pl.kernel, mesh=_sc_mesh(),
        out_type=jax.ShapeDtypeStruct((n_slots, width), rows.dtype),
        scratch_types=[
            pltpu.VMEM((n_win * K, SC_WINDOW), jnp.int32),
            pltpu.VMEM((SC_WINDOW, width), rows.dtype),
        ],
    )
    def dispatch(rows_hbm, dest_hbm, out_hbm, idx_v, rows_v):
        wid = _sc_worker_id()
        pltpu.sync_copy(dest_hbm.at[wid], idx_v)

        @pl.loop(0, n_win)
        def _(j):
            pltpu.sync_copy(rows_hbm.at[pl.ds(wid * per_worker + j * SC_WINDOW, SC_WINDOW)], rows_v)
            for k in range(K):
                pltpu.sync_copy(rows_v, out_hbm.at[idx_v.at[j * K + k]])

    return dispatch(rows, dest_w)


def _swiglu(h_lo, h_hi, wg_ref, wu_ref):
    def proj(w_ref):
        return (jnp.dot(h_lo, w_ref[:HALF, :], preferred_element_type=F32)
                + jnp.dot(h_hi, w_ref[HALF:, :], preferred_element_type=F32))
    gate = proj(wg_ref)
    up = proj(wu_ref)
    return (gate * jax.nn.sigmoid(gate) * up).astype(BF16)


def _expert_kernel(blk_expert_ref, n_real_ref, xs_ref, wg_ref, wu_ref, wd_ref, ys_ref, wg_bf, wu_bf, wd_bf):
    nb = pl.program_id(0)
    last = n_real_ref[0] - 1
    new_expert = jnp.logical_or(nb == 0, blk_expert_ref[jnp.minimum(nb, last)]
                                != blk_expert_ref[jnp.minimum(jnp.maximum(nb - 1, 0), last)])

    @pl.when(new_expert)
    def _():
        wg_bf[...] = wg_ref[0].astype(BF16)
        wu_bf[...] = wu_ref[0].astype(BF16)
        wd_bf[...] = wd_ref[0].astype(BF16)

    @pl.when(nb <= last)
    def _():
        lo, hi = _unpack_bf16_pairs(xs_ref[...])
        a = _swiglu(lo.astype(BF16), hi.astype(BF16), wg_bf, wu_bf)
        y = jnp.dot(a, wd_bf[...], preferred_element_type=F32)
        ys_ref[...] = _pack_bf16_pairs(y.astype(BF16))

    @pl.when(nb > last)
    def _():
        ys_ref[...] = jnp.zeros(ys_ref.shape, ys_ref.dtype)


def _expert_mm(blk_expert, n_real, xs, wg, wu, wd):
    P = xs.shape[0]
    nblk = P // DISPATCH_BLOCK
    D = wg.shape[1]
    row_map = lambda nb, be, nr: (jnp.minimum(nb, nr[0] - 1), 0)
    w_map = lambda nb, be, nr: (be[jnp.minimum(nb, nr[0] - 1)], 0, 0)
    grid_spec = pltpu.PrefetchScalarGridSpec(
        num_scalar_prefetch=2,
        grid=(nblk,),
        in_specs=[
            pl.BlockSpec((DISPATCH_BLOCK, HALF), row_map),
            pl.BlockSpec((1, D, D_EXPERT), w_map),
            pl.BlockSpec((1, D, D_EXPERT), w_map),
            pl.BlockSpec((1, D_EXPERT, D), w_map),
        ],
        out_specs=pl.BlockSpec((DISPATCH_BLOCK, HALF), lambda nb, be, nr: (nb, 0)),
        scratch_shapes=[pltpu.VMEM((D, D_EXPERT), BF16), pltpu.VMEM((D, D_EXPERT), BF16),
                        pltpu.VMEM((D_EXPERT, D), BF16)],
    )
    return pl.pallas_call(
        _expert_kernel,
        grid_spec=grid_spec,
        out_shape=jax.ShapeDtypeStruct((P, HALF), jnp.int32),
        compiler_params=_cparams(("arbitrary",)),
        name="expert_mm",
    )(blk_expert, n_real, xs, wg, wu, wd)


def _rank_kernel(idx_ref, rank_ref, cnt_ref, tri_ref, carry_ref):
    i = pl.program_id(0)
    tm = idx_ref.shape[1]

    @pl.when(i == 0)
    def _():
        r = lax.broadcasted_iota(jnp.int32, (tm, tm), 0)
        c = lax.broadcasted_iota(jnp.int32, (tm, tm), 1)
        tri_ref[...] = (r < c).astype(BF16)
        carry_ref[...] = jnp.zeros(carry_ref.shape, F32)

    e_iota = lax.broadcasted_iota(jnp.int32, (N_EXPERTS, tm), 0)
    hits = [e_iota == idx_ref[k:k + 1, :] for k in range(TOP_K)]
    mask = functools.reduce(jnp.logical_or, hits).astype(F32)
    rank = jnp.dot(mask.astype(BF16), tri_ref[...], preferred_element_type=F32) + carry_ref[:, 0:1]
    for k in range(TOP_K):
        rank_ref[k:k + 1, :] = jnp.sum(jnp.where(hits[k], rank, 0.0), axis=0, keepdims=True).astype(jnp.int32)
    carry_ref[...] = carry_ref[...] + jnp.sum(mask, axis=1, keepdims=True)
    cnt_ref[...] = carry_ref[...]


def _rank(idx_kt):
    K, T = idx_kt.shape
    tm = 1024
    return pl.pallas_call(
        _rank_kernel,
        grid=(T // tm,),
        in_specs=[pl.BlockSpec((K, tm), lambda i: (0, i))],
        out_specs=[pl.BlockSpec((K, tm), lambda i: (0, i)),
                   pl.BlockSpec((N_EXPERTS, LANES), lambda i: (0, 0))],
        out_shape=[jax.ShapeDtypeStruct((K, T), jnp.int32),
                   jax.ShapeDtypeStruct((N_EXPERTS, LANES), F32)],
        scratch_shapes=[pltpu.VMEM((tm, tm), BF16), pltpu.VMEM((N_EXPERTS, LANES), F32)],
        compiler_params=_cparams(("arbitrary",)),
        name="rank",
    )(idx_kt)


def _dest_kernel(idx_ref, rank_ref, start_ref, dest_ref):
    tm = idx_ref.shape[1]
    e_iota = lax.broadcasted_iota(jnp.int32, (N_EXPERTS, tm), 0)
    start = start_ref[:, 0:1]
    for k in range(TOP_K):
        base = jnp.sum(jnp.where(e_iota == idx_ref[k:k + 1, :], start, 0.0), axis=0, keepdims=True)
        dest_ref[k:k + 1, :] = rank_ref[k:k + 1, :] + base.astype(jnp.int32)


def _dest(idx_kt, rank_kt, start):
    K, T = idx_kt.shape
    tm = 1024
    tok_spec = pl.BlockSpec((K, tm), lambda i: (0, i))
    return pl.pallas_call(
        _dest_kernel,
        grid=(T // tm,),
        in_specs=[tok_spec, tok_spec, pl.BlockSpec((N_EXPERTS, LANES), lambda i: (0, 0))],
        out_specs=tok_spec,
        out_shape=jax.ShapeDtypeStruct((K, T), jnp.int32),
        compiler_params=_cparams(("arbitrary",)),
        name="dest",
    )(idx_kt, rank_kt, start)


def _block_plan(counts, n_blocks):
    blk = DISPATCH_BLOCK
    counts = counts.astype(jnp.int32)
    padded = (counts + blk - 1) // blk * blk
    pad_end = jnp.cumsum(padded)
    pad_start = pad_end - padded
    block_first = jnp.arange(n_blocks, dtype=jnp.int32) * blk
    blk_expert = jnp.minimum(jnp.sum(pad_end[None, :] <= block_first[:, None], axis=1), N_EXPERTS - 1)
    n_real = (pad_end[-1] // blk).reshape(1)
    return pad_start, blk_expert.astype(jnp.int32), n_real.astype(jnp.int32)


def _combine_kernel(h_ref, x1_ref, yk_ref, g_ref, mod_ref, wsg_ref, wsu_ref, wsd_ref, lng_ref, lnb_ref, o_ref):
    lo, hi = _unpack_bf16_pairs(h_ref[...])
    a = _swiglu(lo.astype(BF16), hi.astype(BF16), wsg_ref, wsu_ref)
    ffn = jnp.dot(a, wsd_ref[...], preferred_element_type=F32)
    r_lo = jnp.zeros((h_ref.shape[0], HALF), F32)
    r_hi = jnp.zeros((h_ref.shape[0], HALF), F32)
    for k in range(TOP_K):
        y_lo, y_hi = _unpack_bf16_pairs(yk_ref[k])
        g = g_ref[:, k:k + 1]
        r_lo += g * y_lo
        r_hi += g * y_hi
    ffn += jnp.concatenate([r_lo, r_hi], axis=1)
    g2 = mod_ref[0, 5:6, :]
    o_ref[...] = _layer_norm(DEEPNORM_ALPHA * x1_ref[...] + g2 * ffn, lng_ref[...], lnb_ref[...])


def _combine(h2p, x1, yk, gates, mod, wsg, wsu, wsd, lng, lnb, seq):
    T, D = x1.shape
    tm = TM_COMBINE
    per_seq = seq // tm
    const = lambda i: (0, 0)
    return pl.pallas_call(
        _combine_kernel,
        grid=(T // tm,),
        in_specs=[
            pl.BlockSpec((tm, HALF), lambda i: (i, 0)),
            pl.BlockSpec((tm, D), lambda i: (i, 0)),
            pl.BlockSpec((TOP_K, tm, HALF), lambda i: (0, i, 0)),
            pl.BlockSpec((tm, TOP_K), lambda i: (i, 0)),
            pl.BlockSpec((1, 6, D), lambda i: (i // per_seq, 0, 0)),
            pl.BlockSpec(wsg.shape, const),
            pl.BlockSpec(wsu.shape, const),
            pl.BlockSpec(wsd.shape, const),
            pl.BlockSpec(lng.shape, const),
            pl.BlockSpec(lnb.shape, const),
        ],
        out_specs=pl.BlockSpec((tm, D), lambda i: (i, 0)),
        out_shape=jax.ShapeDtypeStruct((T, D), F32),
        compiler_params=_cparams(("arbitrary",)),
        name="combine",
    )(h2p, x1, yk, gates, mod, wsg, wsu, wsd, lng, lnb)


def _rope_pair(w):
    half = QK_ROPE // 2
    return w, jnp.concatenate([-w[:, half:], w[:, :half]], axis=1)


def _prep_layer(w_in, w_uq, w_ukv, w_pool, w_out, w_router):
    D = w_in.shape[0]
    zpad = lambda r: jnp.zeros((r, LANES - QK_ROPE), F32)
    lat = POOL_WIDTH + Q_LORA + KV_LORA
    k_a, k_b = _rope_pair(w_in[:, lat:lat + QK_ROPE])
    w_in_ext = jnp.concatenate([w_in[:, :lat], k_a, zpad(D), k_b, zpad(D)], axis=1).astype(BF16)

    qa_cols, qb_cols = [], []
    per_head = QK_NOPE + QK_ROPE
    for hd in range(N_HEADS):
        w_h = w_uq[:, hd * per_head:(hd + 1) * per_head]
        r_a, r_b = _rope_pair(w_h[:, QK_NOPE:])
        qa_cols += [w_h[:, :QK_NOPE], r_a, zpad(Q_LORA)]
        qb_cols += [r_b, zpad(Q_LORA)]
    w_uqa = jnp.concatenate(qa_cols, axis=1).astype(BF16)
    w_uqb = jnp.concatenate(qb_cols, axis=1).astype(BF16)

    wpool_bd = jax.scipy.linalg.block_diag(*[w_pool[g] for g in range(len(POOL_WINDOWS))]).astype(BF16)
    wr_hi, wr_lo = _split_hi_lo(w_router.T)
    return dict(w_in_ext=w_in_ext, w_uqa=w_uqa, w_uqb=w_uqb, w_ukv=w_ukv.astype(BF16), wpool_bd=wpool_bd,
                wout_p=w_out[:POOL_WIDTH].astype(BF16), wout_a=w_out[POOL_WIDTH:].astype(BF16),
                wr_hi=wr_hi, wr_lo=wr_lo)


def kernel(x, c, positions, w_ada, b_ada, w_in, q_norm_g, kv_norm_g, w_uq, w_ukv, w_pool, pool_scale,
           w_out, ln1_g, ln1_b, w_router, router_bias, w_exp_gate, w_exp_up, w_exp_down,
           w_sh_gate, w_sh_up, w_sh_down, ln2_g, ln2_b):
    B, S, D = x.shape
    T = B * S
    L = w_in.shape[0]
    row = lambda v: v.reshape(1, -1)

    mod_all = _ada_mod(c, w_ada, b_ada).reshape(L, B, 6, D)
    cs = _rope_table(positions)
    for l in range(L):
        p = _prep_layer(w_in[l], w_uq[l], w_ukv[l], w_pool[l], w_out[l], w_router[l])
        mod = mod_all[l]
        u, q, k, v = _in_proj(x, mod, cs, p["w_in_ext"], row(q_norm_g[l]), row(kv_norm_g[l]),
                              p["w_uqa"], p["w_uqb"], p["w_ukv"])
        attn = _attention(q, k, v)
        x1, h2p, lt = _post_attn(x, u, attn, mod, p["wpool_bd"], row(pool_scale[l]), p["wout_p"], p["wout_a"],
                                row(ln1_g[l]), row(ln1_b[l]), p["wr_hi"], p["wr_lo"])
        idx_k, gate_k = _route(lt.reshape(N_EXPERTS, T // LANES, LANES), router_bias[l])
        idx_kt = idx_k.reshape(TOP_K, T)
        gates = gate_k.reshape(TOP_K, T).T
        rank_kt, counts = _rank(idx_kt)
        n_blocks = T * TOP_K // DISPATCH_BLOCK + N_EXPERTS
        pad_start, blk_expert, n_real = _block_plan(counts[:, 0], n_blocks)
        start = jnp.broadcast_to(pad_start.astype(F32)[:, None], (N_EXPERTS, LANES))
        dest_kt = _dest(idx_kt, rank_kt, start)
        h2p = h2p.reshape(T, HALF)
        xs = _sc_dispatch(h2p, dest_kt, n_blocks * DISPATCH_BLOCK)
        ys = _expert_mm(blk_expert, n_real, xs, w_exp_gate[l], w_exp_up[l], w_exp_down[l])
        yk = _sc_gather(ys, dest_kt.reshape(-1)).reshape(TOP_K, T, HALF)
        x = _combine(h2p, x1.reshape(T, D), yk, gates, mod,
                     w_sh_gate[l].astype(BF16), w_sh_up[l].astype(BF16), w_sh_down[l].astype(BF16),
                     row(ln2_g[l]), row(ln2_b[l]), S).reshape(B, S, D)
    return x
```

```python
import functools

import jax
import jax.numpy as jnp
from jax import lax
from jax.experimental import pallas as pl
from jax.experimental.pallas import tpu as pltpu
from jax.experimental.pallas import tpu_sc as plsc

F32 = jnp.float32
BF16 = jnp.bfloat16

D_MODEL = 1024
DEPTH = 4
POOL_WIDTH = 256
POOL_WINDOWS = (2, 4, 8, 16)
POOL_GROUP = 64
POOL_HALO = 16
QK_NOPE = 128
QK_ROPE = 64
V_HEAD = 128
N_HEADS = 6
Q_LORA = 384
KV_LORA = 256
ROPE_THETA = 10000.0
N_EXPERTS = 64
TOP_K = 8
N_GROUPS = 8
GROUP_SIZE = N_EXPERTS // N_GROUPS
TOPK_GROUPS = 4
D_EXPERT = 256
ROUTED_SCALE = 2.5
DEEPNORM_ALPHA = (2.0 * DEPTH) ** 0.25
LN_EPS = 1e-5
RMS_EPS = 1e-6
ATTN_SCALE = (QK_NOPE + QK_ROPE) ** -0.5
LOG2_E = 1.4426950408889634
Q_SCALE = ATTN_SCALE * LOG2_E
LANES = 128
QK_PAD = 2 * LANES
MASK_VALUE = -1e30

TM_PROJ = 512
TQ = 512
TK = 512
ROUTE_ROWS = 8
DISPATCH_BLOCK = 512
TM_COMBINE = 512
SC_CORES = 2
SC_SUBCORES = 16
SC_WINDOW = 64
N_STREAMS = 2
HALF = D_MODEL // 2
VMEM_LIMIT = 48 * 1024 * 1024


def _cparams(sem):
    return pltpu.CompilerParams(dimension_semantics=sem, vmem_limit_bytes=VMEM_LIMIT)


def _split_hi_lo(a):
    hi = a.astype(BF16)
    lo = (a - hi.astype(F32)).astype(BF16)
    return hi, lo


def _pack_bf16_pairs(a):
    bits = lax.bitcast_convert_type(a.astype(F32), jnp.uint32)
    half = a.shape[1] // 2
    word = (bits[:, :half] >> 16) | (bits[:, half:] & jnp.uint32(0xFFFF0000))
    return lax.bitcast_convert_type(word, jnp.int32)


def _unpack_bf16_pairs(w):
    bits = lax.bitcast_convert_type(w, jnp.uint32)
    lo = lax.bitcast_convert_type(bits << 16, F32)
    hi = lax.bitcast_convert_type(bits & jnp.uint32(0xFFFF0000), F32)
    return lo, hi


def _ada_kernel(c_ref, w_ref, b_ref, o_ref):
    c = c_ref[...]
    cond = c * jax.nn.sigmoid(c)
    c_hi, c_lo = _split_hi_lo(cond)
    w_hi, w_lo = _split_hi_lo(w_ref[0])
    acc = jnp.dot(c_hi, w_hi, preferred_element_type=F32)
    acc += jnp.dot(c_lo, w_hi, preferred_element_type=F32)
    acc += jnp.dot(c_hi, w_lo, preferred_element_type=F32)
    o_ref[0] = acc + b_ref[0]


def _ada_mod(c, w_ada, b_ada):
    L, D, N = w_ada.shape
    B = c.shape[0]
    tn = 1536
    return pl.pallas_call(
        _ada_kernel,
        grid=(L, N // tn),
        in_specs=[
            pl.BlockSpec((B, D), lambda l, j: (0, 0)),
            pl.BlockSpec((1, D, tn), lambda l, j: (l, 0, j)),
            pl.BlockSpec((1, 1, tn), lambda l, j: (l, 0, j)),
        ],
        out_specs=pl.BlockSpec((1, B, tn), lambda l, j: (l, 0, j)),
        out_shape=jax.ShapeDtypeStruct((L, B, N), F32),
        compiler_params=_cparams(("arbitrary", "arbitrary")),
        name="ada_mod",
    )(c, w_ada, b_ada.reshape(L, 1, N))


def _rope_table_kernel(pos_ref, freq_ref, o_ref):
    ang = pos_ref[0].astype(F32) * freq_ref[...]
    o_ref[0, :, 0:LANES] = jnp.cos(ang)
    o_ref[0, :, LANES:2 * LANES] = jnp.sin(ang)


def _rope_table(positions):
    B, S = positions.shape
    half = QK_ROPE // 2
    inv_freq = ROPE_THETA ** (-jnp.arange(half, dtype=F32) / half)
    freq = jnp.tile(inv_freq, LANES // half).reshape(1, LANES)
    return pl.pallas_call(
        _rope_table_kernel,
        grid=(B,),
        in_specs=[
            pl.BlockSpec((1, S, 1), lambda b: (b, 0, 0)),
            pl.BlockSpec((1, LANES), lambda b: (0, 0)),
        ],
        out_specs=pl.BlockSpec((1, S, 2 * LANES), lambda b: (b, 0, 0)),
        out_shape=jax.ShapeDtypeStruct((B, S, 2 * LANES), F32),
        compiler_params=_cparams(("arbitrary",)),
        name="rope_table",
    )(positions.reshape(B, S, 1), freq)


def _rms(x, g):
    return x * lax.rsqrt(jnp.mean(x * x, axis=-1, keepdims=True) + RMS_EPS) * g


def _in_proj_kernel(x_ref, mod_ref, cs_ref, w_in_ref, qg_ref, kvg_ref, w_uqa_ref, w_uqb_ref, w_ukv_ref,
                    u_ref, q_ref, k_ref, v_ref):
    x = x_ref[0]
    sh1 = mod_ref[0, 0:1, :]
    sc1 = mod_ref[0, 1:2, :]
    h = (x * (1.0 + sc1) + sh1).astype(BF16)
    proj = jnp.dot(h, w_in_ref[...], preferred_element_type=F32)
    u_ref[0] = proj[:, 0:POOL_WIDTH]
    o = POOL_WIDTH
    q_lat = proj[:, o:o + Q_LORA]
    o += Q_LORA
    kv_lat = proj[:, o:o + KV_LORA]
    o += KV_LORA
    k_a = proj[:, o:o + LANES]
    k_b = proj[:, o + LANES:o + 2 * LANES]
    cosv = cs_ref[0, :, 0:LANES]
    sinv = cs_ref[0, :, LANES:2 * LANES]
    k_rot = (k_a * cosv + k_b * sinv).astype(BF16)

    qn = _rms(q_lat, qg_ref[...]).astype(BF16)
    q_a = jnp.dot(qn, w_uqa_ref[...], preferred_element_type=F32)
    q_b = jnp.dot(qn, w_uqb_ref[...], preferred_element_type=F32)
    kvn = _rms(kv_lat, kvg_ref[...]).astype(BF16)
    kv = jnp.dot(kvn, w_ukv_ref[...], preferred_element_type=F32)
    for hd in range(N_HEADS):
        b0 = hd * QK_PAD
        q_ref[0, hd, :, 0:LANES] = (q_a[:, b0:b0 + LANES] * Q_SCALE).astype(BF16)
        q_rot = q_a[:, b0 + LANES:b0 + QK_PAD] * cosv + q_b[:, hd * LANES:(hd + 1) * LANES] * sinv
        q_ref[0, hd, :, LANES:QK_PAD] = (q_rot * Q_SCALE).astype(BF16)
        k_ref[0, hd, :, 0:LANES] = kv[:, b0:b0 + QK_NOPE].astype(BF16)
        k_ref[0, hd, :, LANES:QK_PAD] = k_rot
        v_ref[0, hd] = kv[:, b0 + QK_NOPE:b0 + QK_NOPE + V_HEAD].astype(BF16)


def _in_proj(x, mod, cs, w_in_ext, qg, kvg, w_uqa, w_uqb, w_ukv_bf):
    B, S, D = x.shape
    tm = TM_PROJ
    const = lambda b, i: (0, 0)
    return pl.pallas_call(
        _in_proj_kernel,
        grid=(B, S // tm),
        in_specs=[
            pl.BlockSpec((1, tm, D), lambda b, i: (b, i, 0)),
            pl.BlockSpec((1, 6, D), lambda b, i: (b, 0, 0)),
            pl.BlockSpec((1, tm, 2 * LANES), lambda b, i: (b, i, 0)),
            pl.BlockSpec(w_in_ext.shape, const),
            pl.BlockSpec(qg.shape, const),
            pl.BlockSpec(kvg.shape, const),
            pl.BlockSpec(w_uqa.shape, const),
            pl.BlockSpec(w_uqb.shape, const),
            pl.BlockSpec(w_ukv_bf.shape, const),
        ],
        out_specs=[
            pl.BlockSpec((1, tm, POOL_WIDTH), lambda b, i: (b, i, 0)),
            pl.BlockSpec((1, N_HEADS, tm, QK_PAD), lambda b, i: (b, 0, i, 0)),
            pl.BlockSpec((1, N_HEADS, tm, QK_PAD), lambda b, i: (b, 0, i, 0)),
            pl.BlockSpec((1, N_HEADS, tm, V_HEAD), lambda b, i: (b, 0, i, 0)),
        ],
        out_shape=[
            jax.ShapeDtypeStruct((B, S, POOL_WIDTH), F32),
            jax.ShapeDtypeStruct((B, N_HEADS, S, QK_PAD), BF16),
            jax.ShapeDtypeStruct((B, N_HEADS, S, QK_PAD), BF16),
            jax.ShapeDtypeStruct((B, N_HEADS, S, V_HEAD), BF16),
        ],
        compiler_params=_cparams(("arbitrary", "arbitrary")),
        name="in_proj",
    )(x, mod, cs, w_in_ext, qg, kvg, w_uqa, w_uqb, w_ukv_bf)


def _softmax_step(q, k, v, carry, mask):
    m, l, acc = carry
    s = lax.dot_general(q, k, (((1,), (1,)), ((), ())), preferred_element_type=F32)
    if mask is not None:
        s = jnp.where(mask, s, MASK_VALUE)
    m_new = jnp.maximum(m, jnp.max(s, axis=-1, keepdims=True))
    alpha = jnp.exp2(m - m_new)
    p = jnp.exp2(s - m_new)
    l_new = alpha * l + jnp.sum(p, axis=-1, keepdims=True)
    acc_new = alpha * acc + jnp.dot(p.astype(BF16), v, preferred_element_type=F32)
    return m_new, l_new, acc_new


def _attn_kernel(q_ref, k_ref, v_ref, o_ref):
    seq = q_ref.shape[2]
    row = lax.broadcasted_iota(jnp.int32, (TQ, TK), 0)
    col = lax.broadcasted_iota(jnp.int32, (TQ, TK), 1)
    diag = row >= col
    for i in range(seq // TQ):
        q = q_ref[0, 0, i * TQ:(i + 1) * TQ, :]
        carry = (jnp.full((TQ, 1), MASK_VALUE, F32), jnp.zeros((TQ, 1), F32), jnp.zeros((TQ, V_HEAD), F32))
        for j in range(i + 1):
            k = k_ref[0, 0, j * TK:(j + 1) * TK, :]
            v = v_ref[0, 0, j * TK:(j + 1) * TK, :]
            carry = _softmax_step(q, k, v, carry, diag if j == i else None)
        _, l, acc = carry
        o_ref[0, i * TQ:(i + 1) * TQ, :] = (acc / l).astype(BF16)


def _attention(q, k, v):
    B, H, S, _ = q.shape
    return pl.pallas_call(
        _attn_kernel,
        grid=(B, H),
        in_specs=[
            pl.BlockSpec((1, 1, S, QK_PAD), lambda b, h: (b, h, 0, 0)),
            pl.BlockSpec((1, 1, S, QK_PAD), lambda b, h: (b, h, 0, 0)),
            pl.BlockSpec((1, 1, S, V_HEAD), lambda b, h: (b, h, 0, 0)),
        ],
        out_specs=pl.BlockSpec((1, S, V_HEAD), lambda b, h: (b, 0, h)),
        out_shape=jax.ShapeDtypeStruct((B, S, H * V_HEAD), BF16),
        compiler_params=_cparams(("arbitrary", "arbitrary")),
        name="attention",
    )(q, k, v)


def _layer_norm(v, g, b):
    mu = jnp.mean(v, axis=-1, keepdims=True)
    d = v - mu
    var = jnp.mean(d * d, axis=-1, keepdims=True)
    return d * lax.rsqrt(var + LN_EPS) * g + b


def _post_kernel(x_ref, u_ref, halo_ref, attn_ref, mod_ref, wpool_ref, pscale_ref, wout_p_ref, wout_a_ref,
                 lng_ref, lnb_ref, wr_hi_ref, wr_lo_ref, x1_ref, h2_ref, lt_ref):
    i = pl.program_id(1)
    tm = u_ref.shape[1]
    u = u_ref[0]
    halo = jnp.where(i > 0, halo_ref[0], 0.0)
    ext = jnp.concatenate([halo, u], axis=0)
    s2 = ext + pltpu.roll(ext, 1, 0)
    s4 = s2 + pltpu.roll(s2, 2, 0)
    s8 = s4 + pltpu.roll(s4, 4, 0)
    s16 = s8 + pltpu.roll(s8, 8, 0)
    lane = lax.broadcasted_iota(jnp.int32, (1, POOL_WIDTH), 1)
    win = jnp.where(lane < POOL_GROUP, s2,
                    jnp.where(lane < 2 * POOL_GROUP, s4, jnp.where(lane < 3 * POOL_GROUP, s8, s16)))
    win = win[POOL_HALO:, :]
    width = jnp.where(lane < POOL_GROUP, POOL_WINDOWS[0],
                      jnp.where(lane < 2 * POOL_GROUP, POOL_WINDOWS[1],
                                jnp.where(lane < 3 * POOL_GROUP, POOL_WINDOWS[2], POOL_WINDOWS[3])))
    t = i * tm + lax.broadcasted_iota(jnp.int32, (tm, 1), 0)
    count = jnp.minimum(t + 1, width).astype(F32)
    token_mix = win / count - u
    pooled = jnp.dot(token_mix.astype(BF16), wpool_ref[...], preferred_element_type=F32) * pscale_ref[...]
    mixed = jnp.dot(pooled.astype(BF16), wout_p_ref[...], preferred_element_type=F32)
    mixed += jnp.dot(attn_ref[0], wout_a_ref[...], preferred_element_type=F32)

    g1 = mod_ref[0, 2:3, :]
    sh2 = mod_ref[0, 3:4, :]
    sc2 = mod_ref[0, 4:5, :]
    x1 = _layer_norm(DEEPNORM_ALPHA * x_ref[0] + g1 * mixed, lng_ref[...], lnb_ref[...])
    x1_ref[0] = x1
    h2 = x1 * (1.0 + sc2) + sh2
    h_hi, h_lo = _split_hi_lo(h2)
    h2_ref[0] = _pack_bf16_pairs(h_hi)
    nt = (((1,), (1,)), ((), ()))
    lt = lax.dot_general(wr_hi_ref[...], h_hi, nt, preferred_element_type=F32)
    lt += lax.dot_general(wr_hi_ref[...], h_lo, nt, preferred_element_type=F32)
    lt += lax.dot_general(wr_lo_ref[...], h_hi, nt, preferred_element_type=F32)
    lt_ref[...] = lt


def _post_attn(x, u, attn, mod, wpool_bd, pscale, wout_p, wout_a, lng, lnb, wr_hi, wr_lo):
    B, S, D = x.shape
    tm = TM_PROJ
    nt = S // tm
    const = lambda b, i: (0, 0)
    halo_blocks = tm // POOL_HALO
    return pl.pallas_call(
        _post_kernel,
        grid=(B, nt),
        in_specs=[
            pl.BlockSpec((1, tm, D), lambda b, i: (b, i, 0)),
            pl.BlockSpec((1, tm, POOL_WIDTH), lambda b, i: (b, i, 0)),
            pl.BlockSpec((1, POOL_HALO, POOL_WIDTH), lambda b, i: (b, jnp.maximum(i * halo_blocks - 1, 0), 0)),
            pl.BlockSpec((1, tm, N_HEADS * V_HEAD), lambda b, i: (b, i, 0)),
            pl.BlockSpec((1, 6, D), lambda b, i: (b, 0, 0)),
            pl.BlockSpec(wpool_bd.shape, const),
            pl.BlockSpec(pscale.shape, const),
            pl.BlockSpec(wout_p.shape, const),
            pl.BlockSpec(wout_a.shape, const),
            pl.BlockSpec(lng.shape, const),
            pl.BlockSpec(lnb.shape, const),
            pl.BlockSpec(wr_hi.shape, const),
            pl.BlockSpec(wr_lo.shape, const),
        ],
        out_specs=[
            pl.BlockSpec((1, tm, D), lambda b, i: (b, i, 0)),
            pl.BlockSpec((1, tm, HALF), lambda b, i: (b, i, 0)),
            pl.BlockSpec((N_EXPERTS, tm), lambda b, i: (0, b * nt + i)),
        ],
        out_shape=[
            jax.ShapeDtypeStruct((B, S, D), F32),
            jax.ShapeDtypeStruct((B, S, HALF), jnp.int32),
            jax.ShapeDtypeStruct((N_EXPERTS, B * S), F32),
        ],
        compiler_params=_cparams(("arbitrary", "arbitrary")),
        name="post_attn",
    )(x, u, u, attn, mod, wpool_bd, pscale, wout_p, wout_a, lng, lnb, wr_hi, wr_lo)


def _select_first_max(vals, n_rounds, payload=None):
    work = list(vals)
    sel = [None] * len(vals)
    rounds = []
    for _ in range(n_rounds):
        m = functools.reduce(jnp.maximum, work)
        taken = None
        win_idx = jnp.zeros(m.shape, jnp.int32)
        win_val = jnp.zeros(m.shape, F32)
        for e in range(len(work)):
            hit = work[e] == m
            first = hit if taken is None else jnp.logical_and(hit, jnp.logical_not(taken))
            taken = hit if taken is None else jnp.logical_or(taken, hit)
            sel[e] = first if sel[e] is None else jnp.logical_or(sel[e], first)
            work[e] = jnp.where(first, -jnp.inf, work[e])
            if payload is not None:
                win_idx = jnp.where(first, e, win_idx)
                win_val = jnp.where(first, payload[e], win_val)
        rounds.append((win_idx, win_val))
    return sel, rounds


def _route_kernel(bias_ref, lt_ref, idx_ref, gate_ref):
    scores = [jax.nn.sigmoid(lt_ref[e]) for e in range(N_EXPERTS)]
    choice = [scores[e] + bias_ref[e] for e in range(N_EXPERTS)]
    group_score = []
    for g in range(N_GROUPS):
        vals = choice[g * GROUP_SIZE:(g + 1) * GROUP_SIZE]
        sel2, _ = _select_first_max(vals, 2)
        group_score.append(functools.reduce(
            jnp.add, [jnp.where(sel2[j], vals[j], 0.0) for j in range(GROUP_SIZE)]))
    group_sel, _ = _select_first_max(group_score, TOPK_GROUPS)
    masked = [jnp.where(group_sel[e // GROUP_SIZE], choice[e], -jnp.inf) for e in range(N_EXPERTS)]
    _, rounds = _select_first_max(masked, TOP_K, payload=scores)
    total = functools.reduce(jnp.add, [w for _, w in rounds])
    for k, (e_k, w_k) in enumerate(rounds):
        idx_ref[k] = e_k
        gate_ref[k] = w_k / total * ROUTED_SCALE


def _route(lt3, bias):
    E, R, _ = lt3.shape
    out_spec = pl.BlockSpec((TOP_K, ROUTE_ROWS, LANES), lambda r: (0, r, 0))
    return pl.pallas_call(
        _route_kernel,
        grid=(R // ROUTE_ROWS,),
        in_specs=[
            pl.BlockSpec(memory_space=pltpu.SMEM),
            pl.BlockSpec((E, ROUTE_ROWS, LANES), lambda r: (0, r, 0)),
        ],
        out_specs=[out_spec, out_spec],
        out_shape=[jax.ShapeDtypeStruct((TOP_K, R, LANES), jnp.int32),
                   jax.ShapeDtypeStruct((TOP_K, R, LANES), F32)],
        compiler_params=_cparams(("arbitrary",)),
        name="route",
    )(bias, lt3)


def _sc_mesh():
    return plsc.VectorSubcoreMesh(core_axis_name="c", subcore_axis_name="s",
                                  num_cores=SC_CORES, num_subcores=SC_SUBCORES)


def _sc_worker_id():
    return lax.axis_index("s") * SC_CORES + lax.axis_index("c")


def _sc_gather(table, idx):
    n = idx.shape[0]
    width = table.shape[1]
    workers = SC_CORES * SC_SUBCORES
    per_worker = n // workers
    n_win = per_worker // SC_WINDOW
    assert n == workers * n_win * SC_WINDOW

    @functools.partial(
        pl.kernel, mesh=_sc_mesh(),
        out_type=jax.ShapeDtypeStruct((n, width), table.dtype),
        scratch_types=[
            pltpu.VMEM((n_win, SC_WINDOW), jnp.int32),
            pltpu.VMEM((SC_WINDOW, width), table.dtype),
            pltpu.SemaphoreType.DMA,
        ],
    )
    def gather(table_hbm, idx_hbm, out_hbm, idx_v, rows_v, sem):
        wid = _sc_worker_id()
        pltpu.sync_copy(idx_hbm.at[wid], idx_v)

        @pl.loop(0, n_win)
        def _(j):
            pltpu.async_copy(table_hbm.at[idx_v.at[j]], rows_v, sem).wait()
            pltpu.sync_copy(rows_v, out_hbm.at[pl.ds(wid * per_worker + j * SC_WINDOW, SC_WINDOW)])

    return gather(table, idx.reshape(workers, n_win, SC_WINDOW))


def _sc_dispatch(rows, dest_kt, n_slots):
    T, width = rows.shape
    K = dest_kt.shape[0]
    workers = SC_CORES * SC_SUBCORES
    per_worker = T // workers
    n_win = per_worker // SC_WINDOW
    assert T == workers * n_win * SC_WINDOW
    dest_w = dest_kt.reshape(K, workers, n_win, SC_WINDOW).transpose(1, 2, 0, 3)
    dest_w = dest_w.reshape(workers, n_win * K, SC_WINDOW)

    @functools.partial(
        pl.kernel, mesh=_sc_mesh(),
        out_type=jax.ShapeDtypeStruct((n_slots, width), rows.dtype),
        scratch_types=[
            pltpu.VMEM((n_win * K, SC_WINDOW), jnp.int32),
            pltpu.VMEM((SC_WINDOW, width), rows.dtype),
        ],
    )
    def dispatch(rows_hbm, dest_hbm, out_hbm, idx_v, rows_v):
        wid = _sc_worker_id()
        pltpu.sync_copy(dest_hbm.at[wid], idx_v)

        @pl.loop(0, n_win)
        def _(j):
            pltpu.sync_copy(rows_hbm.at[pl.ds(wid * per_worker + j * SC_WINDOW, SC_WINDOW)], rows_v)
            for k in range(K):
                pltpu.sync_copy(rows_v, out_hbm.at[idx_v.at[j * K + k]])

    return dispatch(rows, dest_w)


def _swiglu(h_lo, h_hi, wg_ref, wu_ref):
    def proj(w_ref):
        return (jnp.dot(h_lo, w_ref[:HALF, :], preferred_element_type=F32)
                + jnp.dot(h_hi, w_ref[HALF:, :], preferred_element_type=F32))
    gate = proj(wg_ref)
    up = proj(wu_ref)
    return (gate * jax.nn.sigmoid(gate) * up).astype(BF16)


def _expert_kernel(blk_expert_ref, n_real_ref, xs_ref, wg_ref, wu_ref, wd_ref, ys_ref, wg_bf, wu_bf, wd_bf):
    nb = pl.program_id(0)
    last = n_real_ref[0] - 1
    new_expert = jnp.logical_or(nb == 0, blk_expert_ref[jnp.minimum(nb, last)]
                                != blk_expert_ref[jnp.minimum(jnp.maximum(nb - 1, 0), last)])

    @pl.when(new_expert)
    def _():
        wg_bf[...] = wg_ref[0, 0].astype(BF16)
        wu_bf[...] = wu_ref[0, 0].astype(BF16)
        wd_bf[...] = wd_ref[0, 0].astype(BF16)

    @pl.when(nb <= last)
    def _():
        lo, hi = _unpack_bf16_pairs(xs_ref[...])
        a = _swiglu(lo.astype(BF16), hi.astype(BF16), wg_bf, wu_bf)
        y = jnp.dot(a, wd_bf[...], preferred_element_type=F32)
        ys_ref[...] = _pack_bf16_pairs(y.astype(BF16))

    @pl.when(nb > last)
    def _():
        ys_ref[...] = jnp.zeros(ys_ref.shape, ys_ref.dtype)


def _expert_mm(blk_expert, n_real, xs, wg, wu, wd, layer):
    P = xs.shape[0]
    nblk = P // DISPATCH_BLOCK
    D = wg.shape[2]
    row_map = lambda nb, be, nr: (jnp.minimum(nb, nr[0] - 1), 0)
    w_map = lambda nb, be, nr: (layer, be[jnp.minimum(nb, nr[0] - 1)], 0, 0)
    grid_spec = pltpu.PrefetchScalarGridSpec(
        num_scalar_prefetch=2,
        grid=(nblk,),
        in_specs=[
            pl.BlockSpec((DISPATCH_BLOCK, HALF), row_map),
            pl.BlockSpec((1, 1, D, D_EXPERT), w_map),
            pl.BlockSpec((1, 1, D, D_EXPERT), w_map),
            pl.BlockSpec((1, 1, D_EXPERT, D), w_map),
        ],
        out_specs=pl.BlockSpec((DISPATCH_BLOCK, HALF), lambda nb, be, nr: (nb, 0)),
        scratch_shapes=[pltpu.VMEM((D, D_EXPERT), BF16), pltpu.VMEM((D, D_EXPERT), BF16),
                        pltpu.VMEM((D_EXPERT, D), BF16)],
    )
    return pl.pallas_call(
        _expert_kernel,
        grid_spec=grid_spec,
        out_shape=jax.ShapeDtypeStruct((P, HALF), jnp.int32),
        compiler_params=_cparams(("arbitrary",)),
        name="expert_mm",
    )(blk_expert, n_real, xs, wg, wu, wd)


def _rank_kernel(idx_ref, rank_ref, cnt_ref, tri_ref, carry_ref):
    i = pl.program_id(0)
    tm = idx_ref.shape[1]

    @pl.when(i == 0)
    def _():
        r = lax.broadcasted_iota(jnp.int32, (tm, tm), 0)
        c = lax.broadcasted_iota(jnp.int32, (tm, tm), 1)
        tri_ref[...] = (r < c).astype(BF16)
        carry_ref[...] = jnp.zeros(carry_ref.shape, F32)

    e_iota = lax.broadcasted_iota(jnp.int32, (N_EXPERTS, tm), 0)
    hits = [e_iota == idx_ref[k:k + 1, :] for k in range(TOP_K)]
    mask = functools.reduce(jnp.logical_or, hits).astype(F32)
    rank = jnp.dot(mask.astype(BF16), tri_ref[...], preferred_element_type=F32) + carry_ref[:, 0:1]
    for k in range(TOP_K):
        rank_ref[k:k + 1, :] = jnp.sum(jnp.where(hits[k], rank, 0.0), axis=0, keepdims=True).astype(jnp.int32)
    carry_ref[...] = carry_ref[...] + jnp.sum(mask, axis=1, keepdims=True)
    cnt_ref[...] = carry_ref[...]


def _rank(idx_kt):
    K, T = idx_kt.shape
    tm = 1024
    return pl.pallas_call(
        _rank_kernel,
        grid=(T // tm,),
        in_specs=[pl.BlockSpec((K, tm), lambda i: (0, i))],
        out_specs=[pl.BlockSpec((K, tm), lambda i: (0, i)),
                   pl.BlockSpec((N_EXPERTS, LANES), lambda i: (0, 0))],
        out_shape=[jax.ShapeDtypeStruct((K, T), jnp.int32),
                   jax.ShapeDtypeStruct((N_EXPERTS, LANES), F32)],
        scratch_shapes=[pltpu.VMEM((tm, tm), BF16), pltpu.VMEM((N_EXPERTS, LANES), F32)],
        compiler_params=_cparams(("arbitrary",)),
        name="rank",
    )(idx_kt)


def _dest_kernel(idx_ref, rank_ref, start_ref, dest_ref):
    tm = idx_ref.shape[1]
    e_iota = lax.broadcasted_iota(jnp.int32, (N_EXPERTS, tm), 0)
    start = start_ref[:, 0:1]
    for k in range(TOP_K):
        base = jnp.sum(jnp.where(e_iota == idx_ref[k:k + 1, :], start, 0.0), axis=0, keepdims=True)
        dest_ref[k:k + 1, :] = rank_ref[k:k + 1, :] + base.astype(jnp.int32)


def _dest(idx_kt, rank_kt, start):
    K, T = idx_kt.shape
    tm = 1024
    tok_spec = pl.BlockSpec((K, tm), lambda i: (0, i))
    return pl.pallas_call(
        _dest_kernel,
        grid=(T // tm,),
        in_specs=[tok_spec, tok_spec, pl.BlockSpec((N_EXPERTS, LANES), lambda i: (0, 0))],
        out_specs=tok_spec,
        out_shape=jax.ShapeDtypeStruct((K, T), jnp.int32),
        compiler_params=_cparams(("arbitrary",)),
        name="dest",
    )(idx_kt, rank_kt, start)


def _block_plan(counts, n_blocks):
    blk = DISPATCH_BLOCK
    counts = counts.astype(jnp.int32)
    padded = (counts + blk - 1) // blk * blk
    pad_end = jnp.cumsum(padded)
    pad_start = pad_end - padded
    block_first = jnp.arange(n_blocks, dtype=jnp.int32) * blk
    blk_expert = jnp.minimum(jnp.sum(pad_end[None, :] <= block_first[:, None], axis=1), N_EXPERTS - 1)
    n_real = (pad_end[-1] // blk).reshape(1)
    return pad_start, blk_expert.astype(jnp.int32), n_real.astype(jnp.int32)


def _combine_kernel(h_ref, x1_ref, yk_ref, g_ref, mod_ref, wsg_ref, wsu_ref, wsd_ref, lng_ref, lnb_ref, o_ref):
    lo, hi = _unpack_bf16_pairs(h_ref[...])
    a = _swiglu(lo.astype(BF16), hi.astype(BF16), wsg_ref, wsu_ref)
    ffn = jnp.dot(a, wsd_ref[...], preferred_element_type=F32)
    r_lo = jnp.zeros((h_ref.shape[0], HALF), F32)
    r_hi = jnp.zeros((h_ref.shape[0], HALF), F32)
    for k in range(TOP_K):
        y_lo, y_hi = _unpack_bf16_pairs(yk_ref[k])
        g = g_ref[:, k:k + 1]
        r_lo += g * y_lo
        r_hi += g * y_hi
    ffn += jnp.concatenate([r_lo, r_hi], axis=1)
    g2 = mod_ref[0, 5:6, :]
    o_ref[...] = _layer_norm(DEEPNORM_ALPHA * x1_ref[...] + g2 * ffn, lng_ref[...], lnb_ref[...])


def _combine(h2p, x1, yk, gates, mod, wsg, wsu, wsd, lng, lnb, seq):
    T, D = x1.shape
    tm = TM_COMBINE
    per_seq = seq // tm
    const = lambda i: (0, 0)
    return pl.pallas_call(
        _combine_kernel,
        grid=(T // tm,),
        in_specs=[
            pl.BlockSpec((tm, HALF), lambda i: (i, 0)),
            pl.BlockSpec((tm, D), lambda i: (i, 0)),
            pl.BlockSpec((TOP_K, tm, HALF), lambda i: (0, i, 0)),
            pl.BlockSpec((tm, TOP_K), lambda i: (i, 0)),
            pl.BlockSpec((1, 6, D), lambda i: (i // per_seq, 0, 0)),
            pl.BlockSpec(wsg.shape, const),
            pl.BlockSpec(wsu.shape, const),
            pl.BlockSpec(wsd.shape, const),
            pl.BlockSpec(lng.shape, const),
            pl.BlockSpec(lnb.shape, const),
        ],
        out_specs=pl.BlockSpec((tm, D), lambda i: (i, 0)),
        out_shape=jax.ShapeDtypeStruct((T, D), F32),
        compiler_params=_cparams(("arbitrary",)),
        name="combine",
    )(h2p, x1, yk, gates, mod, wsg, wsu, wsd, lng, lnb)


def _rope_pair(w):
    half = QK_ROPE // 2
    return w, jnp.concatenate([-w[:, half:], w[:, :half]], axis=1)


def _prep_layer(w_in, w_uq, w_ukv, w_pool, w_out, w_router):
    D = w_in.shape[0]
    zpad = lambda r: jnp.zeros((r, LANES - QK_ROPE), F32)
    lat = POOL_WIDTH + Q_LORA + KV_LORA
    k_a, k_b = _rope_pair(w_in[:, lat:lat + QK_ROPE])
    w_in_ext = jnp.concatenate([w_in[:, :lat], k_a, zpad(D), k_b, zpad(D)], axis=1).astype(BF16)

    qa_cols, qb_cols = [], []
    per_head = QK_NOPE + QK_ROPE
    for hd in range(N_HEADS):
        w_h = w_uq[:, hd * per_head:(hd + 1) * per_head]
        r_a, r_b = _rope_pair(w_h[:, QK_NOPE:])
        qa_cols += [w_h[:, :QK_NOPE], r_a, zpad(Q_LORA)]
        qb_cols += [r_b, zpad(Q_LORA)]
    w_uqa = jnp.concatenate(qa_cols, axis=1).astype(BF16)
    w_uqb = jnp.concatenate(qb_cols, axis=1).astype(BF16)

    wpool_bd = jax.scipy.linalg.block_diag(*[w_pool[g] for g in range(len(POOL_WINDOWS))]).astype(BF16)
    wr_hi, wr_lo = _split_hi_lo(w_router.T)
    return dict(w_in_ext=w_in_ext, w_uqa=w_uqa, w_uqb=w_uqb, w_ukv=w_ukv.astype(BF16), wpool_bd=wpool_bd,
                wout_p=w_out[:POOL_WIDTH].astype(BF16), wout_a=w_out[POOL_WIDTH:].astype(BF16),
                wr_hi=wr_hi, wr_lo=wr_lo)


def kernel(x, c, positions, w_ada, b_ada, w_in, q_norm_g, kv_norm_g, w_uq, w_ukv, w_pool, pool_scale,
           w_out, ln1_g, ln1_b, w_router, router_bias, w_exp_gate, w_exp_up, w_exp_down,
           w_sh_gate, w_sh_up, w_sh_down, ln2_g, ln2_b):
    B, S, D = x.shape
    L = w_in.shape[0]
    row = lambda v: v.reshape(1, -1)

    mod_all = _ada_mod(c, w_ada, b_ada).reshape(L, B, 6, D)
    cs_all = _rope_table(positions)
    bs = B // N_STREAMS
    T = bs * S
    n_blocks = T * TOP_K // DISPATCH_BLOCK + N_EXPERTS
    xs_streams = [x[s * bs:(s + 1) * bs] for s in range(N_STREAMS)]
    for l in range(L):
        p = _prep_layer(w_in[l], w_uq[l], w_ukv[l], w_pool[l], w_out[l], w_router[l])
        shared = (w_sh_gate[l].astype(BF16), w_sh_up[l].astype(BF16), w_sh_down[l].astype(BF16))
        for s in range(N_STREAMS):
            xl = xs_streams[s]
            mod = mod_all[l, s * bs:(s + 1) * bs]
            cs = cs_all[s * bs:(s + 1) * bs]
            u, q, k, v = _in_proj(xl, mod, cs, p["w_in_ext"], row(q_norm_g[l]), row(kv_norm_g[l]),
                                  p["w_uqa"], p["w_uqb"], p["w_ukv"])
            attn = _attention(q, k, v)
            x1, h2p, lt = _post_attn(xl, u, attn, mod, p["wpool_bd"], row(pool_scale[l]), p["wout_p"],
                                     p["wout_a"], row(ln1_g[l]), row(ln1_b[l]), p["wr_hi"], p["wr_lo"])
            idx_k, gate_k = _route(lt.reshape(N_EXPERTS, T // LANES, LANES), router_bias[l])
            idx_kt = idx_k.reshape(TOP_K, T)
            gates = gate_k.reshape(TOP_K, T).T
            rank_kt, counts = _rank(idx_kt)
            pad_start, blk_expert, n_real = _block_plan(counts[:, 0], n_blocks)
            start = jnp.broadcast_to(pad_start.astype(F32)[:, None], (N_EXPERTS, LANES))
            dest_kt = _dest(idx_kt, rank_kt, start)
            h2p = h2p.reshape(T, HALF)
            rows = _sc_dispatch(h2p, dest_kt, n_blocks * DISPATCH_BLOCK)
            ys = _expert_mm(blk_expert, n_real, rows, w_exp_gate, w_exp_up, w_exp_down, l)
            yk = _sc_gather(ys, dest_kt.reshape(-1)).reshape(TOP_K, T, HALF)
            xs_streams[s] = _combine(h2p, x1.reshape(T, D), yk, gates, mod, *shared,
                                     row(ln2_g[l]), row(ln2_b[l]), S).reshape(bs, S, D)
    return jnp.concatenate(xs_streams, axis=0)
```

```python
import functools

import jax
import jax.numpy as jnp
from jax import lax
from jax.experimental import pallas as pl
from jax.experimental.pallas import tpu as pltpu
from jax.experimental.pallas import tpu_sc as plsc

F32 = jnp.float32
BF16 = jnp.bfloat16

D_MODEL = 1024
DEPTH = 4
POOL_WIDTH = 256
POOL_WINDOWS = (2, 4, 8, 16)
POOL_GROUP = 64
POOL_HALO = 16
QK_NOPE = 128
QK_ROPE = 64
V_HEAD = 128
N_HEADS = 6
Q_LORA = 384
KV_LORA = 256
ROPE_THETA = 10000.0
N_EXPERTS = 64
TOP_K = 8
N_GROUPS = 8
GROUP_SIZE = N_EXPERTS // N_GROUPS
TOPK_GROUPS = 4
D_EXPERT = 256
ROUTED_SCALE = 2.5
DEEPNORM_ALPHA = (2.0 * DEPTH) ** 0.25
LN_EPS = 1e-5
RMS_EPS = 1e-6
ATTN_SCALE = (QK_NOPE + QK_ROPE) ** -0.5
LOG2_E = 1.4426950408889634
Q_SCALE = ATTN_SCALE * LOG2_E
LANES = 128
QK_PAD = 2 * LANES
MASK_VALUE = -1e30

TM_PROJ = 512
TQ = 512
TK = 512
ROUTE_ROWS = 8
DISPATCH_BLOCK = 512
TM_COMBINE = 512
SC_CORES = 2
SC_SUBCORES = 16
SC_WINDOW = 64
N_STREAMS = 2
HALF = D_MODEL // 2
VMEM_LIMIT = 48 * 1024 * 1024


def _cparams(sem):
    return pltpu.CompilerParams(dimension_semantics=sem, vmem_limit_bytes=VMEM_LIMIT)


def _split_hi_lo(a):
    hi = a.astype(BF16)
    lo = (a - hi.astype(F32)).astype(BF16)
    return hi, lo


def _pack_bf16_pairs(a):
    bits = lax.bitcast_convert_type(a.astype(F32), jnp.uint32)
    half = a.shape[1] // 2
    word = (bits[:, :half] >> 16) | (bits[:, half:] & jnp.uint32(0xFFFF0000))
    return lax.bitcast_convert_type(word, jnp.int32)


def _unpack_bf16_pairs(w):
    bits = lax.bitcast_convert_type(w, jnp.uint32)
    lo = lax.bitcast_convert_type(bits << 16, F32)
    hi = lax.bitcast_convert_type(bits & jnp.uint32(0xFFFF0000), F32)
    return lo, hi


def _ada_kernel(c_ref, w_ref, b_ref, o_ref):
    c = c_ref[...]
    cond = c * jax.nn.sigmoid(c)
    c_hi, c_lo = _split_hi_lo(cond)
    w_hi, w_lo = _split_hi_lo(w_ref[0])
    acc = jnp.dot(c_hi, w_hi, preferred_element_type=F32)
    acc += jnp.dot(c_lo, w_hi, preferred_element_type=F32)
    acc += jnp.dot(c_hi, w_lo, preferred_element_type=F32)
    o_ref[0] = acc + b_ref[0]


def _ada_mod(c, w_ada, b_ada):
    L, D, N = w_ada.shape
    B = c.shape[0]
    tn = 1536
    return pl.pallas_call(
        _ada_kernel,
        grid=(L, N // tn),
        in_specs=[
            pl.BlockSpec((B, D), lambda l, j: (0, 0)),
            pl.BlockSpec((1, D, tn), lambda l, j: (l, 0, j)),
            pl.BlockSpec((1, 1, tn), lambda l, j: (l, 0, j)),
        ],
        out_specs=pl.BlockSpec((1, B, tn), lambda l, j: (l, 0, j)),
        out_shape=jax.ShapeDtypeStruct((L, B, N), F32),
        compiler_params=_cparams(("arbitrary", "arbitrary")),
        name="ada_mod",
    )(c, w_ada, b_ada.reshape(L, 1, N))


def _rope_table_kernel(pos_ref, freq_ref, o_ref):
    ang = pos_ref[0].astype(F32) * freq_ref[...]
    o_ref[0, :, 0:LANES] = jnp.cos(ang)
    o_ref[0, :, LANES:2 * LANES] = jnp.sin(ang)


def _rope_table(positions):
    B, S = positions.shape
    half = QK_ROPE // 2
    inv_freq = ROPE_THETA ** (-jnp.arange(half, dtype=F32) / half)
    freq = jnp.tile(inv_freq, LANES // half).reshape(1, LANES)
    return pl.pallas_call(
        _rope_table_kernel,
        grid=(B,),
        in_specs=[
            pl.BlockSpec((1, S, 1), lambda b: (b, 0, 0)),
            pl.BlockSpec((1, LANES), lambda b: (0, 0)),
        ],
        out_specs=pl.BlockSpec((1, S, 2 * LANES), lambda b: (b, 0, 0)),
        out_shape=jax.ShapeDtypeStruct((B, S, 2 * LANES), F32),
        compiler_params=_cparams(("arbitrary",)),
        name="rope_table",
    )(positions.reshape(B, S, 1), freq)


def _rms(x, g):
    return x * lax.rsqrt(jnp.mean(x * x, axis=-1, keepdims=True) + RMS_EPS) * g


def _in_proj_kernel(x_ref, mod_ref, cs_ref, w_in_ref, qg_ref, kvg_ref, w_uqa_ref, w_uqb_ref, w_ukv_ref,
                    u_ref, q_ref, k_ref, v_ref):
    x = x_ref[0]
    sh1 = mod_ref[0, 0:1, :]
    sc1 = mod_ref[0, 1:2, :]
    h = (x * (1.0 + sc1) + sh1).astype(BF16)
    proj = jnp.dot(h, w_in_ref[...], preferred_element_type=F32)
    u_ref[0] = proj[:, 0:POOL_WIDTH]
    o = POOL_WIDTH
    q_lat = proj[:, o:o + Q_LORA]
    o += Q_LORA
    kv_lat = proj[:, o:o + KV_LORA]
    o += KV_LORA
    k_a = proj[:, o:o + LANES]
    k_b = proj[:, o + LANES:o + 2 * LANES]
    cosv = cs_ref[0, :, 0:LANES]
    sinv = cs_ref[0, :, LANES:2 * LANES]
    k_rot = (k_a * cosv + k_b * sinv).astype(BF16)

    qn = _rms(q_lat, qg_ref[...]).astype(BF16)
    q_a = jnp.dot(qn, w_uqa_ref[...], preferred_element_type=F32)
    q_b = jnp.dot(qn, w_uqb_ref[...], preferred_element_type=F32)
    kvn = _rms(kv_lat, kvg_ref[...]).astype(BF16)
    kv = jnp.dot(kvn, w_ukv_ref[...], preferred_element_type=F32)
    for hd in range(N_HEADS):
        b0 = hd * QK_PAD
        q_ref[0, hd, :, 0:LANES] = (q_a[:, b0:b0 + LANES] * Q_SCALE).astype(BF16)
        q_rot = q_a[:, b0 + LANES:b0 + QK_PAD] * cosv + q_b[:, hd * LANES:(hd + 1) * LANES] * sinv
        q_ref[0, hd, :, LANES:QK_PAD] = (q_rot * Q_SCALE).astype(BF16)
        k_ref[0, hd, :, 0:LANES] = kv[:, b0:b0 + QK_NOPE].astype(BF16)
        k_ref[0, hd, :, LANES:QK_PAD] = k_rot
        v_ref[0, hd] = kv[:, b0 + QK_NOPE:b0 + QK_NOPE + V_HEAD].astype(BF16)


def _in_proj(x, mod, cs, w_in_ext, qg, kvg, w_uqa, w_uqb, w_ukv_bf):
    B, S, D = x.shape
    tm = TM_PROJ
    const = lambda b, i: (0, 0)
    return pl.pallas_call(
        _in_proj_kernel,
        grid=(B, S // tm),
        in_specs=[
            pl.BlockSpec((1, tm, D), lambda b, i: (b, i, 0)),
            pl.BlockSpec((1, 6, D), lambda b, i: (b, 0, 0)),
            pl.BlockSpec((1, tm, 2 * LANES), lambda b, i: (b, i, 0)),
            pl.BlockSpec(w_in_ext.shape, const),
            pl.BlockSpec(qg.shape, const),
            pl.BlockSpec(kvg.shape, const),
            pl.BlockSpec(w_uqa.shape, const),
            pl.BlockSpec(w_uqb.shape, const),
            pl.BlockSpec(w_ukv_bf.shape, const),
        ],
        out_specs=[
            pl.BlockSpec((1, tm, POOL_WIDTH), lambda b, i: (b, i, 0)),
            pl.BlockSpec((1, N_HEADS, tm, QK_PAD), lambda b, i: (b, 0, i, 0)),
            pl.BlockSpec((1, N_HEADS, tm, QK_PAD), lambda b, i: (b, 0, i, 0)),
            pl.BlockSpec((1, N_HEADS, tm, V_HEAD), lambda b, i: (b, 0, i, 0)),
        ],
        out_shape=[
            jax.ShapeDtypeStruct((B, S, POOL_WIDTH), F32),
            jax.ShapeDtypeStruct((B, N_HEADS, S, QK_PAD), BF16),
            jax.ShapeDtypeStruct((B, N_HEADS, S, QK_PAD), BF16),
            jax.ShapeDtypeStruct((B, N_HEADS, S, V_HEAD), BF16),
        ],
        compiler_params=_cparams(("arbitrary", "arbitrary")),
        name="in_proj",
    )(x, mod, cs, w_in_ext, qg, kvg, w_uqa, w_uqb, w_ukv_bf)


def _softmax_step(q, k, v, carry, mask):
    m, l, acc = carry
    s = lax.dot_general(q, k, (((1,), (1,)), ((), ())), preferred_element_type=F32)
    if mask is not None:
        s = jnp.where(mask, s, MASK_VALUE)
    m_new = jnp.maximum(m, jnp.max(s, axis=-1, keepdims=True))
    alpha = jnp.exp2(m - m_new)
    p = jnp.exp2(s - m_new)
    l_new = alpha * l + jnp.sum(p, axis=-1, keepdims=True)
    acc_new = alpha * acc + jnp.dot(p.astype(BF16), v, preferred_element_type=F32)
    return m_new, l_new, acc_new


def _attn_kernel(q_ref, k_ref, v_ref, o_ref):
    seq = q_ref.shape[2]
    row = lax.broadcasted_iota(jnp.int32, (TQ, TK), 0)
    col = lax.broadcasted_iota(jnp.int32, (TQ, TK), 1)
    diag = row >= col
    for i in range(seq // TQ):
        q = q_ref[0, 0, i * TQ:(i + 1) * TQ, :]
        carry = (jnp.full((TQ, 1), MASK_VALUE, F32), jnp.zeros((TQ, 1), F32), jnp.zeros((TQ, V_HEAD), F32))
        for j in range(i + 1):
            k = k_ref[0, 0, j * TK:(j + 1) * TK, :]
            v = v_ref[0, 0, j * TK:(j + 1) * TK, :]
            carry = _softmax_step(q, k, v, carry, diag if j == i else None)
        _, l, acc = carry
        o_ref[0, i * TQ:(i + 1) * TQ, :] = (acc / l).astype(BF16)


def _attention(q, k, v):
    B, H, S, _ = q.shape
    return pl.pallas_call(
        _attn_kernel,
        grid=(B, H),
        in_specs=[
            pl.BlockSpec((1, 1, S, QK_PAD), lambda b, h: (b, h, 0, 0)),
            pl.BlockSpec((1, 1, S, QK_PAD), lambda b, h: (b, h, 0, 0)),
            pl.BlockSpec((1, 1, S, V_HEAD), lambda b, h: (b, h, 0, 0)),
        ],
        out_specs=pl.BlockSpec((1, S, V_HEAD), lambda b, h: (b, 0, h)),
        out_shape=jax.ShapeDtypeStruct((B, S, H * V_HEAD), BF16),
        compiler_params=_cparams(("arbitrary", "arbitrary")),
        name="attention",
    )(q, k, v)


def _layer_norm(v, g, b):
    mu = jnp.mean(v, axis=-1, keepdims=True)
    d = v - mu
    var = jnp.mean(d * d, axis=-1, keepdims=True)
    return d * lax.rsqrt(var + LN_EPS) * g + b


def _post_kernel(x_ref, u_ref, halo_ref, attn_ref, mod_ref, wpool_ref, pscale_ref, wout_p_ref, wout_a_ref,
                 lng_ref, lnb_ref, wr_hi_ref, wr_lo_ref, x1_ref, h2_ref, lt_ref):
    i = pl.program_id(1)
    tm = u_ref.shape[1]
    u = u_ref[0]
    halo = jnp.where(i > 0, halo_ref[0], 0.0)
    ext = jnp.concatenate([halo, u], axis=0)
    s2 = ext + pltpu.roll(ext, 1, 0)
    s4 = s2 + pltpu.roll(s2, 2, 0)
    s8 = s4 + pltpu.roll(s4, 4, 0)
    s16 = s8 + pltpu.roll(s8, 8, 0)
    lane = lax.broadcasted_iota(jnp.int32, (1, POOL_WIDTH), 1)
    win = jnp.where(lane < POOL_GROUP, s2,
                    jnp.where(lane < 2 * POOL_GROUP, s4, jnp.where(lane < 3 * POOL_GROUP, s8, s16)))
    win = win[POOL_HALO:, :]
    width = jnp.where(lane < POOL_GROUP, POOL_WINDOWS[0],
                      jnp.where(lane < 2 * POOL_GROUP, POOL_WINDOWS[1],
                                jnp.where(lane < 3 * POOL_GROUP, POOL_WINDOWS[2], POOL_WINDOWS[3])))
    t = i * tm + lax.broadcasted_iota(jnp.int32, (tm, 1), 0)
    count = jnp.minimum(t + 1, width).astype(F32)
    token_mix = win / count - u
    pooled = jnp.dot(token_mix.astype(BF16), wpool_ref[...], preferred_element_type=F32) * pscale_ref[...]
    mixed = jnp.dot(pooled.astype(BF16), wout_p_ref[...], preferred_element_type=F32)
    mixed += jnp.dot(attn_ref[0], wout_a_ref[...], preferred_element_type=F32)

    g1 = mod_ref[0, 2:3, :]
    sh2 = mod_ref[0, 3:4, :]
    sc2 = mod_ref[0, 4:5, :]
    x1 = _layer_norm(DEEPNORM_ALPHA * x_ref[0] + g1 * mixed, lng_ref[...], lnb_ref[...])
    x1_ref[0] = x1
    h2 = x1 * (1.0 + sc2) + sh2
    h_hi, h_lo = _split_hi_lo(h2)
    h2_ref[0] = _pack_bf16_pairs(h_hi)
    nt = (((1,), (1,)), ((), ()))
    lt = lax.dot_general(wr_hi_ref[...], h_hi, nt, preferred_element_type=F32)
    lt += lax.dot_general(wr_hi_ref[...], h_lo, nt, preferred_element_type=F32)
    lt += lax.dot_general(wr_lo_ref[...], h_hi, nt, preferred_element_type=F32)
    lt_ref[...] = lt


def _post_attn(x, u, attn, mod, wpool_bd, pscale, wout_p, wout_a, lng, lnb, wr_hi, wr_lo):
    B, S, D = x.shape
    tm = TM_PROJ
    nt = S // tm
    const = lambda b, i: (0, 0)
    halo_blocks = tm // POOL_HALO
    return pl.pallas_call(
        _post_kernel,
        grid=(B, nt),
        in_specs=[
            pl.BlockSpec((1, tm, D), lambda b, i: (b, i, 0)),
            pl.BlockSpec((1, tm, POOL_WIDTH), lambda b, i: (b, i, 0)),
            pl.BlockSpec((1, POOL_HALO, POOL_WIDTH), lambda b, i: (b, jnp.maximum(i * halo_blocks - 1, 0), 0)),
            pl.BlockSpec((1, tm, N_HEADS * V_HEAD), lambda b, i: (b, i, 0)),
            pl.BlockSpec((1, 6, D), lambda b, i: (b, 0, 0)),
            pl.BlockSpec(wpool_bd.shape, const),
            pl.BlockSpec(pscale.shape, const),
            pl.BlockSpec(wout_p.shape, const),
            pl.BlockSpec(wout_a.shape, const),
            pl.BlockSpec(lng.shape, const),
            pl.BlockSpec(lnb.shape, const),
            pl.BlockSpec(wr_hi.shape, const),
            pl.BlockSpec(wr_lo.shape, const),
        ],
        out_specs=[
            pl.BlockSpec((1, tm, D), lambda b, i: (b, i, 0)),
            pl.BlockSpec((1, tm, HALF), lambda b, i: (b, i, 0)),
            pl.BlockSpec((N_EXPERTS, tm), lambda b, i: (0, b * nt + i)),
        ],
        out_shape=[
            jax.ShapeDtypeStruct((B, S, D), F32),
            jax.ShapeDtypeStruct((B, S, HALF), jnp.int32),
            jax.ShapeDtypeStruct((N_EXPERTS, B * S), F32),
        ],
        compiler_params=_cparams(("arbitrary", "arbitrary")),
        name="post_attn",
    )(x, u, u, attn, mod, wpool_bd, pscale, wout_p, wout_a, lng, lnb, wr_hi, wr_lo)


def _select_first_max(vals, n_rounds, payload=None):
    work = list(vals)
    sel = [None] * len(vals)
    rounds = []
    for _ in range(n_rounds):
        m = functools.reduce(jnp.maximum, work)
        taken = None
        win_idx = jnp.zeros(m.shape, jnp.int32)
        win_val = jnp.zeros(m.shape, F32)
        for e in range(len(work)):
            hit = work[e] == m
            first = hit if taken is None else jnp.logical_and(hit, jnp.logical_not(taken))
            taken = hit if taken is None else jnp.logical_or(taken, hit)
            sel[e] = first if sel[e] is None else jnp.logical_or(sel[e], first)
            work[e] = jnp.where(first, -jnp.inf, work[e])
            if payload is not None:
                win_idx = jnp.where(first, e, win_idx)
                win_val = jnp.where(first, payload[e], win_val)
        rounds.append((win_idx, win_val))
    return sel, rounds


def _route_kernel(bias_ref, lt_ref, idx_ref, gate_ref):
    scores = [jax.nn.sigmoid(lt_ref[e]) for e in range(N_EXPERTS)]
    choice = [scores[e] + bias_ref[e] for e in range(N_EXPERTS)]
    group_score = []
    for g in range(N_GROUPS):
        vals = choice[g * GROUP_SIZE:(g + 1) * GROUP_SIZE]
        sel2, _ = _select_first_max(vals, 2)
        group_score.append(functools.reduce(
            jnp.add, [jnp.where(sel2[j], vals[j], 0.0) for j in range(GROUP_SIZE)]))
    group_sel, _ = _select_first_max(group_score, TOPK_GROUPS)
    masked = [jnp.where(group_sel[e // GROUP_SIZE], choice[e], -jnp.inf) for e in range(N_EXPERTS)]
    _, rounds = _select_first_max(masked, TOP_K, payload=scores)
    total = functools.reduce(jnp.add, [w for _, w in rounds])
    for k, (e_k, w_k) in enumerate(rounds):
        idx_ref[k] = e_k
        gate_ref[k] = w_k / total * ROUTED_SCALE


def _route(lt3, bias):
    E, R, _ = lt3.shape
    out_spec = pl.BlockSpec((TOP_K, ROUTE_ROWS, LANES), lambda r: (0, r, 0))
    return pl.pallas_call(
        _route_kernel,
        grid=(R // ROUTE_ROWS,),
        in_specs=[
            pl.BlockSpec(memory_space=pltpu.SMEM),
            pl.BlockSpec((E, ROUTE_ROWS, LANES), lambda r: (0, r, 0)),
        ],
        out_specs=[out_spec, out_spec],
        out_shape=[jax.ShapeDtypeStruct((TOP_K, R, LANES), jnp.int32),
                   jax.ShapeDtypeStruct((TOP_K, R, LANES), F32)],
        compiler_params=_cparams(("arbitrary",)),
        name="route",
    )(bias, lt3)


def _sc_mesh():
    return plsc.VectorSubcoreMesh(core_axis_name="c", subcore_axis_name="s",
                                  num_cores=SC_CORES, num_subcores=SC_SUBCORES)


def _sc_worker_id():
    return lax.axis_index("s") * SC_CORES + lax.axis_index("c")


def _sc_gather(table, idx):
    n = idx.shape[0]
    width = table.shape[1]
    workers = SC_CORES * SC_SUBCORES
    per_worker = n // workers
    n_win = per_worker // SC_WINDOW
    assert n == workers * n_win * SC_WINDOW

    @functools.partial(
        pl.kernel, mesh=_sc_mesh(),
        out_type=jax.ShapeDtypeStruct((n, width), table.dtype),
        scratch_types=[
            pltpu.VMEM((n_win, SC_WINDOW), jnp.int32),
            pltpu.VMEM((SC_WINDOW, width), table.dtype),
            pltpu.SemaphoreType.DMA,
        ],
    )
    def gather(table_hbm, idx_hbm, out_hbm, idx_v, rows_v, sem):
        wid = _sc_worker_id()
        pltpu.sync_copy(idx_hbm.at[wid], idx_v)

        @pl.loop(0, n_win)
        def _(j):
            pltpu.async_copy(table_hbm.at[idx_v.at[j]], rows_v, sem).wait()
            pltpu.sync_copy(rows_v, out_hbm.at[pl.ds(wid * per_worker + j * SC_WINDOW, SC_WINDOW)])

    return gather(table, idx.reshape(workers, n_win, SC_WINDOW))


def _sc_dispatch(rows, dest_kt, n_slots):
    T, width = rows.shape
    K = dest_kt.shape[0]
    workers = SC_CORES * SC_SUBCORES
    per_worker = T // workers
    n_win = per_worker // SC_WINDOW
    assert T == workers * n_win * SC_WINDOW
    dest_w = dest_kt.reshape(K, workers, n_win, SC_WINDOW).transpose(1, 2, 0, 3)
    dest_w = dest_w.reshape(workers, n_win * K, SC_WINDOW)

    @functools.partial(
        pl.kernel, mesh=_sc_mesh(),
        out_type=jax.ShapeDtypeStruct((n_slots, width), rows.dtype),
        scratch_types=[
            pltpu.VMEM((n_win * K, SC_WINDOW), jnp.int32),
            pltpu.VMEM((SC_WINDOW, width), rows.dtype),
        ],
    )
    def dispatch(rows_hbm, dest_hbm, out_hbm, idx_v, rows_v):
        wid = _sc_worker_id()
        pltpu.sync_copy(dest_hbm.at[wid], idx_v)

        @pl.loop(0, n_win)
        def _(j):
            pltpu.sync_copy(rows_hbm.at[pl.ds(wid * per_worker + j * SC_WINDOW, SC_WINDOW)], rows_v)
            for k in range(K):
                pltpu.sync_copy(rows_v, out_hbm.at[idx_v.at[j * K + k]])

    return dispatch(rows, dest_w)


def _swiglu(h_lo, h_hi, wg_ref, wu_ref):
    def proj(w_ref):
        return (jnp.dot(h_lo, w_ref[:HALF, :], preferred_element_type=F32)
                + jnp.dot(h_hi, w_ref[HALF:, :], preferred_element_type=F32))
    gate = proj(wg_ref)
    up = proj(wu_ref)
    return (gate * jax.nn.sigmoid(gate) * up).astype(BF16)


def _expert_kernel(seg_first_ref, seg_blocks_ref, n_real_ref, xs_hbm, wg_ref, wu_ref, wd_ref, ys_hbm,
                   xbuf, ybuf, wg_bf, wu_bf, wd_bf, in_sems, out_sems):
    e = pl.program_id(0)
    first = seg_first_ref[e]
    count = seg_blocks_ref[e]
    total = n_real_ref[0]
    blk = DISPATCH_BLOCK

    def in_copy(g, slot):
        return pltpu.make_async_copy(xs_hbm.at[pl.ds(g * blk, blk)], xbuf.at[slot], in_sems.at[slot])

    def out_copy(g, slot):
        return pltpu.make_async_copy(ybuf.at[slot], ys_hbm.at[pl.ds(g * blk, blk)], out_sems.at[slot])

    @pl.when(jnp.logical_and(e == 0, total > 0))
    def _():
        in_copy(0, 0).start()

    @pl.when(count > 0)
    def _():
        wg_bf[...] = wg_ref[0, 0].astype(BF16)
        wu_bf[...] = wu_ref[0, 0].astype(BF16)
        wd_bf[...] = wd_ref[0, 0].astype(BF16)

    def block(g, _):
        slot = g % 2
        in_copy(g, slot).wait()

        @pl.when(g + 1 < total)
        def _():
            in_copy(g + 1, 1 - slot).start()

        @pl.when(g >= 2)
        def _():
            out_copy(g - 2, slot).wait()

        lo, hi = _unpack_bf16_pairs(xbuf[slot])
        a = _swiglu(lo.astype(BF16), hi.astype(BF16), wg_bf, wu_bf)
        y = jnp.dot(a, wd_bf[...], preferred_element_type=F32)
        ybuf[slot] = _pack_bf16_pairs(y.astype(BF16))
        out_copy(g, slot).start()
        return 0

    lax.fori_loop(first, first + count, block, 0)

    @pl.when(e == pl.num_programs(0) - 1)
    def _():
        @pl.when(total >= 2)
        def _():
            out_copy(total - 2, total % 2).wait()

        @pl.when(total >= 1)
        def _():
            out_copy(total - 1, (total - 1) % 2).wait()


def _expert_mm(seg_first, seg_blocks, n_real, xs, wg, wu, wd, layer):
    P = xs.shape[0]
    D = wg.shape[2]
    w_map = lambda e, sf, sb, nr: (layer, e, 0, 0)
    grid_spec = pltpu.PrefetchScalarGridSpec(
        num_scalar_prefetch=3,
        grid=(N_EXPERTS,),
        in_specs=[
            pl.BlockSpec(memory_space=pl.ANY),
            pl.BlockSpec((1, 1, D, D_EXPERT), w_map),
            pl.BlockSpec((1, 1, D, D_EXPERT), w_map),
            pl.BlockSpec((1, 1, D_EXPERT, D), w_map),
        ],
        out_specs=pl.BlockSpec(memory_space=pl.ANY),
        scratch_shapes=[
            pltpu.VMEM((2, DISPATCH_BLOCK, HALF), jnp.int32),
            pltpu.VMEM((2, DISPATCH_BLOCK, HALF), jnp.int32),
            pltpu.VMEM((D, D_EXPERT), BF16), pltpu.VMEM((D, D_EXPERT), BF16), pltpu.VMEM((D_EXPERT, D), BF16),
            pltpu.SemaphoreType.DMA((2,)), pltpu.SemaphoreType.DMA((2,)),
        ],
    )
    return pl.pallas_call(
        _expert_kernel,
        grid_spec=grid_spec,
        out_shape=jax.ShapeDtypeStruct((P, HALF), jnp.int32),
        compiler_params=_cparams(("arbitrary",)),
        name="expert_mm",
    )(seg_first, seg_blocks, n_real, xs, wg, wu, wd)


def _rank_kernel(idx_ref, rank_ref, cnt_ref, tri_ref, carry_ref):
    i = pl.program_id(0)
    tm = idx_ref.shape[1]

    @pl.when(i == 0)
    def _():
        r = lax.broadcasted_iota(jnp.int32, (tm, tm), 0)
        c = lax.broadcasted_iota(jnp.int32, (tm, tm), 1)
        tri_ref[...] = (r < c).astype(BF16)
        carry_ref[...] = jnp.zeros(carry_ref.shape, F32)

    e_iota = lax.broadcasted_iota(jnp.int32, (N_EXPERTS, tm), 0)
    hits = [e_iota == idx_ref[k:k + 1, :] for k in range(TOP_K)]
    mask = functools.reduce(jnp.logical_or, hits).astype(F32)
    rank = jnp.dot(mask.astype(BF16), tri_ref[...], preferred_element_type=F32) + carry_ref[:, 0:1]
    for k in range(TOP_K):
        rank_ref[k:k + 1, :] = jnp.sum(jnp.where(hits[k], rank, 0.0), axis=0, keepdims=True).astype(jnp.int32)
    carry_ref[...] = carry_ref[...] + jnp.sum(mask, axis=1, keepdims=True)
    cnt_ref[...] = carry_ref[...]


def _rank(idx_kt):
    K, T = idx_kt.shape
    tm = 1024
    return pl.pallas_call(
        _rank_kernel,
        grid=(T // tm,),
        in_specs=[pl.BlockSpec((K, tm), lambda i: (0, i))],
        out_specs=[pl.BlockSpec((K, tm), lambda i: (0, i)),
                   pl.BlockSpec((N_EXPERTS, LANES), lambda i: (0, 0))],
        out_shape=[jax.ShapeDtypeStruct((K, T), jnp.int32),
                   jax.ShapeDtypeStruct((N_EXPERTS, LANES), F32)],
        scratch_shapes=[pltpu.VMEM((tm, tm), BF16), pltpu.VMEM((N_EXPERTS, LANES), F32)],
        compiler_params=_cparams(("arbitrary",)),
        name="rank",
    )(idx_kt)


def _dest_kernel(idx_ref, rank_ref, start_ref, dest_ref):
    tm = idx_ref.shape[1]
    e_iota = lax.broadcasted_iota(jnp.int32, (N_EXPERTS, tm), 0)
    start = start_ref[:, 0:1]
    for k in range(TOP_K):
        base = jnp.sum(jnp.where(e_iota == idx_ref[k:k + 1, :], start, 0.0), axis=0, keepdims=True)
        dest_ref[k:k + 1, :] = rank_ref[k:k + 1, :] + base.astype(jnp.int32)


def _dest(idx_kt, rank_kt, start):
    K, T = idx_kt.shape
    tm = 1024
    tok_spec = pl.BlockSpec((K, tm), lambda i: (0, i))
    return pl.pallas_call(
        _dest_kernel,
        grid=(T // tm,),
        in_specs=[tok_spec, tok_spec, pl.BlockSpec((N_EXPERTS, LANES), lambda i: (0, 0))],
        out_specs=tok_spec,
        out_shape=jax.ShapeDtypeStruct((K, T), jnp.int32),
        compiler_params=_cparams(("arbitrary",)),
        name="dest",
    )(idx_kt, rank_kt, start)


def _block_plan(counts):
    blk = DISPATCH_BLOCK
    seg_blocks = (counts.astype(jnp.int32) + blk - 1) // blk
    seg_end = jnp.cumsum(seg_blocks)
    seg_first = seg_end - seg_blocks
    return seg_first * blk, seg_first, seg_blocks, seg_end[-1:]


def _combine_kernel(h_ref, x1_ref, yk_ref, g_ref, mod_ref, wsg_ref, wsu_ref, wsd_ref, lng_ref, lnb_ref, o_ref):
    lo, hi = _unpack_bf16_pairs(h_ref[...])
    a = _swiglu(lo.astype(BF16), hi.astype(BF16), wsg_ref, wsu_ref)
    ffn = jnp.dot(a, wsd_ref[...], preferred_element_type=F32)
    r_lo = jnp.zeros((h_ref.shape[0], HALF), F32)
    r_hi = jnp.zeros((h_ref.shape[0], HALF), F32)
    for k in range(TOP_K):
        y_lo, y_hi = _unpack_bf16_pairs(yk_ref[k])
        g = g_ref[:, k:k + 1]
        r_lo += g * y_lo
        r_hi += g * y_hi
    ffn += jnp.concatenate([r_lo, r_hi], axis=1)
    g2 = mod_ref[0, 5:6, :]
    o_ref[...] = _layer_norm(DEEPNORM_ALPHA * x1_ref[...] + g2 * ffn, lng_ref[...], lnb_ref[...])


def _combine(h2p, x1, yk, gates, mod, wsg, wsu, wsd, lng, lnb, seq):
    T, D = x1.shape
    tm = TM_COMBINE
    per_seq = seq // tm
    const = lambda i: (0, 0)
    return pl.pallas_call(
        _combine_kernel,
        grid=(T // tm,),
        in_specs=[
            pl.BlockSpec((tm, HALF), lambda i: (i, 0)),
            pl.BlockSpec((tm, D), lambda i: (i, 0)),
            pl.BlockSpec((TOP_K, tm, HALF), lambda i: (0, i, 0)),
            pl.BlockSpec((tm, TOP_K), lambda i: (i, 0)),
            pl.BlockSpec((1, 6, D), lambda i: (i // per_seq, 0, 0)),
            pl.BlockSpec(wsg.shape, const),
            pl.BlockSpec(wsu.shape, const),
            pl.BlockSpec(wsd.shape, const),
            pl.BlockSpec(lng.shape, const),
            pl.BlockSpec(lnb.shape, const),
        ],
        out_specs=pl.BlockSpec((tm, D), lambda i: (i, 0)),
        out_shape=jax.ShapeDtypeStruct((T, D), F32),
        compiler_params=_cparams(("arbitrary",)),
        name="combine",
    )(h2p, x1, yk, gates, mod, wsg, wsu, wsd, lng, lnb)


def _rope_pair(w):
    half = QK_ROPE // 2
    return w, jnp.concatenate([-w[:, half:], w[:, :half]], axis=1)


def _prep_layer(w_in, w_uq, w_ukv, w_pool, w_out, w_router):
    D = w_in.shape[0]
    zpad = lambda r: jnp.zeros((r, LANES - QK_ROPE), F32)
    lat = POOL_WIDTH + Q_LORA + KV_LORA
    k_a, k_b = _rope_pair(w_in[:, lat:lat + QK_ROPE])
    w_in_ext = jnp.concatenate([w_in[:, :lat], k_a, zpad(D), k_b, zpad(D)], axis=1).astype(BF16)

    qa_cols, qb_cols = [], []
    per_head = QK_NOPE + QK_ROPE
    for hd in range(N_HEADS):
        w_h = w_uq[:, hd * per_head:(hd + 1) * per_head]
        r_a, r_b = _rope_pair(w_h[:, QK_NOPE:])
        qa_cols += [w_h[:, :QK_NOPE], r_a, zpad(Q_LORA)]
        qb_cols += [r_b, zpad(Q_LORA)]
    w_uqa = jnp.concatenate(qa_cols, axis=1).astype(BF16)
    w_uqb = jnp.concatenate(qb_cols, axis=1).astype(BF16)

    wpool_bd = jax.scipy.linalg.block_diag(*[w_pool[g] for g in range(len(POOL_WINDOWS))]).astype(BF16)
    wr_hi, wr_lo = _split_hi_lo(w_router.T)
    return dict(w_in_ext=w_in_ext, w_uqa=w_uqa, w_uqb=w_uqb, w_ukv=w_ukv.astype(BF16), wpool_bd=wpool_bd,
                wout_p=w_out[:POOL_WIDTH].astype(BF16), wout_a=w_out[POOL_WIDTH:].astype(BF16),
                wr_hi=wr_hi, wr_lo=wr_lo)


def kernel(x, c, positions, w_ada, b_ada, w_in, q_norm_g, kv_norm_g, w_uq, w_ukv, w_pool, pool_scale,
           w_out, ln1_g, ln1_b, w_router, router_bias, w_exp_gate, w_exp_up, w_exp_down,
           w_sh_gate, w_sh_up, w_sh_down, ln2_g, ln2_b):
    B, S, D = x.shape
    L = w_in.shape[0]
    row = lambda v: v.reshape(1, -1)

    mod_all = _ada_mod(c, w_ada, b_ada).reshape(L, B, 6, D)
    cs_all = _rope_table(positions)
    bs = B // N_STREAMS
    T = bs * S
    n_blocks = T * TOP_K // DISPATCH_BLOCK + N_EXPERTS
    xs_streams = [x[s * bs:(s + 1) * bs] for s in range(N_STREAMS)]
    for l in range(L):
        p = _prep_layer(w_in[l], w_uq[l], w_ukv[l], w_pool[l], w_out[l], w_router[l])
        shared = (w_sh_gate[l].astype(BF16), w_sh_up[l].astype(BF16), w_sh_down[l].astype(BF16))
        for s in range(N_STREAMS):
            xl = xs_streams[s]
            mod = mod_all[l, s * bs:(s + 1) * bs]
            cs = cs_all[s * bs:(s + 1) * bs]
            u, q, k, v = _in_proj(xl, mod, cs, p["w_in_ext"], row(q_norm_g[l]), row(kv_norm_g[l]),
                                  p["w_uqa"], p["w_uqb"], p["w_ukv"])
            attn = _attention(q, k, v)
            x1, h2p, lt = _post_attn(xl, u, attn, mod, p["wpool_bd"], row(pool_scale[l]), p["wout_p"],
                                     p["wout_a"], row(ln1_g[l]), row(ln1_b[l]), p["wr_hi"], p["wr_lo"])
            idx_k, gate_k = _route(lt.reshape(N_EXPERTS, T // LANES, LANES), router_bias[l])
            idx_kt = idx_k.reshape(TOP_K, T)
            gates = gate_k.reshape(TOP_K, T).T
            rank_kt, counts = _rank(idx_kt)
            pad_start, seg_first, seg_blocks, n_real = _block_plan(counts[:, 0])
            start = jnp.broadcast_to(pad_start.astype(F32)[:, None], (N_EXPERTS, LANES))
            dest_kt = _dest(idx_kt, rank_kt, start)
            h2p = h2p.reshape(T, HALF)
            rows = _sc_dispatch(h2p, dest_kt, n_blocks * DISPATCH_BLOCK)
            ys = _expert_mm(seg_first, seg_blocks, n_real, rows, w_exp_gate, w_exp_up, w_exp_down, l)
            yk = _sc_gather(ys, dest_kt.reshape(-1)).reshape(TOP_K, T, HALF)
            xs_streams[s] = _combine(h2p, x1.reshape(T, D), yk, gates, mod, *shared,
                                     row(ln2_g[l]), row(ln2_b[l]), S).reshape(bs, S, D)
    return jnp.concatenate(xs_streams, axis=0)
```

```python
import functools

import jax
import jax.numpy as jnp
from jax import lax
from jax.experimental import pallas as pl
from jax.experimental.pallas import tpu as pltpu
from jax.experimental.pallas import tpu_sc as plsc

F32 = jnp.float32
BF16 = jnp.bfloat16

D_MODEL = 1024
DEPTH = 4
POOL_WIDTH = 256
POOL_WINDOWS = (2, 4, 8, 16)
POOL_GROUP = 64
POOL_HALO = 16
QK_NOPE = 128
QK_ROPE = 64
V_HEAD = 128
N_HEADS = 6
Q_LORA = 384
KV_LORA = 256
ROPE_THETA = 10000.0
N_EXPERTS = 64
TOP_K = 8
N_GROUPS = 8
GROUP_SIZE = N_EXPERTS // N_GROUPS
TOPK_GROUPS = 4
D_EXPERT = 256
ROUTED_SCALE = 2.5
DEEPNORM_ALPHA = (2.0 * DEPTH) ** 0.25
LN_EPS = 1e-5
RMS_EPS = 1e-6
ATTN_SCALE = (QK_NOPE + QK_ROPE) ** -0.5
LOG2_E = 1.4426950408889634
Q_SCALE = ATTN_SCALE * LOG2_E
LANES = 128
QK_PAD = 2 * LANES
MASK_VALUE = -1e30

TM_PROJ = 512
TQ = 512
TK = 512
ROUTE_ROWS = 8
DISPATCH_BLOCK = 512
EXPERT_RING = 4
TM_COMBINE = 512
SC_CORES = 2
SC_SUBCORES = 16
SC_WINDOW = 64
N_STREAMS = 2
HALF = D_MODEL // 2
VMEM_LIMIT = 48 * 1024 * 1024


def _cparams(sem):
    return pltpu.CompilerParams(dimension_semantics=sem, vmem_limit_bytes=VMEM_LIMIT)


def _split_hi_lo(a):
    hi = a.astype(BF16)
    lo = (a - hi.astype(F32)).astype(BF16)
    return hi, lo


def _pack_bf16_pairs(a):
    bits = lax.bitcast_convert_type(a.astype(F32), jnp.uint32)
    half = a.shape[1] // 2
    word = (bits[:, :half] >> 16) | (bits[:, half:] & jnp.uint32(0xFFFF0000))
    return lax.bitcast_convert_type(word, jnp.int32)


def _unpack_bf16_pairs(w):
    bits = lax.bitcast_convert_type(w, jnp.uint32)
    lo = lax.bitcast_convert_type(bits << 16, F32)
    hi = lax.bitcast_convert_type(bits & jnp.uint32(0xFFFF0000), F32)
    return lo, hi


def _ada_kernel(c_ref, w_ref, b_ref, o_ref):
    c = c_ref[...]
    cond = c * jax.nn.sigmoid(c)
    c_hi, c_lo = _split_hi_lo(cond)
    w_hi, w_lo = _split_hi_lo(w_ref[0])
    acc = jnp.dot(c_hi, w_hi, preferred_element_type=F32)
    acc += jnp.dot(c_lo, w_hi, preferred_element_type=F32)
    acc += jnp.dot(c_hi, w_lo, preferred_element_type=F32)
    o_ref[0] = acc + b_ref[0]


def _ada_mod(c, w_ada, b_ada):
    L, D, N = w_ada.shape
    B = c.shape[0]
    tn = 1536
    return pl.pallas_call(
        _ada_kernel,
        grid=(L, N // tn),
        in_specs=[
            pl.BlockSpec((B, D), lambda l, j: (0, 0)),
            pl.BlockSpec((1, D, tn), lambda l, j: (l, 0, j)),
            pl.BlockSpec((1, 1, tn), lambda l, j: (l, 0, j)),
        ],
        out_specs=pl.BlockSpec((1, B, tn), lambda l, j: (l, 0, j)),
        out_shape=jax.ShapeDtypeStruct((L, B, N), F32),
        compiler_params=_cparams(("arbitrary", "arbitrary")),
        name="ada_mod",
    )(c, w_ada, b_ada.reshape(L, 1, N))


def _rope_table_kernel(pos_ref, freq_ref, o_ref):
    ang = pos_ref[0].astype(F32) * freq_ref[...]
    o_ref[0, :, 0:LANES] = jnp.cos(ang)
    o_ref[0, :, LANES:2 * LANES] = jnp.sin(ang)


def _rope_table(positions):
    B, S = positions.shape
    half = QK_ROPE // 2
    inv_freq = ROPE_THETA ** (-jnp.arange(half, dtype=F32) / half)
    freq = jnp.tile(inv_freq, LANES // half).reshape(1, LANES)
    return pl.pallas_call(
        _rope_table_kernel,
        grid=(B,),
        in_specs=[
            pl.BlockSpec((1, S, 1), lambda b: (b, 0, 0)),
            pl.BlockSpec((1, LANES), lambda b: (0, 0)),
        ],
        out_specs=pl.BlockSpec((1, S, 2 * LANES), lambda b: (b, 0, 0)),
        out_shape=jax.ShapeDtypeStruct((B, S, 2 * LANES), F32),
        compiler_params=_cparams(("arbitrary",)),
        name="rope_table",
    )(positions.reshape(B, S, 1), freq)


def _rms(x, g):
    return x * lax.rsqrt(jnp.mean(x * x, axis=-1, keepdims=True) + RMS_EPS) * g


def _in_proj_kernel(x_ref, mod_ref, cs_ref, w_in_ref, qg_ref, kvg_ref, w_uqa_ref, w_uqb_ref, w_ukv_ref,
                    u_ref, q_ref, k_ref, v_ref):
    x = x_ref[0]
    sh1 = mod_ref[0, 0:1, :]
    sc1 = mod_ref[0, 1:2, :]
    h = (x * (1.0 + sc1) + sh1).astype(BF16)
    proj = jnp.dot(h, w_in_ref[...], preferred_element_type=F32)
    u_ref[0] = proj[:, 0:POOL_WIDTH]
    o = POOL_WIDTH
    q_lat = proj[:, o:o + Q_LORA]
    o += Q_LORA
    kv_lat = proj[:, o:o + KV_LORA]
    o += KV_LORA
    k_a = proj[:, o:o + LANES]
    k_b = proj[:, o + LANES:o + 2 * LANES]
    cosv = cs_ref[0, :, 0:LANES]
    sinv = cs_ref[0, :, LANES:2 * LANES]
    k_rot = (k_a * cosv + k_b * sinv).astype(BF16)

    qn = _rms(q_lat, qg_ref[...]).astype(BF16)
    q_a = jnp.dot(qn, w_uqa_ref[...], preferred_element_type=F32)
    q_b = jnp.dot(qn, w_uqb_ref[...], preferred_element_type=F32)
    kvn = _rms(kv_lat, kvg_ref[...]).astype(BF16)
    kv = jnp.dot(kvn, w_ukv_ref[...], preferred_element_type=F32)
    for hd in range(N_HEADS):
        b0 = hd * QK_PAD
        q_ref[0, hd, :, 0:LANES] = (q_a[:, b0:b0 + LANES] * Q_SCALE).astype(BF16)
        q_rot = q_a[:, b0 + LANES:b0 + QK_PAD] * cosv + q_b[:, hd * LANES:(hd + 1) * LANES] * sinv
        q_ref[0, hd, :, LANES:QK_PAD] = (q_rot * Q_SCALE).astype(BF16)
        k_ref[0, hd, :, 0:LANES] = kv[:, b0:b0 + QK_NOPE].astype(BF16)
        k_ref[0, hd, :, LANES:QK_PAD] = k_rot
        v_ref[0, hd] = kv[:, b0 + QK_NOPE:b0 + QK_NOPE + V_HEAD].astype(BF16)


def _in_proj(x, mod, cs, w_in_ext, qg, kvg, w_uqa, w_uqb, w_ukv_bf):
    B, S, D = x.shape
    tm = TM_PROJ
    const = lambda b, i: (0, 0)
    return pl.pallas_call(
        _in_proj_kernel,
        grid=(B, S // tm),
        in_specs=[
            pl.BlockSpec((1, tm, D), lambda b, i: (b, i, 0)),
            pl.BlockSpec((1, 6, D), lambda b, i: (b, 0, 0)),
            pl.BlockSpec((1, tm, 2 * LANES), lambda b, i: (b, i, 0)),
            pl.BlockSpec(w_in_ext.shape, const),
            pl.BlockSpec(qg.shape, const),
            pl.BlockSpec(kvg.shape, const),
            pl.BlockSpec(w_uqa.shape, const),
            pl.BlockSpec(w_uqb.shape, const),
            pl.BlockSpec(w_ukv_bf.shape, const),
        ],
        out_specs=[
            pl.BlockSpec((1, tm, POOL_WIDTH), lambda b, i: (b, i, 0)),
            pl.BlockSpec((1, N_HEADS, tm, QK_PAD), lambda b, i: (b, 0, i, 0)),
            pl.BlockSpec((1, N_HEADS, tm, QK_PAD), lambda b, i: (b, 0, i, 0)),
            pl.BlockSpec((1, N_HEADS, tm, V_HEAD), lambda b, i: (b, 0, i, 0)),
        ],
        out_shape=[
            jax.ShapeDtypeStruct((B, S, POOL_WIDTH), F32),
            jax.ShapeDtypeStruct((B, N_HEADS, S, QK_PAD), BF16),
            jax.ShapeDtypeStruct((B, N_HEADS, S, QK_PAD), BF16),
            jax.ShapeDtypeStruct((B, N_HEADS, S, V_HEAD), BF16),
        ],
        compiler_params=_cparams(("arbitrary", "arbitrary")),
        name="in_proj",
    )(x, mod, cs, w_in_ext, qg, kvg, w_uqa, w_uqb, w_ukv_bf)


def _softmax_step(q, k, v, carry, mask):
    m, l, acc = carry
    s = lax.dot_general(q, k, (((1,), (1,)), ((), ())), preferred_element_type=F32)
    if mask is not None:
        s = jnp.where(mask, s, MASK_VALUE)
    m_new = jnp.maximum(m, jnp.max(s, axis=-1, keepdims=True))
    alpha = jnp.exp2(m - m_new)
    p = jnp.exp2(s - m_new)
    l_new = alpha * l + jnp.sum(p, axis=-1, keepdims=True)
    acc_new = alpha * acc + jnp.dot(p.astype(BF16), v, preferred_element_type=F32)
    return m_new, l_new, acc_new


def _attn_kernel(q_ref, k_ref, v_ref, o_ref):
    seq = q_ref.shape[2]
    row = lax.broadcasted_iota(jnp.int32, (TQ, TK), 0)
    col = lax.broadcasted_iota(jnp.int32, (TQ, TK), 1)
    diag = row >= col
    for i in range(seq // TQ):
        q = q_ref[0, 0, i * TQ:(i + 1) * TQ, :]
        carry = (jnp.full((TQ, 1), MASK_VALUE, F32), jnp.zeros((TQ, 1), F32), jnp.zeros((TQ, V_HEAD), F32))
        for j in range(i + 1):
            k = k_ref[0, 0, j * TK:(j + 1) * TK, :]
            v = v_ref[0, 0, j * TK:(j + 1) * TK, :]
            carry = _softmax_step(q, k, v, carry, diag if j == i else None)
        _, l, acc = carry
        o_ref[0, i * TQ:(i + 1) * TQ, :] = (acc / l).astype(BF16)


def _attention(q, k, v):
    B, H, S, _ = q.shape
    return pl.pallas_call(
        _attn_kernel,
        grid=(B, H),
        in_specs=[
            pl.BlockSpec((1, 1, S, QK_PAD), lambda b, h: (b, h, 0, 0)),
            pl.BlockSpec((1, 1, S, QK_PAD), lambda b, h: (b, h, 0, 0)),
            pl.BlockSpec((1, 1, S, V_HEAD), lambda b, h: (b, h, 0, 0)),
        ],
        out_specs=pl.BlockSpec((1, S, V_HEAD), lambda b, h: (b, 0, h)),
        out_shape=jax.ShapeDtypeStruct((B, S, H * V_HEAD), BF16),
        compiler_params=_cparams(("arbitrary", "arbitrary")),
        name="attention",
    )(q, k, v)


def _layer_norm(v, g, b):
    mu = jnp.mean(v, axis=-1, keepdims=True)
    d = v - mu
    var = jnp.mean(d * d, axis=-1, keepdims=True)
    return d * lax.rsqrt(var + LN_EPS) * g + b


def _post_kernel(x_ref, u_ref, halo_ref, attn_ref, mod_ref, wpool_ref, pscale_ref, wout_p_ref, wout_a_ref,
                 lng_ref, lnb_ref, wr_hi_ref, wr_lo_ref, x1_ref, h2_ref, lt_ref):
    i = pl.program_id(1)
    tm = u_ref.shape[1]
    u = u_ref[0]
    halo = jnp.where(i > 0, halo_ref[0], 0.0)
    ext = jnp.concatenate([halo, u], axis=0)
    s2 = ext + pltpu.roll(ext, 1, 0)
    s4 = s2 + pltpu.roll(s2, 2, 0)
    s8 = s4 + pltpu.roll(s4, 4, 0)
    s16 = s8 + pltpu.roll(s8, 8, 0)
    lane = lax.broadcasted_iota(jnp.int32, (1, POOL_WIDTH), 1)
    win = jnp.where(lane < POOL_GROUP, s2,
                    jnp.where(lane < 2 * POOL_GROUP, s4, jnp.where(lane < 3 * POOL_GROUP, s8, s16)))
    win = win[POOL_HALO:, :]
    width = jnp.where(lane < POOL_GROUP, POOL_WINDOWS[0],
                      jnp.where(lane < 2 * POOL_GROUP, POOL_WINDOWS[1],
                                jnp.where(lane < 3 * POOL_GROUP, POOL_WINDOWS[2], POOL_WINDOWS[3])))
    t = i * tm + lax.broadcasted_iota(jnp.int32, (tm, 1), 0)
    count = jnp.minimum(t + 1, width).astype(F32)
    token_mix = win / count - u
    pooled = jnp.dot(token_mix.astype(BF16), wpool_ref[...], preferred_element_type=F32) * pscale_ref[...]
    mixed = jnp.dot(pooled.astype(BF16), wout_p_ref[...], preferred_element_type=F32)
    mixed += jnp.dot(attn_ref[0], wout_a_ref[...], preferred_element_type=F32)

    g1 = mod_ref[0, 2:3, :]
    sh2 = mod_ref[0, 3:4, :]
    sc2 = mod_ref[0, 4:5, :]
    x1 = _layer_norm(DEEPNORM_ALPHA * x_ref[0] + g1 * mixed, lng_ref[...], lnb_ref[...])
    x1_ref[0] = x1
    h2 = x1 * (1.0 + sc2) + sh2
    h_hi, h_lo = _split_hi_lo(h2)
    h2_ref[0] = _pack_bf16_pairs(h_hi)
    nt = (((1,), (1,)), ((), ()))
    lt = lax.dot_general(wr_hi_ref[...], h_hi, nt, preferred_element_type=F32)
    lt += lax.dot_general(wr_hi_ref[...], h_lo, nt, preferred_element_type=F32)
    lt += lax.dot_general(wr_lo_ref[...], h_hi, nt, preferred_element_type=F32)
    lt_ref[...] = lt


def _post_attn(x, u, attn, mod, wpool_bd, pscale, wout_p, wout_a, lng, lnb, wr_hi, wr_lo):
    B, S, D = x.shape
    tm = TM_PROJ
    nt = S // tm
    const = lambda b, i: (0, 0)
    halo_blocks = tm // POOL_HALO
    return pl.pallas_call(
        _post_kernel,
        grid=(B, nt),
        in_specs=[
            pl.BlockSpec((1, tm, D), lambda b, i: (b, i, 0)),
            pl.BlockSpec((1, tm, POOL_WIDTH), lambda b, i: (b, i, 0)),
            pl.BlockSpec((1, POOL_HALO, POOL_WIDTH), lambda b, i: (b, jnp.maximum(i * halo_blocks - 1, 0), 0)),
            pl.BlockSpec((1, tm, N_HEADS * V_HEAD), lambda b, i: (b, i, 0)),
            pl.BlockSpec((1, 6, D), lambda b, i: (b, 0, 0)),
            pl.BlockSpec(wpool_bd.shape, const),
            pl.BlockSpec(pscale.shape, const),
            pl.BlockSpec(wout_p.shape, const),
            pl.BlockSpec(wout_a.shape, const),
            pl.BlockSpec(lng.shape, const),
            pl.BlockSpec(lnb.shape, const),
            pl.BlockSpec(wr_hi.shape, const),
            pl.BlockSpec(wr_lo.shape, const),
        ],
        out_specs=[
            pl.BlockSpec((1, tm, D), lambda b, i: (b, i, 0)),
            pl.BlockSpec((1, tm, HALF), lambda b, i: (b, i, 0)),
            pl.BlockSpec((N_EXPERTS, tm), lambda b, i: (0, b * nt + i)),
        ],
        out_shape=[
            jax.ShapeDtypeStruct((B, S, D), F32),
            jax.ShapeDtypeStruct((B, S, HALF), jnp.int32),
            jax.ShapeDtypeStruct((N_EXPERTS, B * S), F32),
        ],
        compiler_params=_cparams(("arbitrary", "arbitrary")),
        name="post_attn",
    )(x, u, u, attn, mod, wpool_bd, pscale, wout_p, wout_a, lng, lnb, wr_hi, wr_lo)


def _select_first_max(vals, n_rounds, payload=None):
    work = list(vals)
    sel = [None] * len(vals)
    rounds = []
    for _ in range(n_rounds):
        m = functools.reduce(jnp.maximum, work)
        taken = None
        win_idx = jnp.zeros(m.shape, jnp.int32)
        win_val = jnp.zeros(m.shape, F32)
        for e in range(len(work)):
            hit = work[e] == m
            first = hit if taken is None else jnp.logical_and(hit, jnp.logical_not(taken))
            taken = hit if taken is None else jnp.logical_or(taken, hit)
            sel[e] = first if sel[e] is None else jnp.logical_or(sel[e], first)
            work[e] = jnp.where(first, -jnp.inf, work[e])
            if payload is not None:
                win_idx = jnp.where(first, e, win_idx)
                win_val = jnp.where(first, payload[e], win_val)
        rounds.append((win_idx, win_val))
    return sel, rounds


def _route_kernel(bias_ref, lt_ref, idx_ref, gate_ref):
    scores = [jax.nn.sigmoid(lt_ref[e]) for e in range(N_EXPERTS)]
    choice = [scores[e] + bias_ref[e] for e in range(N_EXPERTS)]
    group_score = []
    for g in range(N_GROUPS):
        vals = choice[g * GROUP_SIZE:(g + 1) * GROUP_SIZE]
        sel2, _ = _select_first_max(vals, 2)
        group_score.append(functools.reduce(
            jnp.add, [jnp.where(sel2[j], vals[j], 0.0) for j in range(GROUP_SIZE)]))
    group_sel, _ = _select_first_max(group_score, TOPK_GROUPS)
    masked = [jnp.where(group_sel[e // GROUP_SIZE], choice[e], -jnp.inf) for e in range(N_EXPERTS)]
    _, rounds = _select_first_max(masked, TOP_K, payload=scores)
    total = functools.reduce(jnp.add, [w for _, w in rounds])
    for k, (e_k, w_k) in enumerate(rounds):
        idx_ref[k] = e_k
        gate_ref[k] = w_k / total * ROUTED_SCALE


def _route(lt3, bias):
    E, R, _ = lt3.shape
    out_spec = pl.BlockSpec((TOP_K, ROUTE_ROWS, LANES), lambda r: (0, r, 0))
    return pl.pallas_call(
        _route_kernel,
        grid=(R // ROUTE_ROWS,),
        in_specs=[
            pl.BlockSpec(memory_space=pltpu.SMEM),
            pl.BlockSpec((E, ROUTE_ROWS, LANES), lambda r: (0, r, 0)),
        ],
        out_specs=[out_spec, out_spec],
        out_shape=[jax.ShapeDtypeStruct((TOP_K, R, LANES), jnp.int32),
                   jax.ShapeDtypeStruct((TOP_K, R, LANES), F32)],
        compiler_params=_cparams(("arbitrary",)),
        name="route",
    )(bias, lt3)


def _sc_mesh():
    return plsc.VectorSubcoreMesh(core_axis_name="c", subcore_axis_name="s",
                                  num_cores=SC_CORES, num_subcores=SC_SUBCORES)


def _sc_worker_id():
    return lax.axis_index("s") * SC_CORES + lax.axis_index("c")


def _sc_gather(table, idx):
    n = idx.shape[0]
    width = table.shape[1]
    workers = SC_CORES * SC_SUBCORES
    per_worker = n // workers
    n_win = per_worker // SC_WINDOW
    assert n == workers * n_win * SC_WINDOW

    @functools.partial(
        pl.kernel, mesh=_sc_mesh(),
        out_type=jax.ShapeDtypeStruct((n, width), table.dtype),
        scratch_types=[
            pltpu.VMEM((n_win, SC_WINDOW), jnp.int32),
            pltpu.VMEM((SC_WINDOW, width), table.dtype),
            pltpu.SemaphoreType.DMA,
        ],
    )
    def gather(table_hbm, idx_hbm, out_hbm, idx_v, rows_v, sem):
        wid = _sc_worker_id()
        pltpu.sync_copy(idx_hbm.at[wid], idx_v)

        @pl.loop(0, n_win)
        def _(j):
            pltpu.async_copy(table_hbm.at[idx_v.at[j]], rows_v, sem).wait()
            pltpu.sync_copy(rows_v, out_hbm.at[pl.ds(wid * per_worker + j * SC_WINDOW, SC_WINDOW)])

    return gather(table, idx.reshape(workers, n_win, SC_WINDOW))


def _sc_dispatch(rows, dest_kt, n_slots):
    T, width = rows.shape
    K = dest_kt.shape[0]
    workers = SC_CORES * SC_SUBCORES
    per_worker = T // workers
    n_win = per_worker // SC_WINDOW
    assert T == workers * n_win * SC_WINDOW
    dest_w = dest_kt.reshape(K, workers, n_win, SC_WINDOW).transpose(1, 2, 0, 3)
    dest_w = dest_w.reshape(workers, n_win * K, SC_WINDOW)

    @functools.partial(
        pl.kernel, mesh=_sc_mesh(),
        out_type=jax.ShapeDtypeStruct((n_slots, width), rows.dtype),
        scratch_types=[
            pltpu.VMEM((n_win * K, SC_WINDOW), jnp.int32),
            pltpu.VMEM((SC_WINDOW, width), rows.dtype),
        ],
    )
    def dispatch(rows_hbm, dest_hbm, out_hbm, idx_v, rows_v):
        wid = _sc_worker_id()
        pltpu.sync_copy(dest_hbm.at[wid], idx_v)

        @pl.loop(0, n_win)
        def _(j):
            pltpu.sync_copy(rows_hbm.at[pl.ds(wid * per_worker + j * SC_WINDOW, SC_WINDOW)], rows_v)
            for k in range(K):
                pltpu.sync_copy(rows_v, out_hbm.at[idx_v.at[j * K + k]])

    return dispatch(rows, dest_w)


def _swiglu(h_lo, h_hi, wg_ref, wu_ref):
    def proj(w_ref):
        return (jnp.dot(h_lo, w_ref[:HALF, :], preferred_element_type=F32)
                + jnp.dot(h_hi, w_ref[HALF:, :], preferred_element_type=F32))
    gate = proj(wg_ref)
    up = proj(wu_ref)
    return (gate * jax.nn.sigmoid(gate) * up).astype(BF16)


def _expert_kernel(seg_first_ref, seg_blocks_ref, n_real_ref, xs_hbm, wg_ref, wu_ref, wd_ref, ys_hbm,
                   xbuf, ybuf, wg_bf, wu_bf, wd_bf, in_sems, out_sems):
    e = pl.program_id(0)
    first = seg_first_ref[e]
    count = seg_blocks_ref[e]
    total = n_real_ref[0]
    blk = DISPATCH_BLOCK
    ring = xbuf.shape[0]

    def in_copy(g):
        slot = g % ring
        return pltpu.make_async_copy(xs_hbm.at[pl.ds(g * blk, blk)], xbuf.at[slot], in_sems.at[slot])

    def out_copy(g):
        slot = g % ring
        return pltpu.make_async_copy(ybuf.at[slot], ys_hbm.at[pl.ds(g * blk, blk)], out_sems.at[slot])

    @pl.when(e == 0)
    def _():
        for g in range(ring - 1):
            @pl.when(g < total)
            def _():
                in_copy(g).start()

    @pl.when(count > 0)
    def _():
        wg_bf[...] = wg_ref[0, 0].astype(BF16)
        wu_bf[...] = wu_ref[0, 0].astype(BF16)
        wd_bf[...] = wd_ref[0, 0].astype(BF16)

    def block(g, _):
        slot = g % ring
        in_copy(g).wait()

        @pl.when(g + ring - 1 < total)
        def _():
            in_copy(g + ring - 1).start()

        @pl.when(g >= ring)
        def _():
            out_copy(g - ring).wait()

        lo, hi = _unpack_bf16_pairs(xbuf[slot])
        a = _swiglu(lo.astype(BF16), hi.astype(BF16), wg_bf, wu_bf)
        y = jnp.dot(a, wd_bf[...], preferred_element_type=F32)
        ybuf[slot] = _pack_bf16_pairs(y.astype(BF16))
        out_copy(g).start()
        return 0

    lax.fori_loop(first, first + count, block, 0)

    @pl.when(e == pl.num_programs(0) - 1)
    def _():
        for back in range(ring, 0, -1):
            @pl.when(total >= back)
            def _():
                out_copy(total - back).wait()


def _expert_mm(seg_first, seg_blocks, n_real, xs, wg, wu, wd, layer):
    P = xs.shape[0]
    D = wg.shape[2]
    w_map = lambda e, sf, sb, nr: (layer, e, 0, 0)
    grid_spec = pltpu.PrefetchScalarGridSpec(
        num_scalar_prefetch=3,
        grid=(N_EXPERTS,),
        in_specs=[
            pl.BlockSpec(memory_space=pl.ANY),
            pl.BlockSpec((1, 1, D, D_EXPERT), w_map),
            pl.BlockSpec((1, 1, D, D_EXPERT), w_map),
            pl.BlockSpec((1, 1, D_EXPERT, D), w_map),
        ],
        out_specs=pl.BlockSpec(memory_space=pl.ANY),
        scratch_shapes=[
            pltpu.VMEM((EXPERT_RING, DISPATCH_BLOCK, HALF), jnp.int32),
            pltpu.VMEM((EXPERT_RING, DISPATCH_BLOCK, HALF), jnp.int32),
            pltpu.VMEM((D, D_EXPERT), BF16), pltpu.VMEM((D, D_EXPERT), BF16), pltpu.VMEM((D_EXPERT, D), BF16),
            pltpu.SemaphoreType.DMA((EXPERT_RING,)), pltpu.SemaphoreType.DMA((EXPERT_RING,)),
        ],
    )
    return pl.pallas_call(
        _expert_kernel,
        grid_spec=grid_spec,
        out_shape=jax.ShapeDtypeStruct((P, HALF), jnp.int32),
        compiler_params=_cparams(("arbitrary",)),
        name="expert_mm",
    )(seg_first, seg_blocks, n_real, xs, wg, wu, wd)


def _rank_kernel(idx_ref, rank_ref, cnt_ref, tri_ref, carry_ref):
    i = pl.program_id(0)
    tm = idx_ref.shape[1]

    @pl.when(i == 0)
    def _():
        r = lax.broadcasted_iota(jnp.int32, (tm, tm), 0)
        c = lax.broadcasted_iota(jnp.int32, (tm, tm), 1)
        tri_ref[...] = (r < c).astype(BF16)
        carry_ref[...] = jnp.zeros(carry_ref.shape, F32)

    e_iota = lax.broadcasted_iota(jnp.int32, (N_EXPERTS, tm), 0)
    hits = [e_iota == idx_ref[k:k + 1, :] for k in range(TOP_K)]
    mask = functools.reduce(jnp.logical_or, hits).astype(F32)
    rank = jnp.dot(mask.astype(BF16), tri_ref[...], preferred_element_type=F32) + carry_ref[:, 0:1]
    for k in range(TOP_K):
        rank_ref[k:k + 1, :] = jnp.sum(jnp.where(hits[k], rank, 0.0), axis=0, keepdims=True).astype(jnp.int32)
    carry_ref[...] = carry_ref[...] + jnp.sum(mask, axis=1, keepdims=True)
    cnt_ref[...] = carry_ref[...]


def _rank(idx_kt):
    K, T = idx_kt.shape
    tm = 1024
    return pl.pallas_call(
        _rank_kernel,
        grid=(T // tm,),
        in_specs=[pl.BlockSpec((K, tm), lambda i: (0, i))],
        out_specs=[pl.BlockSpec((K, tm), lambda i: (0, i)),
                   pl.BlockSpec((N_EXPERTS, LANES), lambda i: (0, 0))],
        out_shape=[jax.ShapeDtypeStruct((K, T), jnp.int32),
                   jax.ShapeDtypeStruct((N_EXPERTS, LANES), F32)],
        scratch_shapes=[pltpu.VMEM((tm, tm), BF16), pltpu.VMEM((N_EXPERTS, LANES), F32)],
        compiler_params=_cparams(("arbitrary",)),
        name="rank",
    )(idx_kt)


def _dest_kernel(idx_ref, rank_ref, start_ref, dest_ref):
    tm = idx_ref.shape[1]
    e_iota = lax.broadcasted_iota(jnp.int32, (N_EXPERTS, tm), 0)
    start = start_ref[:, 0:1]
    for k in range(TOP_K):
        base = jnp.sum(jnp.where(e_iota == idx_ref[k:k + 1, :], start, 0.0), axis=0, keepdims=True)
        dest_ref[k:k + 1, :] = rank_ref[k:k + 1, :] + base.astype(jnp.int32)


def _dest(idx_kt, rank_kt, start):
    K, T = idx_kt.shape
    tm = 1024
    tok_spec = pl.BlockSpec((K, tm), lambda i: (0, i))
    return pl.pallas_call(
        _dest_kernel,
        grid=(T // tm,),
        in_specs=[tok_spec, tok_spec, pl.BlockSpec((N_EXPERTS, LANES), lambda i: (0, 0))],
        out_specs=tok_spec,
        out_shape=jax.ShapeDtypeStruct((K, T), jnp.int32),
        compiler_params=_cparams(("arbitrary",)),
        name="dest",
    )(idx_kt, rank_kt, start)


def _block_plan(counts):
    blk = DISPATCH_BLOCK
    seg_blocks = (counts.astype(jnp.int32) + blk - 1) // blk
    seg_end = jnp.cumsum(seg_blocks)
    seg_first = seg_end - seg_blocks
    return seg_first * blk, seg_first, seg_blocks, seg_end[-1:]


def _combine_kernel(h_ref, x1_ref, yk_ref, g_ref, mod_ref, wsg_ref, wsu_ref, wsd_ref, lng_ref, lnb_ref, o_ref):
    lo, hi = _unpack_bf16_pairs(h_ref[...])
    a = _swiglu(lo.astype(BF16), hi.astype(BF16), wsg_ref, wsu_ref)
    ffn = jnp.dot(a, wsd_ref[...], preferred_element_type=F32)
    r_lo = jnp.zeros((h_ref.shape[0], HALF), F32)
    r_hi = jnp.zeros((h_ref.shape[0], HALF), F32)
    for k in range(TOP_K):
        y_lo, y_hi = _unpack_bf16_pairs(yk_ref[k])
        g = g_ref[:, k:k + 1]
        r_lo += g * y_lo
        r_hi += g * y_hi
    ffn += jnp.concatenate([r_lo, r_hi], axis=1)
    g2 = mod_ref[0, 5:6, :]
    o_ref[...] = _layer_norm(DEEPNORM_ALPHA * x1_ref[...] + g2 * ffn, lng_ref[...], lnb_ref[...])


def _combine(h2p, x1, yk, gates, mod, wsg, wsu, wsd, lng, lnb, seq):
    T, D = x1.shape
    tm = TM_COMBINE
    per_seq = seq // tm
    const = lambda i: (0, 0)
    return pl.pallas_call(
        _combine_kernel,
        grid=(T // tm,),
        in_specs=[
            pl.BlockSpec((tm, HALF), lambda i: (i, 0)),
            pl.BlockSpec((tm, D), lambda i: (i, 0)),
            pl.BlockSpec((TOP_K, tm, HALF), lambda i: (0, i, 0)),
            pl.BlockSpec((tm, TOP_K), lambda i: (i, 0)),
            pl.BlockSpec((1, 6, D), lambda i: (i // per_seq, 0, 0)),
            pl.BlockSpec(wsg.shape, const),
            pl.BlockSpec(wsu.shape, const),
            pl.BlockSpec(wsd.shape, const),
            pl.BlockSpec(lng.shape, const),
            pl.BlockSpec(lnb.shape, const),
        ],
        out_specs=pl.BlockSpec((tm, D), lambda i: (i, 0)),
        out_shape=jax.ShapeDtypeStruct((T, D), F32),
        compiler_params=_cparams(("arbitrary",)),
        name="combine",
    )(h2p, x1, yk, gates, mod, wsg, wsu, wsd, lng, lnb)


def _rope_pair(w):
    half = QK_ROPE // 2
    return w, jnp.concatenate([-w[:, half:], w[:, :half]], axis=1)


def _prep_layer(w_in, w_uq, w_ukv, w_pool, w_out, w_router):
    D = w_in.shape[0]
    zpad = lambda r: jnp.zeros((r, LANES - QK_ROPE), F32)
    lat = POOL_WIDTH + Q_LORA + KV_LORA
    k_a, k_b = _rope_pair(w_in[:, lat:lat + QK_ROPE])
    w_in_ext = jnp.concatenate([w_in[:, :lat], k_a, zpad(D), k_b, zpad(D)], axis=1).astype(BF16)

    qa_cols, qb_cols = [], []
    per_head = QK_NOPE + QK_ROPE
    for hd in range(N_HEADS):
        w_h = w_uq[:, hd * per_head:(hd + 1) * per_head]
        r_a, r_b = _rope_pair(w_h[:, QK_NOPE:])
        qa_cols += [w_h[:, :QK_NOPE], r_a, zpad(Q_LORA)]
        qb_cols += [r_b, zpad(Q_LORA)]
    w_uqa = jnp.concatenate(qa_cols, axis=1).astype(BF16)
    w_uqb = jnp.concatenate(qb_cols, axis=1).astype(BF16)

    wpool_bd = jax.scipy.linalg.block_diag(*[w_pool[g] for g in range(len(POOL_WINDOWS))]).astype(BF16)
    wr_hi, wr_lo = _split_hi_lo(w_router.T)
    return dict(w_in_ext=w_in_ext, w_uqa=w_uqa, w_uqb=w_uqb, w_ukv=w_ukv.astype(BF16), wpool_bd=wpool_bd,
                wout_p=w_out[:POOL_WIDTH].astype(BF16), wout_a=w_out[POOL_WIDTH:].astype(BF16),
                wr_hi=wr_hi, wr_lo=wr_lo)


def kernel(x, c, positions, w_ada, b_ada, w_in, q_norm_g, kv_norm_g, w_uq, w_ukv, w_pool, pool_scale,
           w_out, ln1_g, ln1_b, w_router, router_bias, w_exp_gate, w_exp_up, w_exp_down,
           w_sh_gate, w_sh_up, w_sh_down, ln2_g, ln2_b):
    B, S, D = x.shape
    L = w_in.shape[0]
    row = lambda v: v.reshape(1, -1)

    mod_all = _ada_mod(c, w_ada, b_ada).reshape(L, B, 6, D)
    cs_all = _rope_table(positions)
    bs = B // N_STREAMS
    T = bs * S
    n_blocks = T * TOP_K // DISPATCH_BLOCK + N_EXPERTS
    xs_streams = [x[s * bs:(s + 1) * bs] for s in range(N_STREAMS)]
    for l in range(L):
        p = _prep_layer(w_in[l], w_uq[l], w_ukv[l], w_pool[l], w_out[l], w_router[l])
        shared = (w_sh_gate[l].astype(BF16), w_sh_up[l].astype(BF16), w_sh_down[l].astype(BF16))
        for s in range(N_STREAMS):
            xl = xs_streams[s]
            mod = mod_all[l, s * bs:(s + 1) * bs]
            cs = cs_all[s * bs:(s + 1) * bs]
            u, q, k, v = _in_proj(xl, mod, cs, p["w_in_ext"], row(q_norm_g[l]), row(kv_norm_g[l]),
                                  p["w_uqa"], p["w_uqb"], p["w_ukv"])
            attn = _attention(q, k, v)
            x1, h2p, lt = _post_attn(xl, u, attn, mod, p["wpool_bd"], row(pool_scale[l]), p["wout_p"],
                                     p["wout_a"], row(ln1_g[l]), row(ln1_b[l]), p["wr_hi"], p["wr_lo"])
            idx_k, gate_k = _route(lt.reshape(N_EXPERTS, T // LANES, LANES), router_bias[l])
            idx_kt = idx_k.reshape(TOP_K, T)
            gates = gate_k.reshape(TOP_K, T).T
            rank_kt, counts = _rank(idx_kt)
            pad_start, seg_first, seg_blocks, n_real = _block_plan(counts[:, 0])
            start = jnp.broadcast_to(pad_start.astype(F32)[:, None], (N_EXPERTS, LANES))
            dest_kt = _dest(idx_kt, rank_kt, start)
            h2p = h2p.reshape(T, HALF)
            rows = _sc_dispatch(h2p, dest_kt, n_blocks * DISPATCH_BLOCK)
            ys = _expert_mm(seg_first, seg_blocks, n_real, rows, w_exp_gate, w_exp_up, w_exp_down, l)
            yk = _sc_gather(ys, dest_kt.reshape(-1)).reshape(TOP_K, T, HALF)
            xs_streams[s] = _combine(h2p, x1.reshape(T, D), yk, gates, mod, *shared,
                                     row(ln2_g[l]), row(ln2_b[l]), S).reshape(bs, S, D)
    return jnp.concatenate(xs_streams, axis=0)
```

```python
import functools

import jax
import jax.numpy as jnp
from jax import lax
from jax.experimental import pallas as pl
from jax.experimental.pallas import tpu as pltpu
from jax.experimental.pallas import tpu_sc as plsc

F32 = jnp.float32
BF16 = jnp.bfloat16

D_MODEL = 1024
DEPTH = 4
POOL_WIDTH = 256
POOL_WINDOWS = (2, 4, 8, 16)
POOL_GROUP = 64
POOL_HALO = 16
QK_NOPE = 128
QK_ROPE = 64
V_HEAD = 128
N_HEADS = 6
Q_LORA = 384
KV_LORA = 256
ROPE_THETA = 10000.0
N_EXPERTS = 64
TOP_K = 8
N_GROUPS = 8
GROUP_SIZE = N_EXPERTS // N_GROUPS
TOPK_GROUPS = 4
D_EXPERT = 256
ROUTED_SCALE = 2.5
DEEPNORM_ALPHA = (2.0 * DEPTH) ** 0.25
LN_EPS = 1e-5
RMS_EPS = 1e-6
ATTN_SCALE = (QK_NOPE + QK_ROPE) ** -0.5
LOG2_E = 1.4426950408889634
Q_SCALE = ATTN_SCALE * LOG2_E
LANES = 128
QK_PAD = 2 * LANES
MASK_VALUE = -1e30

TM_PROJ = 512
TQ = 512
TK = 512
ROUTE_ROWS = 8
DISPATCH_BLOCK = 512
EXPERT_RING = 4
TM_COMBINE = 512
SC_CORES = 2
SC_SUBCORES = 16
SC_WINDOW = 64
N_STREAMS = 2
HALF = D_MODEL // 2
VMEM_LIMIT = 48 * 1024 * 1024


def _cparams(sem):
    return pltpu.CompilerParams(dimension_semantics=sem, vmem_limit_bytes=VMEM_LIMIT)


def _split_hi_lo(a):
    hi = a.astype(BF16)
    lo = (a - hi.astype(F32)).astype(BF16)
    return hi, lo


def _pack_bf16_pairs(a):
    bits = lax.bitcast_convert_type(a.astype(F32), jnp.uint32)
    half = a.shape[1] // 2
    word = (bits[:, :half] >> 16) | (bits[:, half:] & jnp.uint32(0xFFFF0000))
    return lax.bitcast_convert_type(word, jnp.int32)


def _unpack_bf16_pairs(w):
    bits = lax.bitcast_convert_type(w, jnp.uint32)
    lo = lax.bitcast_convert_type(bits << 16, F32)
    hi = lax.bitcast_convert_type(bits & jnp.uint32(0xFFFF0000), F32)
    return lo, hi


def _ada_kernel(c_ref, w_ref, b_ref, o_ref):
    c = c_ref[...]
    cond = c * jax.nn.sigmoid(c)
    c_hi, c_lo = _split_hi_lo(cond)
    w_hi, w_lo = _split_hi_lo(w_ref[0])
    acc = jnp.dot(c_hi, w_hi, preferred_element_type=F32)
    acc += jnp.dot(c_lo, w_hi, preferred_element_type=F32)
    acc += jnp.dot(c_hi, w_lo, preferred_element_type=F32)
    o_ref[0] = acc + b_ref[0]


def _ada_mod(c, w_ada, b_ada):
    L, D, N = w_ada.shape
    B = c.shape[0]
    tn = 1536
    return pl.pallas_call(
        _ada_kernel,
        grid=(L, N // tn),
        in_specs=[
            pl.BlockSpec((B, D), lambda l, j: (0, 0)),
            pl.BlockSpec((1, D, tn), lambda l, j: (l, 0, j)),
            pl.BlockSpec((1, 1, tn), lambda l, j: (l, 0, j)),
        ],
        out_specs=pl.BlockSpec((1, B, tn), lambda l, j: (l, 0, j)),
        out_shape=jax.ShapeDtypeStruct((L, B, N), F32),
        compiler_params=_cparams(("arbitrary", "arbitrary")),
        name="ada_mod",
    )(c, w_ada, b_ada.reshape(L, 1, N))


def _rope_table_kernel(pos_ref, freq_ref, o_ref):
    ang = pos_ref[0].astype(F32) * freq_ref[...]
    o_ref[0, :, 0:LANES] = jnp.cos(ang)
    o_ref[0, :, LANES:2 * LANES] = jnp.sin(ang)


def _rope_table(positions):
    B, S = positions.shape
    half = QK_ROPE // 2
    inv_freq = ROPE_THETA ** (-jnp.arange(half, dtype=F32) / half)
    freq = jnp.tile(inv_freq, LANES // half).reshape(1, LANES)
    return pl.pallas_call(
        _rope_table_kernel,
        grid=(B,),
        in_specs=[
            pl.BlockSpec((1, S, 1), lambda b: (b, 0, 0)),
            pl.BlockSpec((1, LANES), lambda b: (0, 0)),
        ],
        out_specs=pl.BlockSpec((1, S, 2 * LANES), lambda b: (b, 0, 0)),
        out_shape=jax.ShapeDtypeStruct((B, S, 2 * LANES), F32),
        compiler_params=_cparams(("arbitrary",)),
        name="rope_table",
    )(positions.reshape(B, S, 1), freq)


def _rms(x, g):
    return x * lax.rsqrt(jnp.mean(x * x, axis=-1, keepdims=True) + RMS_EPS) * g


def _in_proj_kernel(x_ref, mod_ref, cs_ref, w_in_ref, qg_ref, kvg_ref, w_uqa_ref, w_uqb_ref, w_ukv_ref,
                    u_ref, q_ref, k_ref, v_ref):
    x = x_ref[0]
    sh1 = mod_ref[0, 0:1, :]
    sc1 = mod_ref[0, 1:2, :]
    h = (x * (1.0 + sc1) + sh1).astype(BF16)
    proj = jnp.dot(h, w_in_ref[...], preferred_element_type=F32)
    u_ref[0] = proj[:, 0:POOL_WIDTH]
    o = POOL_WIDTH
    q_lat = proj[:, o:o + Q_LORA]
    o += Q_LORA
    kv_lat = proj[:, o:o + KV_LORA]
    o += KV_LORA
    k_a = proj[:, o:o + LANES]
    k_b = proj[:, o + LANES:o + 2 * LANES]
    cosv = cs_ref[0, :, 0:LANES]
    sinv = cs_ref[0, :, LANES:2 * LANES]
    k_rot = (k_a * cosv + k_b * sinv).astype(BF16)

    qn = _rms(q_lat, qg_ref[...]).astype(BF16)
    q_a = jnp.dot(qn, w_uqa_ref[...], preferred_element_type=F32)
    q_b = jnp.dot(qn, w_uqb_ref[...], preferred_element_type=F32)
    kvn = _rms(kv_lat, kvg_ref[...]).astype(BF16)
    kv = jnp.dot(kvn, w_ukv_ref[...], preferred_element_type=F32)
    for hd in range(N_HEADS):
        b0 = hd * QK_PAD
        q_ref[0, hd, :, 0:LANES] = (q_a[:, b0:b0 + LANES] * Q_SCALE).astype(BF16)
        q_rot = q_a[:, b0 + LANES:b0 + QK_PAD] * cosv + q_b[:, hd * LANES:(hd + 1) * LANES] * sinv
        q_ref[0, hd, :, LANES:QK_PAD] = (q_rot * Q_SCALE).astype(BF16)
        k_ref[0, hd, :, 0:LANES] = kv[:, b0:b0 + QK_NOPE].astype(BF16)
        k_ref[0, hd, :, LANES:QK_PAD] = k_rot
        v_ref[0, hd] = kv[:, b0 + QK_NOPE:b0 + QK_NOPE + V_HEAD].astype(BF16)


def _in_proj(x, mod, cs, w_in_ext, qg, kvg, w_uqa, w_uqb, w_ukv_bf, B, x_off, b_off):
    _, S, D = x.shape
    tm = TM_PROJ
    const = lambda b, i: (0, 0)
    return pl.pallas_call(
        _in_proj_kernel,
        grid=(B, S // tm),
        in_specs=[
            pl.BlockSpec((1, tm, D), lambda b, i: (b + x_off, i, 0)),
            pl.BlockSpec((1, 6, D), lambda b, i: (b + b_off, 0, 0)),
            pl.BlockSpec((1, tm, 2 * LANES), lambda b, i: (b + b_off, i, 0)),
            pl.BlockSpec(w_in_ext.shape, const),
            pl.BlockSpec(qg.shape, const),
            pl.BlockSpec(kvg.shape, const),
            pl.BlockSpec(w_uqa.shape, const),
            pl.BlockSpec(w_uqb.shape, const),
            pl.BlockSpec(w_ukv_bf.shape, const),
        ],
        out_specs=[
            pl.BlockSpec((1, tm, POOL_WIDTH), lambda b, i: (b, i, 0)),
            pl.BlockSpec((1, N_HEADS, tm, QK_PAD), lambda b, i: (b, 0, i, 0)),
            pl.BlockSpec((1, N_HEADS, tm, QK_PAD), lambda b, i: (b, 0, i, 0)),
            pl.BlockSpec((1, N_HEADS, tm, V_HEAD), lambda b, i: (b, 0, i, 0)),
        ],
        out_shape=[
            jax.ShapeDtypeStruct((B, S, POOL_WIDTH), F32),
            jax.ShapeDtypeStruct((B, N_HEADS, S, QK_PAD), BF16),
            jax.ShapeDtypeStruct((B, N_HEADS, S, QK_PAD), BF16),
            jax.ShapeDtypeStruct((B, N_HEADS, S, V_HEAD), BF16),
        ],
        compiler_params=_cparams(("arbitrary", "arbitrary")),
        name="in_proj",
    )(x, mod, cs, w_in_ext, qg, kvg, w_uqa, w_uqb, w_ukv_bf)


def _softmax_step(q, k, v, carry, mask):
    m, l, acc = carry
    s = lax.dot_general(q, k, (((1,), (1,)), ((), ())), preferred_element_type=F32)
    if mask is not None:
        s = jnp.where(mask, s, MASK_VALUE)
    m_new = jnp.maximum(m, jnp.max(s, axis=-1, keepdims=True))
    alpha = jnp.exp2(m - m_new)
    p = jnp.exp2(s - m_new)
    l_new = alpha * l + jnp.sum(p, axis=-1, keepdims=True)
    acc_new = alpha * acc + jnp.dot(p.astype(BF16), v, preferred_element_type=F32)
    return m_new, l_new, acc_new


def _attn_kernel(q_ref, k_ref, v_ref, o_ref):
    seq = q_ref.shape[2]
    row = lax.broadcasted_iota(jnp.int32, (TQ, TK), 0)
    col = lax.broadcasted_iota(jnp.int32, (TQ, TK), 1)
    diag = row >= col
    for i in range(seq // TQ):
        q = q_ref[0, 0, i * TQ:(i + 1) * TQ, :]
        carry = (jnp.full((TQ, 1), MASK_VALUE, F32), jnp.zeros((TQ, 1), F32), jnp.zeros((TQ, V_HEAD), F32))
        for j in range(i + 1):
            k = k_ref[0, 0, j * TK:(j + 1) * TK, :]
            v = v_ref[0, 0, j * TK:(j + 1) * TK, :]
            carry = _softmax_step(q, k, v, carry, diag if j == i else None)
        _, l, acc = carry
        o_ref[0, 0, i * TQ:(i + 1) * TQ, :] = (acc / l).astype(BF16)


def _attention(q, k, v):
    B, H, S, _ = q.shape
    return pl.pallas_call(
        _attn_kernel,
        grid=(B, H),
        in_specs=[
            pl.BlockSpec((1, 1, S, QK_PAD), lambda b, h: (b, h, 0, 0)),
            pl.BlockSpec((1, 1, S, QK_PAD), lambda b, h: (b, h, 0, 0)),
            pl.BlockSpec((1, 1, S, V_HEAD), lambda b, h: (b, h, 0, 0)),
        ],
        out_specs=pl.BlockSpec((1, 1, S, V_HEAD), lambda b, h: (b, h, 0, 0)),
        out_shape=jax.ShapeDtypeStruct((B, H, S, V_HEAD), BF16),
        compiler_params=_cparams(("arbitrary", "arbitrary")),
        name="attention",
    )(q, k, v)


def _layer_norm(v, g, b):
    mu = jnp.mean(v, axis=-1, keepdims=True)
    d = v - mu
    var = jnp.mean(d * d, axis=-1, keepdims=True)
    return d * lax.rsqrt(var + LN_EPS) * g + b


def _post_kernel(x_ref, u_ref, halo_ref, attn_ref, mod_ref, wpool_ref, pscale_ref, wout_p_ref, wout_a_ref,
                 lng_ref, lnb_ref, wr_hi_ref, wr_lo_ref, x1_ref, h2_ref, lt_ref):
    i = pl.program_id(1)
    tm = u_ref.shape[1]
    u = u_ref[0]
    halo = jnp.where(i > 0, halo_ref[0], 0.0)
    ext = jnp.concatenate([halo, u], axis=0)
    s2 = ext + pltpu.roll(ext, 1, 0)
    s4 = s2 + pltpu.roll(s2, 2, 0)
    s8 = s4 + pltpu.roll(s4, 4, 0)
    s16 = s8 + pltpu.roll(s8, 8, 0)
    lane = lax.broadcasted_iota(jnp.int32, (1, POOL_WIDTH), 1)
    win = jnp.where(lane < POOL_GROUP, s2,
                    jnp.where(lane < 2 * POOL_GROUP, s4, jnp.where(lane < 3 * POOL_GROUP, s8, s16)))
    win = win[POOL_HALO:, :]
    width = jnp.where(lane < POOL_GROUP, POOL_WINDOWS[0],
                      jnp.where(lane < 2 * POOL_GROUP, POOL_WINDOWS[1],
                                jnp.where(lane < 3 * POOL_GROUP, POOL_WINDOWS[2], POOL_WINDOWS[3])))
    t = i * tm + lax.broadcasted_iota(jnp.int32, (tm, 1), 0)
    count = jnp.minimum(t + 1, width).astype(F32)
    token_mix = win / count - u
    pooled = jnp.dot(token_mix.astype(BF16), wpool_ref[...], preferred_element_type=F32) * pscale_ref[...]
    mixed = jnp.dot(pooled.astype(BF16), wout_p_ref[...], preferred_element_type=F32)
    attn = jnp.concatenate([attn_ref[0, hd] for hd in range(N_HEADS)], axis=1)
    mixed += jnp.dot(attn, wout_a_ref[...], preferred_element_type=F32)

    g1 = mod_ref[0, 2:3, :]
    sh2 = mod_ref[0, 3:4, :]
    sc2 = mod_ref[0, 4:5, :]
    x1 = _layer_norm(DEEPNORM_ALPHA * x_ref[0] + g1 * mixed, lng_ref[...], lnb_ref[...])
    x1_ref[0] = x1
    h2 = x1 * (1.0 + sc2) + sh2
    h_hi, h_lo = _split_hi_lo(h2)
    h2_ref[0] = _pack_bf16_pairs(h_hi)
    nt = (((1,), (1,)), ((), ()))
    lt = lax.dot_general(wr_hi_ref[...], h_hi, nt, preferred_element_type=F32)
    lt += lax.dot_general(wr_hi_ref[...], h_lo, nt, preferred_element_type=F32)
    lt += lax.dot_general(wr_lo_ref[...], h_hi, nt, preferred_element_type=F32)
    lt_ref[...] = lt


def _post_attn(x, u, attn, mod, wpool_bd, pscale, wout_p, wout_a, lng, lnb, wr_hi, wr_lo, x_off, b_off):
    B, S, _ = u.shape
    D = x.shape[2]
    tm = TM_PROJ
    nt = S // tm
    const = lambda b, i: (0, 0)
    halo_blocks = tm // POOL_HALO
    return pl.pallas_call(
        _post_kernel,
        grid=(B, nt),
        in_specs=[
            pl.BlockSpec((1, tm, D), lambda b, i: (b + x_off, i, 0)),
            pl.BlockSpec((1, tm, POOL_WIDTH), lambda b, i: (b, i, 0)),
            pl.BlockSpec((1, POOL_HALO, POOL_WIDTH), lambda b, i: (b, jnp.maximum(i * halo_blocks - 1, 0), 0)),
            pl.BlockSpec((1, N_HEADS, tm, V_HEAD), lambda b, i: (b, 0, i, 0)),
            pl.BlockSpec((1, 6, D), lambda b, i: (b + b_off, 0, 0)),
            pl.BlockSpec(wpool_bd.shape, const),
            pl.BlockSpec(pscale.shape, const),
            pl.BlockSpec(wout_p.shape, const),
            pl.BlockSpec(wout_a.shape, const),
            pl.BlockSpec(lng.shape, const),
            pl.BlockSpec(lnb.shape, const),
            pl.BlockSpec(wr_hi.shape, const),
            pl.BlockSpec(wr_lo.shape, const),
        ],
        out_specs=[
            pl.BlockSpec((1, tm, D), lambda b, i: (b, i, 0)),
            pl.BlockSpec((1, tm, HALF), lambda b, i: (b, i, 0)),
            pl.BlockSpec((N_EXPERTS, tm), lambda b, i: (0, b * nt + i)),
        ],
        out_shape=[
            jax.ShapeDtypeStruct((B, S, D), F32),
            jax.ShapeDtypeStruct((B, S, HALF), jnp.int32),
            jax.ShapeDtypeStruct((N_EXPERTS, B * S), F32),
        ],
        compiler_params=_cparams(("arbitrary", "arbitrary")),
        name="post_attn",
    )(x, u, u, attn, mod, wpool_bd, pscale, wout_p, wout_a, lng, lnb, wr_hi, wr_lo)


def _select_first_max(vals, n_rounds, payload=None):
    work = list(vals)
    sel = [None] * len(vals)
    rounds = []
    for _ in range(n_rounds):
        m = functools.reduce(jnp.maximum, work)
        taken = None
        win_idx = jnp.zeros(m.shape, jnp.int32)
        win_val = jnp.zeros(m.shape, F32)
        for e in range(len(work)):
            hit = work[e] == m
            first = hit if taken is None else jnp.logical_and(hit, jnp.logical_not(taken))
            taken = hit if taken is None else jnp.logical_or(taken, hit)
            sel[e] = first if sel[e] is None else jnp.logical_or(sel[e], first)
            work[e] = jnp.where(first, -jnp.inf, work[e])
            if payload is not None:
                win_idx = jnp.where(first, e, win_idx)
                win_val = jnp.where(first, payload[e], win_val)
        rounds.append((win_idx, win_val))
    return sel, rounds


def _route_kernel(bias_ref, lt_ref, idx_ref, gate_ref):
    scores = [jax.nn.sigmoid(lt_ref[e]) for e in range(N_EXPERTS)]
    choice = [scores[e] + bias_ref[e] for e in range(N_EXPERTS)]
    group_score = []
    for g in range(N_GROUPS):
        vals = choice[g * GROUP_SIZE:(g + 1) * GROUP_SIZE]
        sel2, _ = _select_first_max(vals, 2)
        group_score.append(functools.reduce(
            jnp.add, [jnp.where(sel2[j], vals[j], 0.0) for j in range(GROUP_SIZE)]))
    group_sel, _ = _select_first_max(group_score, TOPK_GROUPS)
    masked = [jnp.where(group_sel[e // GROUP_SIZE], choice[e], -jnp.inf) for e in range(N_EXPERTS)]
    _, rounds = _select_first_max(masked, TOP_K, payload=scores)
    total = functools.reduce(jnp.add, [w for _, w in rounds])
    for k, (e_k, w_k) in enumerate(rounds):
        idx_ref[k] = e_k
        gate_ref[k] = w_k / total * ROUTED_SCALE


def _route(lt3, bias):
    E, R, _ = lt3.shape
    out_spec = pl.BlockSpec((TOP_K, ROUTE_ROWS, LANES), lambda r: (0, r, 0))
    return pl.pallas_call(
        _route_kernel,
        grid=(R // ROUTE_ROWS,),
        in_specs=[
            pl.BlockSpec(memory_space=pltpu.SMEM),
            pl.BlockSpec((E, ROUTE_ROWS, LANES), lambda r: (0, r, 0)),
        ],
        out_specs=[out_spec, out_spec],
        out_shape=[jax.ShapeDtypeStruct((TOP_K, R, LANES), jnp.int32),
                   jax.ShapeDtypeStruct((TOP_K, R, LANES), F32)],
        compiler_params=_cparams(("arbitrary",)),
        name="route",
    )(bias, lt3)


def _sc_mesh():
    return plsc.VectorSubcoreMesh(core_axis_name="c", subcore_axis_name="s",
                                  num_cores=SC_CORES, num_subcores=SC_SUBCORES)


def _sc_worker_id():
    return lax.axis_index("s") * SC_CORES + lax.axis_index("c")


def _sc_gather(table, idx):
    n = idx.shape[0]
    width = table.shape[1]
    workers = SC_CORES * SC_SUBCORES
    per_worker = n // workers
    n_win = per_worker // SC_WINDOW
    assert n == workers * n_win * SC_WINDOW

    @functools.partial(
        pl.kernel, mesh=_sc_mesh(),
        out_type=jax.ShapeDtypeStruct((n, width), table.dtype),
        scratch_types=[
            pltpu.VMEM((n_win, SC_WINDOW), jnp.int32),
            pltpu.VMEM((SC_WINDOW, width), table.dtype),
            pltpu.SemaphoreType.DMA,
        ],
    )
    def gather(table_hbm, idx_hbm, out_hbm, idx_v, rows_v, sem):
        wid = _sc_worker_id()
        pltpu.sync_copy(idx_hbm.at[wid], idx_v)

        @pl.loop(0, n_win)
        def _(j):
            pltpu.async_copy(table_hbm.at[idx_v.at[j]], rows_v, sem).wait()
            pltpu.sync_copy(rows_v, out_hbm.at[pl.ds(wid * per_worker + j * SC_WINDOW, SC_WINDOW)])

    return gather(table, idx.reshape(workers, n_win, SC_WINDOW))


def _sc_dispatch(rows, dest_kt, n_slots):
    T, width = rows.shape
    K = dest_kt.shape[0]
    workers = SC_CORES * SC_SUBCORES
    per_worker = T // workers
    n_win = per_worker // SC_WINDOW
    assert T == workers * n_win * SC_WINDOW
    dest_w = dest_kt.reshape(K, workers, n_win, SC_WINDOW).transpose(1, 2, 0, 3)
    dest_w = dest_w.reshape(workers, n_win * K, SC_WINDOW)

    @functools.partial(
        pl.kernel, mesh=_sc_mesh(),
        out_type=jax.ShapeDtypeStruct((n_slots, width), rows.dtype),
        scratch_types=[
            pltpu.VMEM((n_win * K, SC_WINDOW), jnp.int32),
            pltpu.VMEM((SC_WINDOW, width), rows.dtype),
        ],
    )
    def dispatch(rows_hbm, dest_hbm, out_hbm, idx_v, rows_v):
        wid = _sc_worker_id()
        pltpu.sync_copy(dest_hbm.at[wid], idx_v)

        @pl.loop(0, n_win)
        def _(j):
            pltpu.sync_copy(rows_hbm.at[pl.ds(wid * per_worker + j * SC_WINDOW, SC_WINDOW)], rows_v)
            for k in range(K):
                pltpu.sync_copy(rows_v, out_hbm.at[idx_v.at[j * K + k]])

    return dispatch(rows, dest_w)


def _swiglu(h_lo, h_hi, wg_ref, wu_ref):
    def proj(w_ref):
        return (jnp.dot(h_lo, w_ref[:HALF, :], preferred_element_type=F32)
                + jnp.dot(h_hi, w_ref[HALF:, :], preferred_element_type=F32))
    gate = proj(wg_ref)
    up = proj(wu_ref)
    return (gate * jax.nn.sigmoid(gate) * up).astype(BF16)


def _expert_kernel(seg_first_ref, seg_blocks_ref, n_real_ref, xs_hbm, wg_ref, wu_ref, wd_ref, ys_hbm,
                   xbuf, ybuf, wg_bf, wu_bf, wd_bf, in_sems, out_sems):
    e = pl.program_id(0)
    first = seg_first_ref[e]
    count = seg_blocks_ref[e]
    total = n_real_ref[0]
    blk = DISPATCH_BLOCK
    ring = xbuf.shape[0]

    def in_copy(g):
        slot = g % ring
        return pltpu.make_async_copy(xs_hbm.at[pl.ds(g * blk, blk)], xbuf.at[slot], in_sems.at[slot])

    def out_copy(g):
        slot = g % ring
        return pltpu.make_async_copy(ybuf.at[slot], ys_hbm.at[pl.ds(g * blk, blk)], out_sems.at[slot])

    @pl.when(e == 0)
    def _():
        for g in range(ring - 1):
            @pl.when(g < total)
            def _():
                in_copy(g).start()

    @pl.when(count > 0)
    def _():
        wg_bf[...] = wg_ref[0, 0].astype(BF16)
        wu_bf[...] = wu_ref[0, 0].astype(BF16)
        wd_bf[...] = wd_ref[0, 0].astype(BF16)

    def block(g, _):
        slot = g % ring
        in_copy(g).wait()

        @pl.when(g + ring - 1 < total)
        def _():
            in_copy(g + ring - 1).start()

        @pl.when(g >= ring)
        def _():
            out_copy(g - ring).wait()

        lo, hi = _unpack_bf16_pairs(xbuf[slot])
        a = _swiglu(lo.astype(BF16), hi.astype(BF16), wg_bf, wu_bf)
        y = jnp.dot(a, wd_bf[...], preferred_element_type=F32)
        ybuf[slot] = _pack_bf16_pairs(y.astype(BF16))
        out_copy(g).start()
        return 0

    lax.fori_loop(first, first + count, block, 0)

    @pl.when(e == pl.num_programs(0) - 1)
    def _():
        for back in range(ring, 0, -1):
            @pl.when(total >= back)
            def _():
                out_copy(total - back).wait()


def _expert_mm(seg_first, seg_blocks, n_real, xs, wg, wu, wd, layer):
    P = xs.shape[0]
    D = wg.shape[2]
    w_map = lambda e, sf, sb, nr: (layer, e, 0, 0)
    grid_spec = pltpu.PrefetchScalarGridSpec(
        num_scalar_prefetch=3,
        grid=(N_EXPERTS,),
        in_specs=[
            pl.BlockSpec(memory_space=pl.ANY),
            pl.BlockSpec((1, 1, D, D_EXPERT), w_map),
            pl.BlockSpec((1, 1, D, D_EXPERT), w_map),
            pl.BlockSpec((1, 1, D_EXPERT, D), w_map),
        ],
        out_specs=pl.BlockSpec(memory_space=pl.ANY),
        scratch_shapes=[
            pltpu.VMEM((EXPERT_RING, DISPATCH_BLOCK, HALF), jnp.int32),
            pltpu.VMEM((EXPERT_RING, DISPATCH_BLOCK, HALF), jnp.int32),
            pltpu.VMEM((D, D_EXPERT), BF16), pltpu.VMEM((D, D_EXPERT), BF16), pltpu.VMEM((D_EXPERT, D), BF16),
            pltpu.SemaphoreType.DMA((EXPERT_RING,)), pltpu.SemaphoreType.DMA((EXPERT_RING,)),
        ],
    )
    return pl.pallas_call(
        _expert_kernel,
        grid_spec=grid_spec,
        out_shape=jax.ShapeDtypeStruct((P, HALF), jnp.int32),
        compiler_params=_cparams(("arbitrary",)),
        name="expert_mm",
    )(seg_first, seg_blocks, n_real, xs, wg, wu, wd)


def _rank_kernel(idx_ref, rank_ref, cnt_ref, tri_ref, carry_ref):
    i = pl.program_id(0)
    tm = idx_ref.shape[1]

    @pl.when(i == 0)
    def _():
        r = lax.broadcasted_iota(jnp.int32, (tm, tm), 0)
        c = lax.broadcasted_iota(jnp.int32, (tm, tm), 1)
        tri_ref[...] = (r < c).astype(BF16)
        carry_ref[...] = jnp.zeros(carry_ref.shape, F32)

    e_iota = lax.broadcasted_iota(jnp.int32, (N_EXPERTS, tm), 0)
    hits = [e_iota == idx_ref[k:k + 1, :] for k in range(TOP_K)]
    mask = functools.reduce(jnp.logical_or, hits).astype(F32)
    rank = jnp.dot(mask.astype(BF16), tri_ref[...], preferred_element_type=F32) + carry_ref[:, 0:1]
    for k in range(TOP_K):
        rank_ref[k:k + 1, :] = jnp.sum(jnp.where(hits[k], rank, 0.0), axis=0, keepdims=True).astype(jnp.int32)
    carry_ref[...] = carry_ref[...] + jnp.sum(mask, axis=1, keepdims=True)
    cnt_ref[...] = carry_ref[...]


def _rank(idx_kt):
    K, T = idx_kt.shape
    tm = 1024
    return pl.pallas_call(
        _rank_kernel,
        grid=(T // tm,),
        in_specs=[pl.BlockSpec((K, tm), lambda i: (0, i))],
        out_specs=[pl.BlockSpec((K, tm), lambda i: (0, i)),
                   pl.BlockSpec((N_EXPERTS, LANES), lambda i: (0, 0))],
        out_shape=[jax.ShapeDtypeStruct((K, T), jnp.int32),
                   jax.ShapeDtypeStruct((N_EXPERTS, LANES), F32)],
        scratch_shapes=[pltpu.VMEM((tm, tm), BF16), pltpu.VMEM((N_EXPERTS, LANES), F32)],
        compiler_params=_cparams(("arbitrary",)),
        name="rank",
    )(idx_kt)


def _dest_kernel(idx_ref, rank_ref, start_ref, dest_ref):
    tm = idx_ref.shape[1]
    e_iota = lax.broadcasted_iota(jnp.int32, (N_EXPERTS, tm), 0)
    start = start_ref[:, 0:1]
    for k in range(TOP_K):
        base = jnp.sum(jnp.where(e_iota == idx_ref[k:k + 1, :], start, 0.0), axis=0, keepdims=True)
        dest_ref[k:k + 1, :] = rank_ref[k:k + 1, :] + base.astype(jnp.int32)


def _dest(idx_kt, rank_kt, start):
    K, T = idx_kt.shape
    tm = 1024
    tok_spec = pl.BlockSpec((K, tm), lambda i: (0, i))
    return pl.pallas_call(
        _dest_kernel,
        grid=(T // tm,),
        in_specs=[tok_spec, tok_spec, pl.BlockSpec((N_EXPERTS, LANES), lambda i: (0, 0))],
        out_specs=tok_spec,
        out_shape=jax.ShapeDtypeStruct((K, T), jnp.int32),
        compiler_params=_cparams(("arbitrary",)),
        name="dest",
    )(idx_kt, rank_kt, start)


def _block_plan(counts):
    blk = DISPATCH_BLOCK
    seg_blocks = (counts.astype(jnp.int32) + blk - 1) // blk
    seg_end = jnp.cumsum(seg_blocks)
    seg_first = seg_end - seg_blocks
    return seg_first * blk, seg_first, seg_blocks, seg_end[-1:]


def _combine_kernel(h_ref, x1_ref, yk_ref, g_ref, mod_ref, wsg_ref, wsu_ref, wsd_ref, lng_ref, lnb_ref, o_ref):
    lo, hi = _unpack_bf16_pairs(h_ref[...])
    a = _swiglu(lo.astype(BF16), hi.astype(BF16), wsg_ref, wsu_ref)
    ffn = jnp.dot(a, wsd_ref[...], preferred_element_type=F32)
    r_lo = jnp.zeros((h_ref.shape[0], HALF), F32)
    r_hi = jnp.zeros((h_ref.shape[0], HALF), F32)
    for k in range(TOP_K):
        y_lo, y_hi = _unpack_bf16_pairs(yk_ref[k])
        g = g_ref[:, k:k + 1]
        r_lo += g * y_lo
        r_hi += g * y_hi
    ffn += jnp.concatenate([r_lo, r_hi], axis=1)
    g2 = mod_ref[0, 5:6, :]
    o_ref[...] = _layer_norm(DEEPNORM_ALPHA * x1_ref[...] + g2 * ffn, lng_ref[...], lnb_ref[...])


def _combine_into_kernel(h_ref, x1_ref, yk_ref, g_ref, mod_ref, wsg_ref, wsu_ref, wsd_ref, lng_ref, lnb_ref,
                         prev_ref, o_ref):
    del prev_ref
    _combine_kernel(h_ref, x1_ref, yk_ref, g_ref, mod_ref, wsg_ref, wsu_ref, wsd_ref, lng_ref, lnb_ref, o_ref)


def _combine(h2p, x1, yk, gates, mod, wsg, wsu, wsd, lng, lnb, seq, b_off, out_rows, out_off, out_prev):
    T, D = x1.shape
    tm = TM_COMBINE
    per_seq = seq // tm
    blk_off = out_off // tm
    const = lambda i: (0, 0)
    in_specs = [
        pl.BlockSpec((tm, HALF), lambda i: (i, 0)),
        pl.BlockSpec((tm, D), lambda i: (i, 0)),
        pl.BlockSpec((TOP_K, tm, HALF), lambda i: (0, i, 0)),
        pl.BlockSpec((tm, TOP_K), lambda i: (i, 0)),
        pl.BlockSpec((1, 6, D), lambda i: (i // per_seq + b_off, 0, 0)),
        pl.BlockSpec(wsg.shape, const),
        pl.BlockSpec(wsu.shape, const),
        pl.BlockSpec(wsd.shape, const),
        pl.BlockSpec(lng.shape, const),
        pl.BlockSpec(lnb.shape, const),
    ]
    args = [h2p, x1, yk, gates, mod, wsg, wsu, wsd, lng, lnb]
    body = _combine_kernel
    aliases = {}
    if out_prev is not None:
        in_specs.append(pl.BlockSpec(memory_space=pl.ANY))
        args.append(out_prev)
        aliases = {len(args) - 1: 0}
        body = _combine_into_kernel
    return pl.pallas_call(
        body,
        grid=(T // tm,),
        in_specs=in_specs,
        out_specs=pl.BlockSpec((tm, D), lambda i: (i + blk_off, 0)),
        out_shape=jax.ShapeDtypeStruct((out_rows, D), F32),
        input_output_aliases=aliases,
        compiler_params=_cparams(("arbitrary",)),
        name="combine",
    )(*args)


def _rope_pair(w):
    half = QK_ROPE // 2
    return w, jnp.concatenate([-w[:, half:], w[:, :half]], axis=1)


def _prep_layer(w_in, w_uq, w_ukv, w_pool, w_out, w_router):
    D = w_in.shape[0]
    zpad = lambda r: jnp.zeros((r, LANES - QK_ROPE), F32)
    lat = POOL_WIDTH + Q_LORA + KV_LORA
    k_a, k_b = _rope_pair(w_in[:, lat:lat + QK_ROPE])
    w_in_ext = jnp.concatenate([w_in[:, :lat], k_a, zpad(D), k_b, zpad(D)], axis=1).astype(BF16)

    qa_cols, qb_cols = [], []
    per_head = QK_NOPE + QK_ROPE
    for hd in range(N_HEADS):
        w_h = w_uq[:, hd * per_head:(hd + 1) * per_head]
        r_a, r_b = _rope_pair(w_h[:, QK_NOPE:])
        qa_cols += [w_h[:, :QK_NOPE], r_a, zpad(Q_LORA)]
        qb_cols += [r_b, zpad(Q_LORA)]
    w_uqa = jnp.concatenate(qa_cols, axis=1).astype(BF16)
    w_uqb = jnp.concatenate(qb_cols, axis=1).astype(BF16)

    wpool_bd = jax.scipy.linalg.block_diag(*[w_pool[g] for g in range(len(POOL_WINDOWS))]).astype(BF16)
    wr_hi, wr_lo = _split_hi_lo(w_router.T)
    return dict(w_in_ext=w_in_ext, w_uqa=w_uqa, w_uqb=w_uqb, w_ukv=w_ukv.astype(BF16), wpool_bd=wpool_bd,
                wout_p=w_out[:POOL_WIDTH].astype(BF16), wout_a=w_out[POOL_WIDTH:].astype(BF16),
                wr_hi=wr_hi, wr_lo=wr_lo)


def kernel(x, c, positions, w_ada, b_ada, w_in, q_norm_g, kv_norm_g, w_uq, w_ukv, w_pool, pool_scale,
           w_out, ln1_g, ln1_b, w_router, router_bias, w_exp_gate, w_exp_up, w_exp_down,
           w_sh_gate, w_sh_up, w_sh_down, ln2_g, ln2_b):
    B, S, D = x.shape
    L = w_in.shape[0]
    row = lambda v: v.reshape(1, -1)

    mod_all = _ada_mod(c, w_ada, b_ada).reshape(L, B, 6, D)
    cs_all = _rope_table(positions)
    bs = B // N_STREAMS
    T = bs * S
    n_blocks = T * TOP_K // DISPATCH_BLOCK + N_EXPERTS
    xs_streams = [x] * N_STREAMS
    result = None
    for l in range(L):
        p = _prep_layer(w_in[l], w_uq[l], w_ukv[l], w_pool[l], w_out[l], w_router[l])
        shared = (w_sh_gate[l].astype(BF16), w_sh_up[l].astype(BF16), w_sh_down[l].astype(BF16))
        mod = mod_all[l]
        for s in range(N_STREAMS):
            xl = xs_streams[s]
            b_off = s * bs
            x_off = b_off if l == 0 else 0
            u, q, k, v = _in_proj(xl, mod, cs_all, p["w_in_ext"], row(q_norm_g[l]), row(kv_norm_g[l]),
                                  p["w_uqa"], p["w_uqb"], p["w_ukv"], bs, x_off, b_off)
            attn = _attention(q, k, v)
            x1, h2p, lt = _post_attn(xl, u, attn, mod, p["wpool_bd"], row(pool_scale[l]), p["wout_p"],
                                     p["wout_a"], row(ln1_g[l]), row(ln1_b[l]), p["wr_hi"], p["wr_lo"],
                                     x_off, b_off)
            idx_k, gate_k = _route(lt.reshape(N_EXPERTS, T // LANES, LANES), router_bias[l])
            idx_kt = idx_k.reshape(TOP_K, T)
            gates = gate_k.reshape(TOP_K, T).T
            rank_kt, counts = _rank(idx_kt)
            pad_start, seg_first, seg_blocks, n_real = _block_plan(counts[:, 0])
            start = jnp.broadcast_to(pad_start.astype(F32)[:, None], (N_EXPERTS, LANES))
            dest_kt = _dest(idx_kt, rank_kt, start)
            h2p = h2p.reshape(T, HALF)
            rows = _sc_dispatch(h2p, dest_kt, n_blocks * DISPATCH_BLOCK)
            ys = _expert_mm(seg_first, seg_blocks, n_real, rows, w_exp_gate, w_exp_up, w_exp_down, l)
            yk = _sc_gather(ys, dest_kt.reshape(-1)).reshape(TOP_K, T, HALF)
            if l < L - 1:
                xs_streams[s] = _combine(h2p, x1.reshape(T, D), yk, gates, mod, *shared, row(ln2_g[l]),
                                         row(ln2_b[l]), S, b_off, T, 0, None).reshape(bs, S, D)
            else:
                result = _combine(h2p, x1.reshape(T, D), yk, gates, mod, *shared, row(ln2_g[l]),
                                  row(ln2_b[l]), S, b_off, B * S, s * T, result)
    return result.reshape(B, S, D)
```

```python
import functools

import jax
import jax.numpy as jnp
from jax import lax
from jax.experimental import pallas as pl
from jax.experimental.pallas import tpu as pltpu
from jax.experimental.pallas import tpu_sc as plsc

F32 = jnp.float32
BF16 = jnp.bfloat16

D_MODEL = 1024
DEPTH = 4
POOL_WIDTH = 256
POOL_WINDOWS = (2, 4, 8, 16)
POOL_GROUP = 64
POOL_HALO = 16
QK_NOPE = 128
QK_ROPE = 64
V_HEAD = 128
N_HEADS = 6
Q_LORA = 384
KV_LORA = 256
ROPE_THETA = 10000.0
N_EXPERTS = 64
TOP_K = 8
N_GROUPS = 8
GROUP_SIZE = N_EXPERTS // N_GROUPS
TOPK_GROUPS = 4
D_EXPERT = 256
ROUTED_SCALE = 2.5
DEEPNORM_ALPHA = (2.0 * DEPTH) ** 0.25
LN_EPS = 1e-5
RMS_EPS = 1e-6
ATTN_SCALE = (QK_NOPE + QK_ROPE) ** -0.5
LOG2_E = 1.4426950408889634
Q_SCALE = ATTN_SCALE * LOG2_E
LANES = 128
QK_PAD = 2 * LANES
MASK_VALUE = -1e30

TM_PROJ = 512
TQ = 512
TK = 512
ROUTE_ROWS = 8
DISPATCH_BLOCK = 512
EXPERT_RING = 6
TM_COMBINE = 512
SC_CORES = 2
SC_SUBCORES = 16
SC_WINDOW = 128
N_STREAMS = 2
HALF = D_MODEL // 2
VMEM_LIMIT = 48 * 1024 * 1024


def _cparams(sem):
    return pltpu.CompilerParams(dimension_semantics=sem, vmem_limit_bytes=VMEM_LIMIT)


def _split_hi_lo(a):
    hi = a.astype(BF16)
    lo = (a - hi.astype(F32)).astype(BF16)
    return hi, lo


def _pack_bf16_pairs(a):
    bits = lax.bitcast_convert_type(a.astype(F32), jnp.uint32)
    half = a.shape[1] // 2
    word = (bits[:, :half] >> 16) | (bits[:, half:] & jnp.uint32(0xFFFF0000))
    return lax.bitcast_convert_type(word, jnp.int32)


def _unpack_bf16_pairs(w):
    bits = lax.bitcast_convert_type(w, jnp.uint32)
    lo = lax.bitcast_convert_type(bits << 16, F32)
    hi = lax.bitcast_convert_type(bits & jnp.uint32(0xFFFF0000), F32)
    return lo, hi


def _ada_kernel(c_ref, w_ref, b_ref, o_ref):
    c = c_ref[...]
    cond = c * jax.nn.sigmoid(c)
    c_hi, c_lo = _split_hi_lo(cond)
    w_hi, w_lo = _split_hi_lo(w_ref[0])
    acc = jnp.dot(c_hi, w_hi, preferred_element_type=F32)
    acc += jnp.dot(c_lo, w_hi, preferred_element_type=F32)
    acc += jnp.dot(c_hi, w_lo, preferred_element_type=F32)
    o_ref[0] = acc + b_ref[0]


def _ada_mod(c, w_ada, b_ada):
    L, D, N = w_ada.shape
    B = c.shape[0]
    tn = 1536
    return pl.pallas_call(
        _ada_kernel,
        grid=(L, N // tn),
        in_specs=[
            pl.BlockSpec((B, D), lambda l, j: (0, 0)),
            pl.BlockSpec((1, D, tn), lambda l, j: (l, 0, j)),
            pl.BlockSpec((1, 1, tn), lambda l, j: (l, 0, j)),
        ],
        out_specs=pl.BlockSpec((1, B, tn), lambda l, j: (l, 0, j)),
        out_shape=jax.ShapeDtypeStruct((L, B, N), F32),
        compiler_params=_cparams(("arbitrary", "arbitrary")),
        name="ada_mod",
    )(c, w_ada, b_ada.reshape(L, 1, N))


def _rope_table_kernel(pos_ref, freq_ref, o_ref):
    ang = pos_ref[0].astype(F32) * freq_ref[...]
    o_ref[0, :, 0:LANES] = jnp.cos(ang)
    o_ref[0, :, LANES:2 * LANES] = jnp.sin(ang)


def _rope_table(positions):
    B, S = positions.shape
    half = QK_ROPE // 2
    inv_freq = ROPE_THETA ** (-jnp.arange(half, dtype=F32) / half)
    freq = jnp.tile(inv_freq, LANES // half).reshape(1, LANES)
    return pl.pallas_call(
        _rope_table_kernel,
        grid=(B,),
        in_specs=[
            pl.BlockSpec((1, S, 1), lambda b: (b, 0, 0)),
            pl.BlockSpec((1, LANES), lambda b: (0, 0)),
        ],
        out_specs=pl.BlockSpec((1, S, 2 * LANES), lambda b: (b, 0, 0)),
        out_shape=jax.ShapeDtypeStruct((B, S, 2 * LANES), F32),
        compiler_params=_cparams(("arbitrary",)),
        name="rope_table",
    )(positions.reshape(B, S, 1), freq)


def _rms(x, g):
    return x * lax.rsqrt(jnp.mean(x * x, axis=-1, keepdims=True) + RMS_EPS) * g


def _in_proj_kernel(x_ref, mod_ref, cs_ref, w_in_ref, qg_ref, kvg_ref, w_uqa_ref, w_uqb_ref, w_ukv_ref,
                    u_ref, q_ref, k_ref, v_ref):
    x = x_ref[0]
    sh1 = mod_ref[0, 0:1, :]
    sc1 = mod_ref[0, 1:2, :]
    h = (x * (1.0 + sc1) + sh1).astype(BF16)
    proj = jnp.dot(h, w_in_ref[...], preferred_element_type=F32)
    u_ref[0] = proj[:, 0:POOL_WIDTH]
    o = POOL_WIDTH
    q_lat = proj[:, o:o + Q_LORA]
    o += Q_LORA
    kv_lat = proj[:, o:o + KV_LORA]
    o += KV_LORA
    k_a = proj[:, o:o + LANES]
    k_b = proj[:, o + LANES:o + 2 * LANES]
    cosv = cs_ref[0, :, 0:LANES]
    sinv = cs_ref[0, :, LANES:2 * LANES]
    k_rot = (k_a * cosv + k_b * sinv).astype(BF16)

    qn = _rms(q_lat, qg_ref[...]).astype(BF16)
    q_a = jnp.dot(qn, w_uqa_ref[...], preferred_element_type=F32)
    q_b = jnp.dot(qn, w_uqb_ref[...], preferred_element_type=F32)
    kvn = _rms(kv_lat, kvg_ref[...]).astype(BF16)
    kv = jnp.dot(kvn, w_ukv_ref[...], preferred_element_type=F32)
    for hd in range(N_HEADS):
        b0 = hd * QK_PAD
        q_ref[0, hd, :, 0:LANES] = (q_a[:, b0:b0 + LANES] * Q_SCALE).astype(BF16)
        q_rot = q_a[:, b0 + LANES:b0 + QK_PAD] * cosv + q_b[:, hd * LANES:(hd + 1) * LANES] * sinv
        q_ref[0, hd, :, LANES:QK_PAD] = (q_rot * Q_SCALE).astype(BF16)
        k_ref[0, hd, :, 0:LANES] = kv[:, b0:b0 + QK_NOPE].astype(BF16)
        k_ref[0, hd, :, LANES:QK_PAD] = k_rot
        v_ref[0, hd] = kv[:, b0 + QK_NOPE:b0 + QK_NOPE + V_HEAD].astype(BF16)


def _in_proj(x, mod, cs, w_in_ext, qg, kvg, w_uqa, w_uqb, w_ukv_bf, B, x_off, b_off):
    _, S, D = x.shape
    tm = TM_PROJ
    const = lambda b, i: (0, 0)
    return pl.pallas_call(
        _in_proj_kernel,
        grid=(B, S // tm),
        in_specs=[
            pl.BlockSpec((1, tm, D), lambda b, i: (b + x_off, i, 0)),
            pl.BlockSpec((1, 6, D), lambda b, i: (b + b_off, 0, 0)),
            pl.BlockSpec((1, tm, 2 * LANES), lambda b, i: (b + b_off, i, 0)),
            pl.BlockSpec(w_in_ext.shape, const),
            pl.BlockSpec(qg.shape, const),
            pl.BlockSpec(kvg.shape, const),
            pl.BlockSpec(w_uqa.shape, const),
            pl.BlockSpec(w_uqb.shape, const),
            pl.BlockSpec(w_ukv_bf.shape, const),
        ],
        out_specs=[
            pl.BlockSpec((1, tm, POOL_WIDTH), lambda b, i: (b, i, 0)),
            pl.BlockSpec((1, N_HEADS, tm, QK_PAD), lambda b, i: (b, 0, i, 0)),
            pl.BlockSpec((1, N_HEADS, tm, QK_PAD), lambda b, i: (b, 0, i, 0)),
            pl.BlockSpec((1, N_HEADS, tm, V_HEAD), lambda b, i: (b, 0, i, 0)),
        ],
        out_shape=[
            jax.ShapeDtypeStruct((B, S, POOL_WIDTH), F32),
            jax.ShapeDtypeStruct((B, N_HEADS, S, QK_PAD), BF16),
            jax.ShapeDtypeStruct((B, N_HEADS, S, QK_PAD), BF16),
            jax.ShapeDtypeStruct((B, N_HEADS, S, V_HEAD), BF16),
        ],
        compiler_params=_cparams(("arbitrary", "arbitrary")),
        name="in_proj",
    )(x, mod, cs, w_in_ext, qg, kvg, w_uqa, w_uqb, w_ukv_bf)


def _softmax_step(q, k, v, carry, mask):
    m, l, acc = carry
    s = lax.dot_general(q, k, (((1,), (1,)), ((), ())), preferred_element_type=F32)
    if mask is not None:
        s = jnp.where(mask, s, MASK_VALUE)
    m_new = jnp.maximum(m, jnp.max(s, axis=-1, keepdims=True))
    alpha = jnp.exp2(m - m_new)
    p = jnp.exp2(s - m_new)
    l_new = alpha * l + jnp.sum(p, axis=-1, keepdims=True)
    acc_new = alpha * acc + jnp.dot(p.astype(BF16), v, preferred_element_type=F32)
    return m_new, l_new, acc_new


def _attn_kernel(q_ref, k_ref, v_ref, o_ref):
    seq = q_ref.shape[2]
    row = lax.broadcasted_iota(jnp.int32, (TQ, TK), 0)
    col = lax.broadcasted_iota(jnp.int32, (TQ, TK), 1)
    diag = row >= col
    for i in range(seq // TQ):
        q = q_ref[0, 0, i * TQ:(i + 1) * TQ, :]
        carry = (jnp.full((TQ, 1), MASK_VALUE, F32), jnp.zeros((TQ, 1), F32), jnp.zeros((TQ, V_HEAD), F32))
        for j in range(i + 1):
            k = k_ref[0, 0, j * TK:(j + 1) * TK, :]
            v = v_ref[0, 0, j * TK:(j + 1) * TK, :]
            carry = _softmax_step(q, k, v, carry, diag if j == i else None)
        _, l, acc = carry
        o_ref[0, 0, i * TQ:(i + 1) * TQ, :] = (acc / l).astype(BF16)


def _attention(q, k, v):
    B, H, S, _ = q.shape
    return pl.pallas_call(
        _attn_kernel,
        grid=(B, H),
        in_specs=[
            pl.BlockSpec((1, 1, S, QK_PAD), lambda b, h: (b, h, 0, 0)),
            pl.BlockSpec((1, 1, S, QK_PAD), lambda b, h: (b, h, 0, 0)),
            pl.BlockSpec((1, 1, S, V_HEAD), lambda b, h: (b, h, 0, 0)),
        ],
        out_specs=pl.BlockSpec((1, 1, S, V_HEAD), lambda b, h: (b, h, 0, 0)),
        out_shape=jax.ShapeDtypeStruct((B, H, S, V_HEAD), BF16),
        compiler_params=_cparams(("arbitrary", "arbitrary")),
        name="attention",
    )(q, k, v)


def _layer_norm(v, g, b):
    mu = jnp.mean(v, axis=-1, keepdims=True)
    d = v - mu
    var = jnp.mean(d * d, axis=-1, keepdims=True)
    return d * lax.rsqrt(var + LN_EPS) * g + b


def _post_kernel(x_ref, u_ref, halo_ref, attn_ref, mod_ref, wpool_ref, pscale_ref, wout_p_ref, wout_a_ref,
                 lng_ref, lnb_ref, wr_hi_ref, wr_lo_ref, x1_ref, h2_ref, lt_ref):
    i = pl.program_id(1)
    tm = u_ref.shape[1]
    u = u_ref[0]
    halo = jnp.where(i > 0, halo_ref[0], 0.0)
    ext = jnp.concatenate([halo, u], axis=0)
    s2 = ext + pltpu.roll(ext, 1, 0)
    s4 = s2 + pltpu.roll(s2, 2, 0)
    s8 = s4 + pltpu.roll(s4, 4, 0)
    s16 = s8 + pltpu.roll(s8, 8, 0)
    lane = lax.broadcasted_iota(jnp.int32, (1, POOL_WIDTH), 1)
    win = jnp.where(lane < POOL_GROUP, s2,
                    jnp.where(lane < 2 * POOL_GROUP, s4, jnp.where(lane < 3 * POOL_GROUP, s8, s16)))
    win = win[POOL_HALO:, :]
    width = jnp.where(lane < POOL_GROUP, POOL_WINDOWS[0],
                      jnp.where(lane < 2 * POOL_GROUP, POOL_WINDOWS[1],
                                jnp.where(lane < 3 * POOL_GROUP, POOL_WINDOWS[2], POOL_WINDOWS[3])))
    t = i * tm + lax.broadcasted_iota(jnp.int32, (tm, 1), 0)
    count = jnp.minimum(t + 1, width).astype(F32)
    token_mix = win / count - u
    pooled = jnp.dot(token_mix.astype(BF16), wpool_ref[...], preferred_element_type=F32) * pscale_ref[...]
    mixed = jnp.dot(pooled.astype(BF16), wout_p_ref[...], preferred_element_type=F32)
    attn = jnp.concatenate([attn_ref[0, hd] for hd in range(N_HEADS)], axis=1)
    mixed += jnp.dot(attn, wout_a_ref[...], preferred_element_type=F32)

    g1 = mod_ref[0, 2:3, :]
    sh2 = mod_ref[0, 3:4, :]
    sc2 = mod_ref[0, 4:5, :]
    x1 = _layer_norm(DEEPNORM_ALPHA * x_ref[0] + g1 * mixed, lng_ref[...], lnb_ref[...])
    x1_ref[0] = x1
    h2 = x1 * (1.0 + sc2) + sh2
    h_hi, h_lo = _split_hi_lo(h2)
    h2_ref[0] = _pack_bf16_pairs(h_hi)
    nt = (((1,), (1,)), ((), ()))
    lt = lax.dot_general(wr_hi_ref[...], h_hi, nt, preferred_element_type=F32)
    lt += lax.dot_general(wr_hi_ref[...], h_lo, nt, preferred_element_type=F32)
    lt += lax.dot_general(wr_lo_ref[...], h_hi, nt, preferred_element_type=F32)
    lt_ref[...] = lt


def _post_attn(x, u, attn, mod, wpool_bd, pscale, wout_p, wout_a, lng, lnb, wr_hi, wr_lo, x_off, b_off):
    B, S, _ = u.shape
    D = x.shape[2]
    tm = TM_PROJ
    nt = S // tm
    const = lambda b, i: (0, 0)
    halo_blocks = tm // POOL_HALO
    return pl.pallas_call(
        _post_kernel,
        grid=(B, nt),
        in_specs=[
            pl.BlockSpec((1, tm, D), lambda b, i: (b + x_off, i, 0)),
            pl.BlockSpec((1, tm, POOL_WIDTH), lambda b, i: (b, i, 0)),
            pl.BlockSpec((1, POOL_HALO, POOL_WIDTH), lambda b, i: (b, jnp.maximum(i * halo_blocks - 1, 0), 0)),
            pl.BlockSpec((1, N_HEADS, tm, V_HEAD), lambda b, i: (b, 0, i, 0)),
            pl.BlockSpec((1, 6, D), lambda b, i: (b + b_off, 0, 0)),
            pl.BlockSpec(wpool_bd.shape, const),
            pl.BlockSpec(pscale.shape, const),
            pl.BlockSpec(wout_p.shape, const),
            pl.BlockSpec(wout_a.shape, const),
            pl.BlockSpec(lng.shape, const),
            pl.BlockSpec(lnb.shape, const),
            pl.BlockSpec(wr_hi.shape, const),
            pl.BlockSpec(wr_lo.shape, const),
        ],
        out_specs=[
            pl.BlockSpec((1, tm, D), lambda b, i: (b, i, 0)),
            pl.BlockSpec((1, tm, HALF), lambda b, i: (b, i, 0)),
            pl.BlockSpec((N_EXPERTS, tm), lambda b, i: (0, b * nt + i)),
        ],
        out_shape=[
            jax.ShapeDtypeStruct((B, S, D), F32),
            jax.ShapeDtypeStruct((B, S, HALF), jnp.int32),
            jax.ShapeDtypeStruct((N_EXPERTS, B * S), F32),
        ],
        compiler_params=_cparams(("arbitrary", "arbitrary")),
        name="post_attn",
    )(x, u, u, attn, mod, wpool_bd, pscale, wout_p, wout_a, lng, lnb, wr_hi, wr_lo)


def _select_first_max(vals, n_rounds, payload=None):
    work = list(vals)
    sel = [None] * len(vals)
    rounds = []
    for _ in range(n_rounds):
        m = functools.reduce(jnp.maximum, work)
        taken = None
        win_idx = jnp.zeros(m.shape, jnp.int32)
        win_val = jnp.zeros(m.shape, F32)
        for e in range(len(work)):
            hit = work[e] == m
            first = hit if taken is None else jnp.logical_and(hit, jnp.logical_not(taken))
            taken = hit if taken is None else jnp.logical_or(taken, hit)
            sel[e] = first if sel[e] is None else jnp.logical_or(sel[e], first)
            work[e] = jnp.where(first, -jnp.inf, work[e])
            if payload is not None:
                win_idx = jnp.where(first, e, win_idx)
                win_val = jnp.where(first, payload[e], win_val)
        rounds.append((win_idx, win_val))
    return sel, rounds


def _route_kernel(bias_ref, lt_ref, idx_ref, gate_ref):
    scores = [jax.nn.sigmoid(lt_ref[e]) for e in range(N_EXPERTS)]
    choice = [scores[e] + bias_ref[e] for e in range(N_EXPERTS)]
    group_score = []
    for g in range(N_GROUPS):
        vals = choice[g * GROUP_SIZE:(g + 1) * GROUP_SIZE]
        sel2, _ = _select_first_max(vals, 2)
        group_score.append(functools.reduce(
            jnp.add, [jnp.where(sel2[j], vals[j], 0.0) for j in range(GROUP_SIZE)]))
    group_sel, _ = _select_first_max(group_score, TOPK_GROUPS)
    masked = [jnp.where(group_sel[e // GROUP_SIZE], choice[e], -jnp.inf) for e in range(N_EXPERTS)]
    _, rounds = _select_first_max(masked, TOP_K, payload=scores)
    total = functools.reduce(jnp.add, [w for _, w in rounds])
    for k, (e_k, w_k) in enumerate(rounds):
        idx_ref[k] = e_k
        gate_ref[k] = w_k / total * ROUTED_SCALE


def _route(lt3, bias):
    E, R, _ = lt3.shape
    out_spec = pl.BlockSpec((TOP_K, ROUTE_ROWS, LANES), lambda r: (0, r, 0))
    return pl.pallas_call(
        _route_kernel,
        grid=(R // ROUTE_ROWS,),
        in_specs=[
            pl.BlockSpec(memory_space=pltpu.SMEM),
            pl.BlockSpec((E, ROUTE_ROWS, LANES), lambda r: (0, r, 0)),
        ],
        out_specs=[out_spec, out_spec],
        out_shape=[jax.ShapeDtypeStruct((TOP_K, R, LANES), jnp.int32),
                   jax.ShapeDtypeStruct((TOP_K, R, LANES), F32)],
        compiler_params=_cparams(("arbitrary",)),
        name="route",
    )(bias, lt3)


def _sc_mesh():
    return plsc.VectorSubcoreMesh(core_axis_name="c", subcore_axis_name="s",
                                  num_cores=SC_CORES, num_subcores=SC_SUBCORES)


def _sc_worker_id():
    return lax.axis_index("s") * SC_CORES + lax.axis_index("c")


def _sc_gather(table, idx):
    n = idx.shape[0]
    width = table.shape[1]
    workers = SC_CORES * SC_SUBCORES
    per_worker = n // workers
    n_win = per_worker // SC_WINDOW
    assert n == workers * n_win * SC_WINDOW

    @functools.partial(
        pl.kernel, mesh=_sc_mesh(),
        out_type=jax.ShapeDtypeStruct((n, width), table.dtype),
        scratch_types=[
            pltpu.VMEM((n_win, SC_WINDOW), jnp.int32),
            pltpu.VMEM((SC_WINDOW, width), table.dtype),
            pltpu.SemaphoreType.DMA,
        ],
    )
    def gather(table_hbm, idx_hbm, out_hbm, idx_v, rows_v, sem):
        wid = _sc_worker_id()
        pltpu.sync_copy(idx_hbm.at[wid], idx_v)

        @pl.loop(0, n_win)
        def _(j):
            pltpu.async_copy(table_hbm.at[idx_v.at[j]], rows_v, sem).wait()
            pltpu.sync_copy(rows_v, out_hbm.at[pl.ds(wid * per_worker + j * SC_WINDOW, SC_WINDOW)])

    return gather(table, idx.reshape(workers, n_win, SC_WINDOW))


def _sc_dispatch(rows, dest_kt, n_slots):
    T, width = rows.shape
    K = dest_kt.shape[0]
    workers = SC_CORES * SC_SUBCORES
    per_worker = T // workers
    n_win = per_worker // SC_WINDOW
    assert T == workers * n_win * SC_WINDOW
    dest_w = dest_kt.reshape(K, workers, n_win, SC_WINDOW).transpose(1, 2, 0, 3)
    dest_w = dest_w.reshape(workers, n_win * K, SC_WINDOW)

    @functools.partial(
        pl.kernel, mesh=_sc_mesh(),
        out_type=jax.ShapeDtypeStruct((n_slots, width), rows.dtype),
        scratch_types=[
            pltpu.VMEM((n_win * K, SC_WINDOW), jnp.int32),
            pltpu.VMEM((SC_WINDOW, width), rows.dtype),
        ],
    )
    def dispatch(rows_hbm, dest_hbm, out_hbm, idx_v, rows_v):
        wid = _sc_worker_id()
        pltpu.sync_copy(dest_hbm.at[wid], idx_v)

        @pl.loop(0, n_win)
        def _(j):
            pltpu.sync_copy(rows_hbm.at[pl.ds(wid * per_worker + j * SC_WINDOW, SC_WINDOW)], rows_v)
            for k in range(K):
                pltpu.sync_copy(rows_v, out_hbm.at[idx_v.at[j * K + k]])

    return dispatch(rows, dest_w)


def _swiglu(h_lo, h_hi, wg_ref, wu_ref):
    def proj(w_ref):
        return (jnp.dot(h_lo, w_ref[:HALF, :], preferred_element_type=F32)
                + jnp.dot(h_hi, w_ref[HALF:, :], preferred_element_type=F32))
    gate = proj(wg_ref)
    up = proj(wu_ref)
    return (gate * jax.nn.sigmoid(gate) * up).astype(BF16)


def _expert_kernel(seg_first_ref, seg_blocks_ref, n_real_ref, xs_hbm, wg_ref, wu_ref, wd_ref, ys_hbm,
                   xbuf, ybuf, wg_bf, wu_bf, wd_bf, in_sems, out_sems):
    e = pl.program_id(0)
    first = seg_first_ref[e]
    count = seg_blocks_ref[e]
    total = n_real_ref[0]
    blk = DISPATCH_BLOCK
    ring = xbuf.shape[0]

    def in_copy(g):
        slot = g % ring
        return pltpu.make_async_copy(xs_hbm.at[pl.ds(g * blk, blk)], xbuf.at[slot], in_sems.at[slot])

    def out_copy(g):
        slot = g % ring
        return pltpu.make_async_copy(ybuf.at[slot], ys_hbm.at[pl.ds(g * blk, blk)], out_sems.at[slot])

    @pl.when(e == 0)
    def _():
        for g in range(ring - 1):
            @pl.when(g < total)
            def _():
                in_copy(g).start()

    @pl.when(count > 0)
    def _():
        wg_bf[...] = wg_ref[0, 0].astype(BF16)
        wu_bf[...] = wu_ref[0, 0].astype(BF16)
        wd_bf[...] = wd_ref[0, 0].astype(BF16)

    def block(g, _):
        slot = g % ring
        in_copy(g).wait()

        @pl.when(g + ring - 1 < total)
        def _():
            in_copy(g + ring - 1).start()

        @pl.when(g >= ring)
        def _():
            out_copy(g - ring).wait()

        lo, hi = _unpack_bf16_pairs(xbuf[slot])
        a = _swiglu(lo.astype(BF16), hi.astype(BF16), wg_bf, wu_bf)
        y = jnp.dot(a, wd_bf[...], preferred_element_type=F32)
        ybuf[slot] = _pack_bf16_pairs(y.astype(BF16))
        out_copy(g).start()
        return 0

    lax.fori_loop(first, first + count, block, 0)

    @pl.when(e == pl.num_programs(0) - 1)
    def _():
        for back in range(ring, 0, -1):
            @pl.when(total >= back)
            def _():
                out_copy(total - back).wait()


def _expert_mm(seg_first, seg_blocks, n_real, xs, wg, wu, wd, layer):
    P = xs.shape[0]
    D = wg.shape[2]
    w_map = lambda e, sf, sb, nr: (layer, e, 0, 0)
    grid_spec = pltpu.PrefetchScalarGridSpec(
        num_scalar_prefetch=3,
        grid=(N_EXPERTS,),
        in_specs=[
            pl.BlockSpec(memory_space=pl.ANY),
            pl.BlockSpec((1, 1, D, D_EXPERT), w_map),
            pl.BlockSpec((1, 1, D, D_EXPERT), w_map),
            pl.BlockSpec((1, 1, D_EXPERT, D), w_map),
        ],
        out_specs=pl.BlockSpec(memory_space=pl.ANY),
        scratch_shapes=[
            pltpu.VMEM((EXPERT_RING, DISPATCH_BLOCK, HALF), jnp.int32),
            pltpu.VMEM((EXPERT_RING, DISPATCH_BLOCK, HALF), jnp.int32),
            pltpu.VMEM((D, D_EXPERT), BF16), pltpu.VMEM((D, D_EXPERT), BF16), pltpu.VMEM((D_EXPERT, D), BF16),
            pltpu.SemaphoreType.DMA((EXPERT_RING,)), pltpu.SemaphoreType.DMA((EXPERT_RING,)),
        ],
    )
    return pl.pallas_call(
        _expert_kernel,
        grid_spec=grid_spec,
        out_shape=jax.ShapeDtypeStruct((P, HALF), jnp.int32),
        compiler_params=_cparams(("arbitrary",)),
        name="expert_mm",
    )(seg_first, seg_blocks, n_real, xs, wg, wu, wd)


def _rank_kernel(idx_ref, rank_ref, cnt_ref, tri_ref, carry_ref):
    i = pl.program_id(0)
    tm = idx_ref.shape[1]

    @pl.when(i == 0)
    def _():
        r = lax.broadcasted_iota(jnp.int32, (tm, tm), 0)
        c = lax.broadcasted_iota(jnp.int32, (tm, tm), 1)
        tri_ref[...] = (r < c).astype(BF16)
        carry_ref[...] = jnp.zeros(carry_ref.shape, F32)

    e_iota = lax.broadcasted_iota(jnp.int32, (N_EXPERTS, tm), 0)
    hits = [e_iota == idx_ref[k:k + 1, :] for k in range(TOP_K)]
    mask = functools.reduce(jnp.logical_or, hits).astype(F32)
    rank = jnp.dot(mask.astype(BF16), tri_ref[...], preferred_element_type=F32) + carry_ref[:, 0:1]
    for k in range(TOP_K):
        rank_ref[k:k + 1, :] = jnp.sum(jnp.where(hits[k], rank, 0.0), axis=0, keepdims=True).astype(jnp.int32)
    carry_ref[...] = carry_ref[...] + jnp.sum(mask, axis=1, keepdims=True)
    cnt_ref[...] = carry_ref[...]


def _rank(idx_kt):
    K, T = idx_kt.shape
    tm = 1024
    return pl.pallas_call(
        _rank_kernel,
        grid=(T // tm,),
        in_specs=[pl.BlockSpec((K, tm), lambda i: (0, i))],
        out_specs=[pl.BlockSpec((K, tm), lambda i: (0, i)),
                   pl.BlockSpec((N_EXPERTS, LANES), lambda i: (0, 0))],
        out_shape=[jax.ShapeDtypeStruct((K, T), jnp.int32),
                   jax.ShapeDtypeStruct((N_EXPERTS, LANES), F32)],
        scratch_shapes=[pltpu.VMEM((tm, tm), BF16), pltpu.VMEM((N_EXPERTS, LANES), F32)],
        compiler_params=_cparams(("arbitrary",)),
        name="rank",
    )(idx_kt)


def _dest_kernel(idx_ref, rank_ref, start_ref, dest_ref):
    tm = idx_ref.shape[1]
    e_iota = lax.broadcasted_iota(jnp.int32, (N_EXPERTS, tm), 0)
    start = start_ref[:, 0:1]
    for k in range(TOP_K):
        base = jnp.sum(jnp.where(e_iota == idx_ref[k:k + 1, :], start, 0.0), axis=0, keepdims=True)
        dest_ref[k:k + 1, :] = rank_ref[k:k + 1, :] + base.astype(jnp.int32)


def _dest(idx_kt, rank_kt, start):
    K, T = idx_kt.shape
    tm = 1024
    tok_spec = pl.BlockSpec((K, tm), lambda i: (0, i))
    return pl.pallas_call(
        _dest_kernel,
        grid=(T // tm,),
        in_specs=[tok_spec, tok_spec, pl.BlockSpec((N_EXPERTS, LANES), lambda i: (0, 0))],
        out_specs=tok_spec,
        out_shape=jax.ShapeDtypeStruct((K, T), jnp.int32),
        compiler_params=_cparams(("arbitrary",)),
        name="dest",
    )(idx_kt, rank_kt, start)


def _block_plan(counts):
    blk = DISPATCH_BLOCK
    seg_blocks = (counts.astype(jnp.int32) + blk - 1) // blk
    seg_end = jnp.cumsum(seg_blocks)
    seg_first = seg_end - seg_blocks
    return seg_first * blk, seg_first, seg_blocks, seg_end[-1:]


def _combine_kernel(h_ref, x1_ref, yk_ref, g_ref, mod_ref, wsg_ref, wsu_ref, wsd_ref, lng_ref, lnb_ref, o_ref):
    lo, hi = _unpack_bf16_pairs(h_ref[...])
    a = _swiglu(lo.astype(BF16), hi.astype(BF16), wsg_ref, wsu_ref)
    ffn = jnp.dot(a, wsd_ref[...], preferred_element_type=F32)
    r_lo = jnp.zeros((h_ref.shape[0], HALF), F32)
    r_hi = jnp.zeros((h_ref.shape[0], HALF), F32)
    for k in range(TOP_K):
        y_lo, y_hi = _unpack_bf16_pairs(yk_ref[k])
        g = g_ref[:, k:k + 1]
        r_lo += g * y_lo
        r_hi += g * y_hi
    ffn += jnp.concatenate([r_lo, r_hi], axis=1)
    g2 = mod_ref[0, 5:6, :]
    o_ref[...] = _layer_norm(DEEPNORM_ALPHA * x1_ref[...] + g2 * ffn, lng_ref[...], lnb_ref[...])


def _combine_into_kernel(h_ref, x1_ref, yk_ref, g_ref, mod_ref, wsg_ref, wsu_ref, wsd_ref, lng_ref, lnb_ref,
                         prev_ref, o_ref):
    del prev_ref
    _combine_kernel(h_ref, x1_ref, yk_ref, g_ref, mod_ref, wsg_ref, wsu_ref, wsd_ref, lng_ref, lnb_ref, o_ref)


def _combine(h2p, x1, yk, gates, mod, wsg, wsu, wsd, lng, lnb, seq, b_off, out_rows, out_off, out_prev):
    T, D = x1.shape
    tm = TM_COMBINE
    per_seq = seq // tm
    blk_off = out_off // tm
    const = lambda i: (0, 0)
    in_specs = [
        pl.BlockSpec((tm, HALF), lambda i: (i, 0)),
        pl.BlockSpec((tm, D), lambda i: (i, 0)),
        pl.BlockSpec((TOP_K, tm, HALF), lambda i: (0, i, 0)),
        pl.BlockSpec((tm, TOP_K), lambda i: (i, 0)),
        pl.BlockSpec((1, 6, D), lambda i: (i // per_seq + b_off, 0, 0)),
        pl.BlockSpec(wsg.shape, const),
        pl.BlockSpec(wsu.shape, const),
        pl.BlockSpec(wsd.shape, const),
        pl.BlockSpec(lng.shape, const),
        pl.BlockSpec(lnb.shape, const),
    ]
    args = [h2p, x1, yk, gates, mod, wsg, wsu, wsd, lng, lnb]
    body = _combine_kernel
    aliases = {}
    if out_prev is not None:
        in_specs.append(pl.BlockSpec(memory_space=pl.ANY))
        args.append(out_prev)
        aliases = {len(args) - 1: 0}
        body = _combine_into_kernel
    return pl.pallas_call(
        body,
        grid=(T // tm,),
        in_specs=in_specs,
        out_specs=pl.BlockSpec((tm, D), lambda i: (i + blk_off, 0)),
        out_shape=jax.ShapeDtypeStruct((out_rows, D), F32),
        input_output_aliases=aliases,
        compiler_params=_cparams(("arbitrary",)),
        name="combine",
    )(*args)


def _rope_pair(w):
    half = QK_ROPE // 2
    return w, jnp.concatenate([-w[:, half:], w[:, :half]], axis=1)


def _prep_layer(w_in, w_uq, w_ukv, w_pool, w_out, w_router):
    D = w_in.shape[0]
    zpad = lambda r: jnp.zeros((r, LANES - QK_ROPE), F32)
    lat = POOL_WIDTH + Q_LORA + KV_LORA
    k_a, k_b = _rope_pair(w_in[:, lat:lat + QK_ROPE])
    w_in_ext = jnp.concatenate([w_in[:, :lat], k_a, zpad(D), k_b, zpad(D)], axis=1).astype(BF16)

    qa_cols, qb_cols = [], []
    per_head = QK_NOPE + QK_ROPE
    for hd in range(N_HEADS):
        w_h = w_uq[:, hd * per_head:(hd + 1) * per_head]
        r_a, r_b = _rope_pair(w_h[:, QK_NOPE:])
        qa_cols += [w_h[:, :QK_NOPE], r_a, zpad(Q_LORA)]
        qb_cols += [r_b, zpad(Q_LORA)]
    w_uqa = jnp.concatenate(qa_cols, axis=1).astype(BF16)
    w_uqb = jnp.concatenate(qb_cols, axis=1).astype(BF16)

    wpool_bd = jax.scipy.linalg.block_diag(*[w_pool[g] for g in range(len(POOL_WINDOWS))]).astype(BF16)
    wr_hi, wr_lo = _split_hi_lo(w_router.T)
    return dict(w_in_ext=w_in_ext, w_uqa=w_uqa, w_uqb=w_uqb, w_ukv=w_ukv.astype(BF16), wpool_bd=wpool_bd,
                wout_p=w_out[:POOL_WIDTH].astype(BF16), wout_a=w_out[POOL_WIDTH:].astype(BF16),
                wr_hi=wr_hi, wr_lo=wr_lo)


def kernel(x, c, positions, w_ada, b_ada, w_in, q_norm_g, kv_norm_g, w_uq, w_ukv, w_pool, pool_scale,
           w_out, ln1_g, ln1_b, w_router, router_bias, w_exp_gate, w_exp_up, w_exp_down,
           w_sh_gate, w_sh_up, w_sh_down, ln2_g, ln2_b):
    B, S, D = x.shape
    L = w_in.shape[0]
    row = lambda v: v.reshape(1, -1)

    mod_all = _ada_mod(c, w_ada, b_ada).reshape(L, B, 6, D)
    cs_all = _rope_table(positions)
    bs = B // N_STREAMS
    T = bs * S
    n_blocks = T * TOP_K // DISPATCH_BLOCK + N_EXPERTS
    xs_streams = [x] * N_STREAMS
    result = None
    for l in range(L):
        p = _prep_layer(w_in[l], w_uq[l], w_ukv[l], w_pool[l], w_out[l], w_router[l])
        shared = (w_sh_gate[l].astype(BF16), w_sh_up[l].astype(BF16), w_sh_down[l].astype(BF16))
        mod = mod_all[l]
        for s in range(N_STREAMS):
            xl = xs_streams[s]
            b_off = s * bs
            x_off = b_off if l == 0 else 0
            u, q, k, v = _in_proj(xl, mod, cs_all, p["w_in_ext"], row(q_norm_g[l]), row(kv_norm_g[l]),
                                  p["w_uqa"], p["w_uqb"], p["w_ukv"], bs, x_off, b_off)
            attn = _attention(q, k, v)
            x1, h2p, lt = _post_attn(xl, u, attn, mod, p["wpool_bd"], row(pool_scale[l]), p["wout_p"],
                                     p["wout_a"], row(ln1_g[l]), row(ln1_b[l]), p["wr_hi"], p["wr_lo"],
                                     x_off, b_off)
            idx_k, gate_k = _route(lt.reshape(N_EXPERTS, T // LANES, LANES), router_bias[l])
            idx_kt = idx_k.reshape(TOP_K, T)
            gates = gate_k.reshape(TOP_K, T).T
            rank_kt, counts = _rank(idx_kt)
            pad_start, seg_first, seg_blocks, n_real = _block_plan(counts[:, 0])
            start = jnp.broadcast_to(pad_start.astype(F32)[:, None], (N_EXPERTS, LANES))
            dest_kt = _dest(idx_kt, rank_kt, start)
            h2p = h2p.reshape(T, HALF)
            rows = _sc_dispatch(h2p, dest_kt, n_blocks * DISPATCH_BLOCK)
            ys = _expert_mm(seg_first, seg_blocks, n_real, rows, w_exp_gate, w_exp_up, w_exp_down, l)
            yk = _sc_gather(ys, dest_kt.reshape(-1)).reshape(TOP_K, T, HALF)
            if l < L - 1:
                xs_streams[s] = _combine(h2p, x1.reshape(T, D), yk, gates, mod, *shared, row(ln2_g[l]),
                                         row(ln2_b[l]), S, b_off, T, 0, None).reshape(bs, S, D)
            else:
                result = _combine(h2p, x1.reshape(T, D), yk, gates, mod, *shared, row(ln2_g[l]),
                                  row(ln2_b[l]), S, b_off, B * S, s * T, result)
    return result.reshape(B, S, D)
```

```python
import functools

import jax
import jax.numpy as jnp
from jax import lax
from jax.experimental import pallas as pl
from jax.experimental.pallas import tpu as pltpu
from jax.experimental.pallas import tpu_sc as plsc

F32 = jnp.float32
BF16 = jnp.bfloat16

D_MODEL = 1024
DEPTH = 4
POOL_WIDTH = 256
POOL_WINDOWS = (2, 4, 8, 16)
POOL_GROUP = 64
POOL_HALO = 16
QK_NOPE = 128
QK_ROPE = 64
V_HEAD = 128
N_HEADS = 6
Q_LORA = 384
KV_LORA = 256
ROPE_THETA = 10000.0
N_EXPERTS = 64
TOP_K = 8
N_GROUPS = 8
GROUP_SIZE = N_EXPERTS // N_GROUPS
TOPK_GROUPS = 4
D_EXPERT = 256
ROUTED_SCALE = 2.5
DEEPNORM_ALPHA = (2.0 * DEPTH) ** 0.25
LN_EPS = 1e-5
RMS_EPS = 1e-6
ATTN_SCALE = (QK_NOPE + QK_ROPE) ** -0.5
LOG2_E = 1.4426950408889634
Q_SCALE = ATTN_SCALE * LOG2_E
LANES = 128
QK_PAD = 2 * LANES
MASK_VALUE = -1e30

TM_PROJ = 512
TQ = 512
TK = 512
ROUTE_ROWS = 8
DISPATCH_BLOCK = 512
EXPERT_RING = 6
TM_COMBINE = 512
SC_CORES = 2
SC_SUBCORES = 16
SC_WINDOW = 128
SC_LANES = 16
HIGH_HALF_MASK = -65536
SC_SUM_TOKENS = 16
N_STREAMS = 2
HALF = D_MODEL // 2
VMEM_LIMIT = 48 * 1024 * 1024


def _cparams(sem):
    return pltpu.CompilerParams(dimension_semantics=sem, vmem_limit_bytes=VMEM_LIMIT)


def _split_hi_lo(a):
    hi = a.astype(BF16)
    lo = (a - hi.astype(F32)).astype(BF16)
    return hi, lo


def _pack_bf16_pairs(a):
    bits = lax.bitcast_convert_type(a.astype(F32), jnp.uint32)
    half = a.shape[1] // 2
    word = (bits[:, :half] >> 16) | (bits[:, half:] & jnp.uint32(0xFFFF0000))
    return lax.bitcast_convert_type(word, jnp.int32)


def _unpack_bf16_pairs(w):
    bits = lax.bitcast_convert_type(w, jnp.uint32)
    lo = lax.bitcast_convert_type(bits << 16, F32)
    hi = lax.bitcast_convert_type(bits & jnp.uint32(0xFFFF0000), F32)
    return lo, hi


def _ada_kernel(c_ref, w_ref, b_ref, o_ref):
    c = c_ref[...]
    cond = c * jax.nn.sigmoid(c)
    c_hi, c_lo = _split_hi_lo(cond)
    w_hi, w_lo = _split_hi_lo(w_ref[0])
    acc = jnp.dot(c_hi, w_hi, preferred_element_type=F32)
    acc += jnp.dot(c_lo, w_hi, preferred_element_type=F32)
    acc += jnp.dot(c_hi, w_lo, preferred_element_type=F32)
    o_ref[0] = acc + b_ref[0]


def _ada_mod(c, w_ada, b_ada):
    L, D, N = w_ada.shape
    B = c.shape[0]
    tn = 1536
    return pl.pallas_call(
        _ada_kernel,
        grid=(L, N // tn),
        in_specs=[
            pl.BlockSpec((B, D), lambda l, j: (0, 0)),
            pl.BlockSpec((1, D, tn), lambda l, j: (l, 0, j)),
            pl.BlockSpec((1, 1, tn), lambda l, j: (l, 0, j)),
        ],
        out_specs=pl.BlockSpec((1, B, tn), lambda l, j: (l, 0, j)),
        out_shape=jax.ShapeDtypeStruct((L, B, N), F32),
        compiler_params=_cparams(("arbitrary", "arbitrary")),
        name="ada_mod",
    )(c, w_ada, b_ada.reshape(L, 1, N))


def _rope_table_kernel(pos_ref, freq_ref, o_ref):
    ang = pos_ref[0].astype(F32) * freq_ref[...]
    o_ref[0, :, 0:LANES] = jnp.cos(ang)
    o_ref[0, :, LANES:2 * LANES] = jnp.sin(ang)


def _rope_table(positions):
    B, S = positions.shape
    half = QK_ROPE // 2
    inv_freq = ROPE_THETA ** (-jnp.arange(half, dtype=F32) / half)
    freq = jnp.tile(inv_freq, LANES // half).reshape(1, LANES)
    return pl.pallas_call(
        _rope_table_kernel,
        grid=(B,),
        in_specs=[
            pl.BlockSpec((1, S, 1), lambda b: (b, 0, 0)),
            pl.BlockSpec((1, LANES), lambda b: (0, 0)),
        ],
        out_specs=pl.BlockSpec((1, S, 2 * LANES), lambda b: (b, 0, 0)),
        out_shape=jax.ShapeDtypeStruct((B, S, 2 * LANES), F32),
        compiler_params=_cparams(("arbitrary",)),
        name="rope_table",
    )(positions.reshape(B, S, 1), freq)


def _rms(x, g):
    return x * lax.rsqrt(jnp.mean(x * x, axis=-1, keepdims=True) + RMS_EPS) * g


def _in_proj_kernel(x_ref, mod_ref, cs_ref, w_in_ref, qg_ref, kvg_ref, w_uqa_ref, w_uqb_ref, w_ukv_ref,
                    u_ref, q_ref, k_ref, v_ref):
    x = x_ref[0]
    sh1 = mod_ref[0, 0:1, :]
    sc1 = mod_ref[0, 1:2, :]
    h = (x * (1.0 + sc1) + sh1).astype(BF16)
    proj = jnp.dot(h, w_in_ref[...], preferred_element_type=F32)
    u_ref[0] = proj[:, 0:POOL_WIDTH]
    o = POOL_WIDTH
    q_lat = proj[:, o:o + Q_LORA]
    o += Q_LORA
    kv_lat = proj[:, o:o + KV_LORA]
    o += KV_LORA
    k_a = proj[:, o:o + LANES]
    k_b = proj[:, o + LANES:o + 2 * LANES]
    cosv = cs_ref[0, :, 0:LANES]
    sinv = cs_ref[0, :, LANES:2 * LANES]
    k_rot = (k_a * cosv + k_b * sinv).astype(BF16)

    qn = _rms(q_lat, qg_ref[...]).astype(BF16)
    q_a = jnp.dot(qn, w_uqa_ref[...], preferred_element_type=F32)
    q_b = jnp.dot(qn, w_uqb_ref[...], preferred_element_type=F32)
    kvn = _rms(kv_lat, kvg_ref[...]).astype(BF16)
    kv = jnp.dot(kvn, w_ukv_ref[...], preferred_element_type=F32)
    for hd in range(N_HEADS):
        b0 = hd * QK_PAD
        q_ref[0, hd, :, 0:LANES] = (q_a[:, b0:b0 + LANES] * Q_SCALE).astype(BF16)
        q_rot = q_a[:, b0 + LANES:b0 + QK_PAD] * cosv + q_b[:, hd * LANES:(hd + 1) * LANES] * sinv
        q_ref[0, hd, :, LANES:QK_PAD] = (q_rot * Q_SCALE).astype(BF16)
        k_ref[0, hd, :, 0:LANES] = kv[:, b0:b0 + QK_NOPE].astype(BF16)
        k_ref[0, hd, :, LANES:QK_PAD] = k_rot
        v_ref[0, hd] = kv[:, b0 + QK_NOPE:b0 + QK_NOPE + V_HEAD].astype(BF16)


def _in_proj(x, mod, cs, w_in_ext, qg, kvg, w_uqa, w_uqb, w_ukv_bf, B, x_off, b_off):
    _, S, D = x.shape
    tm = TM_PROJ
    const = lambda b, i: (0, 0)
    return pl.pallas_call(
        _in_proj_kernel,
        grid=(B, S // tm),
        in_specs=[
            pl.BlockSpec((1, tm, D), lambda b, i: (b + x_off, i, 0)),
            pl.BlockSpec((1, 6, D), lambda b, i: (b + b_off, 0, 0)),
            pl.BlockSpec((1, tm, 2 * LANES), lambda b, i: (b + b_off, i, 0)),
            pl.BlockSpec(w_in_ext.shape, const),
            pl.BlockSpec(qg.shape, const),
            pl.BlockSpec(kvg.shape, const),
            pl.BlockSpec(w_uqa.shape, const),
            pl.BlockSpec(w_uqb.shape, const),
            pl.BlockSpec(w_ukv_bf.shape, const),
        ],
        out_specs=[
            pl.BlockSpec((1, tm, POOL_WIDTH), lambda b, i: (b, i, 0)),
            pl.BlockSpec((1, N_HEADS, tm, QK_PAD), lambda b, i: (b, 0, i, 0)),
            pl.BlockSpec((1, N_HEADS, tm, QK_PAD), lambda b, i: (b, 0, i, 0)),
            pl.BlockSpec((1, N_HEADS, tm, V_HEAD), lambda b, i: (b, 0, i, 0)),
        ],
        out_shape=[
            jax.ShapeDtypeStruct((B, S, POOL_WIDTH), F32),
            jax.ShapeDtypeStruct((B, N_HEADS, S, QK_PAD), BF16),
            jax.ShapeDtypeStruct((B, N_HEADS, S, QK_PAD), BF16),
            jax.ShapeDtypeStruct((B, N_HEADS, S, V_HEAD), BF16),
        ],
        compiler_params=_cparams(("arbitrary", "arbitrary")),
        name="in_proj",
    )(x, mod, cs, w_in_ext, qg, kvg, w_uqa, w_uqb, w_ukv_bf)


def _softmax_step(q, k, v, carry, mask):
    m, l, acc = carry
    s = lax.dot_general(q, k, (((1,), (1,)), ((), ())), preferred_element_type=F32)
    if mask is not None:
        s = jnp.where(mask, s, MASK_VALUE)
    m_new = jnp.maximum(m, jnp.max(s, axis=-1, keepdims=True))
    alpha = jnp.exp2(m - m_new)
    p = jnp.exp2(s - m_new)
    l_new = alpha * l + jnp.sum(p, axis=-1, keepdims=True)
    acc_new = alpha * acc + jnp.dot(p.astype(BF16), v, preferred_element_type=F32)
    return m_new, l_new, acc_new


def _attn_kernel(q_ref, k_ref, v_ref, o_ref):
    seq = q_ref.shape[2]
    row = lax.broadcasted_iota(jnp.int32, (TQ, TK), 0)
    col = lax.broadcasted_iota(jnp.int32, (TQ, TK), 1)
    diag = row >= col
    for i in range(seq // TQ):
        q = q_ref[0, 0, i * TQ:(i + 1) * TQ, :]
        carry = (jnp.full((TQ, 1), MASK_VALUE, F32), jnp.zeros((TQ, 1), F32), jnp.zeros((TQ, V_HEAD), F32))
        for j in range(i + 1):
            k = k_ref[0, 0, j * TK:(j + 1) * TK, :]
            v = v_ref[0, 0, j * TK:(j + 1) * TK, :]
            carry = _softmax_step(q, k, v, carry, diag if j == i else None)
        _, l, acc = carry
        o_ref[0, 0, i * TQ:(i + 1) * TQ, :] = (acc / l).astype(BF16)


def _attention(q, k, v):
    B, H, S, _ = q.shape
    return pl.pallas_call(
        _attn_kernel,
        grid=(B, H),
        in_specs=[
            pl.BlockSpec((1, 1, S, QK_PAD), lambda b, h: (b, h, 0, 0)),
            pl.BlockSpec((1, 1, S, QK_PAD), lambda b, h: (b, h, 0, 0)),
            pl.BlockSpec((1, 1, S, V_HEAD), lambda b, h: (b, h, 0, 0)),
        ],
        out_specs=pl.BlockSpec((1, 1, S, V_HEAD), lambda b, h: (b, h, 0, 0)),
        out_shape=jax.ShapeDtypeStruct((B, H, S, V_HEAD), BF16),
        compiler_params=_cparams(("arbitrary", "arbitrary")),
        name="attention",
    )(q, k, v)


def _layer_norm(v, g, b):
    mu = jnp.mean(v, axis=-1, keepdims=True)
    d = v - mu
    var = jnp.mean(d * d, axis=-1, keepdims=True)
    return d * lax.rsqrt(var + LN_EPS) * g + b


def _post_kernel(x_ref, u_ref, halo_ref, attn_ref, mod_ref, wpool_ref, pscale_ref, wout_p_ref, wout_a_ref,
                 lng_ref, lnb_ref, wr_hi_ref, wr_lo_ref, x1_ref, h2_ref, lt_ref):
    i = pl.program_id(1)
    tm = u_ref.shape[1]
    u = u_ref[0]
    halo = jnp.where(i > 0, halo_ref[0], 0.0)
    ext = jnp.concatenate([halo, u], axis=0)
    s2 = ext + pltpu.roll(ext, 1, 0)
    s4 = s2 + pltpu.roll(s2, 2, 0)
    s8 = s4 + pltpu.roll(s4, 4, 0)
    s16 = s8 + pltpu.roll(s8, 8, 0)
    lane = lax.broadcasted_iota(jnp.int32, (1, POOL_WIDTH), 1)
    win = jnp.where(lane < POOL_GROUP, s2,
                    jnp.where(lane < 2 * POOL_GROUP, s4, jnp.where(lane < 3 * POOL_GROUP, s8, s16)))
    win = win[POOL_HALO:, :]
    width = jnp.where(lane < POOL_GROUP, POOL_WINDOWS[0],
                      jnp.where(lane < 2 * POOL_GROUP, POOL_WINDOWS[1],
                                jnp.where(lane < 3 * POOL_GROUP, POOL_WINDOWS[2], POOL_WINDOWS[3])))
    t = i * tm + lax.broadcasted_iota(jnp.int32, (tm, 1), 0)
    count = jnp.minimum(t + 1, width).astype(F32)
    token_mix = win / count - u
    pooled = jnp.dot(token_mix.astype(BF16), wpool_ref[...], preferred_element_type=F32) * pscale_ref[...]
    mixed = jnp.dot(pooled.astype(BF16), wout_p_ref[...], preferred_element_type=F32)
    attn = jnp.concatenate([attn_ref[0, hd] for hd in range(N_HEADS)], axis=1)
    mixed += jnp.dot(attn, wout_a_ref[...], preferred_element_type=F32)

    g1 = mod_ref[0, 2:3, :]
    sh2 = mod_ref[0, 3:4, :]
    sc2 = mod_ref[0, 4:5, :]
    x1 = _layer_norm(DEEPNORM_ALPHA * x_ref[0] + g1 * mixed, lng_ref[...], lnb_ref[...])
    x1_ref[0] = x1
    h2 = x1 * (1.0 + sc2) + sh2
    h_hi, h_lo = _split_hi_lo(h2)
    h2_ref[0] = _pack_bf16_pairs(h_hi)
    nt = (((1,), (1,)), ((), ()))
    lt = lax.dot_general(wr_hi_ref[...], h_hi, nt, preferred_element_type=F32)
    lt += lax.dot_general(wr_hi_ref[...], h_lo, nt, preferred_element_type=F32)
    lt += lax.dot_general(wr_lo_ref[...], h_hi, nt, preferred_element_type=F32)
    lt_ref[...] = lt


def _post_attn(x, u, attn, mod, wpool_bd, pscale, wout_p, wout_a, lng, lnb, wr_hi, wr_lo, x_off, b_off):
    B, S, _ = u.shape
    D = x.shape[2]
    tm = TM_PROJ
    nt = S // tm
    const = lambda b, i: (0, 0)
    halo_blocks = tm // POOL_HALO
    return pl.pallas_call(
        _post_kernel,
        grid=(B, nt),
        in_specs=[
            pl.BlockSpec((1, tm, D), lambda b, i: (b + x_off, i, 0)),
            pl.BlockSpec((1, tm, POOL_WIDTH), lambda b, i: (b, i, 0)),
            pl.BlockSpec((1, POOL_HALO, POOL_WIDTH), lambda b, i: (b, jnp.maximum(i * halo_blocks - 1, 0), 0)),
            pl.BlockSpec((1, N_HEADS, tm, V_HEAD), lambda b, i: (b, 0, i, 0)),
            pl.BlockSpec((1, 6, D), lambda b, i: (b + b_off, 0, 0)),
            pl.BlockSpec(wpool_bd.shape, const),
            pl.BlockSpec(pscale.shape, const),
            pl.BlockSpec(wout_p.shape, const),
            pl.BlockSpec(wout_a.shape, const),
            pl.BlockSpec(lng.shape, const),
            pl.BlockSpec(lnb.shape, const),
            pl.BlockSpec(wr_hi.shape, const),
            pl.BlockSpec(wr_lo.shape, const),
        ],
        out_specs=[
            pl.BlockSpec((1, tm, D), lambda b, i: (b, i, 0)),
            pl.BlockSpec((1, tm, HALF), lambda b, i: (b, i, 0)),
            pl.BlockSpec((N_EXPERTS, tm), lambda b, i: (0, b * nt + i)),
        ],
        out_shape=[
            jax.ShapeDtypeStruct((B, S, D), F32),
            jax.ShapeDtypeStruct((B, S, HALF), jnp.int32),
            jax.ShapeDtypeStruct((N_EXPERTS, B * S), F32),
        ],
        compiler_params=_cparams(("arbitrary", "arbitrary")),
        name="post_attn",
    )(x, u, u, attn, mod, wpool_bd, pscale, wout_p, wout_a, lng, lnb, wr_hi, wr_lo)


def _select_first_max(vals, n_rounds, payload=None):
    work = list(vals)
    sel = [None] * len(vals)
    rounds = []
    for _ in range(n_rounds):
        m = functools.reduce(jnp.maximum, work)
        taken = None
        win_idx = jnp.zeros(m.shape, jnp.int32)
        win_val = jnp.zeros(m.shape, F32)
        for e in range(len(work)):
            hit = work[e] == m
            first = hit if taken is None else jnp.logical_and(hit, jnp.logical_not(taken))
            taken = hit if taken is None else jnp.logical_or(taken, hit)
            sel[e] = first if sel[e] is None else jnp.logical_or(sel[e], first)
            work[e] = jnp.where(first, -jnp.inf, work[e])
            if payload is not None:
                win_idx = jnp.where(first, e, win_idx)
                win_val = jnp.where(first, payload[e], win_val)
        rounds.append((win_idx, win_val))
    return sel, rounds


def _route_kernel(bias_ref, lt_ref, idx_ref, gate_ref):
    scores = [jax.nn.sigmoid(lt_ref[e]) for e in range(N_EXPERTS)]
    choice = [scores[e] + bias_ref[e] for e in range(N_EXPERTS)]
    group_score = []
    for g in range(N_GROUPS):
        vals = choice[g * GROUP_SIZE:(g + 1) * GROUP_SIZE]
        sel2, _ = _select_first_max(vals, 2)
        group_score.append(functools.reduce(
            jnp.add, [jnp.where(sel2[j], vals[j], 0.0) for j in range(GROUP_SIZE)]))
    group_sel, _ = _select_first_max(group_score, TOPK_GROUPS)
    masked = [jnp.where(group_sel[e // GROUP_SIZE], choice[e], -jnp.inf) for e in range(N_EXPERTS)]
    _, rounds = _select_first_max(masked, TOP_K, payload=scores)
    total = functools.reduce(jnp.add, [w for _, w in rounds])
    for k, (e_k, w_k) in enumerate(rounds):
        idx_ref[k] = e_k
        gate_ref[k] = w_k / total * ROUTED_SCALE


def _route(lt3, bias):
    E, R, _ = lt3.shape
    out_spec = pl.BlockSpec((TOP_K, ROUTE_ROWS, LANES), lambda r: (0, r, 0))
    return pl.pallas_call(
        _route_kernel,
        grid=(R // ROUTE_ROWS,),
        in_specs=[
            pl.BlockSpec(memory_space=pltpu.SMEM),
            pl.BlockSpec((E, ROUTE_ROWS, LANES), lambda r: (0, r, 0)),
        ],
        out_specs=[out_spec, out_spec],
        out_shape=[jax.ShapeDtypeStruct((TOP_K, R, LANES), jnp.int32),
                   jax.ShapeDtypeStruct((TOP_K, R, LANES), F32)],
        compiler_params=_cparams(("arbitrary",)),
        name="route",
    )(bias, lt3)


def _sc_mesh():
    return plsc.VectorSubcoreMesh(core_axis_name="c", subcore_axis_name="s",
                                  num_cores=SC_CORES, num_subcores=SC_SUBCORES)


def _sc_worker_id():
    return lax.axis_index("s") * SC_CORES + lax.axis_index("c")


def _sc_gather(table, idx):
    n = idx.shape[0]
    width = table.shape[1]
    workers = SC_CORES * SC_SUBCORES
    per_worker = n // workers
    n_win = per_worker // SC_WINDOW
    assert n == workers * n_win * SC_WINDOW

    @functools.partial(
        pl.kernel, mesh=_sc_mesh(),
        out_type=jax.ShapeDtypeStruct((n, width), table.dtype),
        scratch_types=[
            pltpu.VMEM((n_win, SC_WINDOW), jnp.int32),
            pltpu.VMEM((SC_WINDOW, width), table.dtype),
            pltpu.SemaphoreType.DMA,
        ],
    )
    def gather(table_hbm, idx_hbm, out_hbm, idx_v, rows_v, sem):
        wid = _sc_worker_id()
        pltpu.sync_copy(idx_hbm.at[wid], idx_v)

        @pl.loop(0, n_win)
        def _(j):
            pltpu.async_copy(table_hbm.at[idx_v.at[j]], rows_v, sem).wait()
            pltpu.sync_copy(rows_v, out_hbm.at[pl.ds(wid * per_worker + j * SC_WINDOW, SC_WINDOW)])

    return gather(table, idx.reshape(workers, n_win, SC_WINDOW))


def _sc_gather_sum(ys, dest_kt, gates_kt):
    K, T = dest_kt.shape
    half = ys.shape[1]
    workers = SC_CORES * SC_SUBCORES
    per_worker = T // workers
    wt = SC_SUM_TOKENS
    n_win = per_worker // wt
    n_idx = K * wt
    assert T == workers * n_win * wt and n_idx <= 128 and half % SC_LANES == 0
    dest_w = dest_kt.reshape(K, workers, n_win, wt).transpose(1, 2, 0, 3).reshape(workers, n_win, n_idx)
    gates_w = gates_kt.reshape(K, workers, n_win, wt).transpose(1, 2, 0, 3).reshape(workers * n_win * n_idx, 1)
    gates_w = jnp.broadcast_to(gates_w, (workers * n_win * n_idx, SC_LANES))

    @functools.partial(
        pl.kernel, mesh=_sc_mesh(),
        out_type=jax.ShapeDtypeStruct((T, 2 * half), F32),
        scratch_types=[
            pltpu.VMEM((n_win, n_idx), jnp.int32),
            pltpu.VMEM((n_idx, half), jnp.int32),
            pltpu.VMEM((n_idx, SC_LANES), F32),
            pltpu.VMEM((wt, 2 * half), F32),
            pltpu.SemaphoreType.DMA,
        ],
        compiler_params=pltpu.CompilerParams(needs_layout_passes=False),
    )
    def gather_sum(ys_hbm, dest_hbm, gates_hbm, out_hbm, idx_v, rows_v, gate_v, out_v, sem):
        wid = _sc_worker_id()
        pltpu.sync_copy(dest_hbm.at[wid], idx_v)

        @pl.loop(0, n_win)
        def _(j):
            pltpu.async_copy(ys_hbm.at[idx_v.at[j]], rows_v, sem).wait()
            pltpu.sync_copy(gates_hbm.at[pl.ds((wid * n_win + j) * n_idx, n_idx)], gate_v)

            @pl.loop(0, wt)
            def _(t):
                g = [gate_v[k * wt + t, :] for k in range(K)]

                @pl.loop(0, half // SC_LANES)
                def _(c):
                    col = pl.ds(c * SC_LANES, SC_LANES)
                    acc_lo = jnp.zeros((SC_LANES,), F32)
                    acc_hi = jnp.zeros((SC_LANES,), F32)
                    for k in range(K):
                        w = rows_v[k * wt + t, col]
                        acc_lo = acc_lo + g[k] * plsc.bitcast(w << 16, F32)
                        acc_hi = acc_hi + g[k] * plsc.bitcast(w & HIGH_HALF_MASK, F32)
                    out_v[t, col] = acc_lo
                    out_v[t, pl.ds(half + c * SC_LANES, SC_LANES)] = acc_hi

            pltpu.sync_copy(out_v, out_hbm.at[pl.ds(wid * per_worker + j * wt, wt)])

    return gather_sum(ys, dest_w, gates_w)


def _sc_dispatch(rows, dest_kt, n_slots):
    T, width = rows.shape
    K = dest_kt.shape[0]
    workers = SC_CORES * SC_SUBCORES
    per_worker = T // workers
    n_win = per_worker // SC_WINDOW
    assert T == workers * n_win * SC_WINDOW
    dest_w = dest_kt.reshape(K, workers, n_win, SC_WINDOW).transpose(1, 2, 0, 3)
    dest_w = dest_w.reshape(workers, n_win * K, SC_WINDOW)

    @functools.partial(
        pl.kernel, mesh=_sc_mesh(),
        out_type=jax.ShapeDtypeStruct((n_slots, width), rows.dtype),
        scratch_types=[
            pltpu.VMEM((n_win * K, SC_WINDOW), jnp.int32),
            pltpu.VMEM((SC_WINDOW, width), rows.dtype),
        ],
    )
    def dispatch(rows_hbm, dest_hbm, out_hbm, idx_v, rows_v):
        wid = _sc_worker_id()
        pltpu.sync_copy(dest_hbm.at[wid], idx_v)

        @pl.loop(0, n_win)
        def _(j):
            pltpu.sync_copy(rows_hbm.at[pl.ds(wid * per_worker + j * SC_WINDOW, SC_WINDOW)], rows_v)
            for k in range(K):
                pltpu.sync_copy(rows_v, out_hbm.at[idx_v.at[j * K + k]])

    return dispatch(rows, dest_w)


def _swiglu(h_lo, h_hi, wg_ref, wu_ref):
    def proj(w_ref):
        return (jnp.dot(h_lo, w_ref[:HALF, :], preferred_element_type=F32)
                + jnp.dot(h_hi, w_ref[HALF:, :], preferred_element_type=F32))
    gate = proj(wg_ref)
    up = proj(wu_ref)
    return (gate * jax.nn.sigmoid(gate) * up).astype(BF16)


def _expert_kernel(seg_first_ref, seg_blocks_ref, n_real_ref, xs_hbm, wg_ref, wu_ref, wd_ref, ys_hbm,
                   xbuf, ybuf, wg_bf, wu_bf, wd_bf, in_sems, out_sems):
    e = pl.program_id(0)
    first = seg_first_ref[e]
    count = seg_blocks_ref[e]
    total = n_real_ref[0]
    blk = DISPATCH_BLOCK
    ring = xbuf.shape[0]

    def in_copy(g):
        slot = g % ring
        return pltpu.make_async_copy(xs_hbm.at[pl.ds(g * blk, blk)], xbuf.at[slot], in_sems.at[slot])

    def out_copy(g):
        slot = g % ring
        return pltpu.make_async_copy(ybuf.at[slot], ys_hbm.at[pl.ds(g * blk, blk)], out_sems.at[slot])

    @pl.when(e == 0)
    def _():
        for g in range(ring - 1):
            @pl.when(g < total)
            def _():
                in_copy(g).start()

    @pl.when(count > 0)
    def _():
        wg_bf[...] = wg_ref[0, 0].astype(BF16)
        wu_bf[...] = wu_ref[0, 0].astype(BF16)
        wd_bf[...] = wd_ref[0, 0].astype(BF16)

    def block(g, _):
        slot = g % ring
        in_copy(g).wait()

        @pl.when(g + ring - 1 < total)
        def _():
            in_copy(g + ring - 1).start()

        @pl.when(g >= ring)
        def _():
            out_copy(g - ring).wait()

        lo, hi = _unpack_bf16_pairs(xbuf[slot])
        a = _swiglu(lo.astype(BF16), hi.astype(BF16), wg_bf, wu_bf)
        y = jnp.dot(a, wd_bf[...], preferred_element_type=F32)
        ybuf[slot] = _pack_bf16_pairs(y.astype(BF16))
        out_copy(g).start()
        return 0

    lax.fori_loop(first, first + count, block, 0)

    @pl.when(e == pl.num_programs(0) - 1)
    def _():
        for back in range(ring, 0, -1):
            @pl.when(total >= back)
            def _():
                out_copy(total - back).wait()


def _expert_mm(seg_first, seg_blocks, n_real, xs, wg, wu, wd, layer):
    P = xs.shape[0]
    D = wg.shape[2]
    w_map = lambda e, sf, sb, nr: (layer, e, 0, 0)
    grid_spec = pltpu.PrefetchScalarGridSpec(
        num_scalar_prefetch=3,
        grid=(N_EXPERTS,),
        in_specs=[
            pl.BlockSpec(memory_space=pl.ANY),
            pl.BlockSpec((1, 1, D, D_EXPERT), w_map),
            pl.BlockSpec((1, 1, D, D_EXPERT), w_map),
            pl.BlockSpec((1, 1, D_EXPERT, D), w_map),
        ],
        out_specs=pl.BlockSpec(memory_space=pl.ANY),
        scratch_shapes=[
            pltpu.VMEM((EXPERT_RING, DISPATCH_BLOCK, HALF), jnp.int32),
            pltpu.VMEM((EXPERT_RING, DISPATCH_BLOCK, HALF), jnp.int32),
            pltpu.VMEM((D, D_EXPERT), BF16), pltpu.VMEM((D, D_EXPERT), BF16), pltpu.VMEM((D_EXPERT, D), BF16),
            pltpu.SemaphoreType.DMA((EXPERT_RING,)), pltpu.SemaphoreType.DMA((EXPERT_RING,)),
        ],
    )
    return pl.pallas_call(
        _expert_kernel,
        grid_spec=grid_spec,
        out_shape=jax.ShapeDtypeStruct((P, HALF), jnp.int32),
        compiler_params=_cparams(("arbitrary",)),
        name="expert_mm",
    )(seg_first, seg_blocks, n_real, xs, wg, wu, wd)


def _rank_kernel(idx_ref, rank_ref, cnt_ref, tri_ref, carry_ref):
    i = pl.program_id(0)
    tm = idx_ref.shape[1]

    @pl.when(i == 0)
    def _():
        r = lax.broadcasted_iota(jnp.int32, (tm, tm), 0)
        c = lax.broadcasted_iota(jnp.int32, (tm, tm), 1)
        tri_ref[...] = (r < c).astype(BF16)
        carry_ref[...] = jnp.zeros(carry_ref.shape, F32)

    e_iota = lax.broadcasted_iota(jnp.int32, (N_EXPERTS, tm), 0)
    hits = [e_iota == idx_ref[k:k + 1, :] for k in range(TOP_K)]
    mask = functools.reduce(jnp.logical_or, hits).astype(F32)
    rank = jnp.dot(mask.astype(BF16), tri_ref[...], preferred_element_type=F32) + carry_ref[:, 0:1]
    for k in range(TOP_K):
        rank_ref[k:k + 1, :] = jnp.sum(jnp.where(hits[k], rank, 0.0), axis=0, keepdims=True).astype(jnp.int32)
    carry_ref[...] = carry_ref[...] + jnp.sum(mask, axis=1, keepdims=True)
    cnt_ref[...] = carry_ref[...]


def _rank(idx_kt):
    K, T = idx_kt.shape
    tm = 1024
    return pl.pallas_call(
        _rank_kernel,
        grid=(T // tm,),
        in_specs=[pl.BlockSpec((K, tm), lambda i: (0, i))],
        out_specs=[pl.BlockSpec((K, tm), lambda i: (0, i)),
                   pl.BlockSpec((N_EXPERTS, LANES), lambda i: (0, 0))],
        out_shape=[jax.ShapeDtypeStruct((K, T), jnp.int32),
                   jax.ShapeDtypeStruct((N_EXPERTS, LANES), F32)],
        scratch_shapes=[pltpu.VMEM((tm, tm), BF16), pltpu.VMEM((N_EXPERTS, LANES), F32)],
        compiler_params=_cparams(("arbitrary",)),
        name="rank",
    )(idx_kt)


def _dest_kernel(idx_ref, rank_ref, start_ref, dest_ref):
    tm = idx_ref.shape[1]
    e_iota = lax.broadcasted_iota(jnp.int32, (N_EXPERTS, tm), 0)
    start = start_ref[:, 0:1]
    for k in range(TOP_K):
        base = jnp.sum(jnp.where(e_iota == idx_ref[k:k + 1, :], start, 0.0), axis=0, keepdims=True)
        dest_ref[k:k + 1, :] = rank_ref[k:k + 1, :] + base.astype(jnp.int32)


def _dest(idx_kt, rank_kt, start):
    K, T = idx_kt.shape
    tm = 1024
    tok_spec = pl.BlockSpec((K, tm), lambda i: (0, i))
    return pl.pallas_call(
        _dest_kernel,
        grid=(T // tm,),
        in_specs=[tok_spec, tok_spec, pl.BlockSpec((N_EXPERTS, LANES), lambda i: (0, 0))],
        out_specs=tok_spec,
        out_shape=jax.ShapeDtypeStruct((K, T), jnp.int32),
        compiler_params=_cparams(("arbitrary",)),
        name="dest",
    )(idx_kt, rank_kt, start)


def _block_plan(counts):
    blk = DISPATCH_BLOCK
    seg_blocks = (counts.astype(jnp.int32) + blk - 1) // blk
    seg_end = jnp.cumsum(seg_blocks)
    seg_first = seg_end - seg_blocks
    return seg_first * blk, seg_first, seg_blocks, seg_end[-1:]


def _combine_kernel(h_ref, x1_ref, routed_ref, mod_ref, wsg_ref, wsu_ref, wsd_ref, lng_ref, lnb_ref, o_ref):
    lo, hi = _unpack_bf16_pairs(h_ref[...])
    a = _swiglu(lo.astype(BF16), hi.astype(BF16), wsg_ref, wsu_ref)
    ffn = jnp.dot(a, wsd_ref[...], preferred_element_type=F32) + routed_ref[...]
    g2 = mod_ref[0, 5:6, :]
    o_ref[...] = _layer_norm(DEEPNORM_ALPHA * x1_ref[...] + g2 * ffn, lng_ref[...], lnb_ref[...])


def _combine_into_kernel(h_ref, x1_ref, routed_ref, mod_ref, wsg_ref, wsu_ref, wsd_ref, lng_ref, lnb_ref,
                         prev_ref, o_ref):
    del prev_ref
    _combine_kernel(h_ref, x1_ref, routed_ref, mod_ref, wsg_ref, wsu_ref, wsd_ref, lng_ref, lnb_ref, o_ref)


def _combine(h2p, x1, routed, mod, wsg, wsu, wsd, lng, lnb, seq, b_off, out_rows, out_off, out_prev):
    T, D = x1.shape
    tm = TM_COMBINE
    per_seq = seq // tm
    blk_off = out_off // tm
    const = lambda i: (0, 0)
    in_specs = [
        pl.BlockSpec((tm, HALF), lambda i: (i, 0)),
        pl.BlockSpec((tm, D), lambda i: (i, 0)),
        pl.BlockSpec((tm, D), lambda i: (i, 0)),
        pl.BlockSpec((1, 6, D), lambda i: (i // per_seq + b_off, 0, 0)),
        pl.BlockSpec(wsg.shape, const),
        pl.BlockSpec(wsu.shape, const),
        pl.BlockSpec(wsd.shape, const),
        pl.BlockSpec(lng.shape, const),
        pl.BlockSpec(lnb.shape, const),
    ]
    args = [h2p, x1, routed, mod, wsg, wsu, wsd, lng, lnb]
    body = _combine_kernel
    aliases = {}
    if out_prev is not None:
        in_specs.append(pl.BlockSpec(memory_space=pl.ANY))
        args.append(out_prev)
        aliases = {len(args) - 1: 0}
        body = _combine_into_kernel
    return pl.pallas_call(
        body,
        grid=(T // tm,),
        in_specs=in_specs,
        out_specs=pl.BlockSpec((tm, D), lambda i: (i + blk_off, 0)),
        out_shape=jax.ShapeDtypeStruct((out_rows, D), F32),
        input_output_aliases=aliases,
        compiler_params=_cparams(("arbitrary",)),
        name="combine",
    )(*args)


def _rope_pair(w):
    half = QK_ROPE // 2
    return w, jnp.concatenate([-w[:, half:], w[:, :half]], axis=1)


def _prep_layer(w_in, w_uq, w_ukv, w_pool, w_out, w_router):
    D = w_in.shape[0]
    zpad = lambda r: jnp.zeros((r, LANES - QK_ROPE), F32)
    lat = POOL_WIDTH + Q_LORA + KV_LORA
    k_a, k_b = _rope_pair(w_in[:, lat:lat + QK_ROPE])
    w_in_ext = jnp.concatenate([w_in[:, :lat], k_a, zpad(D), k_b, zpad(D)], axis=1).astype(BF16)

    qa_cols, qb_cols = [], []
    per_head = QK_NOPE + QK_ROPE
    for hd in range(N_HEADS):
        w_h = w_uq[:, hd * per_head:(hd + 1) * per_head]
        r_a, r_b = _rope_pair(w_h[:, QK_NOPE:])
        qa_cols += [w_h[:, :QK_NOPE], r_a, zpad(Q_LORA)]
        qb_cols += [r_b, zpad(Q_LORA)]
    w_uqa = jnp.concatenate(qa_cols, axis=1).astype(BF16)
    w_uqb = jnp.concatenate(qb_cols, axis=1).astype(BF16)

    wpool_bd = jax.scipy.linalg.block_diag(*[w_pool[g] for g in range(len(POOL_WINDOWS))]).astype(BF16)
    wr_hi, wr_lo = _split_hi_lo(w_router.T)
    return dict(w_in_ext=w_in_ext, w_uqa=w_uqa, w_uqb=w_uqb, w_ukv=w_ukv.astype(BF16), wpool_bd=wpool_bd,
                wout_p=w_out[:POOL_WIDTH].astype(BF16), wout_a=w_out[POOL_WIDTH:].astype(BF16),
                wr_hi=wr_hi, wr_lo=wr_lo)


def kernel(x, c, positions, w_ada, b_ada, w_in, q_norm_g, kv_norm_g, w_uq, w_ukv, w_pool, pool_scale,
           w_out, ln1_g, ln1_b, w_router, router_bias, w_exp_gate, w_exp_up, w_exp_down,
           w_sh_gate, w_sh_up, w_sh_down, ln2_g, ln2_b):
    B, S, D = x.shape
    L = w_in.shape[0]
    row = lambda v: v.reshape(1, -1)

    mod_all = _ada_mod(c, w_ada, b_ada).reshape(L, B, 6, D)
    cs_all = _rope_table(positions)
    bs = B // N_STREAMS
    T = bs * S
    n_blocks = T * TOP_K // DISPATCH_BLOCK + N_EXPERTS
    xs_streams = [x] * N_STREAMS
    result = None
    for l in range(L):
        p = _prep_layer(w_in[l], w_uq[l], w_ukv[l], w_pool[l], w_out[l], w_router[l])
        shared = (w_sh_gate[l].astype(BF16), w_sh_up[l].astype(BF16), w_sh_down[l].astype(BF16))
        mod = mod_all[l]
        for s in range(N_STREAMS):
            xl = xs_streams[s]
            b_off = s * bs
            x_off = b_off if l == 0 else 0
            u, q, k, v = _in_proj(xl, mod, cs_all, p["w_in_ext"], row(q_norm_g[l]), row(kv_norm_g[l]),
                                  p["w_uqa"], p["w_uqb"], p["w_ukv"], bs, x_off, b_off)
            attn = _attention(q, k, v)
            x1, h2p, lt = _post_attn(xl, u, attn, mod, p["wpool_bd"], row(pool_scale[l]), p["wout_p"],
                                     p["wout_a"], row(ln1_g[l]), row(ln1_b[l]), p["wr_hi"], p["wr_lo"],
                                     x_off, b_off)
            idx_k, gate_k = _route(lt.reshape(N_EXPERTS, T // LANES, LANES), router_bias[l])
            idx_kt = idx_k.reshape(TOP_K, T)
            gates_kt = gate_k.reshape(TOP_K, T)
            rank_kt, counts = _rank(idx_kt)
            pad_start, seg_first, seg_blocks, n_real = _block_plan(counts[:, 0])
            start = jnp.broadcast_to(pad_start.astype(F32)[:, None], (N_EXPERTS, LANES))
            dest_kt = _dest(idx_kt, rank_kt, start)
            h2p = h2p.reshape(T, HALF)
            rows = _sc_dispatch(h2p, dest_kt, n_blocks * DISPATCH_BLOCK)
            ys = _expert_mm(seg_first, seg_blocks, n_real, rows, w_exp_gate, w_exp_up, w_exp_down, l)
            routed = _sc_gather_sum(ys, dest_kt, gates_kt)
            if l < L - 1:
                xs_streams[s] = _combine(h2p, x1.reshape(T, D), routed, mod, *shared, row(ln2_g[l]),
                                         row(ln2_b[l]), S, b_off, T, 0, None).reshape(bs, S, D)
            else:
                result = _combine(h2p, x1.reshape(T, D), routed, mod, *shared, row(ln2_g[l]),
                                  row(ln2_b[l]), S, b_off, B * S, s * T, result)
    return result.reshape(B, S, D)
```

```python
import functools

import jax
import jax.numpy as jnp
from jax import lax
from jax.experimental import pallas as pl
from jax.experimental.pallas import tpu as pltpu
from jax.experimental.pallas import tpu_sc as plsc

F32 = jnp.float32
BF16 = jnp.bfloat16

D_MODEL = 1024
DEPTH = 4
POOL_WIDTH = 256
POOL_WINDOWS = (2, 4, 8, 16)
POOL_GROUP = 64
POOL_HALO = 16
QK_NOPE = 128
QK_ROPE = 64
V_HEAD = 128
N_HEADS = 6
Q_LORA = 384
KV_LORA = 256
ROPE_THETA = 10000.0
N_EXPERTS = 64
TOP_K = 8
N_GROUPS = 8
GROUP_SIZE = N_EXPERTS // N_GROUPS
TOPK_GROUPS = 4
D_EXPERT = 256
ROUTED_SCALE = 2.5
DEEPNORM_ALPHA = (2.0 * DEPTH) ** 0.25
LN_EPS = 1e-5
RMS_EPS = 1e-6
ATTN_SCALE = (QK_NOPE + QK_ROPE) ** -0.5
LOG2_E = 1.4426950408889634
Q_SCALE = ATTN_SCALE * LOG2_E
LANES = 128
QK_PAD = 2 * LANES
MASK_VALUE = -1e30

TM_PROJ = 512
TQ = 512
TK = 512
ROUTE_ROWS = 8
DISPATCH_BLOCK = 512
EXPERT_RING = 6
TM_COMBINE = 512
SC_CORES = 2
SC_SUBCORES = 16
SC_WINDOW = 128
SC_LANES = 16
HIGH_HALF_MASK = -65536
SC_SUM_TOKENS = 16
N_STREAMS = 2
HALF = D_MODEL // 2
VMEM_LIMIT = 48 * 1024 * 1024


def _cparams(sem):
    return pltpu.CompilerParams(dimension_semantics=sem, vmem_limit_bytes=VMEM_LIMIT)


def _split_hi_lo(a):
    hi = a.astype(BF16)
    lo = (a - hi.astype(F32)).astype(BF16)
    return hi, lo


def _pack_bf16_pairs(a):
    bits = lax.bitcast_convert_type(a.astype(F32), jnp.uint32)
    half = a.shape[1] // 2
    word = (bits[:, :half] >> 16) | (bits[:, half:] & jnp.uint32(0xFFFF0000))
    return lax.bitcast_convert_type(word, jnp.int32)


def _unpack_bf16_pairs(w):
    bits = lax.bitcast_convert_type(w, jnp.uint32)
    lo = lax.bitcast_convert_type(bits << 16, F32)
    hi = lax.bitcast_convert_type(bits & jnp.uint32(0xFFFF0000), F32)
    return lo, hi


def _ada_kernel(c_ref, w_ref, b_ref, o_ref):
    c = c_ref[...]
    cond = c * jax.nn.sigmoid(c)
    c_hi, c_lo = _split_hi_lo(cond)
    w_hi, w_lo = _split_hi_lo(w_ref[0])
    acc = jnp.dot(c_hi, w_hi, preferred_element_type=F32)
    acc += jnp.dot(c_lo, w_hi, preferred_element_type=F32)
    acc += jnp.dot(c_hi, w_lo, preferred_element_type=F32)
    o_ref[0] = acc + b_ref[0]


def _ada_mod(c, w_ada, b_ada):
    L, D, N = w_ada.shape
    B = c.shape[0]
    tn = 1536
    return pl.pallas_call(
        _ada_kernel,
        grid=(L, N // tn),
        in_specs=[
            pl.BlockSpec((B, D), lambda l, j: (0, 0)),
            pl.BlockSpec((1, D, tn), lambda l, j: (l, 0, j)),
            pl.BlockSpec((1, 1, tn), lambda l, j: (l, 0, j)),
        ],
        out_specs=pl.BlockSpec((1, B, tn), lambda l, j: (l, 0, j)),
        out_shape=jax.ShapeDtypeStruct((L, B, N), F32),
        compiler_params=_cparams(("arbitrary", "arbitrary")),
        name="ada_mod",
    )(c, w_ada, b_ada.reshape(L, 1, N))


def _rope_table_kernel(pos_ref, freq_ref, o_ref):
    ang = pos_ref[0].astype(F32) * freq_ref[...]
    o_ref[0, :, 0:LANES] = jnp.cos(ang)
    o_ref[0, :, LANES:2 * LANES] = jnp.sin(ang)


def _rope_table(positions):
    B, S = positions.shape
    half = QK_ROPE // 2
    inv_freq = ROPE_THETA ** (-jnp.arange(half, dtype=F32) / half)
    freq = jnp.tile(inv_freq, LANES // half).reshape(1, LANES)
    return pl.pallas_call(
        _rope_table_kernel,
        grid=(B,),
        in_specs=[
            pl.BlockSpec((1, S, 1), lambda b: (b, 0, 0)),
            pl.BlockSpec((1, LANES), lambda b: (0, 0)),
        ],
        out_specs=pl.BlockSpec((1, S, 2 * LANES), lambda b: (b, 0, 0)),
        out_shape=jax.ShapeDtypeStruct((B, S, 2 * LANES), F32),
        compiler_params=_cparams(("arbitrary",)),
        name="rope_table",
    )(positions.reshape(B, S, 1), freq)


def _rms(x, g):
    return x * lax.rsqrt(jnp.mean(x * x, axis=-1, keepdims=True) + RMS_EPS) * g


def _in_proj_kernel(x_ref, mod_ref, cs_ref, w_in_ref, qg_ref, kvg_ref, w_uqa_ref, w_uqb_ref, w_ukv_ref,
                    u_ref, q_ref, k_ref, v_ref):
    x = x_ref[0]
    sh1 = mod_ref[0, 0:1, :]
    sc1 = mod_ref[0, 1:2, :]
    h = (x * (1.0 + sc1) + sh1).astype(BF16)
    proj = jnp.dot(h, w_in_ref[...], preferred_element_type=F32)
    u_ref[0] = proj[:, 0:POOL_WIDTH]
    o = POOL_WIDTH
    q_lat = proj[:, o:o + Q_LORA]
    o += Q_LORA
    kv_lat = proj[:, o:o + KV_LORA]
    o += KV_LORA
    k_a = proj[:, o:o + LANES]
    k_b = proj[:, o + LANES:o + 2 * LANES]
    cosv = cs_ref[0, :, 0:LANES]
    sinv = cs_ref[0, :, LANES:2 * LANES]
    k_rot = (k_a * cosv + k_b * sinv).astype(BF16)

    qn = _rms(q_lat, qg_ref[...]).astype(BF16)
    q_a = jnp.dot(qn, w_uqa_ref[...], preferred_element_type=F32)
    q_b = jnp.dot(qn, w_uqb_ref[...], preferred_element_type=F32)
    kvn = _rms(kv_lat, kvg_ref[...]).astype(BF16)
    kv = jnp.dot(kvn, w_ukv_ref[...], preferred_element_type=F32)
    for hd in range(N_HEADS):
        b0 = hd * QK_PAD
        q_ref[0, hd, :, 0:LANES] = (q_a[:, b0:b0 + LANES] * Q_SCALE).astype(BF16)
        q_rot = q_a[:, b0 + LANES:b0 + QK_PAD] * cosv + q_b[:, hd * LANES:(hd + 1) * LANES] * sinv
        q_ref[0, hd, :, LANES:QK_PAD] = (q_rot * Q_SCALE).astype(BF16)
        k_ref[0, hd, :, 0:LANES] = kv[:, b0:b0 + QK_NOPE].astype(BF16)
        k_ref[0, hd, :, LANES:QK_PAD] = k_rot
        v_ref[0, hd] = kv[:, b0 + QK_NOPE:b0 + QK_NOPE + V_HEAD].astype(BF16)


def _in_proj(x, mod, cs, w_in_ext, qg, kvg, w_uqa, w_uqb, w_ukv_bf, B, x_off, b_off):
    _, S, D = x.shape
    tm = TM_PROJ
    const = lambda b, i: (0, 0)
    return pl.pallas_call(
        _in_proj_kernel,
        grid=(B, S // tm),
        in_specs=[
            pl.BlockSpec((1, tm, D), lambda b, i: (b + x_off, i, 0)),
            pl.BlockSpec((1, 6, D), lambda b, i: (b + b_off, 0, 0)),
            pl.BlockSpec((1, tm, 2 * LANES), lambda b, i: (b + b_off, i, 0)),
            pl.BlockSpec(w_in_ext.shape, const),
            pl.BlockSpec(qg.shape, const),
            pl.BlockSpec(kvg.shape, const),
            pl.BlockSpec(w_uqa.shape, const),
            pl.BlockSpec(w_uqb.shape, const),
            pl.BlockSpec(w_ukv_bf.shape, const),
        ],
        out_specs=[
            pl.BlockSpec((1, tm, POOL_WIDTH), lambda b, i: (b, i, 0)),
            pl.BlockSpec((1, N_HEADS, tm, QK_PAD), lambda b, i: (b, 0, i, 0)),
            pl.BlockSpec((1, N_HEADS, tm, QK_PAD), lambda b, i: (b, 0, i, 0)),
            pl.BlockSpec((1, N_HEADS, tm, V_HEAD), lambda b, i: (b, 0, i, 0)),
        ],
        out_shape=[
            jax.ShapeDtypeStruct((B, S, POOL_WIDTH), F32),
            jax.ShapeDtypeStruct((B, N_HEADS, S, QK_PAD), BF16),
            jax.ShapeDtypeStruct((B, N_HEADS, S, QK_PAD), BF16),
            jax.ShapeDtypeStruct((B, N_HEADS, S, V_HEAD), BF16),
        ],
        compiler_params=_cparams(("arbitrary", "arbitrary")),
        name="in_proj",
    )(x, mod, cs, w_in_ext, qg, kvg, w_uqa, w_uqb, w_ukv_bf)


def _softmax_step(q, k, v, carry, mask):
    m, l, acc = carry
    s = lax.dot_general(q, k, (((1,), (1,)), ((), ())), preferred_element_type=F32)
    if mask is not None:
        s = jnp.where(mask, s, MASK_VALUE)
    m_new = jnp.maximum(m, jnp.max(s, axis=-1, keepdims=True))
    alpha = jnp.exp2(m - m_new)
    p = jnp.exp2(s - m_new)
    l_new = alpha * l + jnp.sum(p, axis=-1, keepdims=True)
    acc_new = alpha * acc + jnp.dot(p.astype(BF16), v, preferred_element_type=F32)
    return m_new, l_new, acc_new


def _attn_kernel(q_ref, k_ref, v_ref, o_ref):
    seq = q_ref.shape[2]
    row = lax.broadcasted_iota(jnp.int32, (TQ, TK), 0)
    col = lax.broadcasted_iota(jnp.int32, (TQ, TK), 1)
    diag = row >= col
    for i in range(seq // TQ):
        q = q_ref[0, 0, i * TQ:(i + 1) * TQ, :]
        carry = (jnp.full((TQ, 1), MASK_VALUE, F32), jnp.zeros((TQ, 1), F32), jnp.zeros((TQ, V_HEAD), F32))
        for j in range(i + 1):
            k = k_ref[0, 0, j * TK:(j + 1) * TK, :]
            v = v_ref[0, 0, j * TK:(j + 1) * TK, :]
            carry = _softmax_step(q, k, v, carry, diag if j == i else None)
        _, l, acc = carry
        o_ref[0, 0, i * TQ:(i + 1) * TQ, :] = (acc / l).astype(BF16)


def _attention(q, k, v):
    B, H, S, _ = q.shape
    return pl.pallas_call(
        _attn_kernel,
        grid=(B, H),
        in_specs=[
            pl.BlockSpec((1, 1, S, QK_PAD), lambda b, h: (b, h, 0, 0)),
            pl.BlockSpec((1, 1, S, QK_PAD), lambda b, h: (b, h, 0, 0)),
            pl.BlockSpec((1, 1, S, V_HEAD), lambda b, h: (b, h, 0, 0)),
        ],
        out_specs=pl.BlockSpec((1, 1, S, V_HEAD), lambda b, h: (b, h, 0, 0)),
        out_shape=jax.ShapeDtypeStruct((B, H, S, V_HEAD), BF16),
        compiler_params=_cparams(("arbitrary", "arbitrary")),
        name="attention",
    )(q, k, v)


def _layer_norm(v, g, b):
    mu = jnp.mean(v, axis=-1, keepdims=True)
    d = v - mu
    var = jnp.mean(d * d, axis=-1, keepdims=True)
    return d * lax.rsqrt(var + LN_EPS) * g + b


def _post_kernel(x_ref, u_ref, halo_ref, attn_ref, mod_ref, wpool_ref, pscale_ref, wout_p_ref, wout_a_ref,
                 lng_ref, lnb_ref, wr_hi_ref, wr_lo_ref, x1_ref, h2_ref, lt_ref):
    i = pl.program_id(1)
    tm = u_ref.shape[1]
    u = u_ref[0]
    halo = jnp.where(i > 0, halo_ref[0], 0.0)
    ext = jnp.concatenate([halo, u], axis=0)
    s2 = ext + pltpu.roll(ext, 1, 0)
    s4 = s2 + pltpu.roll(s2, 2, 0)
    s8 = s4 + pltpu.roll(s4, 4, 0)
    s16 = s8 + pltpu.roll(s8, 8, 0)
    lane = lax.broadcasted_iota(jnp.int32, (1, POOL_WIDTH), 1)
    win = jnp.where(lane < POOL_GROUP, s2,
                    jnp.where(lane < 2 * POOL_GROUP, s4, jnp.where(lane < 3 * POOL_GROUP, s8, s16)))
    win = win[POOL_HALO:, :]
    width = jnp.where(lane < POOL_GROUP, POOL_WINDOWS[0],
                      jnp.where(lane < 2 * POOL_GROUP, POOL_WINDOWS[1],
                                jnp.where(lane < 3 * POOL_GROUP, POOL_WINDOWS[2], POOL_WINDOWS[3])))
    t = i * tm + lax.broadcasted_iota(jnp.int32, (tm, 1), 0)
    count = jnp.minimum(t + 1, width).astype(F32)
    token_mix = win / count - u
    pooled = jnp.dot(token_mix.astype(BF16), wpool_ref[...], preferred_element_type=F32) * pscale_ref[...]
    mixed = jnp.dot(pooled.astype(BF16), wout_p_ref[...], preferred_element_type=F32)
    attn = jnp.concatenate([attn_ref[0, hd] for hd in range(N_HEADS)], axis=1)
    mixed += jnp.dot(attn, wout_a_ref[...], preferred_element_type=F32)

    g1 = mod_ref[0, 2:3, :]
    sh2 = mod_ref[0, 3:4, :]
    sc2 = mod_ref[0, 4:5, :]
    x1 = _layer_norm(DEEPNORM_ALPHA * x_ref[0] + g1 * mixed, lng_ref[...], lnb_ref[...])
    x1_ref[0] = x1
    h2 = x1 * (1.0 + sc2) + sh2
    h_hi, h_lo = _split_hi_lo(h2)
    h2_ref[0] = _pack_bf16_pairs(h_hi)
    nt = (((1,), (1,)), ((), ()))
    lt = lax.dot_general(wr_hi_ref[...], h_hi, nt, preferred_element_type=F32)
    lt += lax.dot_general(wr_hi_ref[...], h_lo, nt, preferred_element_type=F32)
    lt += lax.dot_general(wr_lo_ref[...], h_hi, nt, preferred_element_type=F32)
    lt_ref[...] = lt


def _post_attn(x, u, attn, mod, wpool_bd, pscale, wout_p, wout_a, lng, lnb, wr_hi, wr_lo, x_off, b_off):
    B, S, _ = u.shape
    D = x.shape[2]
    tm = TM_PROJ
    nt = S // tm
    const = lambda b, i: (0, 0)
    halo_blocks = tm // POOL_HALO
    return pl.pallas_call(
        _post_kernel,
        grid=(B, nt),
        in_specs=[
            pl.BlockSpec((1, tm, D), lambda b, i: (b + x_off, i, 0)),
            pl.BlockSpec((1, tm, POOL_WIDTH), lambda b, i: (b, i, 0)),
            pl.BlockSpec((1, POOL_HALO, POOL_WIDTH), lambda b, i: (b, jnp.maximum(i * halo_blocks - 1, 0), 0)),
            pl.BlockSpec((1, N_HEADS, tm, V_HEAD), lambda b, i: (b, 0, i, 0)),
            pl.BlockSpec((1, 6, D), lambda b, i: (b + b_off, 0, 0)),
            pl.BlockSpec(wpool_bd.shape, const),
            pl.BlockSpec(pscale.shape, const),
            pl.BlockSpec(wout_p.shape, const),
            pl.BlockSpec(wout_a.shape, const),
            pl.BlockSpec(lng.shape, const),
            pl.BlockSpec(lnb.shape, const),
            pl.BlockSpec(wr_hi.shape, const),
            pl.BlockSpec(wr_lo.shape, const),
        ],
        out_specs=[
            pl.BlockSpec((1, tm, D), lambda b, i: (b, i, 0)),
            pl.BlockSpec((1, tm, HALF), lambda b, i: (b, i, 0)),
            pl.BlockSpec((N_EXPERTS, tm), lambda b, i: (0, b * nt + i)),
        ],
        out_shape=[
            jax.ShapeDtypeStruct((B, S, D), F32),
            jax.ShapeDtypeStruct((B, S, HALF), jnp.int32),
            jax.ShapeDtypeStruct((N_EXPERTS, B * S), F32),
        ],
        compiler_params=_cparams(("arbitrary", "arbitrary")),
        name="post_attn",
    )(x, u, u, attn, mod, wpool_bd, pscale, wout_p, wout_a, lng, lnb, wr_hi, wr_lo)


def _select_first_max(vals, n_rounds, payload=None):
    work = list(vals)
    sel = [None] * len(vals)
    rounds = []
    for _ in range(n_rounds):
        m = functools.reduce(jnp.maximum, work)
        taken = None
        win_idx = jnp.zeros(m.shape, jnp.int32)
        win_val = jnp.zeros(m.shape, F32)
        for e in range(len(work)):
            hit = work[e] == m
            first = hit if taken is None else jnp.logical_and(hit, jnp.logical_not(taken))
            taken = hit if taken is None else jnp.logical_or(taken, hit)
            sel[e] = first if sel[e] is None else jnp.logical_or(sel[e], first)
            work[e] = jnp.where(first, -jnp.inf, work[e])
            if payload is not None:
                win_idx = jnp.where(first, e, win_idx)
                win_val = jnp.where(first, payload[e], win_val)
        rounds.append((win_idx, win_val))
    return sel, rounds


def _route_kernel(bias_ref, lt_ref, idx_ref, gate_ref):
    scores = [jax.nn.sigmoid(lt_ref[e]) for e in range(N_EXPERTS)]
    choice = [scores[e] + bias_ref[e] for e in range(N_EXPERTS)]
    group_score = []
    for g in range(N_GROUPS):
        vals = choice[g * GROUP_SIZE:(g + 1) * GROUP_SIZE]
        sel2, _ = _select_first_max(vals, 2)
        group_score.append(functools.reduce(
            jnp.add, [jnp.where(sel2[j], vals[j], 0.0) for j in range(GROUP_SIZE)]))
    group_sel, _ = _select_first_max(group_score, TOPK_GROUPS)
    masked = [jnp.where(group_sel[e // GROUP_SIZE], choice[e], -jnp.inf) for e in range(N_EXPERTS)]
    _, rounds = _select_first_max(masked, TOP_K, payload=scores)
    total = functools.reduce(jnp.add, [w for _, w in rounds])
    for k, (e_k, w_k) in enumerate(rounds):
        idx_ref[k] = e_k
        gate_ref[k] = w_k / total * ROUTED_SCALE


def _route(lt3, bias):
    E, R, _ = lt3.shape
    out_spec = pl.BlockSpec((TOP_K, ROUTE_ROWS, LANES), lambda r: (0, r, 0))
    return pl.pallas_call(
        _route_kernel,
        grid=(R // ROUTE_ROWS,),
        in_specs=[
            pl.BlockSpec(memory_space=pltpu.SMEM),
            pl.BlockSpec((E, ROUTE_ROWS, LANES), lambda r: (0, r, 0)),
        ],
        out_specs=[out_spec, out_spec],
        out_shape=[jax.ShapeDtypeStruct((TOP_K, R, LANES), jnp.int32),
                   jax.ShapeDtypeStruct((TOP_K, R, LANES), F32)],
        compiler_params=_cparams(("arbitrary",)),
        name="route",
    )(bias, lt3)


def _sc_mesh():
    return plsc.VectorSubcoreMesh(core_axis_name="c", subcore_axis_name="s",
                                  num_cores=SC_CORES, num_subcores=SC_SUBCORES)


def _sc_worker_id():
    return lax.axis_index("s") * SC_CORES + lax.axis_index("c")


def _sc_gather(table, idx):
    n = idx.shape[0]
    width = table.shape[1]
    workers = SC_CORES * SC_SUBCORES
    per_worker = n // workers
    n_win = per_worker // SC_WINDOW
    assert n == workers * n_win * SC_WINDOW

    @functools.partial(
        pl.kernel, mesh=_sc_mesh(),
        out_type=jax.ShapeDtypeStruct((n, width), table.dtype),
        scratch_types=[
            pltpu.VMEM((n_win, SC_WINDOW), jnp.int32),
            pltpu.VMEM((SC_WINDOW, width), table.dtype),
            pltpu.SemaphoreType.DMA,
        ],
    )
    def gather(table_hbm, idx_hbm, out_hbm, idx_v, rows_v, sem):
        wid = _sc_worker_id()
        pltpu.sync_copy(idx_hbm.at[wid], idx_v)

        @pl.loop(0, n_win)
        def _(j):
            pltpu.async_copy(table_hbm.at[idx_v.at[j]], rows_v, sem).wait()
            pltpu.sync_copy(rows_v, out_hbm.at[pl.ds(wid * per_worker + j * SC_WINDOW, SC_WINDOW)])

    return gather(table, idx.reshape(workers, n_win, SC_WINDOW))


def _sc_gather_sum(ys, dest_kt, gates_kt):
    K, T = dest_kt.shape
    half = ys.shape[1]
    workers = SC_CORES * SC_SUBCORES
    per_worker = T // workers
    wt = SC_SUM_TOKENS
    n_win = per_worker // wt
    n_idx = K * wt
    assert T == workers * n_win * wt and n_idx <= 128 and half % SC_LANES == 0
    dest_w = dest_kt.reshape(K, workers, n_win, wt).transpose(1, 2, 0, 3).reshape(workers, n_win, n_idx)
    gates_w = gates_kt.reshape(K, workers, n_win, wt).transpose(1, 2, 0, 3).reshape(workers * n_win * n_idx)

    @functools.partial(
        pl.kernel, mesh=_sc_mesh(),
        out_type=jax.ShapeDtypeStruct((T, 2 * half), F32),
        scratch_types=[
            pltpu.VMEM((n_win, n_idx), jnp.int32),
            pltpu.VMEM((n_idx, half), jnp.int32),
            pltpu.VMEM((n_idx,), F32),
            pltpu.VMEM((wt, 2 * half), F32),
            pltpu.SemaphoreType.DMA,
        ],
        compiler_params=pltpu.CompilerParams(needs_layout_passes=False),
    )
    def gather_sum(ys_hbm, dest_hbm, gates_hbm, out_hbm, idx_v, rows_v, gate_v, out_v, sem):
        wid = _sc_worker_id()
        pltpu.sync_copy(dest_hbm.at[wid], idx_v)

        @pl.loop(0, n_win)
        def _(j):
            pltpu.async_copy(ys_hbm.at[idx_v.at[j]], rows_v, sem).wait()
            pltpu.sync_copy(gates_hbm.at[pl.ds((wid * n_win + j) * n_idx, n_idx)], gate_v)

            @pl.loop(0, wt)
            def _(t):
                g = [plsc.load_gather(gate_v, [jnp.full((SC_LANES,), k * wt + t, jnp.int32)]) for k in range(K)]

                @pl.loop(0, half // SC_LANES)
                def _(c):
                    col = pl.ds(c * SC_LANES, SC_LANES)
                    acc_lo = jnp.zeros((SC_LANES,), F32)
                    acc_hi = jnp.zeros((SC_LANES,), F32)
                    for k in range(K):
                        w = rows_v[k * wt + t, col]
                        acc_lo = acc_lo + g[k] * plsc.bitcast(w << 16, F32)
                        acc_hi = acc_hi + g[k] * plsc.bitcast(w & HIGH_HALF_MASK, F32)
                    out_v[t, col] = acc_lo
                    out_v[t, pl.ds(half + c * SC_LANES, SC_LANES)] = acc_hi

            pltpu.sync_copy(out_v, out_hbm.at[pl.ds(wid * per_worker + j * wt, wt)])

    return gather_sum(ys, dest_w, gates_w)


def _sc_dispatch(rows, dest_kt, n_slots):
    T, width = rows.shape
    K = dest_kt.shape[0]
    workers = SC_CORES * SC_SUBCORES
    per_worker = T // workers
    n_win = per_worker // SC_WINDOW
    assert T == workers * n_win * SC_WINDOW
    dest_w = dest_kt.reshape(K, workers, n_win, SC_WINDOW).transpose(1, 2, 0, 3)
    dest_w = dest_w.reshape(workers, n_win * K, SC_WINDOW)

    @functools.partial(
        pl.kernel, mesh=_sc_mesh(),
        out_type=jax.ShapeDtypeStruct((n_slots, width), rows.dtype),
        scratch_types=[
            pltpu.VMEM((n_win * K, SC_WINDOW), jnp.int32),
            pltpu.VMEM((SC_WINDOW, width), rows.dtype),
        ],
    )
    def dispatch(rows_hbm, dest_hbm, out_hbm, idx_v, rows_v):
        wid = _sc_worker_id()
        pltpu.sync_copy(dest_hbm.at[wid], idx_v)

        @pl.loop(0, n_win)
        def _(j):
            pltpu.sync_copy(rows_hbm.at[pl.ds(wid * per_worker + j * SC_WINDOW, SC_WINDOW)], rows_v)
            for k in range(K):
                pltpu.sync_copy(rows_v, out_hbm.at[idx_v.at[j * K + k]])

    return dispatch(rows, dest_w)


def _swiglu(h_lo, h_hi, wg_ref, wu_ref):
    def proj(w_ref):
        return (jnp.dot(h_lo, w_ref[:HALF, :], preferred_element_type=F32)
                + jnp.dot(h_hi, w_ref[HALF:, :], preferred_element_type=F32))
    gate = proj(wg_ref)
    up = proj(wu_ref)
    return (gate * jax.nn.sigmoid(gate) * up).astype(BF16)


def _expert_kernel(seg_first_ref, seg_blocks_ref, n_real_ref, xs_hbm, wg_ref, wu_ref, wd_ref, ys_hbm,
                   xbuf, ybuf, wg_bf, wu_bf, wd_bf, in_sems, out_sems):
    e = pl.program_id(0)
    first = seg_first_ref[e]
    count = seg_blocks_ref[e]
    total = n_real_ref[0]
    blk = DISPATCH_BLOCK
    ring = xbuf.shape[0]

    def in_copy(g):
        slot = g % ring
        return pltpu.make_async_copy(xs_hbm.at[pl.ds(g * blk, blk)], xbuf.at[slot], in_sems.at[slot])

    def out_copy(g):
        slot = g % ring
        return pltpu.make_async_copy(ybuf.at[slot], ys_hbm.at[pl.ds(g * blk, blk)], out_sems.at[slot])

    @pl.when(e == 0)
    def _():
        for g in range(ring - 1):
            @pl.when(g < total)
            def _():
                in_copy(g).start()

    @pl.when(count > 0)
    def _():
        wg_bf[...] = wg_ref[0, 0].astype(BF16)
        wu_bf[...] = wu_ref[0, 0].astype(BF16)
        wd_bf[...] = wd_ref[0, 0].astype(BF16)

    def block(g, _):
        slot = g % ring
        in_copy(g).wait()

        @pl.when(g + ring - 1 < total)
        def _():
            in_copy(g + ring - 1).start()

        @pl.when(g >= ring)
        def _():
            out_copy(g - ring).wait()

        lo, hi = _unpack_bf16_pairs(xbuf[slot])
        a = _swiglu(lo.astype(BF16), hi.astype(BF16), wg_bf, wu_bf)
        y = jnp.dot(a, wd_bf[...], preferred_element_type=F32)
        ybuf[slot] = _pack_bf16_pairs(y.astype(BF16))
        out_copy(g).start()
        return 0

    lax.fori_loop(first, first + count, block, 0)

    @pl.when(e == pl.num_programs(0) - 1)
    def _():
        for back in range(ring, 0, -1):
            @pl.when(total >= back)
            def _():
                out_copy(total - back).wait()


def _expert_mm(seg_first, seg_blocks, n_real, xs, wg, wu, wd, layer):
    P = xs.shape[0]
    D = wg.shape[2]
    w_map = lambda e, sf, sb, nr: (layer, e, 0, 0)
    grid_spec = pltpu.PrefetchScalarGridSpec(
        num_scalar_prefetch=3,
        grid=(N_EXPERTS,),
        in_specs=[
            pl.BlockSpec(memory_space=pl.ANY),
            pl.BlockSpec((1, 1, D, D_EXPERT), w_map),
            pl.BlockSpec((1, 1, D, D_EXPERT), w_map),
            pl.BlockSpec((1, 1, D_EXPERT, D), w_map),
        ],
        out_specs=pl.BlockSpec(memory_space=pl.ANY),
        scratch_shapes=[
            pltpu.VMEM((EXPERT_RING, DISPATCH_BLOCK, HALF), jnp.int32),
            pltpu.VMEM((EXPERT_RING, DISPATCH_BLOCK, HALF), jnp.int32),
            pltpu.VMEM((D, D_EXPERT), BF16), pltpu.VMEM((D, D_EXPERT), BF16), pltpu.VMEM((D_EXPERT, D), BF16),
            pltpu.SemaphoreType.DMA((EXPERT_RING,)), pltpu.SemaphoreType.DMA((EXPERT_RING,)),
        ],
    )
    return pl.pallas_call(
        _expert_kernel,
        grid_spec=grid_spec,
        out_shape=jax.ShapeDtypeStruct((P, HALF), jnp.int32),
        compiler_params=_cparams(("arbitrary",)),
        name="expert_mm",
    )(seg_first, seg_blocks, n_real, xs, wg, wu, wd)


def _rank_kernel(idx_ref, rank_ref, cnt_ref, tri_ref, carry_ref):
    i = pl.program_id(0)
    tm = idx_ref.shape[1]

    @pl.when(i == 0)
    def _():
        r = lax.broadcasted_iota(jnp.int32, (tm, tm), 0)
        c = lax.broadcasted_iota(jnp.int32, (tm, tm), 1)
        tri_ref[...] = (r < c).astype(BF16)
        carry_ref[...] = jnp.zeros(carry_ref.shape, F32)

    e_iota = lax.broadcasted_iota(jnp.int32, (N_EXPERTS, tm), 0)
    hits = [e_iota == idx_ref[k:k + 1, :] for k in range(TOP_K)]
    mask = functools.reduce(jnp.logical_or, hits).astype(F32)
    rank = jnp.dot(mask.astype(BF16), tri_ref[...], preferred_element_type=F32) + carry_ref[:, 0:1]
    for k in range(TOP_K):
        rank_ref[k:k + 1, :] = jnp.sum(jnp.where(hits[k], rank, 0.0), axis=0, keepdims=True).astype(jnp.int32)
    carry_ref[...] = carry_ref[...] + jnp.sum(mask, axis=1, keepdims=True)
    cnt_ref[...] = carry_ref[...]


def _rank(idx_kt):
    K, T = idx_kt.shape
    tm = 1024
    return pl.pallas_call(
        _rank_kernel,
        grid=(T // tm,),
        in_specs=[pl.BlockSpec((K, tm), lambda i: (0, i))],
        out_specs=[pl.BlockSpec((K, tm), lambda i: (0, i)),
                   pl.BlockSpec((N_EXPERTS, LANES), lambda i: (0, 0))],
        out_shape=[jax.ShapeDtypeStruct((K, T), jnp.int32),
                   jax.ShapeDtypeStruct((N_EXPERTS, LANES), F32)],
        scratch_shapes=[pltpu.VMEM((tm, tm), BF16), pltpu.VMEM((N_EXPERTS, LANES), F32)],
        compiler_params=_cparams(("arbitrary",)),
        name="rank",
    )(idx_kt)


def _dest_kernel(idx_ref, rank_ref, start_ref, dest_ref):
    tm = idx_ref.shape[1]
    e_iota = lax.broadcasted_iota(jnp.int32, (N_EXPERTS, tm), 0)
    start = start_ref[:, 0:1]
    for k in range(TOP_K):
        base = jnp.sum(jnp.where(e_iota == idx_ref[k:k + 1, :], start, 0.0), axis=0, keepdims=True)
        dest_ref[k:k + 1, :] = rank_ref[k:k + 1, :] + base.astype(jnp.int32)


def _dest(idx_kt, rank_kt, start):
    K, T = idx_kt.shape
    tm = 1024
    tok_spec = pl.BlockSpec((K, tm), lambda i: (0, i))
    return pl.pallas_call(
        _dest_kernel,
        grid=(T // tm,),
        in_specs=[tok_spec, tok_spec, pl.BlockSpec((N_EXPERTS, LANES), lambda i: (0, 0))],
        out_specs=tok_spec,
        out_shape=jax.ShapeDtypeStruct((K, T), jnp.int32),
        compiler_params=_cparams(("arbitrary",)),
        name="dest",
    )(idx_kt, rank_kt, start)


def _block_plan(counts):
    blk = DISPATCH_BLOCK
    seg_blocks = (counts.astype(jnp.int32) + blk - 1) // blk
    seg_end = jnp.cumsum(seg_blocks)
    seg_first = seg_end - seg_blocks
    return seg_first * blk, seg_first, seg_blocks, seg_end[-1:]


def _combine_kernel(h_ref, x1_ref, routed_ref, mod_ref, wsg_ref, wsu_ref, wsd_ref, lng_ref, lnb_ref, o_ref):
    lo, hi = _unpack_bf16_pairs(h_ref[...])
    a = _swiglu(lo.astype(BF16), hi.astype(BF16), wsg_ref, wsu_ref)
    ffn = jnp.dot(a, wsd_ref[...], preferred_element_type=F32) + routed_ref[...]
    g2 = mod_ref[0, 5:6, :]
    o_ref[...] = _layer_norm(DEEPNORM_ALPHA * x1_ref[...] + g2 * ffn, lng_ref[...], lnb_ref[...])


def _combine_into_kernel(h_ref, x1_ref, routed_ref, mod_ref, wsg_ref, wsu_ref, wsd_ref, lng_ref, lnb_ref,
                         prev_ref, o_ref):
    del prev_ref
    _combine_kernel(h_ref, x1_ref, routed_ref, mod_ref, wsg_ref, wsu_ref, wsd_ref, lng_ref, lnb_ref, o_ref)


def _combine(h2p, x1, routed, mod, wsg, wsu, wsd, lng, lnb, seq, b_off, out_rows, out_off, out_prev):
    T, D = x1.shape
    tm = TM_COMBINE
    per_seq = seq // tm
    blk_off = out_off // tm
    const = lambda i: (0, 0)
    in_specs = [
        pl.BlockSpec((tm, HALF), lambda i: (i, 0)),
        pl.BlockSpec((tm, D), lambda i: (i, 0)),
        pl.BlockSpec((tm, D), lambda i: (i, 0)),
        pl.BlockSpec((1, 6, D), lambda i: (i // per_seq + b_off, 0, 0)),
        pl.BlockSpec(wsg.shape, const),
        pl.BlockSpec(wsu.shape, const),
        pl.BlockSpec(wsd.shape, const),
        pl.BlockSpec(lng.shape, const),
        pl.BlockSpec(lnb.shape, const),
    ]
    args = [h2p, x1, routed, mod, wsg, wsu, wsd, lng, lnb]
    body = _combine_kernel
    aliases = {}
    if out_prev is not None:
        in_specs.append(pl.BlockSpec(memory_space=pl.ANY))
        args.append(out_prev)
        aliases = {len(args) - 1: 0}
        body = _combine_into_kernel
    return pl.pallas_call(
        body,
        grid=(T // tm,),
        in_specs=in_specs,
        out_specs=pl.BlockSpec((tm, D), lambda i: (i + blk_off, 0)),
        out_shape=jax.ShapeDtypeStruct((out_rows, D), F32),
        input_output_aliases=aliases,
        compiler_params=_cparams(("arbitrary",)),
        name="combine",
    )(*args)


def _rope_pair(w):
    half = QK_ROPE // 2
    return w, jnp.concatenate([-w[:, half:], w[:, :half]], axis=1)


def _prep_layer(w_in, w_uq, w_ukv, w_pool, w_out, w_router):
    D = w_in.shape[0]
    zpad = lambda r: jnp.zeros((r, LANES - QK_ROPE), F32)
    lat = POOL_WIDTH + Q_LORA + KV_LORA
    k_a, k_b = _rope_pair(w_in[:, lat:lat + QK_ROPE])
    w_in_ext = jnp.concatenate([w_in[:, :lat], k_a, zpad(D), k_b, zpad(D)], axis=1).astype(BF16)

    qa_cols, qb_cols = [], []
    per_head = QK_NOPE + QK_ROPE
    for hd in range(N_HEADS):
        w_h = w_uq[:, hd * per_head:(hd + 1) * per_head]
        r_a, r_b = _rope_pair(w_h[:, QK_NOPE:])
        qa_cols += [w_h[:, :QK_NOPE], r_a, zpad(Q_LORA)]
        qb_cols += [r_b, zpad(Q_LORA)]
    w_uqa = jnp.concatenate(qa_cols, axis=1).astype(BF16)
    w_uqb = jnp.concatenate(qb_cols, axis=1).astype(BF16)

    wpool_bd = jax.scipy.linalg.block_diag(*[w_pool[g] for g in range(len(POOL_WINDOWS))]).astype(BF16)
    wr_hi, wr_lo = _split_hi_lo(w_router.T)
    return dict(w_in_ext=w_in_ext, w_uqa=w_uqa, w_uqb=w_uqb, w_ukv=w_ukv.astype(BF16), wpool_bd=wpool_bd,
                wout_p=w_out[:POOL_WIDTH].astype(BF16), wout_a=w_out[POOL_WIDTH:].astype(BF16),
                wr_hi=wr_hi, wr_lo=wr_lo)


def kernel(x, c, positions, w_ada, b_ada, w_in, q_norm_g, kv_norm_g, w_uq, w_ukv, w_pool, pool_scale,
           w_out, ln1_g, ln1_b, w_router, router_bias, w_exp_gate, w_exp_up, w_exp_down,
           w_sh_gate, w_sh_up, w_sh_down, ln2_g, ln2_b):
    B, S, D = x.shape
    L = w_in.shape[0]
    row = lambda v: v.reshape(1, -1)

    mod_all = _ada_mod(c, w_ada, b_ada).reshape(L, B, 6, D)
    cs_all = _rope_table(positions)
    bs = B // N_STREAMS
    T = bs * S
    n_blocks = T * TOP_K // DISPATCH_BLOCK + N_EXPERTS
    xs_streams = [x] * N_STREAMS
    result = None
    for l in range(L):
        p = _prep_layer(w_in[l], w_uq[l], w_ukv[l], w_pool[l], w_out[l], w_router[l])
        shared = (w_sh_gate[l].astype(BF16), w_sh_up[l].astype(BF16), w_sh_down[l].astype(BF16))
        mod = mod_all[l]
        for s in range(N_STREAMS):
            xl = xs_streams[s]
            b_off = s * bs
            x_off = b_off if l == 0 else 0
            u, q, k, v = _in_proj(xl, mod, cs_all, p["w_in_ext"], row(q_norm_g[l]), row(kv_norm_g[l]),
                                  p["w_uqa"], p["w_uqb"], p["w_ukv"], bs, x_off, b_off)
            attn = _attention(q, k, v)
            x1, h2p, lt = _post_attn(xl, u, attn, mod, p["wpool_bd"], row(pool_scale[l]), p["wout_p"],
                                     p["wout_a"], row(ln1_g[l]), row(ln1_b[l]), p["wr_hi"], p["wr_lo"],
                                     x_off, b_off)
            idx_k, gate_k = _route(lt.reshape(N_EXPERTS, T // LANES, LANES), router_bias[l])
            idx_kt = idx_k.reshape(TOP_K, T)
            gates_kt = gate_k.reshape(TOP_K, T)
            rank_kt, counts = _rank(idx_kt)
            pad_start, seg_first, seg_blocks, n_real = _block_plan(counts[:, 0])
            start = jnp.broadcast_to(pad_start.astype(F32)[:, None], (N_EXPERTS, LANES))
            dest_kt = _dest(idx_kt, rank_kt, start)
            h2p = h2p.reshape(T, HALF)
            rows = _sc_dispatch(h2p, dest_kt, n_blocks * DISPATCH_BLOCK)
            ys = _expert_mm(seg_first, seg_blocks, n_real, rows, w_exp_gate, w_exp_up, w_exp_down, l)
            routed = _sc_gather_sum(ys, dest_kt, gates_kt)
            if l < L - 1:
                xs_streams[s] = _combine(h2p, x1.reshape(T, D), routed, mod, *shared, row(ln2_g[l]),
                                         row(ln2_b[l]), S, b_off, T, 0, None).reshape(bs, S, D)
            else:
                result = _combine(h2p, x1.reshape(T, D), routed, mod, *shared, row(ln2_g[l]),
                                  row(ln2_b[l]), S, b_off, B * S, s * T, result)
    return result.reshape(B, S, D)
```

```python
import functools

import jax
import jax.numpy as jnp
from jax import lax
from jax.experimental import pallas as pl
from jax.experimental.pallas import tpu as pltpu
from jax.experimental.pallas import tpu_sc as plsc

F32 = jnp.float32
BF16 = jnp.bfloat16

D_MODEL = 1024
DEPTH = 4
POOL_WIDTH = 256
POOL_WINDOWS = (2, 4, 8, 16)
POOL_GROUP = 64
POOL_HALO = 16
QK_NOPE = 128
QK_ROPE = 64
V_HEAD = 128
N_HEADS = 6
Q_LORA = 384
KV_LORA = 256
ROPE_THETA = 10000.0
N_EXPERTS = 64
TOP_K = 8
N_GROUPS = 8
GROUP_SIZE = N_EXPERTS // N_GROUPS
TOPK_GROUPS = 4
D_EXPERT = 256
ROUTED_SCALE = 2.5
DEEPNORM_ALPHA = (2.0 * DEPTH) ** 0.25
LN_EPS = 1e-5
RMS_EPS = 1e-6
ATTN_SCALE = (QK_NOPE + QK_ROPE) ** -0.5
LOG2_E = 1.4426950408889634
Q_SCALE = ATTN_SCALE * LOG2_E
LANES = 128
QK_PAD = 2 * LANES
MASK_VALUE = -1e30

TM_PROJ = 512
TQ = 512
TK = 512
ROUTE_ROWS = 8
DISPATCH_BLOCK = 512
EXPERT_RING = 6
TM_COMBINE = 512
SC_CORES = 2
SC_SUBCORES = 16
SC_WINDOW = 128
SC_LANES = 16
HIGH_HALF_MASK = -65536
SC_SUM_TOKENS = 16
N_STREAMS = 2
HALF = D_MODEL // 2
VMEM_LIMIT = 48 * 1024 * 1024


def _cparams(sem):
    return pltpu.CompilerParams(dimension_semantics=sem, vmem_limit_bytes=VMEM_LIMIT)


def _split_hi_lo(a):
    hi = a.astype(BF16)
    lo = (a - hi.astype(F32)).astype(BF16)
    return hi, lo


def _pack_bf16_pairs(a):
    bits = lax.bitcast_convert_type(a.astype(F32), jnp.uint32)
    half = a.shape[1] // 2
    word = (bits[:, :half] >> 16) | (bits[:, half:] & jnp.uint32(0xFFFF0000))
    return lax.bitcast_convert_type(word, jnp.int32)


def _unpack_bf16_pairs(w):
    bits = lax.bitcast_convert_type(w, jnp.uint32)
    lo = lax.bitcast_convert_type(bits << 16, F32)
    hi = lax.bitcast_convert_type(bits & jnp.uint32(0xFFFF0000), F32)
    return lo, hi


def _ada_kernel(c_ref, w_ref, b_ref, o_ref):
    c = c_ref[...]
    cond = c * jax.nn.sigmoid(c)
    c_hi, c_lo = _split_hi_lo(cond)
    w_hi, w_lo = _split_hi_lo(w_ref[0])
    acc = jnp.dot(c_hi, w_hi, preferred_element_type=F32)
    acc += jnp.dot(c_lo, w_hi, preferred_element_type=F32)
    acc += jnp.dot(c_hi, w_lo, preferred_element_type=F32)
    o_ref[0] = acc + b_ref[0]


def _ada_mod(c, w_ada, b_ada):
    L, D, N = w_ada.shape
    B = c.shape[0]
    tn = 1536
    return pl.pallas_call(
        _ada_kernel,
        grid=(L, N // tn),
        in_specs=[
            pl.BlockSpec((B, D), lambda l, j: (0, 0)),
            pl.BlockSpec((1, D, tn), lambda l, j: (l, 0, j)),
            pl.BlockSpec((1, 1, tn), lambda l, j: (l, 0, j)),
        ],
        out_specs=pl.BlockSpec((1, B, tn), lambda l, j: (l, 0, j)),
        out_shape=jax.ShapeDtypeStruct((L, B, N), F32),
        compiler_params=_cparams(("arbitrary", "arbitrary")),
        name="ada_mod",
    )(c, w_ada, b_ada.reshape(L, 1, N))


def _rope_table_kernel(pos_ref, freq_ref, o_ref):
    ang = pos_ref[0].astype(F32) * freq_ref[...]
    lane = lax.broadcasted_iota(jnp.int32, (1, LANES), 1)
    o_ref[0] = jnp.where(lane < QK_ROPE, jnp.cos(ang), jnp.sin(ang))


def _rope_table(positions):
    B, S = positions.shape
    half = QK_ROPE // 2
    inv_freq = ROPE_THETA ** (-jnp.arange(half, dtype=F32) / half)
    freq = jnp.tile(inv_freq, LANES // half).reshape(1, LANES)
    return pl.pallas_call(
        _rope_table_kernel,
        grid=(B,),
        in_specs=[
            pl.BlockSpec((1, S, 1), lambda b: (b, 0, 0)),
            pl.BlockSpec((1, LANES), lambda b: (0, 0)),
        ],
        out_specs=pl.BlockSpec((1, S, LANES), lambda b: (b, 0, 0)),
        out_shape=jax.ShapeDtypeStruct((B, S, LANES), F32),
        compiler_params=_cparams(("arbitrary",)),
        name="rope_table",
    )(positions.reshape(B, S, 1), freq)


def _rms(x, g):
    return x * lax.rsqrt(jnp.mean(x * x, axis=-1, keepdims=True) + RMS_EPS) * g


def _in_proj_kernel(x_ref, mod_ref, cs_ref, w_in_ref, qg_ref, kvg_ref, w_uqa_ref, w_ukv_ref,
                    u_ref, q_ref, k_ref, v_ref):
    x = x_ref[0]
    sh1 = mod_ref[0, 0:1, :]
    sc1 = mod_ref[0, 1:2, :]
    h = (x * (1.0 + sc1) + sh1).astype(BF16)
    proj = jnp.dot(h, w_in_ref[...], preferred_element_type=F32)
    u_ref[0] = proj[:, 0:POOL_WIDTH]
    o = POOL_WIDTH
    q_lat = proj[:, o:o + Q_LORA]
    o += Q_LORA
    kv_lat = proj[:, o:o + KV_LORA]
    o += KV_LORA
    cos_sin = cs_ref[0]

    def rotate(pair):
        t = pair * cos_sin
        return t + pltpu.roll(t, QK_ROPE, 1)

    k_rot = rotate(proj[:, o:o + LANES]).astype(BF16)
    lane = lax.broadcasted_iota(jnp.int32, (1, LANES), 1)
    q_keep = jnp.where(lane < QK_ROPE, Q_SCALE, 0.0)

    qn = _rms(q_lat, qg_ref[...]).astype(BF16)
    q_a = jnp.dot(qn, w_uqa_ref[...], preferred_element_type=F32)
    kvn = _rms(kv_lat, kvg_ref[...]).astype(BF16)
    kv = jnp.dot(kvn, w_ukv_ref[...], preferred_element_type=F32)
    for hd in range(N_HEADS):
        b0 = hd * QK_PAD
        q_ref[0, hd, :, 0:LANES] = (q_a[:, b0:b0 + LANES] * Q_SCALE).astype(BF16)
        q_ref[0, hd, :, LANES:QK_PAD] = (rotate(q_a[:, b0 + LANES:b0 + QK_PAD]) * q_keep).astype(BF16)
        k_ref[0, hd, :, 0:LANES] = kv[:, b0:b0 + QK_NOPE].astype(BF16)
        k_ref[0, hd, :, LANES:QK_PAD] = k_rot
        v_ref[0, hd] = kv[:, b0 + QK_NOPE:b0 + QK_NOPE + V_HEAD].astype(BF16)


def _in_proj(x, mod, cs, w_in_ext, qg, kvg, w_uqa, w_ukv_bf, B, x_off, b_off):
    _, S, D = x.shape
    tm = TM_PROJ
    const = lambda b, i: (0, 0)
    return pl.pallas_call(
        _in_proj_kernel,
        grid=(B, S // tm),
        in_specs=[
            pl.BlockSpec((1, tm, D), lambda b, i: (b + x_off, i, 0)),
            pl.BlockSpec((1, 6, D), lambda b, i: (b + b_off, 0, 0)),
            pl.BlockSpec((1, tm, LANES), lambda b, i: (b + b_off, i, 0)),
            pl.BlockSpec(w_in_ext.shape, const),
            pl.BlockSpec(qg.shape, const),
            pl.BlockSpec(kvg.shape, const),
            pl.BlockSpec(w_uqa.shape, const),
            pl.BlockSpec(w_ukv_bf.shape, const),
        ],
        out_specs=[
            pl.BlockSpec((1, tm, POOL_WIDTH), lambda b, i: (b, i, 0)),
            pl.BlockSpec((1, N_HEADS, tm, QK_PAD), lambda b, i: (b, 0, i, 0)),
            pl.BlockSpec((1, N_HEADS, tm, QK_PAD), lambda b, i: (b, 0, i, 0)),
            pl.BlockSpec((1, N_HEADS, tm, V_HEAD), lambda b, i: (b, 0, i, 0)),
        ],
        out_shape=[
            jax.ShapeDtypeStruct((B, S, POOL_WIDTH), F32),
            jax.ShapeDtypeStruct((B, N_HEADS, S, QK_PAD), BF16),
            jax.ShapeDtypeStruct((B, N_HEADS, S, QK_PAD), BF16),
            jax.ShapeDtypeStruct((B, N_HEADS, S, V_HEAD), BF16),
        ],
        compiler_params=_cparams(("arbitrary", "arbitrary")),
        name="in_proj",
    )(x, mod, cs, w_in_ext, qg, kvg, w_uqa, w_ukv_bf)


def _softmax_step(q, k, v, carry, mask):
    m, l, acc = carry
    s = lax.dot_general(q, k, (((1,), (1,)), ((), ())), preferred_element_type=F32)
    if mask is not None:
        s = jnp.where(mask, s, MASK_VALUE)
    m_new = jnp.maximum(m, jnp.max(s, axis=-1, keepdims=True))
    alpha = jnp.exp2(m - m_new)
    p = jnp.exp2(s - m_new)
    l_new = alpha * l + jnp.sum(p, axis=-1, keepdims=True)
    acc_new = alpha * acc + jnp.dot(p.astype(BF16), v, preferred_element_type=F32)
    return m_new, l_new, acc_new


def _attn_kernel(q_ref, k_ref, v_ref, o_ref):
    seq = q_ref.shape[2]
    row = lax.broadcasted_iota(jnp.int32, (TQ, TK), 0)
    col = lax.broadcasted_iota(jnp.int32, (TQ, TK), 1)
    diag = row >= col
    for i in range(seq // TQ):
        q = q_ref[0, 0, i * TQ:(i + 1) * TQ, :]
        carry = (jnp.full((TQ, 1), MASK_VALUE, F32), jnp.zeros((TQ, 1), F32), jnp.zeros((TQ, V_HEAD), F32))
        for j in range(i + 1):
            k = k_ref[0, 0, j * TK:(j + 1) * TK, :]
            v = v_ref[0, 0, j * TK:(j + 1) * TK, :]
            carry = _softmax_step(q, k, v, carry, diag if j == i else None)
        _, l, acc = carry
        o_ref[0, 0, i * TQ:(i + 1) * TQ, :] = (acc / l).astype(BF16)


def _attention(q, k, v):
    B, H, S, _ = q.shape
    return pl.pallas_call(
        _attn_kernel,
        grid=(B, H),
        in_specs=[
            pl.BlockSpec((1, 1, S, QK_PAD), lambda b, h: (b, h, 0, 0)),
            pl.BlockSpec((1, 1, S, QK_PAD), lambda b, h: (b, h, 0, 0)),
            pl.BlockSpec((1, 1, S, V_HEAD), lambda b, h: (b, h, 0, 0)),
        ],
        out_specs=pl.BlockSpec((1, 1, S, V_HEAD), lambda b, h: (b, h, 0, 0)),
        out_shape=jax.ShapeDtypeStruct((B, H, S, V_HEAD), BF16),
        compiler_params=_cparams(("arbitrary", "arbitrary")),
        name="attention",
    )(q, k, v)


def _layer_norm(v, g, b):
    mu = jnp.mean(v, axis=-1, keepdims=True)
    d = v - mu
    var = jnp.mean(d * d, axis=-1, keepdims=True)
    return d * lax.rsqrt(var + LN_EPS) * g + b


def _post_kernel(x_ref, u_ref, halo_ref, attn_ref, mod_ref, wpool_ref, pscale_ref, wout_p_ref, wout_a_ref,
                 lng_ref, lnb_ref, wr_hi_ref, wr_lo_ref, x1_ref, h2_ref, lt_ref):
    i = pl.program_id(1)
    tm = u_ref.shape[1]
    u = u_ref[0]
    halo = jnp.where(i > 0, halo_ref[0], 0.0)
    ext = jnp.concatenate([halo, u], axis=0)
    s2 = ext + pltpu.roll(ext, 1, 0)
    s4 = s2 + pltpu.roll(s2, 2, 0)
    s8 = s4 + pltpu.roll(s4, 4, 0)
    s16 = s8 + pltpu.roll(s8, 8, 0)
    lane = lax.broadcasted_iota(jnp.int32, (1, POOL_WIDTH), 1)
    win = jnp.where(lane < POOL_GROUP, s2,
                    jnp.where(lane < 2 * POOL_GROUP, s4, jnp.where(lane < 3 * POOL_GROUP, s8, s16)))
    win = win[POOL_HALO:, :]
    width = jnp.where(lane < POOL_GROUP, POOL_WINDOWS[0],
                      jnp.where(lane < 2 * POOL_GROUP, POOL_WINDOWS[1],
                                jnp.where(lane < 3 * POOL_GROUP, POOL_WINDOWS[2], POOL_WINDOWS[3])))
    t = i * tm + lax.broadcasted_iota(jnp.int32, (tm, 1), 0)
    count = jnp.minimum(t + 1, width).astype(F32)
    token_mix = win / count - u
    pooled = jnp.dot(token_mix.astype(BF16), wpool_ref[...], preferred_element_type=F32) * pscale_ref[...]
    mixed = jnp.dot(pooled.astype(BF16), wout_p_ref[...], preferred_element_type=F32)
    attn = jnp.concatenate([attn_ref[0, hd] for hd in range(N_HEADS)], axis=1)
    mixed += jnp.dot(attn, wout_a_ref[...], preferred_element_type=F32)

    g1 = mod_ref[0, 2:3, :]
    sh2 = mod_ref[0, 3:4, :]
    sc2 = mod_ref[0, 4:5, :]
    x1 = _layer_norm(DEEPNORM_ALPHA * x_ref[0] + g1 * mixed, lng_ref[...], lnb_ref[...])
    x1_ref[0] = x1
    h2 = x1 * (1.0 + sc2) + sh2
    h_hi, h_lo = _split_hi_lo(h2)
    h2_ref[0] = _pack_bf16_pairs(h_hi)
    nt = (((1,), (1,)), ((), ()))
    lt = lax.dot_general(wr_hi_ref[...], h_hi, nt, preferred_element_type=F32)
    lt += lax.dot_general(wr_hi_ref[...], h_lo, nt, preferred_element_type=F32)
    lt += lax.dot_general(wr_lo_ref[...], h_hi, nt, preferred_element_type=F32)
    lt_ref[...] = lt


def _post_attn(x, u, attn, mod, wpool_bd, pscale, wout_p, wout_a, lng, lnb, wr_hi, wr_lo, x_off, b_off):
    B, S, _ = u.shape
    D = x.shape[2]
    tm = TM_PROJ
    nt = S // tm
    const = lambda b, i: (0, 0)
    halo_blocks = tm // POOL_HALO
    return pl.pallas_call(
        _post_kernel,
        grid=(B, nt),
        in_specs=[
            pl.BlockSpec((1, tm, D), lambda b, i: (b + x_off, i, 0)),
            pl.BlockSpec((1, tm, POOL_WIDTH), lambda b, i: (b, i, 0)),
            pl.BlockSpec((1, POOL_HALO, POOL_WIDTH), lambda b, i: (b, jnp.maximum(i * halo_blocks - 1, 0), 0)),
            pl.BlockSpec((1, N_HEADS, tm, V_HEAD), lambda b, i: (b, 0, i, 0)),
            pl.BlockSpec((1, 6, D), lambda b, i: (b + b_off, 0, 0)),
            pl.BlockSpec(wpool_bd.shape, const),
            pl.BlockSpec(pscale.shape, const),
            pl.BlockSpec(wout_p.shape, const),
            pl.BlockSpec(wout_a.shape, const),
            pl.BlockSpec(lng.shape, const),
            pl.BlockSpec(lnb.shape, const),
            pl.BlockSpec(wr_hi.shape, const),
            pl.BlockSpec(wr_lo.shape, const),
        ],
        out_specs=[
            pl.BlockSpec((1, tm, D), lambda b, i: (b, i, 0)),
            pl.BlockSpec((1, tm, HALF), lambda b, i: (b, i, 0)),
            pl.BlockSpec((N_EXPERTS, tm), lambda b, i: (0, b * nt + i)),
        ],
        out_shape=[
            jax.ShapeDtypeStruct((B, S, D), F32),
            jax.ShapeDtypeStruct((B, S, HALF), jnp.int32),
            jax.ShapeDtypeStruct((N_EXPERTS, B * S), F32),
        ],
        compiler_params=_cparams(("arbitrary", "arbitrary")),
        name="post_attn",
    )(x, u, u, attn, mod, wpool_bd, pscale, wout_p, wout_a, lng, lnb, wr_hi, wr_lo)


def _select_first_max(vals, n_rounds, payload=None):
    work = list(vals)
    sel = [None] * len(vals)
    rounds = []
    for _ in range(n_rounds):
        m = functools.reduce(jnp.maximum, work)
        taken = None
        win_idx = jnp.zeros(m.shape, jnp.int32)
        win_val = jnp.zeros(m.shape, F32)
        for e in range(len(work)):
            hit = work[e] == m
            first = hit if taken is None else jnp.logical_and(hit, jnp.logical_not(taken))
            taken = hit if taken is None else jnp.logical_or(taken, hit)
            sel[e] = first if sel[e] is None else jnp.logical_or(sel[e], first)
            work[e] = jnp.where(first, -jnp.inf, work[e])
            if payload is not None:
                win_idx = jnp.where(first, e, win_idx)
                win_val = jnp.where(first, payload[e], win_val)
        rounds.append((win_idx, win_val))
    return sel, rounds


def _route_kernel(bias_ref, lt_ref, idx_ref, gate_ref):
    scores = [jax.nn.sigmoid(lt_ref[e]) for e in range(N_EXPERTS)]
    choice = [scores[e] + bias_ref[e] for e in range(N_EXPERTS)]
    group_score = []
    for g in range(N_GROUPS):
        vals = choice[g * GROUP_SIZE:(g + 1) * GROUP_SIZE]
        sel2, _ = _select_first_max(vals, 2)
        group_score.append(functools.reduce(
            jnp.add, [jnp.where(sel2[j], vals[j], 0.0) for j in range(GROUP_SIZE)]))
    group_sel, _ = _select_first_max(group_score, TOPK_GROUPS)
    masked = [jnp.where(group_sel[e // GROUP_SIZE], choice[e], -jnp.inf) for e in range(N_EXPERTS)]
    _, rounds = _select_first_max(masked, TOP_K, payload=scores)
    total = functools.reduce(jnp.add, [w for _, w in rounds])
    for k, (e_k, w_k) in enumerate(rounds):
        idx_ref[k] = e_k
        gate_ref[k] = w_k / total * ROUTED_SCALE


def _route(lt3, bias):
    E, R, _ = lt3.shape
    out_spec = pl.BlockSpec((TOP_K, ROUTE_ROWS, LANES), lambda r: (0, r, 0))
    return pl.pallas_call(
        _route_kernel,
        grid=(R // ROUTE_ROWS,),
        in_specs=[
            pl.BlockSpec(memory_space=pltpu.SMEM),
            pl.BlockSpec((E, ROUTE_ROWS, LANES), lambda r: (0, r, 0)),
        ],
        out_specs=[out_spec, out_spec],
        out_shape=[jax.ShapeDtypeStruct((TOP_K, R, LANES), jnp.int32),
                   jax.ShapeDtypeStruct((TOP_K, R, LANES), F32)],
        compiler_params=_cparams(("arbitrary",)),
        name="route",
    )(bias, lt3)


def _sc_mesh():
    return plsc.VectorSubcoreMesh(core_axis_name="c", subcore_axis_name="s",
                                  num_cores=SC_CORES, num_subcores=SC_SUBCORES)


def _sc_worker_id():
    return lax.axis_index("s") * SC_CORES + lax.axis_index("c")


def _sc_gather(table, idx):
    n = idx.shape[0]
    width = table.shape[1]
    workers = SC_CORES * SC_SUBCORES
    per_worker = n // workers
    n_win = per_worker // SC_WINDOW
    assert n == workers * n_win * SC_WINDOW

    @functools.partial(
        pl.kernel, mesh=_sc_mesh(),
        out_type=jax.ShapeDtypeStruct((n, width), table.dtype),
        scratch_types=[
            pltpu.VMEM((n_win, SC_WINDOW), jnp.int32),
            pltpu.VMEM((SC_WINDOW, width), table.dtype),
            pltpu.SemaphoreType.DMA,
        ],
    )
    def gather(table_hbm, idx_hbm, out_hbm, idx_v, rows_v, sem):
        wid = _sc_worker_id()
        pltpu.sync_copy(idx_hbm.at[wid], idx_v)

        @pl.loop(0, n_win)
        def _(j):
            pltpu.async_copy(table_hbm.at[idx_v.at[j]], rows_v, sem).wait()
            pltpu.sync_copy(rows_v, out_hbm.at[pl.ds(wid * per_worker + j * SC_WINDOW, SC_WINDOW)])

    return gather(table, idx.reshape(workers, n_win, SC_WINDOW))


def _sc_gather_sum(ys, dest_kt, gates_kt):
    K, T = dest_kt.shape
    half = ys.shape[1]
    workers = SC_CORES * SC_SUBCORES
    per_worker = T // workers
    wt = SC_SUM_TOKENS
    n_win = per_worker // wt
    n_idx = K * wt
    assert T == workers * n_win * wt and n_idx <= 128 and half % SC_LANES == 0
    dest_w = dest_kt.reshape(K, workers, n_win, wt).transpose(1, 2, 0, 3).reshape(workers, n_win, n_idx)
    gates_w = gates_kt.reshape(K, workers, n_win, wt).transpose(1, 2, 0, 3).reshape(workers * n_win * n_idx)

    @functools.partial(
        pl.kernel, mesh=_sc_mesh(),
        out_type=jax.ShapeDtypeStruct((T, 2 * half), F32),
        scratch_types=[
            pltpu.VMEM((n_win, n_idx), jnp.int32),
            pltpu.VMEM((n_idx, half), jnp.int32),
            pltpu.VMEM((n_idx,), F32),
            pltpu.VMEM((wt, 2 * half), F32),
            pltpu.SemaphoreType.DMA,
        ],
        compiler_params=pltpu.CompilerParams(needs_layout_passes=False),
    )
    def gather_sum(ys_hbm, dest_hbm, gates_hbm, out_hbm, idx_v, rows_v, gate_v, out_v, sem):
        wid = _sc_worker_id()
        pltpu.sync_copy(dest_hbm.at[wid], idx_v)

        @pl.loop(0, n_win)
        def _(j):
            pltpu.async_copy(ys_hbm.at[idx_v.at[j]], rows_v, sem).wait()
            pltpu.sync_copy(gates_hbm.at[pl.ds((wid * n_win + j) * n_idx, n_idx)], gate_v)

            @pl.loop(0, wt)
            def _(t):
                g = [plsc.load_gather(gate_v, [jnp.full((SC_LANES,), k * wt + t, jnp.int32)]) for k in range(K)]

                @pl.loop(0, half // SC_LANES)
                def _(c):
                    col = pl.ds(c * SC_LANES, SC_LANES)
                    acc_lo = jnp.zeros((SC_LANES,), F32)
                    acc_hi = jnp.zeros((SC_LANES,), F32)
                    for k in range(K):
                        w = rows_v[k * wt + t, col]
                        acc_lo = acc_lo + g[k] * plsc.bitcast(w << 16, F32)
                        acc_hi = acc_hi + g[k] * plsc.bitcast(w & HIGH_HALF_MASK, F32)
                    out_v[t, col] = acc_lo
                    out_v[t, pl.ds(half + c * SC_LANES, SC_LANES)] = acc_hi

            pltpu.sync_copy(out_v, out_hbm.at[pl.ds(wid * per_worker + j * wt, wt)])

    return gather_sum(ys, dest_w, gates_w)


def _sc_dispatch(rows, dest_kt, n_slots):
    T, width = rows.shape
    K = dest_kt.shape[0]
    workers = SC_CORES * SC_SUBCORES
    per_worker = T // workers
    n_win = per_worker // SC_WINDOW
    assert T == workers * n_win * SC_WINDOW
    dest_w = dest_kt.reshape(K, workers, n_win, SC_WINDOW).transpose(1, 2, 0, 3)
    dest_w = dest_w.reshape(workers, n_win * K, SC_WINDOW)

    @functools.partial(
        pl.kernel, mesh=_sc_mesh(),
        out_type=jax.ShapeDtypeStruct((n_slots, width), rows.dtype),
        scratch_types=[
            pltpu.VMEM((n_win * K, SC_WINDOW), jnp.int32),
            pltpu.VMEM((SC_WINDOW, width), rows.dtype),
        ],
    )
    def dispatch(rows_hbm, dest_hbm, out_hbm, idx_v, rows_v):
        wid = _sc_worker_id()
        pltpu.sync_copy(dest_hbm.at[wid], idx_v)

        @pl.loop(0, n_win)
        def _(j):
            pltpu.sync_copy(rows_hbm.at[pl.ds(wid * per_worker + j * SC_WINDOW, SC_WINDOW)], rows_v)
            for k in range(K):
                pltpu.sync_copy(rows_v, out_hbm.at[idx_v.at[j * K + k]])

    return dispatch(rows, dest_w)


def _swiglu(h_lo, h_hi, wg_ref, wu_ref):
    def proj(w_ref):
        return (jnp.dot(h_lo, w_ref[:HALF, :], preferred_element_type=F32)
                + jnp.dot(h_hi, w_ref[HALF:, :], preferred_element_type=F32))
    gate = proj(wg_ref)
    up = proj(wu_ref)
    return (gate * jax.nn.sigmoid(gate) * up).astype(BF16)


def _expert_kernel(seg_first_ref, seg_blocks_ref, n_real_ref, xs_hbm, wg_ref, wu_ref, wd_ref, ys_hbm,
                   xbuf, ybuf, wg_bf, wu_bf, wd_bf, in_sems, out_sems):
    e = pl.program_id(0)
    first = seg_first_ref[e]
    count = seg_blocks_ref[e]
    total = n_real_ref[0]
    blk = DISPATCH_BLOCK
    ring = xbuf.shape[0]

    def in_copy(g):
        slot = g % ring
        return pltpu.make_async_copy(xs_hbm.at[pl.ds(g * blk, blk)], xbuf.at[slot], in_sems.at[slot])

    def out_copy(g):
        slot = g % ring
        return pltpu.make_async_copy(ybuf.at[slot], ys_hbm.at[pl.ds(g * blk, blk)], out_sems.at[slot])

    @pl.when(e == 0)
    def _():
        for g in range(ring - 1):
            @pl.when(g < total)
            def _():
                in_copy(g).start()

    @pl.when(count > 0)
    def _():
        wg_bf[...] = wg_ref[0, 0].astype(BF16)
        wu_bf[...] = wu_ref[0, 0].astype(BF16)
        wd_bf[...] = wd_ref[0, 0].astype(BF16)

    def block(g, _):
        slot = g % ring
        in_copy(g).wait()

        @pl.when(g + ring - 1 < total)
        def _():
            in_copy(g + ring - 1).start()

        @pl.when(g >= ring)
        def _():
            out_copy(g - ring).wait()

        lo, hi = _unpack_bf16_pairs(xbuf[slot])
        a = _swiglu(lo.astype(BF16), hi.astype(BF16), wg_bf, wu_bf)
        y = jnp.dot(a, wd_bf[...], preferred_element_type=F32)
        ybuf[slot] = _pack_bf16_pairs(y.astype(BF16))
        out_copy(g).start()
        return 0

    lax.fori_loop(first, first + count, block, 0)

    @pl.when(e == pl.num_programs(0) - 1)
    def _():
        for back in range(ring, 0, -1):
            @pl.when(total >= back)
            def _():
                out_copy(total - back).wait()


def _expert_mm(seg_first, seg_blocks, n_real, xs, wg, wu, wd, layer):
    P = xs.shape[0]
    D = wg.shape[2]
    w_map = lambda e, sf, sb, nr: (layer, e, 0, 0)
    grid_spec = pltpu.PrefetchScalarGridSpec(
        num_scalar_prefetch=3,
        grid=(N_EXPERTS,),
        in_specs=[
            pl.BlockSpec(memory_space=pl.ANY),
            pl.BlockSpec((1, 1, D, D_EXPERT), w_map),
            pl.BlockSpec((1, 1, D, D_EXPERT), w_map),
            pl.BlockSpec((1, 1, D_EXPERT, D), w_map),
        ],
        out_specs=pl.BlockSpec(memory_space=pl.ANY),
        scratch_shapes=[
            pltpu.VMEM((EXPERT_RING, DISPATCH_BLOCK, HALF), jnp.int32),
            pltpu.VMEM((EXPERT_RING, DISPATCH_BLOCK, HALF), jnp.int32),
            pltpu.VMEM((D, D_EXPERT), BF16), pltpu.VMEM((D, D_EXPERT), BF16), pltpu.VMEM((D_EXPERT, D), BF16),
            pltpu.SemaphoreType.DMA((EXPERT_RING,)), pltpu.SemaphoreType.DMA((EXPERT_RING,)),
        ],
    )
    return pl.pallas_call(
        _expert_kernel,
        grid_spec=grid_spec,
        out_shape=jax.ShapeDtypeStruct((P, HALF), jnp.int32),
        compiler_params=_cparams(("arbitrary",)),
        name="expert_mm",
    )(seg_first, seg_blocks, n_real, xs, wg, wu, wd)


def _rank_kernel(idx_ref, rank_ref, cnt_ref, tri_ref, carry_ref):
    i = pl.program_id(0)
    tm = idx_ref.shape[1]

    @pl.when(i == 0)
    def _():
        r = lax.broadcasted_iota(jnp.int32, (tm, tm), 0)
        c = lax.broadcasted_iota(jnp.int32, (tm, tm), 1)
        tri_ref[...] = (r < c).astype(BF16)
        carry_ref[...] = jnp.zeros(carry_ref.shape, F32)

    e_iota = lax.broadcasted_iota(jnp.int32, (N_EXPERTS, tm), 0)
    hits = [e_iota == idx_ref[k:k + 1, :] for k in range(TOP_K)]
    mask = functools.reduce(jnp.logical_or, hits).astype(F32)
    rank = jnp.dot(mask.astype(BF16), tri_ref[...], preferred_element_type=F32) + carry_ref[:, 0:1]
    for k in range(TOP_K):
        rank_ref[k:k + 1, :] = jnp.sum(jnp.where(hits[k], rank, 0.0), axis=0, keepdims=True).astype(jnp.int32)
    carry_ref[...] = carry_ref[...] + jnp.sum(mask, axis=1, keepdims=True)
    cnt_ref[...] = carry_ref[...]


def _rank(idx_kt):
    K, T = idx_kt.shape
    tm = 1024
    return pl.pallas_call(
        _rank_kernel,
        grid=(T // tm,),
        in_specs=[pl.BlockSpec((K, tm), lambda i: (0, i))],
        out_specs=[pl.BlockSpec((K, tm), lambda i: (0, i)),
                   pl.BlockSpec((N_EXPERTS, LANES), lambda i: (0, 0))],
        out_shape=[jax.ShapeDtypeStruct((K, T), jnp.int32),
                   jax.ShapeDtypeStruct((N_EXPERTS, LANES), F32)],
        scratch_shapes=[pltpu.VMEM((tm, tm), BF16), pltpu.VMEM((N_EXPERTS, LANES), F32)],
        compiler_params=_cparams(("arbitrary",)),
        name="rank",
    )(idx_kt)


def _dest_kernel(idx_ref, rank_ref, start_ref, dest_ref):
    tm = idx_ref.shape[1]
    e_iota = lax.broadcasted_iota(jnp.int32, (N_EXPERTS, tm), 0)
    start = start_ref[:, 0:1]
    for k in range(TOP_K):
        base = jnp.sum(jnp.where(e_iota == idx_ref[k:k + 1, :], start, 0.0), axis=0, keepdims=True)
        dest_ref[k:k + 1, :] = rank_ref[k:k + 1, :] + base.astype(jnp.int32)


def _dest(idx_kt, rank_kt, start):
    K, T = idx_kt.shape
    tm = 1024
    tok_spec = pl.BlockSpec((K, tm), lambda i: (0, i))
    return pl.pallas_call(
        _dest_kernel,
        grid=(T // tm,),
        in_specs=[tok_spec, tok_spec, pl.BlockSpec((N_EXPERTS, LANES), lambda i: (0, 0))],
        out_specs=tok_spec,
        out_shape=jax.ShapeDtypeStruct((K, T), jnp.int32),
        compiler_params=_cparams(("arbitrary",)),
        name="dest",
    )(idx_kt, rank_kt, start)


def _block_plan(counts):
    blk = DISPATCH_BLOCK
    seg_blocks = (counts.astype(jnp.int32) + blk - 1) // blk
    seg_end = jnp.cumsum(seg_blocks)
    seg_first = seg_end - seg_blocks
    return seg_first * blk, seg_first, seg_blocks, seg_end[-1:]


def _combine_kernel(h_ref, x1_ref, routed_ref, mod_ref, wsg_ref, wsu_ref, wsd_ref, lng_ref, lnb_ref, o_ref):
    lo, hi = _unpack_bf16_pairs(h_ref[...])
    a = _swiglu(lo.astype(BF16), hi.astype(BF16), wsg_ref, wsu_ref)
    ffn = jnp.dot(a, wsd_ref[...], preferred_element_type=F32) + routed_ref[...]
    g2 = mod_ref[0, 5:6, :]
    o_ref[...] = _layer_norm(DEEPNORM_ALPHA * x1_ref[...] + g2 * ffn, lng_ref[...], lnb_ref[...])


def _combine_into_kernel(h_ref, x1_ref, routed_ref, mod_ref, wsg_ref, wsu_ref, wsd_ref, lng_ref, lnb_ref,
                         prev_ref, o_ref):
    del prev_ref
    _combine_kernel(h_ref, x1_ref, routed_ref, mod_ref, wsg_ref, wsu_ref, wsd_ref, lng_ref, lnb_ref, o_ref)


def _combine(h2p, x1, routed, mod, wsg, wsu, wsd, lng, lnb, seq, b_off, out_rows, out_off, out_prev):
    T, D = x1.shape
    tm = TM_COMBINE
    per_seq = seq // tm
    blk_off = out_off // tm
    const = lambda i: (0, 0)
    in_specs = [
        pl.BlockSpec((tm, HALF), lambda i: (i, 0)),
        pl.BlockSpec((tm, D), lambda i: (i, 0)),
        pl.BlockSpec((tm, D), lambda i: (i, 0)),
        pl.BlockSpec((1, 6, D), lambda i: (i // per_seq + b_off, 0, 0)),
        pl.BlockSpec(wsg.shape, const),
        pl.BlockSpec(wsu.shape, const),
        pl.BlockSpec(wsd.shape, const),
        pl.BlockSpec(lng.shape, const),
        pl.BlockSpec(lnb.shape, const),
    ]
    args = [h2p, x1, routed, mod, wsg, wsu, wsd, lng, lnb]
    body = _combine_kernel
    aliases = {}
    if out_prev is not None:
        in_specs.append(pl.BlockSpec(memory_space=pl.ANY))
        args.append(out_prev)
        aliases = {len(args) - 1: 0}
        body = _combine_into_kernel
    return pl.pallas_call(
        body,
        grid=(T // tm,),
        in_specs=in_specs,
        out_specs=pl.BlockSpec((tm, D), lambda i: (i + blk_off, 0)),
        out_shape=jax.ShapeDtypeStruct((out_rows, D), F32),
        input_output_aliases=aliases,
        compiler_params=_cparams(("arbitrary",)),
        name="combine",
    )(*args)


def _rope_pair(w):
    half = QK_ROPE // 2
    return w, jnp.concatenate([-w[:, half:], w[:, :half]], axis=1)


def _prep_layer(w_in, w_uq, w_ukv, w_pool, w_out, w_router):
    lat = POOL_WIDTH + Q_LORA + KV_LORA
    k_a, k_b = _rope_pair(w_in[:, lat:lat + QK_ROPE])
    w_in_ext = jnp.concatenate([w_in[:, :lat], k_a, k_b], axis=1).astype(BF16)

    qa_cols = []
    per_head = QK_NOPE + QK_ROPE
    for hd in range(N_HEADS):
        w_h = w_uq[:, hd * per_head:(hd + 1) * per_head]
        r_a, r_b = _rope_pair(w_h[:, QK_NOPE:])
        qa_cols += [w_h[:, :QK_NOPE], r_a, r_b]
    w_uqa = jnp.concatenate(qa_cols, axis=1).astype(BF16)

    wpool_bd = jax.scipy.linalg.block_diag(*[w_pool[g] for g in range(len(POOL_WINDOWS))]).astype(BF16)
    wr_hi, wr_lo = _split_hi_lo(w_router.T)
    return dict(w_in_ext=w_in_ext, w_uqa=w_uqa, w_ukv=w_ukv.astype(BF16), wpool_bd=wpool_bd,
                wout_p=w_out[:POOL_WIDTH].astype(BF16), wout_a=w_out[POOL_WIDTH:].astype(BF16),
                wr_hi=wr_hi, wr_lo=wr_lo)


def kernel(x, c, positions, w_ada, b_ada, w_in, q_norm_g, kv_norm_g, w_uq, w_ukv, w_pool, pool_scale,
           w_out, ln1_g, ln1_b, w_router, router_bias, w_exp_gate, w_exp_up, w_exp_down,
           w_sh_gate, w_sh_up, w_sh_down, ln2_g, ln2_b):
    B, S, D = x.shape
    L = w_in.shape[0]
    row = lambda v: v.reshape(1, -1)

    mod_all = _ada_mod(c, w_ada, b_ada).reshape(L, B, 6, D)
    cs_all = _rope_table(positions)
    bs = B // N_STREAMS
    T = bs * S
    n_blocks = T * TOP_K // DISPATCH_BLOCK + N_EXPERTS
    xs_streams = [x] * N_STREAMS
    result = None
    for l in range(L):
        p = _prep_layer(w_in[l], w_uq[l], w_ukv[l], w_pool[l], w_out[l], w_router[l])
        shared = (w_sh_gate[l].astype(BF16), w_sh_up[l].astype(BF16), w_sh_down[l].astype(BF16))
        mod = mod_all[l]
        for s in range(N_STREAMS):
            xl = xs_streams[s]
            b_off = s * bs
            x_off = b_off if l == 0 else 0
            u, q, k, v = _in_proj(xl, mod, cs_all, p["w_in_ext"], row(q_norm_g[l]), row(kv_norm_g[l]),
                                  p["w_uqa"], p["w_ukv"], bs, x_off, b_off)
            attn = _attention(q, k, v)
            x1, h2p, lt = _post_attn(xl, u, attn, mod, p["wpool_bd"], row(pool_scale[l]), p["wout_p"],
                                     p["wout_a"], row(ln1_g[l]), row(ln1_b[l]), p["wr_hi"], p["wr_lo"],
                                     x_off, b_off)
            idx_k, gate_k = _route(lt.reshape(N_EXPERTS, T // LANES, LANES), router_bias[l])
            idx_kt = idx_k.reshape(TOP_K, T)
            gates_kt = gate_k.reshape(TOP_K, T)
            rank_kt, counts = _rank(idx_kt)
            pad_start, seg_first, seg_blocks, n_real = _block_plan(counts[:, 0])
            start = jnp.broadcast_to(pad_start.astype(F32)[:, None], (N_EXPERTS, LANES))
            dest_kt = _dest(idx_kt, rank_kt, start)
            h2p = h2p.reshape(T, HALF)
            rows = _sc_dispatch(h2p, dest_kt, n_blocks * DISPATCH_BLOCK)
            ys = _expert_mm(seg_first, seg_blocks, n_real, rows, w_exp_gate, w_exp_up, w_exp_down, l)
            routed = _sc_gather_sum(ys, dest_kt, gates_kt)
            if l < L - 1:
                xs_streams[s] = _combine(h2p, x1.reshape(T, D), routed, mod, *shared, row(ln2_g[l]),
                                         row(ln2_b[l]), S, b_off, T, 0, None).reshape(bs, S, D)
            else:
                result = _combine(h2p, x1.reshape(T, D), routed, mod, *shared, row(ln2_g[l]),
                                  row(ln2_b[l]), S, b_off, B * S, s * T, result)
    return result.reshape(B, S, D)
```

```python
import functools

import jax
import jax.numpy as jnp
from jax import lax
from jax.experimental import pallas as pl
from jax.experimental.pallas import tpu as pltpu
from jax.experimental.pallas import tpu_sc as plsc

F32 = jnp.float32
BF16 = jnp.bfloat16

D_MODEL = 1024
DEPTH = 4
POOL_WIDTH = 256
POOL_WINDOWS = (2, 4, 8, 16)
POOL_GROUP = 64
POOL_HALO = 16
QK_NOPE = 128
QK_ROPE = 64
V_HEAD = 128
N_HEADS = 6
Q_LORA = 384
KV_LORA = 256
ROPE_THETA = 10000.0
N_EXPERTS = 64
TOP_K = 8
N_GROUPS = 8
GROUP_SIZE = N_EXPERTS // N_GROUPS
TOPK_GROUPS = 4
D_EXPERT = 256
ROUTED_SCALE = 2.5
DEEPNORM_ALPHA = (2.0 * DEPTH) ** 0.25
LN_EPS = 1e-5
RMS_EPS = 1e-6
ATTN_SCALE = (QK_NOPE + QK_ROPE) ** -0.5
LOG2_E = 1.4426950408889634
Q_SCALE = ATTN_SCALE * LOG2_E
LANES = 128
QK_PAD = 2 * LANES
MASK_VALUE = -1e30

TM_PROJ = 512
TQ = 512
TK = 512
ROUTE_ROWS = 8
DISPATCH_BLOCK = 512
EXPERT_RING = 6
TM_COMBINE = 512
SC_CORES = 2
SC_SUBCORES = 16
SC_WINDOW = 128
SC_LANES = 16
HIGH_HALF_MASK = -65536
SC_SUM_TOKENS = 16
N_STREAMS = 2
COMBINE_CHUNKS = 2
HALF = D_MODEL // 2
VMEM_LIMIT = 48 * 1024 * 1024


def _cparams(sem):
    return pltpu.CompilerParams(dimension_semantics=sem, vmem_limit_bytes=VMEM_LIMIT)


def _split_hi_lo(a):
    hi = a.astype(BF16)
    lo = (a - hi.astype(F32)).astype(BF16)
    return hi, lo


def _pack_bf16_pairs(a):
    bits = lax.bitcast_convert_type(a.astype(F32), jnp.uint32)
    half = a.shape[1] // 2
    word = (bits[:, :half] >> 16) | (bits[:, half:] & jnp.uint32(0xFFFF0000))
    return lax.bitcast_convert_type(word, jnp.int32)


def _unpack_bf16_pairs(w):
    bits = lax.bitcast_convert_type(w, jnp.uint32)
    lo = lax.bitcast_convert_type(bits << 16, F32)
    hi = lax.bitcast_convert_type(bits & jnp.uint32(0xFFFF0000), F32)
    return lo, hi


def _ada_kernel(c_ref, w_ref, b_ref, o_ref):
    c = c_ref[...]
    cond = c * jax.nn.sigmoid(c)
    c_hi, c_lo = _split_hi_lo(cond)
    w_hi, w_lo = _split_hi_lo(w_ref[0])
    acc = jnp.dot(c_hi, w_hi, preferred_element_type=F32)
    acc += jnp.dot(c_lo, w_hi, preferred_element_type=F32)
    acc += jnp.dot(c_hi, w_lo, preferred_element_type=F32)
    o_ref[0] = acc + b_ref[0]


def _ada_mod(c, w_ada, b_ada):
    L, D, N = w_ada.shape
    B = c.shape[0]
    tn = 1536
    return pl.pallas_call(
        _ada_kernel,
        grid=(L, N // tn),
        in_specs=[
            pl.BlockSpec((B, D), lambda l, j: (0, 0)),
            pl.BlockSpec((1, D, tn), lambda l, j: (l, 0, j)),
            pl.BlockSpec((1, 1, tn), lambda l, j: (l, 0, j)),
        ],
        out_specs=pl.BlockSpec((1, B, tn), lambda l, j: (l, 0, j)),
        out_shape=jax.ShapeDtypeStruct((L, B, N), F32),
        compiler_params=_cparams(("arbitrary", "arbitrary")),
        name="ada_mod",
    )(c, w_ada, b_ada.reshape(L, 1, N))


def _rope_table_kernel(pos_ref, freq_ref, o_ref):
    ang = pos_ref[0].astype(F32) * freq_ref[...]
    lane = lax.broadcasted_iota(jnp.int32, (1, LANES), 1)
    o_ref[0] = jnp.where(lane < QK_ROPE, jnp.cos(ang), jnp.sin(ang))


def _rope_table(positions):
    B, S = positions.shape
    half = QK_ROPE // 2
    inv_freq = ROPE_THETA ** (-jnp.arange(half, dtype=F32) / half)
    freq = jnp.tile(inv_freq, LANES // half).reshape(1, LANES)
    return pl.pallas_call(
        _rope_table_kernel,
        grid=(B,),
        in_specs=[
            pl.BlockSpec((1, S, 1), lambda b: (b, 0, 0)),
            pl.BlockSpec((1, LANES), lambda b: (0, 0)),
        ],
        out_specs=pl.BlockSpec((1, S, LANES), lambda b: (b, 0, 0)),
        out_shape=jax.ShapeDtypeStruct((B, S, LANES), F32),
        compiler_params=_cparams(("arbitrary",)),
        name="rope_table",
    )(positions.reshape(B, S, 1), freq)


def _rms(x, g):
    return x * lax.rsqrt(jnp.mean(x * x, axis=-1, keepdims=True) + RMS_EPS) * g


def _in_proj_kernel(x_ref, mod_ref, cs_ref, w_in_ref, qg_ref, kvg_ref, w_uqa_ref, w_ukv_ref,
                    u_ref, q_ref, k_ref, v_ref):
    x = x_ref[0]
    sh1 = mod_ref[0, 0:1, :]
    sc1 = mod_ref[0, 1:2, :]
    h = (x * (1.0 + sc1) + sh1).astype(BF16)
    proj = jnp.dot(h, w_in_ref[...], preferred_element_type=F32)
    u_ref[0] = proj[:, 0:POOL_WIDTH]
    o = POOL_WIDTH
    q_lat = proj[:, o:o + Q_LORA]
    o += Q_LORA
    kv_lat = proj[:, o:o + KV_LORA]
    o += KV_LORA
    cos_sin = cs_ref[0]

    def rotate(pair):
        t = pair * cos_sin
        return t + pltpu.roll(t, QK_ROPE, 1)

    k_rot = rotate(proj[:, o:o + LANES]).astype(BF16)
    lane = lax.broadcasted_iota(jnp.int32, (1, LANES), 1)
    q_keep = jnp.where(lane < QK_ROPE, Q_SCALE, 0.0)

    qn = _rms(q_lat, qg_ref[...]).astype(BF16)
    q_a = jnp.dot(qn, w_uqa_ref[...], preferred_element_type=F32)
    kvn = _rms(kv_lat, kvg_ref[...]).astype(BF16)
    kv = jnp.dot(kvn, w_ukv_ref[...], preferred_element_type=F32)
    for hd in range(N_HEADS):
        b0 = hd * QK_PAD
        q_ref[0, hd, :, 0:LANES] = (q_a[:, b0:b0 + LANES] * Q_SCALE).astype(BF16)
        q_ref[0, hd, :, LANES:QK_PAD] = (rotate(q_a[:, b0 + LANES:b0 + QK_PAD]) * q_keep).astype(BF16)
        k_ref[0, hd, :, 0:LANES] = kv[:, b0:b0 + QK_NOPE].astype(BF16)
        k_ref[0, hd, :, LANES:QK_PAD] = k_rot
        v_ref[0, hd] = kv[:, b0 + QK_NOPE:b0 + QK_NOPE + V_HEAD].astype(BF16)


def _in_proj(x, mod, cs, w_in_ext, qg, kvg, w_uqa, w_ukv_bf, B, x_off, b_off):
    _, S, D = x.shape
    tm = TM_PROJ
    const = lambda b, i: (0, 0)
    return pl.pallas_call(
        _in_proj_kernel,
        grid=(B, S // tm),
        in_specs=[
            pl.BlockSpec((1, tm, D), lambda b, i: (b + x_off, i, 0)),
            pl.BlockSpec((1, 6, D), lambda b, i: (b + b_off, 0, 0)),
            pl.BlockSpec((1, tm, LANES), lambda b, i: (b + b_off, i, 0)),
            pl.BlockSpec(w_in_ext.shape, const),
            pl.BlockSpec(qg.shape, const),
            pl.BlockSpec(kvg.shape, const),
            pl.BlockSpec(w_uqa.shape, const),
            pl.BlockSpec(w_ukv_bf.shape, const),
        ],
        out_specs=[
            pl.BlockSpec((1, tm, POOL_WIDTH), lambda b, i: (b, i, 0)),
            pl.BlockSpec((1, N_HEADS, tm, QK_PAD), lambda b, i: (b, 0, i, 0)),
            pl.BlockSpec((1, N_HEADS, tm, QK_PAD), lambda b, i: (b, 0, i, 0)),
            pl.BlockSpec((1, N_HEADS, tm, V_HEAD), lambda b, i: (b, 0, i, 0)),
        ],
        out_shape=[
            jax.ShapeDtypeStruct((B, S, POOL_WIDTH), F32),
            jax.ShapeDtypeStruct((B, N_HEADS, S, QK_PAD), BF16),
            jax.ShapeDtypeStruct((B, N_HEADS, S, QK_PAD), BF16),
            jax.ShapeDtypeStruct((B, N_HEADS, S, V_HEAD), BF16),
        ],
        compiler_params=_cparams(("arbitrary", "arbitrary")),
        name="in_proj",
    )(x, mod, cs, w_in_ext, qg, kvg, w_uqa, w_ukv_bf)


def _softmax_step(q, k, v, carry, mask):
    m, l, acc = carry
    s = lax.dot_general(q, k, (((1,), (1,)), ((), ())), preferred_element_type=F32)
    if mask is not None:
        s = jnp.where(mask, s, MASK_VALUE)
    m_new = jnp.maximum(m, jnp.max(s, axis=-1, keepdims=True))
    alpha = jnp.exp2(m - m_new)
    p = jnp.exp2(s - m_new)
    l_new = alpha * l + jnp.sum(p, axis=-1, keepdims=True)
    acc_new = alpha * acc + jnp.dot(p.astype(BF16), v, preferred_element_type=F32)
    return m_new, l_new, acc_new


def _attn_kernel(q_ref, k_ref, v_ref, o_ref):
    seq = q_ref.shape[2]
    row = lax.broadcasted_iota(jnp.int32, (TQ, TK), 0)
    col = lax.broadcasted_iota(jnp.int32, (TQ, TK), 1)
    diag = row >= col
    for i in range(seq // TQ):
        q = q_ref[0, 0, i * TQ:(i + 1) * TQ, :]
        carry = (jnp.full((TQ, 1), MASK_VALUE, F32), jnp.zeros((TQ, 1), F32), jnp.zeros((TQ, V_HEAD), F32))
        for j in range(i + 1):
            k = k_ref[0, 0, j * TK:(j + 1) * TK, :]
            v = v_ref[0, 0, j * TK:(j + 1) * TK, :]
            carry = _softmax_step(q, k, v, carry, diag if j == i else None)
        _, l, acc = carry
        o_ref[0, 0, i * TQ:(i + 1) * TQ, :] = (acc / l).astype(BF16)


def _attention(q, k, v):
    B, H, S, _ = q.shape
    return pl.pallas_call(
        _attn_kernel,
        grid=(B, H),
        in_specs=[
            pl.BlockSpec((1, 1, S, QK_PAD), lambda b, h: (b, h, 0, 0)),
            pl.BlockSpec((1, 1, S, QK_PAD), lambda b, h: (b, h, 0, 0)),
            pl.BlockSpec((1, 1, S, V_HEAD), lambda b, h: (b, h, 0, 0)),
        ],
        out_specs=pl.BlockSpec((1, 1, S, V_HEAD), lambda b, h: (b, h, 0, 0)),
        out_shape=jax.ShapeDtypeStruct((B, H, S, V_HEAD), BF16),
        compiler_params=_cparams(("arbitrary", "arbitrary")),
        name="attention",
    )(q, k, v)


def _layer_norm(v, g, b):
    mu = jnp.mean(v, axis=-1, keepdims=True)
    d = v - mu
    var = jnp.mean(d * d, axis=-1, keepdims=True)
    return d * lax.rsqrt(var + LN_EPS) * g + b


def _post_kernel(x_ref, u_ref, halo_ref, attn_ref, mod_ref, wpool_ref, pscale_ref, wout_p_ref, wout_a_ref,
                 lng_ref, lnb_ref, wr_hi_ref, wr_lo_ref, x1_ref, h2_ref, lt_ref):
    i = pl.program_id(1)
    tm = u_ref.shape[1]
    u = u_ref[0]
    halo = jnp.where(i > 0, halo_ref[0], 0.0)
    ext = jnp.concatenate([halo, u], axis=0)
    s2 = ext + pltpu.roll(ext, 1, 0)
    s4 = s2 + pltpu.roll(s2, 2, 0)
    s8 = s4 + pltpu.roll(s4, 4, 0)
    s16 = s8 + pltpu.roll(s8, 8, 0)
    lane = lax.broadcasted_iota(jnp.int32, (1, POOL_WIDTH), 1)
    win = jnp.where(lane < POOL_GROUP, s2,
                    jnp.where(lane < 2 * POOL_GROUP, s4, jnp.where(lane < 3 * POOL_GROUP, s8, s16)))
    win = win[POOL_HALO:, :]
    width = jnp.where(lane < POOL_GROUP, POOL_WINDOWS[0],
                      jnp.where(lane < 2 * POOL_GROUP, POOL_WINDOWS[1],
                                jnp.where(lane < 3 * POOL_GROUP, POOL_WINDOWS[2], POOL_WINDOWS[3])))
    t = i * tm + lax.broadcasted_iota(jnp.int32, (tm, 1), 0)
    count = jnp.minimum(t + 1, width).astype(F32)
    token_mix = win / count - u
    pooled = jnp.dot(token_mix.astype(BF16), wpool_ref[...], preferred_element_type=F32) * pscale_ref[...]
    mixed = jnp.dot(pooled.astype(BF16), wout_p_ref[...], preferred_element_type=F32)
    attn = jnp.concatenate([attn_ref[0, hd] for hd in range(N_HEADS)], axis=1)
    mixed += jnp.dot(attn, wout_a_ref[...], preferred_element_type=F32)

    g1 = mod_ref[0, 2:3, :]
    sh2 = mod_ref[0, 3:4, :]
    sc2 = mod_ref[0, 4:5, :]
    x1 = _layer_norm(DEEPNORM_ALPHA * x_ref[0] + g1 * mixed, lng_ref[...], lnb_ref[...])
    x1_ref[0] = x1
    h2 = x1 * (1.0 + sc2) + sh2
    h_hi, h_lo = _split_hi_lo(h2)
    h2_ref[0] = _pack_bf16_pairs(h_hi)
    nt = (((1,), (1,)), ((), ()))
    lt = lax.dot_general(wr_hi_ref[...], h_hi, nt, preferred_element_type=F32)
    lt += lax.dot_general(wr_hi_ref[...], h_lo, nt, preferred_element_type=F32)
    lt += lax.dot_general(wr_lo_ref[...], h_hi, nt, preferred_element_type=F32)
    lt_ref[...] = lt


def _post_attn(x, u, attn, mod, wpool_bd, pscale, wout_p, wout_a, lng, lnb, wr_hi, wr_lo, x_off, b_off):
    B, S, _ = u.shape
    D = x.shape[2]
    tm = TM_PROJ
    nt = S // tm
    const = lambda b, i: (0, 0)
    halo_blocks = tm // POOL_HALO
    return pl.pallas_call(
        _post_kernel,
        grid=(B, nt),
        in_specs=[
            pl.BlockSpec((1, tm, D), lambda b, i: (b + x_off, i, 0)),
            pl.BlockSpec((1, tm, POOL_WIDTH), lambda b, i: (b, i, 0)),
            pl.BlockSpec((1, POOL_HALO, POOL_WIDTH), lambda b, i: (b, jnp.maximum(i * halo_blocks - 1, 0), 0)),
            pl.BlockSpec((1, N_HEADS, tm, V_HEAD), lambda b, i: (b, 0, i, 0)),
            pl.BlockSpec((1, 6, D), lambda b, i: (b + b_off, 0, 0)),
            pl.BlockSpec(wpool_bd.shape, const),
            pl.BlockSpec(pscale.shape, const),
            pl.BlockSpec(wout_p.shape, const),
            pl.BlockSpec(wout_a.shape, const),
            pl.BlockSpec(lng.shape, const),
            pl.BlockSpec(lnb.shape, const),
            pl.BlockSpec(wr_hi.shape, const),
            pl.BlockSpec(wr_lo.shape, const),
        ],
        out_specs=[
            pl.BlockSpec((1, tm, D), lambda b, i: (b, i, 0)),
            pl.BlockSpec((1, tm, HALF), lambda b, i: (b, i, 0)),
            pl.BlockSpec((N_EXPERTS, tm), lambda b, i: (0, b * nt + i)),
        ],
        out_shape=[
            jax.ShapeDtypeStruct((B, S, D), F32),
            jax.ShapeDtypeStruct((B, S, HALF), jnp.int32),
            jax.ShapeDtypeStruct((N_EXPERTS, B * S), F32),
        ],
        compiler_params=_cparams(("arbitrary", "arbitrary")),
        name="post_attn",
    )(x, u, u, attn, mod, wpool_bd, pscale, wout_p, wout_a, lng, lnb, wr_hi, wr_lo)


def _select_first_max(vals, n_rounds, payload=None):
    work = list(vals)
    sel = [None] * len(vals)
    rounds = []
    for _ in range(n_rounds):
        m = functools.reduce(jnp.maximum, work)
        taken = None
        win_idx = jnp.zeros(m.shape, jnp.int32)
        win_val = jnp.zeros(m.shape, F32)
        for e in range(len(work)):
            hit = work[e] == m
            first = hit if taken is None else jnp.logical_and(hit, jnp.logical_not(taken))
            taken = hit if taken is None else jnp.logical_or(taken, hit)
            sel[e] = first if sel[e] is None else jnp.logical_or(sel[e], first)
            work[e] = jnp.where(first, -jnp.inf, work[e])
            if payload is not None:
                win_idx = jnp.where(first, e, win_idx)
                win_val = jnp.where(first, payload[e], win_val)
        rounds.append((win_idx, win_val))
    return sel, rounds


def _route_kernel(bias_ref, lt_ref, idx_ref, gate_ref):
    scores = [jax.nn.sigmoid(lt_ref[e]) for e in range(N_EXPERTS)]
    choice = [scores[e] + bias_ref[e] for e in range(N_EXPERTS)]
    group_score = []
    for g in range(N_GROUPS):
        vals = choice[g * GROUP_SIZE:(g + 1) * GROUP_SIZE]
        sel2, _ = _select_first_max(vals, 2)
        group_score.append(functools.reduce(
            jnp.add, [jnp.where(sel2[j], vals[j], 0.0) for j in range(GROUP_SIZE)]))
    group_sel, _ = _select_first_max(group_score, TOPK_GROUPS)
    masked = [jnp.where(group_sel[e // GROUP_SIZE], choice[e], -jnp.inf) for e in range(N_EXPERTS)]
    _, rounds = _select_first_max(masked, TOP_K, payload=scores)
    total = functools.reduce(jnp.add, [w for _, w in rounds])
    for k, (e_k, w_k) in enumerate(rounds):
        idx_ref[k] = e_k
        gate_ref[k] = w_k / total * ROUTED_SCALE


def _route(lt3, bias):
    E, R, _ = lt3.shape
    out_spec = pl.BlockSpec((TOP_K, ROUTE_ROWS, LANES), lambda r: (0, r, 0))
    return pl.pallas_call(
        _route_kernel,
        grid=(R // ROUTE_ROWS,),
        in_specs=[
            pl.BlockSpec(memory_space=pltpu.SMEM),
            pl.BlockSpec((E, ROUTE_ROWS, LANES), lambda r: (0, r, 0)),
        ],
        out_specs=[out_spec, out_spec],
        out_shape=[jax.ShapeDtypeStruct((TOP_K, R, LANES), jnp.int32),
                   jax.ShapeDtypeStruct((TOP_K, R, LANES), F32)],
        compiler_params=_cparams(("arbitrary",)),
        name="route",
    )(bias, lt3)


def _sc_mesh():
    return plsc.VectorSubcoreMesh(core_axis_name="c", subcore_axis_name="s",
                                  num_cores=SC_CORES, num_subcores=SC_SUBCORES)


def _sc_worker_id():
    return lax.axis_index("s") * SC_CORES + lax.axis_index("c")


def _sc_gather(table, idx):
    n = idx.shape[0]
    width = table.shape[1]
    workers = SC_CORES * SC_SUBCORES
    per_worker = n // workers
    n_win = per_worker // SC_WINDOW
    assert n == workers * n_win * SC_WINDOW

    @functools.partial(
        pl.kernel, mesh=_sc_mesh(),
        out_type=jax.ShapeDtypeStruct((n, width), table.dtype),
        scratch_types=[
            pltpu.VMEM((n_win, SC_WINDOW), jnp.int32),
            pltpu.VMEM((SC_WINDOW, width), table.dtype),
            pltpu.SemaphoreType.DMA,
        ],
    )
    def gather(table_hbm, idx_hbm, out_hbm, idx_v, rows_v, sem):
        wid = _sc_worker_id()
        pltpu.sync_copy(idx_hbm.at[wid], idx_v)

        @pl.loop(0, n_win)
        def _(j):
            pltpu.async_copy(table_hbm.at[idx_v.at[j]], rows_v, sem).wait()
            pltpu.sync_copy(rows_v, out_hbm.at[pl.ds(wid * per_worker + j * SC_WINDOW, SC_WINDOW)])

    return gather(table, idx.reshape(workers, n_win, SC_WINDOW))


def _sc_gather_sum(ys, dest_kt, gates_kt):
    K, T = dest_kt.shape
    half = ys.shape[1]
    workers = SC_CORES * SC_SUBCORES
    per_worker = T // workers
    wt = SC_SUM_TOKENS
    n_win = per_worker // wt
    n_idx = K * wt
    assert T == workers * n_win * wt and n_idx <= 128 and half % SC_LANES == 0
    dest_w = dest_kt.reshape(K, workers, n_win, wt).transpose(1, 2, 0, 3).reshape(workers, n_win, n_idx)
    gates_w = gates_kt.reshape(K, workers, n_win, wt).transpose(1, 2, 0, 3).reshape(workers * n_win * n_idx)

    @functools.partial(
        pl.kernel, mesh=_sc_mesh(),
        out_type=jax.ShapeDtypeStruct((T, 2 * half), F32),
        scratch_types=[
            pltpu.VMEM((n_win, n_idx), jnp.int32),
            pltpu.VMEM((n_idx, half), jnp.int32),
            pltpu.VMEM((n_idx,), F32),
            pltpu.VMEM((wt, 2 * half), F32),
            pltpu.SemaphoreType.DMA,
        ],
        compiler_params=pltpu.CompilerParams(needs_layout_passes=False),
    )
    def gather_sum(ys_hbm, dest_hbm, gates_hbm, out_hbm, idx_v, rows_v, gate_v, out_v, sem):
        wid = _sc_worker_id()
        pltpu.sync_copy(dest_hbm.at[wid], idx_v)

        @pl.loop(0, n_win)
        def _(j):
            pltpu.async_copy(ys_hbm.at[idx_v.at[j]], rows_v, sem).wait()
            pltpu.sync_copy(gates_hbm.at[pl.ds((wid * n_win + j) * n_idx, n_idx)], gate_v)

            @pl.loop(0, wt)
            def _(t):
                g = [plsc.load_gather(gate_v, [jnp.full((SC_LANES,), k * wt + t, jnp.int32)]) for k in range(K)]

                @pl.loop(0, half // SC_LANES)
                def _(c):
                    col = pl.ds(c * SC_LANES, SC_LANES)
                    acc_lo = jnp.zeros((SC_LANES,), F32)
                    acc_hi = jnp.zeros((SC_LANES,), F32)
                    for k in range(K):
                        w = rows_v[k * wt + t, col]
                        acc_lo = acc_lo + g[k] * plsc.bitcast(w << 16, F32)
                        acc_hi = acc_hi + g[k] * plsc.bitcast(w & HIGH_HALF_MASK, F32)
                    out_v[t, col] = acc_lo
                    out_v[t, pl.ds(half + c * SC_LANES, SC_LANES)] = acc_hi

            pltpu.sync_copy(out_v, out_hbm.at[pl.ds(wid * per_worker + j * wt, wt)])

    return gather_sum(ys, dest_w, gates_w)


def _sc_dispatch(rows, dest_kt, n_slots):
    T, width = rows.shape
    K = dest_kt.shape[0]
    workers = SC_CORES * SC_SUBCORES
    per_worker = T // workers
    n_win = per_worker // SC_WINDOW
    assert T == workers * n_win * SC_WINDOW
    dest_w = dest_kt.reshape(K, workers, n_win, SC_WINDOW).transpose(1, 2, 0, 3)
    dest_w = dest_w.reshape(workers, n_win * K, SC_WINDOW)

    @functools.partial(
        pl.kernel, mesh=_sc_mesh(),
        out_type=jax.ShapeDtypeStruct((n_slots, width), rows.dtype),
        scratch_types=[
            pltpu.VMEM((n_win * K, SC_WINDOW), jnp.int32),
            pltpu.VMEM((SC_WINDOW, width), rows.dtype),
        ],
    )
    def dispatch(rows_hbm, dest_hbm, out_hbm, idx_v, rows_v):
        wid = _sc_worker_id()
        pltpu.sync_copy(dest_hbm.at[wid], idx_v)

        @pl.loop(0, n_win)
        def _(j):
            pltpu.sync_copy(rows_hbm.at[pl.ds(wid * per_worker + j * SC_WINDOW, SC_WINDOW)], rows_v)
            for k in range(K):
                pltpu.sync_copy(rows_v, out_hbm.at[idx_v.at[j * K + k]])

    return dispatch(rows, dest_w)


def _swiglu(h_lo, h_hi, wg_ref, wu_ref):
    def proj(w_ref):
        return (jnp.dot(h_lo, w_ref[:HALF, :], preferred_element_type=F32)
                + jnp.dot(h_hi, w_ref[HALF:, :], preferred_element_type=F32))
    gate = proj(wg_ref)
    up = proj(wu_ref)
    return (gate * jax.nn.sigmoid(gate) * up).astype(BF16)


def _expert_kernel(seg_first_ref, seg_blocks_ref, n_real_ref, xs_hbm, wg_ref, wu_ref, wd_ref, ys_hbm,
                   xbuf, ybuf, wg_bf, wu_bf, wd_bf, in_sems, out_sems):
    e = pl.program_id(0)
    first = seg_first_ref[e]
    count = seg_blocks_ref[e]
    total = n_real_ref[0]
    blk = DISPATCH_BLOCK
    ring = xbuf.shape[0]

    def in_copy(g):
        slot = g % ring
        return pltpu.make_async_copy(xs_hbm.at[pl.ds(g * blk, blk)], xbuf.at[slot], in_sems.at[slot])

    def out_copy(g):
        slot = g % ring
        return pltpu.make_async_copy(ybuf.at[slot], ys_hbm.at[pl.ds(g * blk, blk)], out_sems.at[slot])

    @pl.when(e == 0)
    def _():
        for g in range(ring - 1):
            @pl.when(g < total)
            def _():
                in_copy(g).start()

    @pl.when(count > 0)
    def _():
        wg_bf[...] = wg_ref[0, 0].astype(BF16)
        wu_bf[...] = wu_ref[0, 0].astype(BF16)
        wd_bf[...] = wd_ref[0, 0].astype(BF16)

    def block(g, _):
        slot = g % ring
        in_copy(g).wait()

        @pl.when(g + ring - 1 < total)
        def _():
            in_copy(g + ring - 1).start()

        @pl.when(g >= ring)
        def _():
            out_copy(g - ring).wait()

        lo, hi = _unpack_bf16_pairs(xbuf[slot])
        a = _swiglu(lo.astype(BF16), hi.astype(BF16), wg_bf, wu_bf)
        y = jnp.dot(a, wd_bf[...], preferred_element_type=F32)
        ybuf[slot] = _pack_bf16_pairs(y.astype(BF16))
        out_copy(g).start()
        return 0

    lax.fori_loop(first, first + count, block, 0)

    @pl.when(e == pl.num_programs(0) - 1)
    def _():
        for back in range(ring, 0, -1):
            @pl.when(total >= back)
            def _():
                out_copy(total - back).wait()


def _expert_mm(seg_first, seg_blocks, n_real, xs, wg, wu, wd, layer):
    P = xs.shape[0]
    D = wg.shape[2]
    w_map = lambda e, sf, sb, nr: (layer, e, 0, 0)
    grid_spec = pltpu.PrefetchScalarGridSpec(
        num_scalar_prefetch=3,
        grid=(N_EXPERTS,),
        in_specs=[
            pl.BlockSpec(memory_space=pl.ANY),
            pl.BlockSpec((1, 1, D, D_EXPERT), w_map),
            pl.BlockSpec((1, 1, D, D_EXPERT), w_map),
            pl.BlockSpec((1, 1, D_EXPERT, D), w_map),
        ],
        out_specs=pl.BlockSpec(memory_space=pl.ANY),
        scratch_shapes=[
            pltpu.VMEM((EXPERT_RING, DISPATCH_BLOCK, HALF), jnp.int32),
            pltpu.VMEM((EXPERT_RING, DISPATCH_BLOCK, HALF), jnp.int32),
            pltpu.VMEM((D, D_EXPERT), BF16), pltpu.VMEM((D, D_EXPERT), BF16), pltpu.VMEM((D_EXPERT, D), BF16),
            pltpu.SemaphoreType.DMA((EXPERT_RING,)), pltpu.SemaphoreType.DMA((EXPERT_RING,)),
        ],
    )
    return pl.pallas_call(
        _expert_kernel,
        grid_spec=grid_spec,
        out_shape=jax.ShapeDtypeStruct((P, HALF), jnp.int32),
        compiler_params=_cparams(("arbitrary",)),
        name="expert_mm",
    )(seg_first, seg_blocks, n_real, xs, wg, wu, wd)


def _rank_kernel(idx_ref, rank_ref, cnt_ref, tri_ref, carry_ref):
    i = pl.program_id(0)
    tm = idx_ref.shape[1]

    @pl.when(i == 0)
    def _():
        r = lax.broadcasted_iota(jnp.int32, (tm, tm), 0)
        c = lax.broadcasted_iota(jnp.int32, (tm, tm), 1)
        tri_ref[...] = (r < c).astype(BF16)
        carry_ref[...] = jnp.zeros(carry_ref.shape, F32)

    e_iota = lax.broadcasted_iota(jnp.int32, (N_EXPERTS, tm), 0)
    hits = [e_iota == idx_ref[k:k + 1, :] for k in range(TOP_K)]
    mask = functools.reduce(jnp.logical_or, hits).astype(F32)
    rank = jnp.dot(mask.astype(BF16), tri_ref[...], preferred_element_type=F32) + carry_ref[:, 0:1]
    for k in range(TOP_K):
        rank_ref[k:k + 1, :] = jnp.sum(jnp.where(hits[k], rank, 0.0), axis=0, keepdims=True).astype(jnp.int32)
    carry_ref[...] = carry_ref[...] + jnp.sum(mask, axis=1, keepdims=True)
    cnt_ref[...] = carry_ref[...]


def _rank(idx_kt):
    K, T = idx_kt.shape
    tm = 1024
    return pl.pallas_call(
        _rank_kernel,
        grid=(T // tm,),
        in_specs=[pl.BlockSpec((K, tm), lambda i: (0, i))],
        out_specs=[pl.BlockSpec((K, tm), lambda i: (0, i)),
                   pl.BlockSpec((N_EXPERTS, LANES), lambda i: (0, 0))],
        out_shape=[jax.ShapeDtypeStruct((K, T), jnp.int32),
                   jax.ShapeDtypeStruct((N_EXPERTS, LANES), F32)],
        scratch_shapes=[pltpu.VMEM((tm, tm), BF16), pltpu.VMEM((N_EXPERTS, LANES), F32)],
        compiler_params=_cparams(("arbitrary",)),
        name="rank",
    )(idx_kt)


def _dest_kernel(idx_ref, rank_ref, start_ref, dest_ref):
    tm = idx_ref.shape[1]
    e_iota = lax.broadcasted_iota(jnp.int32, (N_EXPERTS, tm), 0)
    start = start_ref[:, 0:1]
    for k in range(TOP_K):
        base = jnp.sum(jnp.where(e_iota == idx_ref[k:k + 1, :], start, 0.0), axis=0, keepdims=True)
        dest_ref[k:k + 1, :] = rank_ref[k:k + 1, :] + base.astype(jnp.int32)


def _dest(idx_kt, rank_kt, start):
    K, T = idx_kt.shape
    tm = 1024
    tok_spec = pl.BlockSpec((K, tm), lambda i: (0, i))
    return pl.pallas_call(
        _dest_kernel,
        grid=(T // tm,),
        in_specs=[tok_spec, tok_spec, pl.BlockSpec((N_EXPERTS, LANES), lambda i: (0, 0))],
        out_specs=tok_spec,
        out_shape=jax.ShapeDtypeStruct((K, T), jnp.int32),
        compiler_params=_cparams(("arbitrary",)),
        name="dest",
    )(idx_kt, rank_kt, start)


def _block_plan(counts):
    blk = DISPATCH_BLOCK
    seg_blocks = (counts.astype(jnp.int32) + blk - 1) // blk
    seg_end = jnp.cumsum(seg_blocks)
    seg_first = seg_end - seg_blocks
    return seg_first * blk, seg_first, seg_blocks, seg_end[-1:]


def _combine_kernel(h_ref, x1_ref, routed_ref, mod_ref, wsg_ref, wsu_ref, wsd_ref, lng_ref, lnb_ref, o_ref):
    lo, hi = _unpack_bf16_pairs(h_ref[...])
    a = _swiglu(lo.astype(BF16), hi.astype(BF16), wsg_ref, wsu_ref)
    ffn = jnp.dot(a, wsd_ref[...], preferred_element_type=F32) + routed_ref[...]
    g2 = mod_ref[0, 5:6, :]
    o_ref[...] = _layer_norm(DEEPNORM_ALPHA * x1_ref[...] + g2 * ffn, lng_ref[...], lnb_ref[...])


def _combine_into_kernel(h_ref, x1_ref, routed_ref, mod_ref, wsg_ref, wsu_ref, wsd_ref, lng_ref, lnb_ref,
                         prev_ref, o_ref):
    del prev_ref
    _combine_kernel(h_ref, x1_ref, routed_ref, mod_ref, wsg_ref, wsu_ref, wsd_ref, lng_ref, lnb_ref, o_ref)


def _combine(h2p, x1, routed, mod, wsg, wsu, wsd, lng, lnb, seq, b_off, in_off, out_rows, out_off, out_prev):
    n, D = routed.shape
    tm = TM_COMBINE
    per_seq = seq // tm
    in_blk = in_off // tm
    blk_off = out_off // tm
    const = lambda i: (0, 0)
    in_specs = [
        pl.BlockSpec((tm, HALF), lambda i: (i + in_blk, 0)),
        pl.BlockSpec((tm, D), lambda i: (i + in_blk, 0)),
        pl.BlockSpec((tm, D), lambda i: (i, 0)),
        pl.BlockSpec((1, 6, D), lambda i: ((i + in_blk) // per_seq + b_off, 0, 0)),
        pl.BlockSpec(wsg.shape, const),
        pl.BlockSpec(wsu.shape, const),
        pl.BlockSpec(wsd.shape, const),
        pl.BlockSpec(lng.shape, const),
        pl.BlockSpec(lnb.shape, const),
    ]
    args = [h2p, x1, routed, mod, wsg, wsu, wsd, lng, lnb]
    body = _combine_kernel
    aliases = {}
    if out_prev is not None:
        in_specs.append(pl.BlockSpec(memory_space=pl.ANY))
        args.append(out_prev)
        aliases = {len(args) - 1: 0}
        body = _combine_into_kernel
    return pl.pallas_call(
        body,
        grid=(n // tm,),
        in_specs=in_specs,
        out_specs=pl.BlockSpec((tm, D), lambda i: (i + blk_off, 0)),
        out_shape=jax.ShapeDtypeStruct((out_rows, D), F32),
        input_output_aliases=aliases,
        compiler_params=_cparams(("arbitrary",)),
        name="combine",
    )(*args)


def _rope_pair(w):
    half = QK_ROPE // 2
    return w, jnp.concatenate([-w[:, half:], w[:, :half]], axis=1)


def _prep_layer(w_in, w_uq, w_ukv, w_pool, w_out, w_router):
    lat = POOL_WIDTH + Q_LORA + KV_LORA
    k_a, k_b = _rope_pair(w_in[:, lat:lat + QK_ROPE])
    w_in_ext = jnp.concatenate([w_in[:, :lat], k_a, k_b], axis=1).astype(BF16)

    qa_cols = []
    per_head = QK_NOPE + QK_ROPE
    for hd in range(N_HEADS):
        w_h = w_uq[:, hd * per_head:(hd + 1) * per_head]
        r_a, r_b = _rope_pair(w_h[:, QK_NOPE:])
        qa_cols += [w_h[:, :QK_NOPE], r_a, r_b]
    w_uqa = jnp.concatenate(qa_cols, axis=1).astype(BF16)

    wpool_bd = jax.scipy.linalg.block_diag(*[w_pool[g] for g in range(len(POOL_WINDOWS))]).astype(BF16)
    wr_hi, wr_lo = _split_hi_lo(w_router.T)
    return dict(w_in_ext=w_in_ext, w_uqa=w_uqa, w_ukv=w_ukv.astype(BF16), wpool_bd=wpool_bd,
                wout_p=w_out[:POOL_WIDTH].astype(BF16), wout_a=w_out[POOL_WIDTH:].astype(BF16),
                wr_hi=wr_hi, wr_lo=wr_lo)


def kernel(x, c, positions, w_ada, b_ada, w_in, q_norm_g, kv_norm_g, w_uq, w_ukv, w_pool, pool_scale,
           w_out, ln1_g, ln1_b, w_router, router_bias, w_exp_gate, w_exp_up, w_exp_down,
           w_sh_gate, w_sh_up, w_sh_down, ln2_g, ln2_b):
    B, S, D = x.shape
    L = w_in.shape[0]
    row = lambda v: v.reshape(1, -1)

    mod_all = _ada_mod(c, w_ada, b_ada).reshape(L, B, 6, D)
    cs_all = _rope_table(positions)
    bs = B // N_STREAMS
    T = bs * S
    n_blocks = T * TOP_K // DISPATCH_BLOCK + N_EXPERTS
    xs_streams = [x] * N_STREAMS
    result = None
    for l in range(L):
        p = _prep_layer(w_in[l], w_uq[l], w_ukv[l], w_pool[l], w_out[l], w_router[l])
        shared = (w_sh_gate[l].astype(BF16), w_sh_up[l].astype(BF16), w_sh_down[l].astype(BF16))
        mod = mod_all[l]
        for s in range(N_STREAMS):
            xl = xs_streams[s]
            b_off = s * bs
            x_off = b_off if l == 0 else 0
            u, q, k, v = _in_proj(xl, mod, cs_all, p["w_in_ext"], row(q_norm_g[l]), row(kv_norm_g[l]),
                                  p["w_uqa"], p["w_ukv"], bs, x_off, b_off)
            attn = _attention(q, k, v)
            x1, h2p, lt = _post_attn(xl, u, attn, mod, p["wpool_bd"], row(pool_scale[l]), p["wout_p"],
                                     p["wout_a"], row(ln1_g[l]), row(ln1_b[l]), p["wr_hi"], p["wr_lo"],
                                     x_off, b_off)
            idx_k, gate_k = _route(lt.reshape(N_EXPERTS, T // LANES, LANES), router_bias[l])
            idx_kt = idx_k.reshape(TOP_K, T)
            gates_kt = gate_k.reshape(TOP_K, T)
            rank_kt, counts = _rank(idx_kt)
            pad_start, seg_first, seg_blocks, n_real = _block_plan(counts[:, 0])
            start = jnp.broadcast_to(pad_start.astype(F32)[:, None], (N_EXPERTS, LANES))
            dest_kt = _dest(idx_kt, rank_kt, start)
            h2p = h2p.reshape(T, HALF)
            rows = _sc_dispatch(h2p, dest_kt, n_blocks * DISPATCH_BLOCK)
            ys = _expert_mm(seg_first, seg_blocks, n_real, rows, w_exp_gate, w_exp_up, w_exp_down, l)
            tc = T // COMBINE_CHUNKS
            out = None
            for ch in range(COMBINE_CHUNKS):
                lo = ch * tc
                routed = _sc_gather_sum(ys, dest_kt[:, lo:lo + tc], gates_kt[:, lo:lo + tc])
                if l < L - 1:
                    out = _combine(h2p, x1.reshape(T, D), routed, mod, *shared, row(ln2_g[l]), row(ln2_b[l]),
                                   S, b_off, lo, T, lo, out)
                else:
                    result = _combine(h2p, x1.reshape(T, D), routed, mod, *shared, row(ln2_g[l]),
                                      row(ln2_b[l]), S, b_off, lo, B * S, s * T + lo, result)
            if l < L - 1:
                xs_streams[s] = out.reshape(bs, S, D)
    return result.reshape(B, S, D)
```

```python
import functools

import jax
import jax.numpy as jnp
from jax import lax
from jax.experimental import pallas as pl
from jax.experimental.pallas import tpu as pltpu
from jax.experimental.pallas import tpu_sc as plsc

F32 = jnp.float32
BF16 = jnp.bfloat16

D_MODEL = 1024
DEPTH = 4
POOL_WIDTH = 256
POOL_WINDOWS = (2, 4, 8, 16)
POOL_GROUP = 64
POOL_HALO = 16
QK_NOPE = 128
QK_ROPE = 64
V_HEAD = 128
N_HEADS = 6
Q_LORA = 384
KV_LORA = 256
ROPE_THETA = 10000.0
N_EXPERTS = 64
TOP_K = 8
N_GROUPS = 8
GROUP_SIZE = N_EXPERTS // N_GROUPS
TOPK_GROUPS = 4
D_EXPERT = 256
ROUTED_SCALE = 2.5
DEEPNORM_ALPHA = (2.0 * DEPTH) ** 0.25
LN_EPS = 1e-5
RMS_EPS = 1e-6
ATTN_SCALE = (QK_NOPE + QK_ROPE) ** -0.5
LOG2_E = 1.4426950408889634
Q_SCALE = ATTN_SCALE * LOG2_E
LANES = 128
QK_PAD = 2 * LANES
MASK_VALUE = -1e30

TM_PROJ = 512
TM_POST = 1024
TQ = 512
TK = 512
ROUTE_ROWS = 8
DISPATCH_BLOCK = 512
EXPERT_RING = 6
TM_COMBINE = 1024
SC_CORES = 2
SC_SUBCORES = 16
SC_WINDOW = 128
SC_LANES = 16
HIGH_HALF_MASK = -65536
SC_SUM_TOKENS = 16
N_STREAMS = 2
COMBINE_CHUNKS = 2
HALF = D_MODEL // 2
VMEM_LIMIT = 48 * 1024 * 1024


def _cparams(sem):
    return pltpu.CompilerParams(dimension_semantics=sem, vmem_limit_bytes=VMEM_LIMIT)


def _split_hi_lo(a):
    hi = a.astype(BF16)
    lo = (a - hi.astype(F32)).astype(BF16)
    return hi, lo


def _pack_bf16_pairs(a):
    bits = lax.bitcast_convert_type(a.astype(F32), jnp.uint32)
    half = a.shape[1] // 2
    word = (bits[:, :half] >> 16) | (bits[:, half:] & jnp.uint32(0xFFFF0000))
    return lax.bitcast_convert_type(word, jnp.int32)


def _unpack_bf16_pairs(w):
    bits = lax.bitcast_convert_type(w, jnp.uint32)
    lo = lax.bitcast_convert_type(bits << 16, F32)
    hi = lax.bitcast_convert_type(bits & jnp.uint32(0xFFFF0000), F32)
    return lo, hi


def _ada_kernel(c_ref, w_ref, b_ref, o_ref):
    c = c_ref[...]
    cond = c * jax.nn.sigmoid(c)
    c_hi, c_lo = _split_hi_lo(cond)
    w_hi, w_lo = _split_hi_lo(w_ref[0])
    acc = jnp.dot(c_hi, w_hi, preferred_element_type=F32)
    acc += jnp.dot(c_lo, w_hi, preferred_element_type=F32)
    acc += jnp.dot(c_hi, w_lo, preferred_element_type=F32)
    o_ref[0] = acc + b_ref[0]


def _ada_mod(c, w_ada, b_ada):
    L, D, N = w_ada.shape
    B = c.shape[0]
    tn = 1536
    return pl.pallas_call(
        _ada_kernel,
        grid=(L, N // tn),
        in_specs=[
            pl.BlockSpec((B, D), lambda l, j: (0, 0)),
            pl.BlockSpec((1, D, tn), lambda l, j: (l, 0, j)),
            pl.BlockSpec((1, 1, tn), lambda l, j: (l, 0, j)),
        ],
        out_specs=pl.BlockSpec((1, B, tn), lambda l, j: (l, 0, j)),
        out_shape=jax.ShapeDtypeStruct((L, B, N), F32),
        compiler_params=_cparams(("arbitrary", "arbitrary")),
        name="ada_mod",
    )(c, w_ada, b_ada.reshape(L, 1, N))


def _rope_table_kernel(pos_ref, freq_ref, o_ref):
    ang = pos_ref[0].astype(F32) * freq_ref[...]
    lane = lax.broadcasted_iota(jnp.int32, (1, LANES), 1)
    o_ref[0] = jnp.where(lane < QK_ROPE, jnp.cos(ang), jnp.sin(ang))


def _rope_table(positions):
    B, S = positions.shape
    half = QK_ROPE // 2
    inv_freq = ROPE_THETA ** (-jnp.arange(half, dtype=F32) / half)
    freq = jnp.tile(inv_freq, LANES // half).reshape(1, LANES)
    return pl.pallas_call(
        _rope_table_kernel,
        grid=(B,),
        in_specs=[
            pl.BlockSpec((1, S, 1), lambda b: (b, 0, 0)),
            pl.BlockSpec((1, LANES), lambda b: (0, 0)),
        ],
        out_specs=pl.BlockSpec((1, S, LANES), lambda b: (b, 0, 0)),
        out_shape=jax.ShapeDtypeStruct((B, S, LANES), F32),
        compiler_params=_cparams(("arbitrary",)),
        name="rope_table",
    )(positions.reshape(B, S, 1), freq)


def _rms(x, g):
    return x * lax.rsqrt(jnp.mean(x * x, axis=-1, keepdims=True) + RMS_EPS) * g


def _in_proj_kernel(x_ref, mod_ref, cs_ref, w_in_ref, qg_ref, kvg_ref, w_uqa_ref, w_ukv_ref,
                    u_ref, q_ref, k_ref, v_ref):
    x = x_ref[0]
    sh1 = mod_ref[0, 0:1, :]
    sc1 = mod_ref[0, 1:2, :]
    h = (x * (1.0 + sc1) + sh1).astype(BF16)
    proj = jnp.dot(h, w_in_ref[...], preferred_element_type=F32)
    u_ref[0] = proj[:, 0:POOL_WIDTH]
    o = POOL_WIDTH
    q_lat = proj[:, o:o + Q_LORA]
    o += Q_LORA
    kv_lat = proj[:, o:o + KV_LORA]
    o += KV_LORA
    cos_sin = cs_ref[0]

    def rotate(pair):
        t = pair * cos_sin
        return t + pltpu.roll(t, QK_ROPE, 1)

    k_rot = rotate(proj[:, o:o + LANES]).astype(BF16)
    lane = lax.broadcasted_iota(jnp.int32, (1, LANES), 1)
    q_keep = jnp.where(lane < QK_ROPE, Q_SCALE, 0.0)

    qn = _rms(q_lat, qg_ref[...]).astype(BF16)
    q_a = jnp.dot(qn, w_uqa_ref[...], preferred_element_type=F32)
    kvn = _rms(kv_lat, kvg_ref[...]).astype(BF16)
    kv = jnp.dot(kvn, w_ukv_ref[...], preferred_element_type=F32)
    for hd in range(N_HEADS):
        b0 = hd * QK_PAD
        q_ref[0, hd, :, 0:LANES] = (q_a[:, b0:b0 + LANES] * Q_SCALE).astype(BF16)
        q_ref[0, hd, :, LANES:QK_PAD] = (rotate(q_a[:, b0 + LANES:b0 + QK_PAD]) * q_keep).astype(BF16)
        k_ref[0, hd, :, 0:LANES] = kv[:, b0:b0 + QK_NOPE].astype(BF16)
        k_ref[0, hd, :, LANES:QK_PAD] = k_rot
        v_ref[0, hd] = kv[:, b0 + QK_NOPE:b0 + QK_NOPE + V_HEAD].astype(BF16)


def _in_proj(x, mod, cs, w_in_ext, qg, kvg, w_uqa, w_ukv_bf, B, x_off, b_off):
    _, S, D = x.shape
    tm = TM_PROJ
    const = lambda b, i: (0, 0)
    return pl.pallas_call(
        _in_proj_kernel,
        grid=(B, S // tm),
        in_specs=[
            pl.BlockSpec((1, tm, D), lambda b, i: (b + x_off, i, 0)),
            pl.BlockSpec((1, 6, D), lambda b, i: (b + b_off, 0, 0)),
            pl.BlockSpec((1, tm, LANES), lambda b, i: (b + b_off, i, 0)),
            pl.BlockSpec(w_in_ext.shape, const),
            pl.BlockSpec(qg.shape, const),
            pl.BlockSpec(kvg.shape, const),
            pl.BlockSpec(w_uqa.shape, const),
            pl.BlockSpec(w_ukv_bf.shape, const),
        ],
        out_specs=[
            pl.BlockSpec((1, tm, POOL_WIDTH), lambda b, i: (b, i, 0)),
            pl.BlockSpec((1, N_HEADS, tm, QK_PAD), lambda b, i: (b, 0, i, 0)),
            pl.BlockSpec((1, N_HEADS, tm, QK_PAD), lambda b, i: (b, 0, i, 0)),
            pl.BlockSpec((1, N_HEADS, tm, V_HEAD), lambda b, i: (b, 0, i, 0)),
        ],
        out_shape=[
            jax.ShapeDtypeStruct((B, S, POOL_WIDTH), F32),
            jax.ShapeDtypeStruct((B, N_HEADS, S, QK_PAD), BF16),
            jax.ShapeDtypeStruct((B, N_HEADS, S, QK_PAD), BF16),
            jax.ShapeDtypeStruct((B, N_HEADS, S, V_HEAD), BF16),
        ],
        compiler_params=_cparams(("arbitrary", "arbitrary")),
        name="in_proj",
    )(x, mod, cs, w_in_ext, qg, kvg, w_uqa, w_ukv_bf)


def _softmax_step(q, k, v, carry, mask):
    m, l, acc = carry
    s = lax.dot_general(q, k, (((1,), (1,)), ((), ())), preferred_element_type=F32)
    if mask is not None:
        s = jnp.where(mask, s, MASK_VALUE)
    m_new = jnp.maximum(m, jnp.max(s, axis=-1, keepdims=True))
    alpha = jnp.exp2(m - m_new)
    p = jnp.exp2(s - m_new)
    l_new = alpha * l + jnp.sum(p, axis=-1, keepdims=True)
    acc_new = alpha * acc + jnp.dot(p.astype(BF16), v, preferred_element_type=F32)
    return m_new, l_new, acc_new


def _attn_kernel(q_ref, k_ref, v_ref, o_ref):
    seq = q_ref.shape[2]
    row = lax.broadcasted_iota(jnp.int32, (TQ, TK), 0)
    col = lax.broadcasted_iota(jnp.int32, (TQ, TK), 1)
    diag = row >= col
    for i in range(seq // TQ):
        q = q_ref[0, 0, i * TQ:(i + 1) * TQ, :]
        carry = (jnp.full((TQ, 1), MASK_VALUE, F32), jnp.zeros((TQ, 1), F32), jnp.zeros((TQ, V_HEAD), F32))
        for j in range(i + 1):
            k = k_ref[0, 0, j * TK:(j + 1) * TK, :]
            v = v_ref[0, 0, j * TK:(j + 1) * TK, :]
            carry = _softmax_step(q, k, v, carry, diag if j == i else None)
        _, l, acc = carry
        o_ref[0, 0, i * TQ:(i + 1) * TQ, :] = (acc / l).astype(BF16)


def _attention(q, k, v):
    B, H, S, _ = q.shape
    return pl.pallas_call(
        _attn_kernel,
        grid=(B, H),
        in_specs=[
            pl.BlockSpec((1, 1, S, QK_PAD), lambda b, h: (b, h, 0, 0)),
            pl.BlockSpec((1, 1, S, QK_PAD), lambda b, h: (b, h, 0, 0)),
            pl.BlockSpec((1, 1, S, V_HEAD), lambda b, h: (b, h, 0, 0)),
        ],
        out_specs=pl.BlockSpec((1, 1, S, V_HEAD), lambda b, h: (b, h, 0, 0)),
        out_shape=jax.ShapeDtypeStruct((B, H, S, V_HEAD), BF16),
        compiler_params=_cparams(("arbitrary", "arbitrary")),
        name="attention",
    )(q, k, v)


def _layer_norm(v, g, b):
    mu = jnp.mean(v, axis=-1, keepdims=True)
    d = v - mu
    var = jnp.mean(d * d, axis=-1, keepdims=True)
    return d * lax.rsqrt(var + LN_EPS) * g + b


def _post_kernel(x_ref, u_ref, halo_ref, attn_ref, mod_ref, wpool_ref, pscale_ref, wout_p_ref, wout_a_ref,
                 lng_ref, lnb_ref, wr_hi_ref, wr_lo_ref, x1_ref, h2_ref, lt_ref):
    i = pl.program_id(1)
    tm = u_ref.shape[1]
    u = u_ref[0]
    halo = jnp.where(i > 0, halo_ref[0], 0.0)
    ext = jnp.concatenate([halo, u], axis=0)
    s2 = ext + pltpu.roll(ext, 1, 0)
    s4 = s2 + pltpu.roll(s2, 2, 0)
    s8 = s4 + pltpu.roll(s4, 4, 0)
    s16 = s8 + pltpu.roll(s8, 8, 0)
    lane = lax.broadcasted_iota(jnp.int32, (1, POOL_WIDTH), 1)
    win = jnp.where(lane < POOL_GROUP, s2,
                    jnp.where(lane < 2 * POOL_GROUP, s4, jnp.where(lane < 3 * POOL_GROUP, s8, s16)))
    win = win[POOL_HALO:, :]
    width = jnp.where(lane < POOL_GROUP, POOL_WINDOWS[0],
                      jnp.where(lane < 2 * POOL_GROUP, POOL_WINDOWS[1],
                                jnp.where(lane < 3 * POOL_GROUP, POOL_WINDOWS[2], POOL_WINDOWS[3])))
    t = i * tm + lax.broadcasted_iota(jnp.int32, (tm, 1), 0)
    count = jnp.minimum(t + 1, width).astype(F32)
    token_mix = win / count - u
    pooled = jnp.dot(token_mix.astype(BF16), wpool_ref[...], preferred_element_type=F32) * pscale_ref[...]
    mixed = jnp.dot(pooled.astype(BF16), wout_p_ref[...], preferred_element_type=F32)
    attn = jnp.concatenate([attn_ref[0, hd] for hd in range(N_HEADS)], axis=1)
    mixed += jnp.dot(attn, wout_a_ref[...], preferred_element_type=F32)

    g1 = mod_ref[0, 2:3, :]
    sh2 = mod_ref[0, 3:4, :]
    sc2 = mod_ref[0, 4:5, :]
    x1 = _layer_norm(DEEPNORM_ALPHA * x_ref[0] + g1 * mixed, lng_ref[...], lnb_ref[...])
    x1_ref[0] = x1
    h2 = x1 * (1.0 + sc2) + sh2
    h_hi, h_lo = _split_hi_lo(h2)
    h2_ref[0] = _pack_bf16_pairs(h_hi)
    nt = (((1,), (1,)), ((), ()))
    lt = lax.dot_general(wr_hi_ref[...], h_hi, nt, preferred_element_type=F32)
    lt += lax.dot_general(wr_hi_ref[...], h_lo, nt, preferred_element_type=F32)
    lt += lax.dot_general(wr_lo_ref[...], h_hi, nt, preferred_element_type=F32)
    lt_ref[...] = lt


def _post_attn(x, u, attn, mod, wpool_bd, pscale, wout_p, wout_a, lng, lnb, wr_hi, wr_lo, x_off, b_off):
    B, S, _ = u.shape
    D = x.shape[2]
    tm = TM_POST
    nt = S // tm
    const = lambda b, i: (0, 0)
    halo_blocks = tm // POOL_HALO
    return pl.pallas_call(
        _post_kernel,
        grid=(B, nt),
        in_specs=[
            pl.BlockSpec((1, tm, D), lambda b, i: (b + x_off, i, 0)),
            pl.BlockSpec((1, tm, POOL_WIDTH), lambda b, i: (b, i, 0)),
            pl.BlockSpec((1, POOL_HALO, POOL_WIDTH), lambda b, i: (b, jnp.maximum(i * halo_blocks - 1, 0), 0)),
            pl.BlockSpec((1, N_HEADS, tm, V_HEAD), lambda b, i: (b, 0, i, 0)),
            pl.BlockSpec((1, 6, D), lambda b, i: (b + b_off, 0, 0)),
            pl.BlockSpec(wpool_bd.shape, const),
            pl.BlockSpec(pscale.shape, const),
            pl.BlockSpec(wout_p.shape, const),
            pl.BlockSpec(wout_a.shape, const),
            pl.BlockSpec(lng.shape, const),
            pl.BlockSpec(lnb.shape, const),
            pl.BlockSpec(wr_hi.shape, const),
            pl.BlockSpec(wr_lo.shape, const),
        ],
        out_specs=[
            pl.BlockSpec((1, tm, D), lambda b, i: (b, i, 0)),
            pl.BlockSpec((1, tm, HALF), lambda b, i: (b, i, 0)),
            pl.BlockSpec((N_EXPERTS, tm), lambda b, i: (0, b * nt + i)),
        ],
        out_shape=[
            jax.ShapeDtypeStruct((B, S, D), F32),
            jax.ShapeDtypeStruct((B, S, HALF), jnp.int32),
            jax.ShapeDtypeStruct((N_EXPERTS, B * S), F32),
        ],
        compiler_params=_cparams(("arbitrary", "arbitrary")),
        name="post_attn",
    )(x, u, u, attn, mod, wpool_bd, pscale, wout_p, wout_a, lng, lnb, wr_hi, wr_lo)


def _select_first_max(vals, n_rounds, payload=None):
    work = list(vals)
    sel = [None] * len(vals)
    rounds = []
    for _ in range(n_rounds):
        m = functools.reduce(jnp.maximum, work)
        taken = None
        win_idx = jnp.zeros(m.shape, jnp.int32)
        win_val = jnp.zeros(m.shape, F32)
        for e in range(len(work)):
            hit = work[e] == m
            first = hit if taken is None else jnp.logical_and(hit, jnp.logical_not(taken))
            taken = hit if taken is None else jnp.logical_or(taken, hit)
            sel[e] = first if sel[e] is None else jnp.logical_or(sel[e], first)
            work[e] = jnp.where(first, -jnp.inf, work[e])
            if payload is not None:
                win_idx = jnp.where(first, e, win_idx)
                win_val = jnp.where(first, payload[e], win_val)
        rounds.append((win_idx, win_val))
    return sel, rounds


def _route_kernel(bias_ref, lt_ref, idx_ref, gate_ref):
    scores = [jax.nn.sigmoid(lt_ref[e]) for e in range(N_EXPERTS)]
    choice = [scores[e] + bias_ref[e] for e in range(N_EXPERTS)]
    group_score = []
    for g in range(N_GROUPS):
        vals = choice[g * GROUP_SIZE:(g + 1) * GROUP_SIZE]
        sel2, _ = _select_first_max(vals, 2)
        group_score.append(functools.reduce(
            jnp.add, [jnp.where(sel2[j], vals[j], 0.0) for j in range(GROUP_SIZE)]))
    group_sel, _ = _select_first_max(group_score, TOPK_GROUPS)
    masked = [jnp.where(group_sel[e // GROUP_SIZE], choice[e], -jnp.inf) for e in range(N_EXPERTS)]
    _, rounds = _select_first_max(masked, TOP_K, payload=scores)
    total = functools.reduce(jnp.add, [w for _, w in rounds])
    for k, (e_k, w_k) in enumerate(rounds):
        idx_ref[k] = e_k
        gate_ref[k] = w_k / total * ROUTED_SCALE


def _route(lt3, bias):
    E, R, _ = lt3.shape
    out_spec = pl.BlockSpec((TOP_K, ROUTE_ROWS, LANES), lambda r: (0, r, 0))
    return pl.pallas_call(
        _route_kernel,
        grid=(R // ROUTE_ROWS,),
        in_specs=[
            pl.BlockSpec(memory_space=pltpu.SMEM),
            pl.BlockSpec((E, ROUTE_ROWS, LANES), lambda r: (0, r, 0)),
        ],
        out_specs=[out_spec, out_spec],
        out_shape=[jax.ShapeDtypeStruct((TOP_K, R, LANES), jnp.int32),
                   jax.ShapeDtypeStruct((TOP_K, R, LANES), F32)],
        compiler_params=_cparams(("arbitrary",)),
        name="route",
    )(bias, lt3)


def _sc_mesh():
    return plsc.VectorSubcoreMesh(core_axis_name="c", subcore_axis_name="s",
                                  num_cores=SC_CORES, num_subcores=SC_SUBCORES)


def _sc_worker_id():
    return lax.axis_index("s") * SC_CORES + lax.axis_index("c")


def _sc_gather(table, idx):
    n = idx.shape[0]
    width = table.shape[1]
    workers = SC_CORES * SC_SUBCORES
    per_worker = n // workers
    n_win = per_worker // SC_WINDOW
    assert n == workers * n_win * SC_WINDOW

    @functools.partial(
        pl.kernel, mesh=_sc_mesh(),
        out_type=jax.ShapeDtypeStruct((n, width), table.dtype),
        scratch_types=[
            pltpu.VMEM((n_win, SC_WINDOW), jnp.int32),
            pltpu.VMEM((SC_WINDOW, width), table.dtype),
            pltpu.SemaphoreType.DMA,
        ],
    )
    def gather(table_hbm, idx_hbm, out_hbm, idx_v, rows_v, sem):
        wid = _sc_worker_id()
        pltpu.sync_copy(idx_hbm.at[wid], idx_v)

        @pl.loop(0, n_win)
        def _(j):
            pltpu.async_copy(table_hbm.at[idx_v.at[j]], rows_v, sem).wait()
            pltpu.sync_copy(rows_v, out_hbm.at[pl.ds(wid * per_worker + j * SC_WINDOW, SC_WINDOW)])

    return gather(table, idx.reshape(workers, n_win, SC_WINDOW))


def _sc_gather_sum(ys, dest_kt, gates_kt):
    K, T = dest_kt.shape
    half = ys.shape[1]
    workers = SC_CORES * SC_SUBCORES
    per_worker = T // workers
    wt = SC_SUM_TOKENS
    n_win = per_worker // wt
    n_idx = K * wt
    assert T == workers * n_win * wt and n_idx <= 128 and half % SC_LANES == 0
    dest_w = dest_kt.reshape(K, workers, n_win, wt).transpose(1, 2, 0, 3).reshape(workers, n_win, n_idx)
    gates_w = gates_kt.reshape(K, workers, n_win, wt).transpose(1, 2, 0, 3).reshape(workers * n_win * n_idx)

    @functools.partial(
        pl.kernel, mesh=_sc_mesh(),
        out_type=jax.ShapeDtypeStruct((T, 2 * half), F32),
        scratch_types=[
            pltpu.VMEM((n_win, n_idx), jnp.int32),
            pltpu.VMEM((n_idx, half), jnp.int32),
            pltpu.VMEM((n_idx,), F32),
            pltpu.VMEM((wt, 2 * half), F32),
            pltpu.SemaphoreType.DMA,
        ],
        compiler_params=pltpu.CompilerParams(needs_layout_passes=False),
    )
    def gather_sum(ys_hbm, dest_hbm, gates_hbm, out_hbm, idx_v, rows_v, gate_v, out_v, sem):
        wid = _sc_worker_id()
        pltpu.sync_copy(dest_hbm.at[wid], idx_v)

        @pl.loop(0, n_win)
        def _(j):
            pltpu.async_copy(ys_hbm.at[idx_v.at[j]], rows_v, sem).wait()
            pltpu.sync_copy(gates_hbm.at[pl.ds((wid * n_win + j) * n_idx, n_idx)], gate_v)

            @pl.loop(0, wt)
            def _(t):
                g = [plsc.load_gather(gate_v, [jnp.full((SC_LANES,), k * wt + t, jnp.int32)]) for k in range(K)]

                @pl.loop(0, half // SC_LANES)
                def _(c):
                    col = pl.ds(c * SC_LANES, SC_LANES)
                    acc_lo = jnp.zeros((SC_LANES,), F32)
                    acc_hi = jnp.zeros((SC_LANES,), F32)
                    for k in range(K):
                        w = rows_v[k * wt + t, col]
                        acc_lo = acc_lo + g[k] * plsc.bitcast(w << 16, F32)
                        acc_hi = acc_hi + g[k] * plsc.bitcast(w & HIGH_HALF_MASK, F32)
                    out_v[t, col] = acc_lo
                    out_v[t, pl.ds(half + c * SC_LANES, SC_LANES)] = acc_hi

            pltpu.sync_copy(out_v, out_hbm.at[pl.ds(wid * per_worker + j * wt, wt)])

    return gather_sum(ys, dest_w, gates_w)


def _sc_dispatch(rows, dest_kt, n_slots):
    T, width = rows.shape
    K = dest_kt.shape[0]
    workers = SC_CORES * SC_SUBCORES
    per_worker = T // workers
    n_win = per_worker // SC_WINDOW
    assert T == workers * n_win * SC_WINDOW
    dest_w = dest_kt.reshape(K, workers, n_win, SC_WINDOW).transpose(1, 2, 0, 3)
    dest_w = dest_w.reshape(workers, n_win * K, SC_WINDOW)

    @functools.partial(
        pl.kernel, mesh=_sc_mesh(),
        out_type=jax.ShapeDtypeStruct((n_slots, width), rows.dtype),
        scratch_types=[
            pltpu.VMEM((n_win * K, SC_WINDOW), jnp.int32),
            pltpu.VMEM((SC_WINDOW, width), rows.dtype),
        ],
    )
    def dispatch(rows_hbm, dest_hbm, out_hbm, idx_v, rows_v):
        wid = _sc_worker_id()
        pltpu.sync_copy(dest_hbm.at[wid], idx_v)

        @pl.loop(0, n_win)
        def _(j):
            pltpu.sync_copy(rows_hbm.at[pl.ds(wid * per_worker + j * SC_WINDOW, SC_WINDOW)], rows_v)
            for k in range(K):
                pltpu.sync_copy(rows_v, out_hbm.at[idx_v.at[j * K + k]])

    return dispatch(rows, dest_w)


def _swiglu(h_lo, h_hi, wg_ref, wu_ref):
    def proj(w_ref):
        return (jnp.dot(h_lo, w_ref[:HALF, :], preferred_element_type=F32)
                + jnp.dot(h_hi, w_ref[HALF:, :], preferred_element_type=F32))
    gate = proj(wg_ref)
    up = proj(wu_ref)
    return (gate * jax.nn.sigmoid(gate) * up).astype(BF16)


def _expert_kernel(seg_first_ref, seg_blocks_ref, n_real_ref, xs_hbm, wg_ref, wu_ref, wd_ref, ys_hbm,
                   xbuf, ybuf, wg_bf, wu_bf, wd_bf, in_sems, out_sems):
    e = pl.program_id(0)
    first = seg_first_ref[e]
    count = seg_blocks_ref[e]
    total = n_real_ref[0]
    blk = DISPATCH_BLOCK
    ring = xbuf.shape[0]

    def in_copy(g):
        slot = g % ring
        return pltpu.make_async_copy(xs_hbm.at[pl.ds(g * blk, blk)], xbuf.at[slot], in_sems.at[slot])

    def out_copy(g):
        slot = g % ring
        return pltpu.make_async_copy(ybuf.at[slot], ys_hbm.at[pl.ds(g * blk, blk)], out_sems.at[slot])

    @pl.when(e == 0)
    def _():
        for g in range(ring - 1):
            @pl.when(g < total)
            def _():
                in_copy(g).start()

    @pl.when(count > 0)
    def _():
        wg_bf[...] = wg_ref[0, 0].astype(BF16)
        wu_bf[...] = wu_ref[0, 0].astype(BF16)
        wd_bf[...] = wd_ref[0, 0].astype(BF16)

    def block(g, _):
        slot = g % ring
        in_copy(g).wait()

        @pl.when(g + ring - 1 < total)
        def _():
            in_copy(g + ring - 1).start()

        @pl.when(g >= ring)
        def _():
            out_copy(g - ring).wait()

        lo, hi = _unpack_bf16_pairs(xbuf[slot])
        a = _swiglu(lo.astype(BF16), hi.astype(BF16), wg_bf, wu_bf)
        y = jnp.dot(a, wd_bf[...], preferred_element_type=F32)
        ybuf[slot] = _pack_bf16_pairs(y.astype(BF16))
        out_copy(g).start()
        return 0

    lax.fori_loop(first, first + count, block, 0)

    @pl.when(e == pl.num_programs(0) - 1)
    def _():
        for back in range(ring, 0, -1):
            @pl.when(total >= back)
            def _():
                out_copy(total - back).wait()


def _expert_mm(seg_first, seg_blocks, n_real, xs, wg, wu, wd, layer):
    P = xs.shape[0]
    D = wg.shape[2]
    w_map = lambda e, sf, sb, nr: (layer, e, 0, 0)
    grid_spec = pltpu.PrefetchScalarGridSpec(
        num_scalar_prefetch=3,
        grid=(N_EXPERTS,),
        in_specs=[
            pl.BlockSpec(memory_space=pl.ANY),
            pl.BlockSpec((1, 1, D, D_EXPERT), w_map),
            pl.BlockSpec((1, 1, D, D_EXPERT), w_map),
            pl.BlockSpec((1, 1, D_EXPERT, D), w_map),
        ],
        out_specs=pl.BlockSpec(memory_space=pl.ANY),
        scratch_shapes=[
            pltpu.VMEM((EXPERT_RING, DISPATCH_BLOCK, HALF), jnp.int32),
            pltpu.VMEM((EXPERT_RING, DISPATCH_BLOCK, HALF), jnp.int32),
            pltpu.VMEM((D, D_EXPERT), BF16), pltpu.VMEM((D, D_EXPERT), BF16), pltpu.VMEM((D_EXPERT, D), BF16),
            pltpu.SemaphoreType.DMA((EXPERT_RING,)), pltpu.SemaphoreType.DMA((EXPERT_RING,)),
        ],
    )
    return pl.pallas_call(
        _expert_kernel,
        grid_spec=grid_spec,
        out_shape=jax.ShapeDtypeStruct((P, HALF), jnp.int32),
        compiler_params=_cparams(("arbitrary",)),
        name="expert_mm",
    )(seg_first, seg_blocks, n_real, xs, wg, wu, wd)


def _rank_kernel(idx_ref, rank_ref, cnt_ref, tri_ref, carry_ref):
    i = pl.program_id(0)
    tm = idx_ref.shape[1]

    @pl.when(i == 0)
    def _():
        r = lax.broadcasted_iota(jnp.int32, (tm, tm), 0)
        c = lax.broadcasted_iota(jnp.int32, (tm, tm), 1)
        tri_ref[...] = (r < c).astype(BF16)
        carry_ref[...] = jnp.zeros(carry_ref.shape, F32)

    e_iota = lax.broadcasted_iota(jnp.int32, (N_EXPERTS, tm), 0)
    hits = [e_iota == idx_ref[k:k + 1, :] for k in range(TOP_K)]
    mask = functools.reduce(jnp.logical_or, hits).astype(F32)
    rank = jnp.dot(mask.astype(BF16), tri_ref[...], preferred_element_type=F32) + carry_ref[:, 0:1]
    for k in range(TOP_K):
        rank_ref[k:k + 1, :] = jnp.sum(jnp.where(hits[k], rank, 0.0), axis=0, keepdims=True).astype(jnp.int32)
    carry_ref[...] = carry_ref[...] + jnp.sum(mask, axis=1, keepdims=True)
    cnt_ref[...] = carry_ref[...]


def _rank(idx_kt):
    K, T = idx_kt.shape
    tm = 1024
    return pl.pallas_call(
        _rank_kernel,
        grid=(T // tm,),
        in_specs=[pl.BlockSpec((K, tm), lambda i: (0, i))],
        out_specs=[pl.BlockSpec((K, tm), lambda i: (0, i)),
                   pl.BlockSpec((N_EXPERTS, LANES), lambda i: (0, 0))],
        out_shape=[jax.ShapeDtypeStruct((K, T), jnp.int32),
                   jax.ShapeDtypeStruct((N_EXPERTS, LANES), F32)],
        scratch_shapes=[pltpu.VMEM((tm, tm), BF16), pltpu.VMEM((N_EXPERTS, LANES), F32)],
        compiler_params=_cparams(("arbitrary",)),
        name="rank",
    )(idx_kt)


def _dest_kernel(idx_ref, rank_ref, start_ref, dest_ref):
    tm = idx_ref.shape[1]
    e_iota = lax.broadcasted_iota(jnp.int32, (N_EXPERTS, tm), 0)
    start = start_ref[:, 0:1]
    for k in range(TOP_K):
        base = jnp.sum(jnp.where(e_iota == idx_ref[k:k + 1, :], start, 0.0), axis=0, keepdims=True)
        dest_ref[k:k + 1, :] = rank_ref[k:k + 1, :] + base.astype(jnp.int32)


def _dest(idx_kt, rank_kt, start):
    K, T = idx_kt.shape
    tm = 1024
    tok_spec = pl.BlockSpec((K, tm), lambda i: (0, i))
    return pl.pallas_call(
        _dest_kernel,
        grid=(T // tm,),
        in_specs=[tok_spec, tok_spec, pl.BlockSpec((N_EXPERTS, LANES), lambda i: (0, 0))],
        out_specs=tok_spec,
        out_shape=jax.ShapeDtypeStruct((K, T), jnp.int32),
        compiler_params=_cparams(("arbitrary",)),
        name="dest",
    )(idx_kt, rank_kt, start)


def _block_plan(counts):
    blk = DISPATCH_BLOCK
    seg_blocks = (counts.astype(jnp.int32) + blk - 1) // blk
    seg_end = jnp.cumsum(seg_blocks)
    seg_first = seg_end - seg_blocks
    return seg_first * blk, seg_first, seg_blocks, seg_end[-1:]


def _combine_kernel(h_ref, x1_ref, routed_ref, mod_ref, wsg_ref, wsu_ref, wsd_ref, lng_ref, lnb_ref, o_ref):
    lo, hi = _unpack_bf16_pairs(h_ref[...])
    a = _swiglu(lo.astype(BF16), hi.astype(BF16), wsg_ref, wsu_ref)
    ffn = jnp.dot(a, wsd_ref[...], preferred_element_type=F32) + routed_ref[...]
    g2 = mod_ref[0, 5:6, :]
    o_ref[...] = _layer_norm(DEEPNORM_ALPHA * x1_ref[...] + g2 * ffn, lng_ref[...], lnb_ref[...])


def _combine_into_kernel(h_ref, x1_ref, routed_ref, mod_ref, wsg_ref, wsu_ref, wsd_ref, lng_ref, lnb_ref,
                         prev_ref, o_ref):
    del prev_ref
    _combine_kernel(h_ref, x1_ref, routed_ref, mod_ref, wsg_ref, wsu_ref, wsd_ref, lng_ref, lnb_ref, o_ref)


def _combine(h2p, x1, routed, mod, wsg, wsu, wsd, lng, lnb, seq, b_off, in_off, out_rows, out_off, out_prev):
    n, D = routed.shape
    tm = TM_COMBINE
    per_seq = seq // tm
    in_blk = in_off // tm
    blk_off = out_off // tm
    const = lambda i: (0, 0)
    in_specs = [
        pl.BlockSpec((tm, HALF), lambda i: (i + in_blk, 0)),
        pl.BlockSpec((tm, D), lambda i: (i + in_blk, 0)),
        pl.BlockSpec((tm, D), lambda i: (i, 0)),
        pl.BlockSpec((1, 6, D), lambda i: ((i + in_blk) // per_seq + b_off, 0, 0)),
        pl.BlockSpec(wsg.shape, const),
        pl.BlockSpec(wsu.shape, const),
        pl.BlockSpec(wsd.shape, const),
        pl.BlockSpec(lng.shape, const),
        pl.BlockSpec(lnb.shape, const),
    ]
    args = [h2p, x1, routed, mod, wsg, wsu, wsd, lng, lnb]
    body = _combine_kernel
    aliases = {}
    if out_prev is not None:
        in_specs.append(pl.BlockSpec(memory_space=pl.ANY))
        args.append(out_prev)
        aliases = {len(args) - 1: 0}
        body = _combine_into_kernel
    return pl.pallas_call(
        body,
        grid=(n // tm,),
        in_specs=in_specs,
        out_specs=pl.BlockSpec((tm, D), lambda i: (i + blk_off, 0)),
        out_shape=jax.ShapeDtypeStruct((out_rows, D), F32),
        input_output_aliases=aliases,
        compiler_params=_cparams(("arbitrary",)),
        name="combine",
    )(*args)


def _rope_pair(w):
    half = QK_ROPE // 2
    return w, jnp.concatenate([-w[:, half:], w[:, :half]], axis=1)


def _prep_layer(w_in, w_uq, w_ukv, w_pool, w_out, w_router):
    lat = POOL_WIDTH + Q_LORA + KV_LORA
    k_a, k_b = _rope_pair(w_in[:, lat:lat + QK_ROPE])
    w_in_ext = jnp.concatenate([w_in[:, :lat], k_a, k_b], axis=1).astype(BF16)

    qa_cols = []
    per_head = QK_NOPE + QK_ROPE
    for hd in range(N_HEADS):
        w_h = w_uq[:, hd * per_head:(hd + 1) * per_head]
        r_a, r_b = _rope_pair(w_h[:, QK_NOPE:])
        qa_cols += [w_h[:, :QK_NOPE], r_a, r_b]
    w_uqa = jnp.concatenate(qa_cols, axis=1).astype(BF16)

    wpool_bd = jax.scipy.linalg.block_diag(*[w_pool[g] for g in range(len(POOL_WINDOWS))]).astype(BF16)
    wr_hi, wr_lo = _split_hi_lo(w_router.T)
    return dict(w_in_ext=w_in_ext, w_uqa=w_uqa, w_ukv=w_ukv.astype(BF16), wpool_bd=wpool_bd,
                wout_p=w_out[:POOL_WIDTH].astype(BF16), wout_a=w_out[POOL_WIDTH:].astype(BF16),
                wr_hi=wr_hi, wr_lo=wr_lo)


def kernel(x, c, positions, w_ada, b_ada, w_in, q_norm_g, kv_norm_g, w_uq, w_ukv, w_pool, pool_scale,
           w_out, ln1_g, ln1_b, w_router, router_bias, w_exp_gate, w_exp_up, w_exp_down,
           w_sh_gate, w_sh_up, w_sh_down, ln2_g, ln2_b):
    B, S, D = x.shape
    L = w_in.shape[0]
    row = lambda v: v.reshape(1, -1)

    mod_all = _ada_mod(c, w_ada, b_ada).reshape(L, B, 6, D)
    cs_all = _rope_table(positions)
    bs = B // N_STREAMS
    T = bs * S
    n_blocks = T * TOP_K // DISPATCH_BLOCK + N_EXPERTS
    xs_streams = [x] * N_STREAMS
    result = None
    for l in range(L):
        p = _prep_layer(w_in[l], w_uq[l], w_ukv[l], w_pool[l], w_out[l], w_router[l])
        shared = (w_sh_gate[l].astype(BF16), w_sh_up[l].astype(BF16), w_sh_down[l].astype(BF16))
        mod = mod_all[l]
        for s in range(N_STREAMS):
            xl = xs_streams[s]
            b_off = s * bs
            x_off = b_off if l == 0 else 0
            u, q, k, v = _in_proj(xl, mod, cs_all, p["w_in_ext"], row(q_norm_g[l]), row(kv_norm_g[l]),
                                  p["w_uqa"], p["w_ukv"], bs, x_off, b_off)
            attn = _attention(q, k, v)
            x1, h2p, lt = _post_attn(xl, u, attn, mod, p["wpool_bd"], row(pool_scale[l]), p["wout_p"],
                                     p["wout_a"], row(ln1_g[l]), row(ln1_b[l]), p["wr_hi"], p["wr_lo"],
                                     x_off, b_off)
            idx_k, gate_k = _route(lt.reshape(N_EXPERTS, T // LANES, LANES), router_bias[l])
            idx_kt = idx_k.reshape(TOP_K, T)
            gates_kt = gate_k.reshape(TOP_K, T)
            rank_kt, counts = _rank(idx_kt)
            pad_start, seg_first, seg_blocks, n_real = _block_plan(counts[:, 0])
            start = jnp.broadcast_to(pad_start.astype(F32)[:, None], (N_EXPERTS, LANES))
            dest_kt = _dest(idx_kt, rank_kt, start)
            h2p = h2p.reshape(T, HALF)
            rows = _sc_dispatch(h2p, dest_kt, n_blocks * DISPATCH_BLOCK)
            ys = _expert_mm(seg_first, seg_blocks, n_real, rows, w_exp_gate, w_exp_up, w_exp_down, l)
            tc = T // COMBINE_CHUNKS
            out = None
            for ch in range(COMBINE_CHUNKS):
                lo = ch * tc
                routed = _sc_gather_sum(ys, dest_kt[:, lo:lo + tc], gates_kt[:, lo:lo + tc])
                if l < L - 1:
                    out = _combine(h2p, x1.reshape(T, D), routed, mod, *shared, row(ln2_g[l]), row(ln2_b[l]),
                                   S, b_off, lo, T, lo, out)
                else:
                    result = _combine(h2p, x1.reshape(T, D), routed, mod, *shared, row(ln2_g[l]),
                                      row(ln2_b[l]), S, b_off, lo, B * S, s * T + lo, result)
            if l < L - 1:
                xs_streams[s] = out.reshape(bs, S, D)
    return result.reshape(B, S, D)
```

```python
import functools

import jax
import jax.numpy as jnp
from jax import lax
from jax.experimental import pallas as pl
from jax.experimental.pallas import tpu as pltpu
from jax.experimental.pallas import tpu_sc as plsc

F32 = jnp.float32
BF16 = jnp.bfloat16

D_MODEL = 1024
DEPTH = 4
POOL_WIDTH = 256
POOL_WINDOWS = (2, 4, 8, 16)
POOL_GROUP = 64
POOL_HALO = 16
QK_NOPE = 128
QK_ROPE = 64
V_HEAD = 128
N_HEADS = 6
Q_LORA = 384
KV_LORA = 256
ROPE_THETA = 10000.0
N_EXPERTS = 64
TOP_K = 8
N_GROUPS = 8
GROUP_SIZE = N_EXPERTS // N_GROUPS
TOPK_GROUPS = 4
D_EXPERT = 256
ROUTED_SCALE = 2.5
DEEPNORM_ALPHA = (2.0 * DEPTH) ** 0.25
LN_EPS = 1e-5
RMS_EPS = 1e-6
ATTN_SCALE = (QK_NOPE + QK_ROPE) ** -0.5
LOG2_E = 1.4426950408889634
Q_SCALE = ATTN_SCALE * LOG2_E
LANES = 128
QK_PAD = 2 * LANES
MASK_VALUE = -1e30

TM_PROJ = 512
TM_POST = 1024
TQ = 512
TK = 512
ROUTE_ROWS = 8
DISPATCH_BLOCK = 512
EXPERT_RING = 6
TM_COMBINE = 1024
SC_CORES = 2
SC_SUBCORES = 16
SC_WINDOW = 128
SC_LANES = 16
HIGH_HALF_MASK = -65536
SC_SUM_TOKENS = 16
N_STREAMS = 2
COMBINE_CHUNKS = 2
HALF = D_MODEL // 2
VMEM_LIMIT = 48 * 1024 * 1024


def _cparams(sem):
    return pltpu.CompilerParams(dimension_semantics=sem, vmem_limit_bytes=VMEM_LIMIT)


def _split_hi_lo(a):
    hi = a.astype(BF16)
    lo = (a - hi.astype(F32)).astype(BF16)
    return hi, lo


def _pack_bf16_pairs(a):
    bits = lax.bitcast_convert_type(a.astype(F32), jnp.uint32)
    half = a.shape[1] // 2
    word = (bits[:, :half] >> 16) | (bits[:, half:] & jnp.uint32(0xFFFF0000))
    return lax.bitcast_convert_type(word, jnp.int32)


def _unpack_bf16_pairs(w):
    bits = lax.bitcast_convert_type(w, jnp.uint32)
    lo = lax.bitcast_convert_type(bits << 16, F32)
    hi = lax.bitcast_convert_type(bits & jnp.uint32(0xFFFF0000), F32)
    return lo, hi


def _ada_kernel(c_ref, w_ref, b_ref, o_ref):
    c = c_ref[...]
    cond = c * jax.nn.sigmoid(c)
    c_hi, c_lo = _split_hi_lo(cond)
    w_hi, w_lo = _split_hi_lo(w_ref[0])
    acc = jnp.dot(c_hi, w_hi, preferred_element_type=F32)
    acc += jnp.dot(c_lo, w_hi, preferred_element_type=F32)
    acc += jnp.dot(c_hi, w_lo, preferred_element_type=F32)
    o_ref[0] = acc + b_ref[0]


def _ada_mod(c, w_ada, b_ada):
    L, D, N = w_ada.shape
    B = c.shape[0]
    tn = 1536
    return pl.pallas_call(
        _ada_kernel,
        grid=(L, N // tn),
        in_specs=[
            pl.BlockSpec((B, D), lambda l, j: (0, 0)),
            pl.BlockSpec((1, D, tn), lambda l, j: (l, 0, j)),
            pl.BlockSpec((1, 1, tn), lambda l, j: (l, 0, j)),
        ],
        out_specs=pl.BlockSpec((1, B, tn), lambda l, j: (l, 0, j)),
        out_shape=jax.ShapeDtypeStruct((L, B, N), F32),
        compiler_params=_cparams(("arbitrary", "arbitrary")),
        name="ada_mod",
    )(c, w_ada, b_ada.reshape(L, 1, N))


def _rope_table_kernel(pos_ref, freq_ref, o_ref):
    ang = pos_ref[0].astype(F32) * freq_ref[...]
    lane = lax.broadcasted_iota(jnp.int32, (1, LANES), 1)
    o_ref[0] = jnp.where(lane < QK_ROPE, jnp.cos(ang), jnp.sin(ang))


def _rope_table(positions):
    B, S = positions.shape
    half = QK_ROPE // 2
    inv_freq = ROPE_THETA ** (-jnp.arange(half, dtype=F32) / half)
    freq = jnp.tile(inv_freq, LANES // half).reshape(1, LANES)
    return pl.pallas_call(
        _rope_table_kernel,
        grid=(B,),
        in_specs=[
            pl.BlockSpec((1, S, 1), lambda b: (b, 0, 0)),
            pl.BlockSpec((1, LANES), lambda b: (0, 0)),
        ],
        out_specs=pl.BlockSpec((1, S, LANES), lambda b: (b, 0, 0)),
        out_shape=jax.ShapeDtypeStruct((B, S, LANES), F32),
        compiler_params=_cparams(("arbitrary",)),
        name="rope_table",
    )(positions.reshape(B, S, 1), freq)


def _rms(x, g):
    return x * lax.rsqrt(jnp.mean(x * x, axis=-1, keepdims=True) + RMS_EPS) * g


def _in_proj_kernel(x_ref, mod_ref, cs_ref, w_in_ref, qg_ref, kvg_ref, w_uqa_ref, w_ukv_ref,
                    u_ref, q_ref, k_ref, v_ref):
    x = x_ref[0]
    sh1 = mod_ref[0, 0:1, :]
    sc1 = mod_ref[0, 1:2, :]
    h = (x * (1.0 + sc1) + sh1).astype(BF16)
    proj = jnp.dot(h, w_in_ref[...], preferred_element_type=F32)
    u_ref[0] = proj[:, 0:POOL_WIDTH]
    o = POOL_WIDTH
    q_lat = proj[:, o:o + Q_LORA]
    o += Q_LORA
    kv_lat = proj[:, o:o + KV_LORA]
    o += KV_LORA
    cos_sin = cs_ref[0]

    def rotate(pair):
        t = pair * cos_sin
        return t + pltpu.roll(t, QK_ROPE, 1)

    k_rot = rotate(proj[:, o:o + LANES]).astype(BF16)
    lane = lax.broadcasted_iota(jnp.int32, (1, LANES), 1)
    q_keep = jnp.where(lane < QK_ROPE, Q_SCALE, 0.0)

    qn = _rms(q_lat, qg_ref[...]).astype(BF16)
    q_a = jnp.dot(qn, w_uqa_ref[...], preferred_element_type=F32)
    kvn = _rms(kv_lat, kvg_ref[...]).astype(BF16)
    kv = jnp.dot(kvn, w_ukv_ref[...], preferred_element_type=F32)
    for hd in range(N_HEADS):
        b0 = hd * QK_PAD
        q_ref[0, hd, :, 0:LANES] = (q_a[:, b0:b0 + LANES] * Q_SCALE).astype(BF16)
        q_ref[0, hd, :, LANES:QK_PAD] = (rotate(q_a[:, b0 + LANES:b0 + QK_PAD]) * q_keep).astype(BF16)
        k_ref[0, hd, :, 0:LANES] = kv[:, b0:b0 + QK_NOPE].astype(BF16)
        k_ref[0, hd, :, LANES:QK_PAD] = k_rot
        v_ref[0, hd] = kv[:, b0 + QK_NOPE:b0 + QK_NOPE + V_HEAD].astype(BF16)


def _in_proj(x, mod, cs, w_in_ext, qg, kvg, w_uqa, w_ukv_bf, B, x_off, b_off):
    _, S, D = x.shape
    tm = TM_PROJ
    const = lambda b, i: (0, 0)
    return pl.pallas_call(
        _in_proj_kernel,
        grid=(B, S // tm),
        in_specs=[
            pl.BlockSpec((1, tm, D), lambda b, i: (b + x_off, i, 0)),
            pl.BlockSpec((1, 6, D), lambda b, i: (b + b_off, 0, 0)),
            pl.BlockSpec((1, tm, LANES), lambda b, i: (b + b_off, i, 0)),
            pl.BlockSpec(w_in_ext.shape, const),
            pl.BlockSpec(qg.shape, const),
            pl.BlockSpec(kvg.shape, const),
            pl.BlockSpec(w_uqa.shape, const),
            pl.BlockSpec(w_ukv_bf.shape, const),
        ],
        out_specs=[
            pl.BlockSpec((1, tm, POOL_WIDTH), lambda b, i: (b, i, 0)),
            pl.BlockSpec((1, N_HEADS, tm, QK_PAD), lambda b, i: (b, 0, i, 0)),
            pl.BlockSpec((1, N_HEADS, tm, QK_PAD), lambda b, i: (b, 0, i, 0)),
            pl.BlockSpec((1, N_HEADS, tm, V_HEAD), lambda b, i: (b, 0, i, 0)),
        ],
        out_shape=[
            jax.ShapeDtypeStruct((B, S, POOL_WIDTH), F32),
            jax.ShapeDtypeStruct((B, N_HEADS, S, QK_PAD), BF16),
            jax.ShapeDtypeStruct((B, N_HEADS, S, QK_PAD), BF16),
            jax.ShapeDtypeStruct((B, N_HEADS, S, V_HEAD), BF16),
        ],
        compiler_params=_cparams(("arbitrary", "arbitrary")),
        name="in_proj",
    )(x, mod, cs, w_in_ext, qg, kvg, w_uqa, w_ukv_bf)


def _softmax_step(q, k, v, carry, mask):
    m, l, acc = carry
    s = lax.dot_general(q, k, (((1,), (1,)), ((), ())), preferred_element_type=F32)
    if mask is not None:
        s = jnp.where(mask, s, MASK_VALUE)
    m_new = jnp.maximum(m, jnp.max(s, axis=-1, keepdims=True))
    alpha = jnp.exp2(m - m_new)
    p = jnp.exp2(s - m_new)
    l_new = alpha * l + jnp.sum(p, axis=-1, keepdims=True)
    acc_new = alpha * acc + jnp.dot(p.astype(BF16), v, preferred_element_type=F32)
    return m_new, l_new, acc_new


def _attn_kernel(q_ref, k_ref, v_ref, o_ref):
    seq = q_ref.shape[2]
    row = lax.broadcasted_iota(jnp.int32, (TQ, TK), 0)
    col = lax.broadcasted_iota(jnp.int32, (TQ, TK), 1)
    diag = row >= col
    for i in range(seq // TQ):
        q = q_ref[0, 0, i * TQ:(i + 1) * TQ, :]
        carry = (jnp.full((TQ, 1), MASK_VALUE, F32), jnp.zeros((TQ, 1), F32), jnp.zeros((TQ, V_HEAD), F32))
        for j in range(i + 1):
            k = k_ref[0, 0, j * TK:(j + 1) * TK, :]
            v = v_ref[0, 0, j * TK:(j + 1) * TK, :]
            carry = _softmax_step(q, k, v, carry, diag if j == i else None)
        _, l, acc = carry
        o_ref[0, 0, i * TQ:(i + 1) * TQ, :] = (acc / l).astype(BF16)


def _attention(q, k, v):
    B, H, S, _ = q.shape
    return pl.pallas_call(
        _attn_kernel,
        grid=(B, H),
        in_specs=[
            pl.BlockSpec((1, 1, S, QK_PAD), lambda b, h: (b, h, 0, 0)),
            pl.BlockSpec((1, 1, S, QK_PAD), lambda b, h: (b, h, 0, 0)),
            pl.BlockSpec((1, 1, S, V_HEAD), lambda b, h: (b, h, 0, 0)),
        ],
        out_specs=pl.BlockSpec((1, 1, S, V_HEAD), lambda b, h: (b, h, 0, 0)),
        out_shape=jax.ShapeDtypeStruct((B, H, S, V_HEAD), BF16),
        compiler_params=_cparams(("arbitrary", "arbitrary")),
        name="attention",
    )(q, k, v)


def _layer_norm(v, g, b):
    mu = jnp.mean(v, axis=-1, keepdims=True)
    d = v - mu
    var = jnp.mean(d * d, axis=-1, keepdims=True)
    return d * lax.rsqrt(var + LN_EPS) * g + b


def _post_kernel(x_ref, u_ref, halo_ref, attn_ref, mod_ref, wpool_ref, pscale_ref, wout_p_ref, wout_a_ref,
                 lng_ref, lnb_ref, wr_hi_ref, wr_lo_ref, x1_ref, h2_ref, lt_ref):
    i = pl.program_id(1)
    tm = u_ref.shape[1]
    u = u_ref[0]
    halo = jnp.where(i > 0, halo_ref[0], 0.0)
    ext = jnp.concatenate([halo, u], axis=0)
    s2 = ext + pltpu.roll(ext, 1, 0)
    s4 = s2 + pltpu.roll(s2, 2, 0)
    s8 = s4 + pltpu.roll(s4, 4, 0)
    s16 = s8 + pltpu.roll(s8, 8, 0)
    lane = lax.broadcasted_iota(jnp.int32, (1, POOL_WIDTH), 1)
    win = jnp.where(lane < POOL_GROUP, s2,
                    jnp.where(lane < 2 * POOL_GROUP, s4, jnp.where(lane < 3 * POOL_GROUP, s8, s16)))
    win = win[POOL_HALO:, :]
    width = jnp.where(lane < POOL_GROUP, POOL_WINDOWS[0],
                      jnp.where(lane < 2 * POOL_GROUP, POOL_WINDOWS[1],
                                jnp.where(lane < 3 * POOL_GROUP, POOL_WINDOWS[2], POOL_WINDOWS[3])))
    t = i * tm + lax.broadcasted_iota(jnp.int32, (tm, 1), 0)
    count = jnp.minimum(t + 1, width).astype(F32)
    token_mix = win / count - u
    pooled = jnp.dot(token_mix.astype(BF16), wpool_ref[...], preferred_element_type=F32) * pscale_ref[...]
    mixed = jnp.dot(pooled.astype(BF16), wout_p_ref[...], preferred_element_type=F32)
    attn = jnp.concatenate([attn_ref[0, hd] for hd in range(N_HEADS)], axis=1)
    mixed += jnp.dot(attn, wout_a_ref[...], preferred_element_type=F32)

    g1 = mod_ref[0, 2:3, :]
    sh2 = mod_ref[0, 3:4, :]
    sc2 = mod_ref[0, 4:5, :]
    x1 = _layer_norm(DEEPNORM_ALPHA * x_ref[0] + g1 * mixed, lng_ref[...], lnb_ref[...])
    x1_ref[0] = x1
    h2 = x1 * (1.0 + sc2) + sh2
    h_hi, h_lo = _split_hi_lo(h2)
    h2_ref[0] = _pack_bf16_pairs(h_hi)
    nt = (((1,), (1,)), ((), ()))
    lt = lax.dot_general(wr_hi_ref[...], h_hi, nt, preferred_element_type=F32)
    lt += lax.dot_general(wr_hi_ref[...], h_lo, nt, preferred_element_type=F32)
    lt += lax.dot_general(wr_lo_ref[...], h_hi, nt, preferred_element_type=F32)
    lt_ref[...] = lt


def _post_attn(x, u, attn, mod, wpool_bd, pscale, wout_p, wout_a, lng, lnb, wr_hi, wr_lo, x_off, b_off):
    B, S, _ = u.shape
    D = x.shape[2]
    tm = TM_POST
    nt = S // tm
    const = lambda b, i: (0, 0)
    halo_blocks = tm // POOL_HALO
    return pl.pallas_call(
        _post_kernel,
        grid=(B, nt),
        in_specs=[
            pl.BlockSpec((1, tm, D), lambda b, i: (b + x_off, i, 0)),
            pl.BlockSpec((1, tm, POOL_WIDTH), lambda b, i: (b, i, 0)),
            pl.BlockSpec((1, POOL_HALO, POOL_WIDTH), lambda b, i: (b, jnp.maximum(i * halo_blocks - 1, 0), 0)),
            pl.BlockSpec((1, N_HEADS, tm, V_HEAD), lambda b, i: (b, 0, i, 0)),
            pl.BlockSpec((1, 6, D), lambda b, i: (b + b_off, 0, 0)),
            pl.BlockSpec(wpool_bd.shape, const),
            pl.BlockSpec(pscale.shape, const),
            pl.BlockSpec(wout_p.shape, const),
            pl.BlockSpec(wout_a.shape, const),
            pl.BlockSpec(lng.shape, const),
            pl.BlockSpec(lnb.shape, const),
            pl.BlockSpec(wr_hi.shape, const),
            pl.BlockSpec(wr_lo.shape, const),
        ],
        out_specs=[
            pl.BlockSpec((1, tm, D), lambda b, i: (b, i, 0)),
            pl.BlockSpec((1, tm, HALF), lambda b, i: (b, i, 0)),
            pl.BlockSpec((N_EXPERTS, tm), lambda b, i: (0, b * nt + i)),
        ],
        out_shape=[
            jax.ShapeDtypeStruct((B, S, D), F32),
            jax.ShapeDtypeStruct((B, S, HALF), jnp.int32),
            jax.ShapeDtypeStruct((N_EXPERTS, B * S), F32),
        ],
        compiler_params=_cparams(("arbitrary", "arbitrary")),
        name="post_attn",
    )(x, u, u, attn, mod, wpool_bd, pscale, wout_p, wout_a, lng, lnb, wr_hi, wr_lo)


def _select_first_max(vals, n_rounds, payload=None):
    work = list(vals)
    sel = [None] * len(vals)
    rounds = []
    for _ in range(n_rounds):
        m = functools.reduce(jnp.maximum, work)
        taken = None
        win_idx = jnp.zeros(m.shape, jnp.int32)
        win_val = jnp.zeros(m.shape, F32)
        for e in range(len(work)):
            hit = work[e] == m
            first = hit if taken is None else jnp.logical_and(hit, jnp.logical_not(taken))
            taken = hit if taken is None else jnp.logical_or(taken, hit)
            sel[e] = first if sel[e] is None else jnp.logical_or(sel[e], first)
            work[e] = jnp.where(first, -jnp.inf, work[e])
            if payload is not None:
                win_idx = jnp.where(first, e, win_idx)
                win_val = jnp.where(first, payload[e], win_val)
        rounds.append((win_idx, win_val))
    return sel, rounds


def _route_kernel(bias_ref, lt_ref, idx_ref, gate_ref):
    scores = [jax.nn.sigmoid(lt_ref[e]) for e in range(N_EXPERTS)]
    choice = [scores[e] + bias_ref[e] for e in range(N_EXPERTS)]
    group_score = []
    for g in range(N_GROUPS):
        vals = choice[g * GROUP_SIZE:(g + 1) * GROUP_SIZE]
        sel2, _ = _select_first_max(vals, 2)
        group_score.append(functools.reduce(
            jnp.add, [jnp.where(sel2[j], vals[j], 0.0) for j in range(GROUP_SIZE)]))
    group_sel, _ = _select_first_max(group_score, TOPK_GROUPS)
    masked = [jnp.where(group_sel[e // GROUP_SIZE], choice[e], -jnp.inf) for e in range(N_EXPERTS)]
    _, rounds = _select_first_max(masked, TOP_K, payload=scores)
    total = functools.reduce(jnp.add, [w for _, w in rounds])
    for k, (e_k, w_k) in enumerate(rounds):
        g_k = w_k / total * ROUTED_SCALE
        for r in range(ROUTE_ROWS):
            idx_ref[k:k + 1, r * LANES:(r + 1) * LANES] = e_k[r:r + 1, :]
            gate_ref[k:k + 1, r * LANES:(r + 1) * LANES] = g_k[r:r + 1, :]


def _route(lt3, bias):
    E, R, _ = lt3.shape
    out_spec = pl.BlockSpec((TOP_K, ROUTE_ROWS * LANES), lambda r: (0, r))
    return pl.pallas_call(
        _route_kernel,
        grid=(R // ROUTE_ROWS,),
        in_specs=[
            pl.BlockSpec(memory_space=pltpu.SMEM),
            pl.BlockSpec((E, ROUTE_ROWS, LANES), lambda r: (0, r, 0)),
        ],
        out_specs=[out_spec, out_spec],
        out_shape=[jax.ShapeDtypeStruct((TOP_K, R * LANES), jnp.int32),
                   jax.ShapeDtypeStruct((TOP_K, R * LANES), F32)],
        compiler_params=_cparams(("arbitrary",)),
        name="route",
    )(bias, lt3)


def _sc_mesh():
    return plsc.VectorSubcoreMesh(core_axis_name="c", subcore_axis_name="s",
                                  num_cores=SC_CORES, num_subcores=SC_SUBCORES)


def _sc_worker_id():
    return lax.axis_index("s") * SC_CORES + lax.axis_index("c")


def _sc_gather(table, idx):
    n = idx.shape[0]
    width = table.shape[1]
    workers = SC_CORES * SC_SUBCORES
    per_worker = n // workers
    n_win = per_worker // SC_WINDOW
    assert n == workers * n_win * SC_WINDOW

    @functools.partial(
        pl.kernel, mesh=_sc_mesh(),
        out_type=jax.ShapeDtypeStruct((n, width), table.dtype),
        scratch_types=[
            pltpu.VMEM((n_win, SC_WINDOW), jnp.int32),
            pltpu.VMEM((SC_WINDOW, width), table.dtype),
            pltpu.SemaphoreType.DMA,
        ],
    )
    def gather(table_hbm, idx_hbm, out_hbm, idx_v, rows_v, sem):
        wid = _sc_worker_id()
        pltpu.sync_copy(idx_hbm.at[wid], idx_v)

        @pl.loop(0, n_win)
        def _(j):
            pltpu.async_copy(table_hbm.at[idx_v.at[j]], rows_v, sem).wait()
            pltpu.sync_copy(rows_v, out_hbm.at[pl.ds(wid * per_worker + j * SC_WINDOW, SC_WINDOW)])

    return gather(table, idx.reshape(workers, n_win, SC_WINDOW))


def _sc_gather_sum(ys, dest_kt, gates_kt):
    K, T = dest_kt.shape
    half = ys.shape[1]
    workers = SC_CORES * SC_SUBCORES
    per_worker = T // workers
    wt = SC_SUM_TOKENS
    n_win = per_worker // wt
    n_idx = K * wt
    assert T == workers * n_win * wt and n_idx <= 128 and half % SC_LANES == 0
    dest_w = dest_kt.reshape(K, workers, n_win, wt).transpose(1, 2, 0, 3).reshape(workers, n_win, n_idx)
    gates_w = gates_kt.reshape(K, workers, n_win, wt).transpose(1, 2, 0, 3).reshape(workers * n_win * n_idx)

    @functools.partial(
        pl.kernel, mesh=_sc_mesh(),
        out_type=jax.ShapeDtypeStruct((T, 2 * half), F32),
        scratch_types=[
            pltpu.VMEM((n_win, n_idx), jnp.int32),
            pltpu.VMEM((n_idx, half), jnp.int32),
            pltpu.VMEM((n_idx,), F32),
            pltpu.VMEM((wt, 2 * half), F32),
            pltpu.SemaphoreType.DMA,
        ],
        compiler_params=pltpu.CompilerParams(needs_layout_passes=False),
    )
    def gather_sum(ys_hbm, dest_hbm, gates_hbm, out_hbm, idx_v, rows_v, gate_v, out_v, sem):
        wid = _sc_worker_id()
        pltpu.sync_copy(dest_hbm.at[wid], idx_v)

        @pl.loop(0, n_win)
        def _(j):
            pltpu.async_copy(ys_hbm.at[idx_v.at[j]], rows_v, sem).wait()
            pltpu.sync_copy(gates_hbm.at[pl.ds((wid * n_win + j) * n_idx, n_idx)], gate_v)

            @pl.loop(0, wt)
            def _(t):
                g = [plsc.load_gather(gate_v, [jnp.full((SC_LANES,), k * wt + t, jnp.int32)]) for k in range(K)]

                @pl.loop(0, half // SC_LANES)
                def _(c):
                    col = pl.ds(c * SC_LANES, SC_LANES)
                    acc_lo = jnp.zeros((SC_LANES,), F32)
                    acc_hi = jnp.zeros((SC_LANES,), F32)
                    for k in range(K):
                        w = rows_v[k * wt + t, col]
                        acc_lo = acc_lo + g[k] * plsc.bitcast(w << 16, F32)
                        acc_hi = acc_hi + g[k] * plsc.bitcast(w & HIGH_HALF_MASK, F32)
                    out_v[t, col] = acc_lo
                    out_v[t, pl.ds(half + c * SC_LANES, SC_LANES)] = acc_hi

            pltpu.sync_copy(out_v, out_hbm.at[pl.ds(wid * per_worker + j * wt, wt)])

    return gather_sum(ys, dest_w, gates_w)


def _sc_dispatch(rows, dest_kt, n_slots):
    T, width = rows.shape
    K = dest_kt.shape[0]
    workers = SC_CORES * SC_SUBCORES
    per_worker = T // workers
    n_win = per_worker // SC_WINDOW
    assert T == workers * n_win * SC_WINDOW
    dest_w = dest_kt.reshape(K, workers, n_win, SC_WINDOW).transpose(1, 2, 0, 3)
    dest_w = dest_w.reshape(workers, n_win * K, SC_WINDOW)

    @functools.partial(
        pl.kernel, mesh=_sc_mesh(),
        out_type=jax.ShapeDtypeStruct((n_slots, width), rows.dtype),
        scratch_types=[
            pltpu.VMEM((n_win * K, SC_WINDOW), jnp.int32),
            pltpu.VMEM((SC_WINDOW, width), rows.dtype),
        ],
    )
    def dispatch(rows_hbm, dest_hbm, out_hbm, idx_v, rows_v):
        wid = _sc_worker_id()
        pltpu.sync_copy(dest_hbm.at[wid], idx_v)

        @pl.loop(0, n_win)
        def _(j):
            pltpu.sync_copy(rows_hbm.at[pl.ds(wid * per_worker + j * SC_WINDOW, SC_WINDOW)], rows_v)
            for k in range(K):
                pltpu.sync_copy(rows_v, out_hbm.at[idx_v.at[j * K + k]])

    return dispatch(rows, dest_w)


def _swiglu(h_lo, h_hi, wg_ref, wu_ref):
    def proj(w_ref):
        return (jnp.dot(h_lo, w_ref[:HALF, :], preferred_element_type=F32)
                + jnp.dot(h_hi, w_ref[HALF:, :], preferred_element_type=F32))
    gate = proj(wg_ref)
    up = proj(wu_ref)
    return (gate * jax.nn.sigmoid(gate) * up).astype(BF16)


def _expert_kernel(seg_first_ref, seg_blocks_ref, n_real_ref, xs_hbm, wg_ref, wu_ref, wd_ref, ys_hbm,
                   xbuf, ybuf, wg_bf, wu_bf, wd_bf, in_sems, out_sems):
    e = pl.program_id(0)
    first = seg_first_ref[e]
    count = seg_blocks_ref[e]
    total = n_real_ref[0]
    blk = DISPATCH_BLOCK
    ring = xbuf.shape[0]

    def in_copy(g):
        slot = g % ring
        return pltpu.make_async_copy(xs_hbm.at[pl.ds(g * blk, blk)], xbuf.at[slot], in_sems.at[slot])

    def out_copy(g):
        slot = g % ring
        return pltpu.make_async_copy(ybuf.at[slot], ys_hbm.at[pl.ds(g * blk, blk)], out_sems.at[slot])

    @pl.when(e == 0)
    def _():
        for g in range(ring - 1):
            @pl.when(g < total)
            def _():
                in_copy(g).start()

    @pl.when(count > 0)
    def _():
        wg_bf[...] = wg_ref[0, 0].astype(BF16)
        wu_bf[...] = wu_ref[0, 0].astype(BF16)
        wd_bf[...] = wd_ref[0, 0].astype(BF16)

    def block(g, _):
        slot = g % ring
        in_copy(g).wait()

        @pl.when(g + ring - 1 < total)
        def _():
            in_copy(g + ring - 1).start()

        @pl.when(g >= ring)
        def _():
            out_copy(g - ring).wait()

        lo, hi = _unpack_bf16_pairs(xbuf[slot])
        a = _swiglu(lo.astype(BF16), hi.astype(BF16), wg_bf, wu_bf)
        y = jnp.dot(a, wd_bf[...], preferred_element_type=F32)
        ybuf[slot] = _pack_bf16_pairs(y.astype(BF16))
        out_copy(g).start()
        return 0

    lax.fori_loop(first, first + count, block, 0)

    @pl.when(e == pl.num_programs(0) - 1)
    def _():
        for back in range(ring, 0, -1):
            @pl.when(total >= back)
            def _():
                out_copy(total - back).wait()


def _expert_mm(seg_first, seg_blocks, n_real, xs, wg, wu, wd, layer):
    P = xs.shape[0]
    D = wg.shape[2]
    w_map = lambda e, sf, sb, nr: (layer, e, 0, 0)
    grid_spec = pltpu.PrefetchScalarGridSpec(
        num_scalar_prefetch=3,
        grid=(N_EXPERTS,),
        in_specs=[
            pl.BlockSpec(memory_space=pl.ANY),
            pl.BlockSpec((1, 1, D, D_EXPERT), w_map),
            pl.BlockSpec((1, 1, D, D_EXPERT), w_map),
            pl.BlockSpec((1, 1, D_EXPERT, D), w_map),
        ],
        out_specs=pl.BlockSpec(memory_space=pl.ANY),
        scratch_shapes=[
            pltpu.VMEM((EXPERT_RING, DISPATCH_BLOCK, HALF), jnp.int32),
            pltpu.VMEM((EXPERT_RING, DISPATCH_BLOCK, HALF), jnp.int32),
            pltpu.VMEM((D, D_EXPERT), BF16), pltpu.VMEM((D, D_EXPERT), BF16), pltpu.VMEM((D_EXPERT, D), BF16),
            pltpu.SemaphoreType.DMA((EXPERT_RING,)), pltpu.SemaphoreType.DMA((EXPERT_RING,)),
        ],
    )
    return pl.pallas_call(
        _expert_kernel,
        grid_spec=grid_spec,
        out_shape=jax.ShapeDtypeStruct((P, HALF), jnp.int32),
        compiler_params=_cparams(("arbitrary",)),
        name="expert_mm",
    )(seg_first, seg_blocks, n_real, xs, wg, wu, wd)


def _rank_kernel(idx_ref, rank_ref, cnt_ref, tri_ref, carry_ref):
    i = pl.program_id(0)
    tm = idx_ref.shape[1]

    @pl.when(i == 0)
    def _():
        r = lax.broadcasted_iota(jnp.int32, (tm, tm), 0)
        c = lax.broadcasted_iota(jnp.int32, (tm, tm), 1)
        tri_ref[...] = (r < c).astype(BF16)
        carry_ref[...] = jnp.zeros(carry_ref.shape, F32)

    e_iota = lax.broadcasted_iota(jnp.int32, (N_EXPERTS, tm), 0)
    hits = [e_iota == idx_ref[k:k + 1, :] for k in range(TOP_K)]
    mask = functools.reduce(jnp.logical_or, hits).astype(F32)
    rank = jnp.dot(mask.astype(BF16), tri_ref[...], preferred_element_type=F32) + carry_ref[:, 0:1]
    for k in range(TOP_K):
        rank_ref[k:k + 1, :] = jnp.sum(jnp.where(hits[k], rank, 0.0), axis=0, keepdims=True).astype(jnp.int32)
    carry_ref[...] = carry_ref[...] + jnp.sum(mask, axis=1, keepdims=True)
    cnt_ref[...] = carry_ref[...]


def _rank(idx_kt):
    K, T = idx_kt.shape
    tm = 1024
    return pl.pallas_call(
        _rank_kernel,
        grid=(T // tm,),
        in_specs=[pl.BlockSpec((K, tm), lambda i: (0, i))],
        out_specs=[pl.BlockSpec((K, tm), lambda i: (0, i)),
                   pl.BlockSpec((N_EXPERTS, LANES), lambda i: (0, 0))],
        out_shape=[jax.ShapeDtypeStruct((K, T), jnp.int32),
                   jax.ShapeDtypeStruct((N_EXPERTS, LANES), F32)],
        scratch_shapes=[pltpu.VMEM((tm, tm), BF16), pltpu.VMEM((N_EXPERTS, LANES), F32)],
        compiler_params=_cparams(("arbitrary",)),
        name="rank",
    )(idx_kt)


def _dest_kernel(idx_ref, rank_ref, start_ref, dest_ref):
    tm = idx_ref.shape[1]
    e_iota = lax.broadcasted_iota(jnp.int32, (N_EXPERTS, tm), 0)
    start = start_ref[:, 0:1]
    for k in range(TOP_K):
        base = jnp.sum(jnp.where(e_iota == idx_ref[k:k + 1, :], start, 0.0), axis=0, keepdims=True)
        dest_ref[k:k + 1, :] = rank_ref[k:k + 1, :] + base.astype(jnp.int32)


def _dest(idx_kt, rank_kt, start):
    K, T = idx_kt.shape
    tm = 1024
    tok_spec = pl.BlockSpec((K, tm), lambda i: (0, i))
    return pl.pallas_call(
        _dest_kernel,
        grid=(T // tm,),
        in_specs=[tok_spec, tok_spec, pl.BlockSpec((N_EXPERTS, LANES), lambda i: (0, 0))],
        out_specs=tok_spec,
        out_shape=jax.ShapeDtypeStruct((K, T), jnp.int32),
        compiler_params=_cparams(("arbitrary",)),
        name="dest",
    )(idx_kt, rank_kt, start)


def _block_plan(counts):
    blk = DISPATCH_BLOCK
    seg_blocks = (counts.astype(jnp.int32) + blk - 1) // blk
    seg_end = jnp.cumsum(seg_blocks)
    seg_first = seg_end - seg_blocks
    return seg_first * blk, seg_first, seg_blocks, seg_end[-1:]


def _combine_kernel(h_ref, x1_ref, routed_ref, mod_ref, wsg_ref, wsu_ref, wsd_ref, lng_ref, lnb_ref, o_ref):
    lo, hi = _unpack_bf16_pairs(h_ref[...])
    a = _swiglu(lo.astype(BF16), hi.astype(BF16), wsg_ref, wsu_ref)
    ffn = jnp.dot(a, wsd_ref[...], preferred_element_type=F32) + routed_ref[...]
    g2 = mod_ref[0, 5:6, :]
    o_ref[...] = _layer_norm(DEEPNORM_ALPHA * x1_ref[...] + g2 * ffn, lng_ref[...], lnb_ref[...])


def _combine_into_kernel(h_ref, x1_ref, routed_ref, mod_ref, wsg_ref, wsu_ref, wsd_ref, lng_ref, lnb_ref,
                         prev_ref, o_ref):
    del prev_ref
    _combine_kernel(h_ref, x1_ref, routed_ref, mod_ref, wsg_ref, wsu_ref, wsd_ref, lng_ref, lnb_ref, o_ref)


def _combine(h2p, x1, routed, mod, wsg, wsu, wsd, lng, lnb, seq, b_off, in_off, out_rows, out_off, out_prev):
    n, D = routed.shape
    tm = TM_COMBINE
    per_seq = seq // tm
    in_blk = in_off // tm
    blk_off = out_off // tm
    const = lambda i: (0, 0)
    in_specs = [
        pl.BlockSpec((tm, HALF), lambda i: (i + in_blk, 0)),
        pl.BlockSpec((tm, D), lambda i: (i + in_blk, 0)),
        pl.BlockSpec((tm, D), lambda i: (i, 0)),
        pl.BlockSpec((1, 6, D), lambda i: ((i + in_blk) // per_seq + b_off, 0, 0)),
        pl.BlockSpec(wsg.shape, const),
        pl.BlockSpec(wsu.shape, const),
        pl.BlockSpec(wsd.shape, const),
        pl.BlockSpec(lng.shape, const),
        pl.BlockSpec(lnb.shape, const),
    ]
    args = [h2p, x1, routed, mod, wsg, wsu, wsd, lng, lnb]
    body = _combine_kernel
    aliases = {}
    if out_prev is not None:
        in_specs.append(pl.BlockSpec(memory_space=pl.ANY))
        args.append(out_prev)
        aliases = {len(args) - 1: 0}
        body = _combine_into_kernel
    return pl.pallas_call(
        body,
        grid=(n // tm,),
        in_specs=in_specs,
        out_specs=pl.BlockSpec((tm, D), lambda i: (i + blk_off, 0)),
        out_shape=jax.ShapeDtypeStruct((out_rows, D), F32),
        input_output_aliases=aliases,
        compiler_params=_cparams(("arbitrary",)),
        name="combine",
    )(*args)


def _rope_pair(w):
    half = QK_ROPE // 2
    return w, jnp.concatenate([-w[:, half:], w[:, :half]], axis=1)


def _prep_layer(w_in, w_uq, w_ukv, w_pool, w_out, w_router):
    lat = POOL_WIDTH + Q_LORA + KV_LORA
    k_a, k_b = _rope_pair(w_in[:, lat:lat + QK_ROPE])
    w_in_ext = jnp.concatenate([w_in[:, :lat], k_a, k_b], axis=1).astype(BF16)

    qa_cols = []
    per_head = QK_NOPE + QK_ROPE
    for hd in range(N_HEADS):
        w_h = w_uq[:, hd * per_head:(hd + 1) * per_head]
        r_a, r_b = _rope_pair(w_h[:, QK_NOPE:])
        qa_cols += [w_h[:, :QK_NOPE], r_a, r_b]
    w_uqa = jnp.concatenate(qa_cols, axis=1).astype(BF16)

    wpool_bd = jax.scipy.linalg.block_diag(*[w_pool[g] for g in range(len(POOL_WINDOWS))]).astype(BF16)
    wr_hi, wr_lo = _split_hi_lo(w_router.T)
    return dict(w_in_ext=w_in_ext, w_uqa=w_uqa, w_ukv=w_ukv.astype(BF16), wpool_bd=wpool_bd,
                wout_p=w_out[:POOL_WIDTH].astype(BF16), wout_a=w_out[POOL_WIDTH:].astype(BF16),
                wr_hi=wr_hi, wr_lo=wr_lo)


def kernel(x, c, positions, w_ada, b_ada, w_in, q_norm_g, kv_norm_g, w_uq, w_ukv, w_pool, pool_scale,
           w_out, ln1_g, ln1_b, w_router, router_bias, w_exp_gate, w_exp_up, w_exp_down,
           w_sh_gate, w_sh_up, w_sh_down, ln2_g, ln2_b):
    B, S, D = x.shape
    L = w_in.shape[0]
    row = lambda v: v.reshape(1, -1)

    mod_all = _ada_mod(c, w_ada, b_ada).reshape(L, B, 6, D)
    cs_all = _rope_table(positions)
    bs = B // N_STREAMS
    T = bs * S
    n_blocks = T * TOP_K // DISPATCH_BLOCK + N_EXPERTS
    xs_streams = [x] * N_STREAMS
    result = None
    for l in range(L):
        p = _prep_layer(w_in[l], w_uq[l], w_ukv[l], w_pool[l], w_out[l], w_router[l])
        shared = (w_sh_gate[l].astype(BF16), w_sh_up[l].astype(BF16), w_sh_down[l].astype(BF16))
        mod = mod_all[l]
        for s in range(N_STREAMS):
            xl = xs_streams[s]
            b_off = s * bs
            x_off = b_off if l == 0 else 0
            u, q, k, v = _in_proj(xl, mod, cs_all, p["w_in_ext"], row(q_norm_g[l]), row(kv_norm_g[l]),
                                  p["w_uqa"], p["w_ukv"], bs, x_off, b_off)
            attn = _attention(q, k, v)
            x1, h2p, lt = _post_attn(xl, u, attn, mod, p["wpool_bd"], row(pool_scale[l]), p["wout_p"],
                                     p["wout_a"], row(ln1_g[l]), row(ln1_b[l]), p["wr_hi"], p["wr_lo"],
                                     x_off, b_off)
            idx_kt, gates_kt = _route(lt.reshape(N_EXPERTS, T // LANES, LANES), router_bias[l])
            rank_kt, counts = _rank(idx_kt)
            pad_start, seg_first, seg_blocks, n_real = _block_plan(counts[:, 0])
            start = jnp.broadcast_to(pad_start.astype(F32)[:, None], (N_EXPERTS, LANES))
            dest_kt = _dest(idx_kt, rank_kt, start)
            h2p = h2p.reshape(T, HALF)
            rows = _sc_dispatch(h2p, dest_kt, n_blocks * DISPATCH_BLOCK)
            ys = _expert_mm(seg_first, seg_blocks, n_real, rows, w_exp_gate, w_exp_up, w_exp_down, l)
            tc = T // COMBINE_CHUNKS
            out = None
            for ch in range(COMBINE_CHUNKS):
                lo = ch * tc
                routed = _sc_gather_sum(ys, dest_kt[:, lo:lo + tc], gates_kt[:, lo:lo + tc])
                if l < L - 1:
                    out = _combine(h2p, x1.reshape(T, D), routed, mod, *shared, row(ln2_g[l]), row(ln2_b[l]),
                                   S, b_off, lo, T, lo, out)
                else:
                    result = _combine(h2p, x1.reshape(T, D), routed, mod, *shared, row(ln2_g[l]),
                                      row(ln2_b[l]), S, b_off, lo, B * S, s * T + lo, result)
            if l < L - 1:
                xs_streams[s] = out.reshape(bs, S, D)
    return result.reshape(B, S, D)
```

```python
import functools

import jax
import jax.numpy as jnp
from jax import lax
from jax.experimental import pallas as pl
from jax.experimental.pallas import tpu as pltpu
from jax.experimental.pallas import tpu_sc as plsc

F32 = jnp.float32
BF16 = jnp.bfloat16

D_MODEL = 1024
DEPTH = 4
POOL_WIDTH = 256
POOL_WINDOWS = (2, 4, 8, 16)
POOL_GROUP = 64
POOL_HALO = 16
QK_NOPE = 128
QK_ROPE = 64
V_HEAD = 128
N_HEADS = 6
Q_LORA = 384
KV_LORA = 256
ROPE_THETA = 10000.0
N_EXPERTS = 64
TOP_K = 8
N_GROUPS = 8
GROUP_SIZE = N_EXPERTS // N_GROUPS
TOPK_GROUPS = 4
D_EXPERT = 256
ROUTED_SCALE = 2.5
DEEPNORM_ALPHA = (2.0 * DEPTH) ** 0.25
LN_EPS = 1e-5
RMS_EPS = 1e-6
ATTN_SCALE = (QK_NOPE + QK_ROPE) ** -0.5
LOG2_E = 1.4426950408889634
Q_SCALE = ATTN_SCALE * LOG2_E
LANES = 128
QK_PAD = 2 * LANES
MASK_VALUE = -1e30

TM_PROJ = 512
TM_POST = 1024
TQ = 512
TK = 512
ROUTE_ROWS = 8
DISPATCH_BLOCK = 512
EXPERT_RING = 6
TM_COMBINE = 1024
SC_CORES = 2
SC_SUBCORES = 16
SC_WINDOW = 128
SC_LANES = 16
HIGH_HALF_MASK = -65536
SC_SUM_TOKENS = 16
N_STREAMS = 2
COMBINE_CHUNKS = 2
HALF = D_MODEL // 2
VMEM_LIMIT = 48 * 1024 * 1024


def _cparams(sem):
    return pltpu.CompilerParams(dimension_semantics=sem, vmem_limit_bytes=VMEM_LIMIT)


def _split_hi_lo(a):
    hi = a.astype(BF16)
    lo = (a - hi.astype(F32)).astype(BF16)
    return hi, lo


def _pack_bf16_pairs(a):
    bits = lax.bitcast_convert_type(a.astype(F32), jnp.uint32)
    half = a.shape[1] // 2
    word = (bits[:, :half] >> 16) | (bits[:, half:] & jnp.uint32(0xFFFF0000))
    return lax.bitcast_convert_type(word, jnp.int32)


def _unpack_bf16_pairs(w):
    bits = lax.bitcast_convert_type(w, jnp.uint32)
    lo = lax.bitcast_convert_type(bits << 16, F32)
    hi = lax.bitcast_convert_type(bits & jnp.uint32(0xFFFF0000), F32)
    return lo, hi


def _ada_kernel(c_ref, w_ref, b_ref, o_ref):
    c = c_ref[...]
    cond = c * jax.nn.sigmoid(c)
    c_hi, c_lo = _split_hi_lo(cond)
    w_hi, w_lo = _split_hi_lo(w_ref[0])
    acc = jnp.dot(c_hi, w_hi, preferred_element_type=F32)
    acc += jnp.dot(c_lo, w_hi, preferred_element_type=F32)
    acc += jnp.dot(c_hi, w_lo, preferred_element_type=F32)
    o_ref[0] = acc + b_ref[0]


def _ada_mod(c, w_ada, b_ada):
    L, D, N = w_ada.shape
    B = c.shape[0]
    tn = 1536
    return pl.pallas_call(
        _ada_kernel,
        grid=(L, N // tn),
        in_specs=[
            pl.BlockSpec((B, D), lambda l, j: (0, 0)),
            pl.BlockSpec((1, D, tn), lambda l, j: (l, 0, j)),
            pl.BlockSpec((1, 1, tn), lambda l, j: (l, 0, j)),
        ],
        out_specs=pl.BlockSpec((1, B, tn), lambda l, j: (l, 0, j)),
        out_shape=jax.ShapeDtypeStruct((L, B, N), F32),
        compiler_params=_cparams(("arbitrary", "arbitrary")),
        name="ada_mod",
    )(c, w_ada, b_ada.reshape(L, 1, N))


def _rope_table_kernel(pos_ref, freq_ref, o_ref):
    ang = pos_ref[0].astype(F32) * freq_ref[...]
    lane = lax.broadcasted_iota(jnp.int32, (1, LANES), 1)
    o_ref[0] = jnp.where(lane < QK_ROPE, jnp.cos(ang), jnp.sin(ang))


def _rope_table(positions):
    B, S = positions.shape
    half = QK_ROPE // 2
    inv_freq = ROPE_THETA ** (-jnp.arange(half, dtype=F32) / half)
    freq = jnp.tile(inv_freq, LANES // half).reshape(1, LANES)
    return pl.pallas_call(
        _rope_table_kernel,
        grid=(B,),
        in_specs=[
            pl.BlockSpec((1, S, 1), lambda b: (b, 0, 0)),
            pl.BlockSpec((1, LANES), lambda b: (0, 0)),
        ],
        out_specs=pl.BlockSpec((1, S, LANES), lambda b: (b, 0, 0)),
        out_shape=jax.ShapeDtypeStruct((B, S, LANES), F32),
        compiler_params=_cparams(("arbitrary",)),
        name="rope_table",
    )(positions.reshape(B, S, 1), freq)


def _rms(x, g):
    return x * lax.rsqrt(jnp.mean(x * x, axis=-1, keepdims=True) + RMS_EPS) * g


def _in_proj_kernel(x_ref, mod_ref, cs_ref, w_in_ref, qg_ref, kvg_ref, w_uqa_ref, w_ukv_ref,
                    u_ref, q_ref, k_ref, v_ref):
    x = x_ref[0]
    sh1 = mod_ref[0, 0:1, :]
    sc1 = mod_ref[0, 1:2, :]
    h = (x * (1.0 + sc1) + sh1).astype(BF16)
    proj = jnp.dot(h, w_in_ref[...], preferred_element_type=F32)
    u_ref[0] = proj[:, 0:POOL_WIDTH]
    o = POOL_WIDTH
    q_lat = proj[:, o:o + Q_LORA]
    o += Q_LORA
    kv_lat = proj[:, o:o + KV_LORA]
    o += KV_LORA
    cos_sin = cs_ref[0]

    def rotate(pair):
        t = pair * cos_sin
        return t + pltpu.roll(t, QK_ROPE, 1)

    k_rot = rotate(proj[:, o:o + LANES]).astype(BF16)
    lane = lax.broadcasted_iota(jnp.int32, (1, LANES), 1)
    q_keep = jnp.where(lane < QK_ROPE, Q_SCALE, 0.0)

    qn = _rms(q_lat, qg_ref[...]).astype(BF16)
    q_a = jnp.dot(qn, w_uqa_ref[...], preferred_element_type=F32)
    kvn = _rms(kv_lat, kvg_ref[...]).astype(BF16)
    kv = jnp.dot(kvn, w_ukv_ref[...], preferred_element_type=F32)
    for hd in range(N_HEADS):
        b0 = hd * QK_PAD
        q_ref[0, hd, :, 0:LANES] = (q_a[:, b0:b0 + LANES] * Q_SCALE).astype(BF16)
        q_ref[0, hd, :, LANES:QK_PAD] = (rotate(q_a[:, b0 + LANES:b0 + QK_PAD]) * q_keep).astype(BF16)
        k_ref[0, hd, :, 0:LANES] = kv[:, b0:b0 + QK_NOPE].astype(BF16)
        k_ref[0, hd, :, LANES:QK_PAD] = k_rot
        v_ref[0, hd] = kv[:, b0 + QK_NOPE:b0 + QK_NOPE + V_HEAD].astype(BF16)


def _in_proj(x, mod, cs, w_in_ext, qg, kvg, w_uqa, w_ukv_bf, B, x_off, b_off):
    _, S, D = x.shape
    tm = TM_PROJ
    const = lambda b, i: (0, 0)
    return pl.pallas_call(
        _in_proj_kernel,
        grid=(B, S // tm),
        in_specs=[
            pl.BlockSpec((1, tm, D), lambda b, i: (b + x_off, i, 0)),
            pl.BlockSpec((1, 6, D), lambda b, i: (b + b_off, 0, 0)),
            pl.BlockSpec((1, tm, LANES), lambda b, i: (b + b_off, i, 0)),
            pl.BlockSpec(w_in_ext.shape, const),
            pl.BlockSpec(qg.shape, const),
            pl.BlockSpec(kvg.shape, const),
            pl.BlockSpec(w_uqa.shape, const),
            pl.BlockSpec(w_ukv_bf.shape, const),
        ],
        out_specs=[
            pl.BlockSpec((1, tm, POOL_WIDTH), lambda b, i: (b, i, 0)),
            pl.BlockSpec((1, N_HEADS, tm, QK_PAD), lambda b, i: (b, 0, i, 0)),
            pl.BlockSpec((1, N_HEADS, tm, QK_PAD), lambda b, i: (b, 0, i, 0)),
            pl.BlockSpec((1, N_HEADS, tm, V_HEAD), lambda b, i: (b, 0, i, 0)),
        ],
        out_shape=[
            jax.ShapeDtypeStruct((B, S, POOL_WIDTH), F32),
            jax.ShapeDtypeStruct((B, N_HEADS, S, QK_PAD), BF16),
            jax.ShapeDtypeStruct((B, N_HEADS, S, QK_PAD), BF16),
            jax.ShapeDtypeStruct((B, N_HEADS, S, V_HEAD), BF16),
        ],
        compiler_params=_cparams(("arbitrary", "arbitrary")),
        name="in_proj",
    )(x, mod, cs, w_in_ext, qg, kvg, w_uqa, w_ukv_bf)


def _softmax_step(q, k, v, carry, mask):
    m, l, acc = carry
    s = lax.dot_general(q, k, (((1,), (1,)), ((), ())), preferred_element_type=F32)
    if mask is not None:
        s = jnp.where(mask, s, MASK_VALUE)
    m_new = jnp.maximum(m, jnp.max(s, axis=-1, keepdims=True))
    alpha = jnp.exp2(m - m_new)
    p = jnp.exp2(s - m_new)
    l_new = alpha * l + jnp.sum(p, axis=-1, keepdims=True)
    acc_new = alpha * acc + jnp.dot(p.astype(BF16), v, preferred_element_type=F32)
    return m_new, l_new, acc_new


def _attn_kernel(q_ref, k_ref, v_ref, o_ref):
    seq = q_ref.shape[2]
    row = lax.broadcasted_iota(jnp.int32, (TQ, TK), 0)
    col = lax.broadcasted_iota(jnp.int32, (TQ, TK), 1)
    diag = row >= col
    for i in range(seq // TQ):
        q = q_ref[0, 0, i * TQ:(i + 1) * TQ, :]
        carry = (jnp.full((TQ, 1), MASK_VALUE, F32), jnp.zeros((TQ, 1), F32), jnp.zeros((TQ, V_HEAD), F32))
        for j in range(i + 1):
            k = k_ref[0, 0, j * TK:(j + 1) * TK, :]
            v = v_ref[0, 0, j * TK:(j + 1) * TK, :]
            carry = _softmax_step(q, k, v, carry, diag if j == i else None)
        _, l, acc = carry
        o_ref[0, 0, i * TQ:(i + 1) * TQ, :] = (acc / l).astype(BF16)


def _attention(q, k, v):
    B, H, S, _ = q.shape
    return pl.pallas_call(
        _attn_kernel,
        grid=(B, H),
        in_specs=[
            pl.BlockSpec((1, 1, S, QK_PAD), lambda b, h: (b, h, 0, 0)),
            pl.BlockSpec((1, 1, S, QK_PAD), lambda b, h: (b, h, 0, 0)),
            pl.BlockSpec((1, 1, S, V_HEAD), lambda b, h: (b, h, 0, 0)),
        ],
        out_specs=pl.BlockSpec((1, 1, S, V_HEAD), lambda b, h: (b, h, 0, 0)),
        out_shape=jax.ShapeDtypeStruct((B, H, S, V_HEAD), BF16),
        compiler_params=_cparams(("arbitrary", "arbitrary")),
        name="attention",
    )(q, k, v)


def _layer_norm(v, g, b):
    mu = jnp.mean(v, axis=-1, keepdims=True)
    d = v - mu
    var = jnp.mean(d * d, axis=-1, keepdims=True)
    return d * lax.rsqrt(var + LN_EPS) * g + b


def _post_kernel(x_ref, u_ref, halo_ref, attn_ref, mod_ref, wpool_ref, pscale_ref, wout_p_ref, wout_a_ref,
                 lng_ref, lnb_ref, wr_hi_ref, wr_lo_ref, x1_ref, h2_ref, lt_ref):
    i = pl.program_id(1)
    tm = u_ref.shape[1]
    u = u_ref[0]
    halo = jnp.where(i > 0, halo_ref[0], 0.0)
    ext = jnp.concatenate([halo, u], axis=0)
    s2 = ext + pltpu.roll(ext, 1, 0)
    s4 = s2 + pltpu.roll(s2, 2, 0)
    s8 = s4 + pltpu.roll(s4, 4, 0)
    s16 = s8 + pltpu.roll(s8, 8, 0)
    lane = lax.broadcasted_iota(jnp.int32, (1, POOL_WIDTH), 1)
    win = jnp.where(lane < POOL_GROUP, s2,
                    jnp.where(lane < 2 * POOL_GROUP, s4, jnp.where(lane < 3 * POOL_GROUP, s8, s16)))
    win = win[POOL_HALO:, :]
    width = jnp.where(lane < POOL_GROUP, POOL_WINDOWS[0],
                      jnp.where(lane < 2 * POOL_GROUP, POOL_WINDOWS[1],
                                jnp.where(lane < 3 * POOL_GROUP, POOL_WINDOWS[2], POOL_WINDOWS[3])))
    t = i * tm + lax.broadcasted_iota(jnp.int32, (tm, 1), 0)
    count = jnp.minimum(t + 1, width).astype(F32)
    token_mix = win / count - u
    pooled = jnp.dot(token_mix.astype(BF16), wpool_ref[...], preferred_element_type=F32) * pscale_ref[...]
    mixed = jnp.dot(pooled.astype(BF16), wout_p_ref[...], preferred_element_type=F32)
    attn = jnp.concatenate([attn_ref[0, hd] for hd in range(N_HEADS)], axis=1)
    mixed += jnp.dot(attn, wout_a_ref[...], preferred_element_type=F32)

    g1 = mod_ref[0, 2:3, :]
    sh2 = mod_ref[0, 3:4, :]
    sc2 = mod_ref[0, 4:5, :]
    x1 = _layer_norm(DEEPNORM_ALPHA * x_ref[0] + g1 * mixed, lng_ref[...], lnb_ref[...])
    x1_ref[0] = x1
    h2 = x1 * (1.0 + sc2) + sh2
    h_hi, h_lo = _split_hi_lo(h2)
    h2_ref[0] = _pack_bf16_pairs(h_hi)
    nt = (((1,), (1,)), ((), ()))
    lt = lax.dot_general(wr_hi_ref[...], h_hi, nt, preferred_element_type=F32)
    lt += lax.dot_general(wr_hi_ref[...], h_lo, nt, preferred_element_type=F32)
    lt += lax.dot_general(wr_lo_ref[...], h_hi, nt, preferred_element_type=F32)
    lt_ref[...] = lt


def _post_attn(x, u, attn, mod, wpool_bd, pscale, wout_p, wout_a, lng, lnb, wr_hi, wr_lo, x_off, b_off):
    B, S, _ = u.shape
    D = x.shape[2]
    tm = TM_POST
    nt = S // tm
    const = lambda b, i: (0, 0)
    halo_blocks = tm // POOL_HALO
    return pl.pallas_call(
        _post_kernel,
        grid=(B, nt),
        in_specs=[
            pl.BlockSpec((1, tm, D), lambda b, i: (b + x_off, i, 0)),
            pl.BlockSpec((1, tm, POOL_WIDTH), lambda b, i: (b, i, 0)),
            pl.BlockSpec((1, POOL_HALO, POOL_WIDTH), lambda b, i: (b, jnp.maximum(i * halo_blocks - 1, 0), 0)),
            pl.BlockSpec((1, N_HEADS, tm, V_HEAD), lambda b, i: (b, 0, i, 0)),
            pl.BlockSpec((1, 6, D), lambda b, i: (b + b_off, 0, 0)),
            pl.BlockSpec(wpool_bd.shape, const),
            pl.BlockSpec(pscale.shape, const),
            pl.BlockSpec(wout_p.shape, const),
            pl.BlockSpec(wout_a.shape, const),
            pl.BlockSpec(lng.shape, const),
            pl.BlockSpec(lnb.shape, const),
            pl.BlockSpec(wr_hi.shape, const),
            pl.BlockSpec(wr_lo.shape, const),
        ],
        out_specs=[
            pl.BlockSpec((1, tm, D), lambda b, i: (b, i, 0)),
            pl.BlockSpec((1, tm, HALF), lambda b, i: (b, i, 0)),
            pl.BlockSpec((N_EXPERTS, tm), lambda b, i: (0, b * nt + i)),
        ],
        out_shape=[
            jax.ShapeDtypeStruct((B, S, D), F32),
            jax.ShapeDtypeStruct((B, S, HALF), jnp.int32),
            jax.ShapeDtypeStruct((N_EXPERTS, B * S), F32),
        ],
        compiler_params=_cparams(("arbitrary", "arbitrary")),
        name="post_attn",
    )(x, u, u, attn, mod, wpool_bd, pscale, wout_p, wout_a, lng, lnb, wr_hi, wr_lo)


def _select_first_max(vals, n_rounds, payload=None):
    work = list(vals)
    sel = [None] * len(vals)
    rounds = []
    for _ in range(n_rounds):
        m = functools.reduce(jnp.maximum, work)
        taken = None
        win_idx = jnp.zeros(m.shape, jnp.int32)
        win_val = jnp.zeros(m.shape, F32)
        for e in range(len(work)):
            hit = work[e] == m
            first = hit if taken is None else jnp.logical_and(hit, jnp.logical_not(taken))
            taken = hit if taken is None else jnp.logical_or(taken, hit)
            sel[e] = first if sel[e] is None else jnp.logical_or(sel[e], first)
            work[e] = jnp.where(first, -jnp.inf, work[e])
            if payload is not None:
                win_idx = jnp.where(first, e, win_idx)
                win_val = jnp.where(first, payload[e], win_val)
        rounds.append((win_idx, win_val))
    return sel, rounds


def _route_kernel(bias_ref, lt_ref, idx_ref, gate_ref):
    scores = [jax.nn.sigmoid(lt_ref[e]) for e in range(N_EXPERTS)]
    choice = [scores[e] + bias_ref[e] for e in range(N_EXPERTS)]
    group_score = []
    for g in range(N_GROUPS):
        vals = choice[g * GROUP_SIZE:(g + 1) * GROUP_SIZE]
        sel2, _ = _select_first_max(vals, 2)
        group_score.append(functools.reduce(
            jnp.add, [jnp.where(sel2[j], vals[j], 0.0) for j in range(GROUP_SIZE)]))
    group_sel, _ = _select_first_max(group_score, TOPK_GROUPS)
    masked = [jnp.where(group_sel[e // GROUP_SIZE], choice[e], -jnp.inf) for e in range(N_EXPERTS)]
    _, rounds = _select_first_max(masked, TOP_K, payload=scores)
    total = functools.reduce(jnp.add, [w for _, w in rounds])
    for k, (e_k, w_k) in enumerate(rounds):
        g_k = w_k / total * ROUTED_SCALE
        for r in range(ROUTE_ROWS):
            idx_ref[k:k + 1, r * LANES:(r + 1) * LANES] = e_k[r:r + 1, :]
            gate_ref[k:k + 1, r * LANES:(r + 1) * LANES] = g_k[r:r + 1, :]


def _route(lt3, bias):
    E, R, _ = lt3.shape
    out_spec = pl.BlockSpec((TOP_K, ROUTE_ROWS * LANES), lambda r: (0, r))
    return pl.pallas_call(
        _route_kernel,
        grid=(R // ROUTE_ROWS,),
        in_specs=[
            pl.BlockSpec(memory_space=pltpu.SMEM),
            pl.BlockSpec((E, ROUTE_ROWS, LANES), lambda r: (0, r, 0)),
        ],
        out_specs=[out_spec, out_spec],
        out_shape=[jax.ShapeDtypeStruct((TOP_K, R * LANES), jnp.int32),
                   jax.ShapeDtypeStruct((TOP_K, R * LANES), F32)],
        compiler_params=_cparams(("arbitrary",)),
        name="route",
    )(bias, lt3)


def _sc_mesh():
    return plsc.VectorSubcoreMesh(core_axis_name="c", subcore_axis_name="s",
                                  num_cores=SC_CORES, num_subcores=SC_SUBCORES)


def _sc_worker_id():
    return lax.axis_index("s") * SC_CORES + lax.axis_index("c")


def _sc_gather(table, idx):
    n = idx.shape[0]
    width = table.shape[1]
    workers = SC_CORES * SC_SUBCORES
    per_worker = n // workers
    n_win = per_worker // SC_WINDOW
    assert n == workers * n_win * SC_WINDOW

    @functools.partial(
        pl.kernel, mesh=_sc_mesh(),
        out_type=jax.ShapeDtypeStruct((n, width), table.dtype),
        scratch_types=[
            pltpu.VMEM((n_win, SC_WINDOW), jnp.int32),
            pltpu.VMEM((SC_WINDOW, width), table.dtype),
            pltpu.SemaphoreType.DMA,
        ],
    )
    def gather(table_hbm, idx_hbm, out_hbm, idx_v, rows_v, sem):
        wid = _sc_worker_id()
        pltpu.sync_copy(idx_hbm.at[wid], idx_v)

        @pl.loop(0, n_win)
        def _(j):
            pltpu.async_copy(table_hbm.at[idx_v.at[j]], rows_v, sem).wait()
            pltpu.sync_copy(rows_v, out_hbm.at[pl.ds(wid * per_worker + j * SC_WINDOW, SC_WINDOW)])

    return gather(table, idx.reshape(workers, n_win, SC_WINDOW))


def _sc_gather_sum(ys, dest_kt, gates_kt):
    K, T = dest_kt.shape
    half = ys.shape[1]
    workers = SC_CORES * SC_SUBCORES
    per_worker = T // workers
    wt = SC_SUM_TOKENS
    n_win = per_worker // wt
    n_idx = K * wt
    assert T == workers * n_win * wt and n_idx <= 128 and half % SC_LANES == 0
    dest_w = dest_kt.reshape(K, workers, n_win, wt).transpose(1, 2, 0, 3).reshape(workers, n_win, n_idx)
    gates_w = gates_kt.reshape(K, workers, n_win, wt).transpose(1, 2, 0, 3).reshape(workers * n_win * n_idx)

    @functools.partial(
        pl.kernel, mesh=_sc_mesh(),
        out_type=jax.ShapeDtypeStruct((T, 2 * half), F32),
        scratch_types=[
            pltpu.VMEM((n_win, n_idx), jnp.int32),
            pltpu.VMEM((n_idx, half), jnp.int32),
            pltpu.VMEM((n_idx,), F32),
            pltpu.VMEM((wt, 2 * half), F32),
            pltpu.SemaphoreType.DMA,
        ],
        compiler_params=pltpu.CompilerParams(needs_layout_passes=False),
    )
    def gather_sum(ys_hbm, dest_hbm, gates_hbm, out_hbm, idx_v, rows_v, gate_v, out_v, sem):
        wid = _sc_worker_id()
        pltpu.sync_copy(dest_hbm.at[wid], idx_v)

        @pl.loop(0, n_win)
        def _(j):
            pltpu.async_copy(ys_hbm.at[idx_v.at[j]], rows_v, sem).wait()
            pltpu.sync_copy(gates_hbm.at[pl.ds((wid * n_win + j) * n_idx, n_idx)], gate_v)

            @pl.loop(0, wt)
            def _(t):
                g = [plsc.load_gather(gate_v, [jnp.full((SC_LANES,), k * wt + t, jnp.int32)]) for k in range(K)]

                @plsc.parallel_loop(0, half // SC_LANES, unroll=2)
                def _(c):
                    col = pl.ds(c * SC_LANES, SC_LANES)
                    acc_lo = jnp.zeros((SC_LANES,), F32)
                    acc_hi = jnp.zeros((SC_LANES,), F32)
                    for k in range(K):
                        w = rows_v[k * wt + t, col]
                        acc_lo = acc_lo + g[k] * plsc.bitcast(w << 16, F32)
                        acc_hi = acc_hi + g[k] * plsc.bitcast(w & HIGH_HALF_MASK, F32)
                    out_v[t, col] = acc_lo
                    out_v[t, pl.ds(half + c * SC_LANES, SC_LANES)] = acc_hi

            pltpu.sync_copy(out_v, out_hbm.at[pl.ds(wid * per_worker + j * wt, wt)])

    return gather_sum(ys, dest_w, gates_w)


def _sc_dispatch(rows, dest_kt, n_slots):
    T, width = rows.shape
    K = dest_kt.shape[0]
    workers = SC_CORES * SC_SUBCORES
    per_worker = T // workers
    n_win = per_worker // SC_WINDOW
    assert T == workers * n_win * SC_WINDOW
    dest_w = dest_kt.reshape(K, workers, n_win, SC_WINDOW).transpose(1, 2, 0, 3)
    dest_w = dest_w.reshape(workers, n_win * K, SC_WINDOW)

    @functools.partial(
        pl.kernel, mesh=_sc_mesh(),
        out_type=jax.ShapeDtypeStruct((n_slots, width), rows.dtype),
        scratch_types=[
            pltpu.VMEM((n_win * K, SC_WINDOW), jnp.int32),
            pltpu.VMEM((SC_WINDOW, width), rows.dtype),
        ],
    )
    def dispatch(rows_hbm, dest_hbm, out_hbm, idx_v, rows_v):
        wid = _sc_worker_id()
        pltpu.sync_copy(dest_hbm.at[wid], idx_v)

        @pl.loop(0, n_win)
        def _(j):
            pltpu.sync_copy(rows_hbm.at[pl.ds(wid * per_worker + j * SC_WINDOW, SC_WINDOW)], rows_v)
            for k in range(K):
                pltpu.sync_copy(rows_v, out_hbm.at[idx_v.at[j * K + k]])

    return dispatch(rows, dest_w)


def _swiglu(h_lo, h_hi, wg_ref, wu_ref):
    def proj(w_ref):
        return (jnp.dot(h_lo, w_ref[:HALF, :], preferred_element_type=F32)
                + jnp.dot(h_hi, w_ref[HALF:, :], preferred_element_type=F32))
    gate = proj(wg_ref)
    up = proj(wu_ref)
    return (gate * jax.nn.sigmoid(gate) * up).astype(BF16)


def _expert_kernel(seg_first_ref, seg_blocks_ref, n_real_ref, xs_hbm, wg_ref, wu_ref, wd_ref, ys_hbm,
                   xbuf, ybuf, wg_bf, wu_bf, wd_bf, in_sems, out_sems):
    e = pl.program_id(0)
    first = seg_first_ref[e]
    count = seg_blocks_ref[e]
    total = n_real_ref[0]
    blk = DISPATCH_BLOCK
    ring = xbuf.shape[0]

    def in_copy(g):
        slot = g % ring
        return pltpu.make_async_copy(xs_hbm.at[pl.ds(g * blk, blk)], xbuf.at[slot], in_sems.at[slot])

    def out_copy(g):
        slot = g % ring
        return pltpu.make_async_copy(ybuf.at[slot], ys_hbm.at[pl.ds(g * blk, blk)], out_sems.at[slot])

    @pl.when(e == 0)
    def _():
        for g in range(ring - 1):
            @pl.when(g < total)
            def _():
                in_copy(g).start()

    @pl.when(count > 0)
    def _():
        wg_bf[...] = wg_ref[0, 0].astype(BF16)
        wu_bf[...] = wu_ref[0, 0].astype(BF16)
        wd_bf[...] = wd_ref[0, 0].astype(BF16)

    def block(g, _):
        slot = g % ring
        in_copy(g).wait()

        @pl.when(g + ring - 1 < total)
        def _():
            in_copy(g + ring - 1).start()

        @pl.when(g >= ring)
        def _():
            out_copy(g - ring).wait()

        lo, hi = _unpack_bf16_pairs(xbuf[slot])
        a = _swiglu(lo.astype(BF16), hi.astype(BF16), wg_bf, wu_bf)
        y = jnp.dot(a, wd_bf[...], preferred_element_type=F32)
        ybuf[slot] = _pack_bf16_pairs(y.astype(BF16))
        out_copy(g).start()
        return 0

    lax.fori_loop(first, first + count, block, 0)

    @pl.when(e == pl.num_programs(0) - 1)
    def _():
        for back in range(ring, 0, -1):
            @pl.when(total >= back)
            def _():
                out_copy(total - back).wait()


def _expert_mm(seg_first, seg_blocks, n_real, xs, wg, wu, wd, layer):
    P = xs.shape[0]
    D = wg.shape[2]
    w_map = lambda e, sf, sb, nr: (layer, e, 0, 0)
    grid_spec = pltpu.PrefetchScalarGridSpec(
        num_scalar_prefetch=3,
        grid=(N_EXPERTS,),
        in_specs=[
            pl.BlockSpec(memory_space=pl.ANY),
            pl.BlockSpec((1, 1, D, D_EXPERT), w_map),
            pl.BlockSpec((1, 1, D, D_EXPERT), w_map),
            pl.BlockSpec((1, 1, D_EXPERT, D), w_map),
        ],
        out_specs=pl.BlockSpec(memory_space=pl.ANY),
        scratch_shapes=[
            pltpu.VMEM((EXPERT_RING, DISPATCH_BLOCK, HALF), jnp.int32),
            pltpu.VMEM((EXPERT_RING, DISPATCH_BLOCK, HALF), jnp.int32),
            pltpu.VMEM((D, D_EXPERT), BF16), pltpu.VMEM((D, D_EXPERT), BF16), pltpu.VMEM((D_EXPERT, D), BF16),
            pltpu.SemaphoreType.DMA((EXPERT_RING,)), pltpu.SemaphoreType.DMA((EXPERT_RING,)),
        ],
    )
    return pl.pallas_call(
        _expert_kernel,
        grid_spec=grid_spec,
        out_shape=jax.ShapeDtypeStruct((P, HALF), jnp.int32),
        compiler_params=_cparams(("arbitrary",)),
        name="expert_mm",
    )(seg_first, seg_blocks, n_real, xs, wg, wu, wd)


def _rank_kernel(idx_ref, rank_ref, cnt_ref, tri_ref, carry_ref):
    i = pl.program_id(0)
    tm = idx_ref.shape[1]

    @pl.when(i == 0)
    def _():
        r = lax.broadcasted_iota(jnp.int32, (tm, tm), 0)
        c = lax.broadcasted_iota(jnp.int32, (tm, tm), 1)
        tri_ref[...] = (r < c).astype(BF16)
        carry_ref[...] = jnp.zeros(carry_ref.shape, F32)

    e_iota = lax.broadcasted_iota(jnp.int32, (N_EXPERTS, tm), 0)
    hits = [e_iota == idx_ref[k:k + 1, :] for k in range(TOP_K)]
    mask = functools.reduce(jnp.logical_or, hits).astype(F32)
    rank = jnp.dot(mask.astype(BF16), tri_ref[...], preferred_element_type=F32) + carry_ref[:, 0:1]
    for k in range(TOP_K):
        rank_ref[k:k + 1, :] = jnp.sum(jnp.where(hits[k], rank, 0.0), axis=0, keepdims=True).astype(jnp.int32)
    carry_ref[...] = carry_ref[...] + jnp.sum(mask, axis=1, keepdims=True)
    cnt_ref[...] = carry_ref[...]


def _rank(idx_kt):
    K, T = idx_kt.shape
    tm = 1024
    return pl.pallas_call(
        _rank_kernel,
        grid=(T // tm,),
        in_specs=[pl.BlockSpec((K, tm), lambda i: (0, i))],
        out_specs=[pl.BlockSpec((K, tm), lambda i: (0, i)),
                   pl.BlockSpec((N_EXPERTS, LANES), lambda i: (0, 0))],
        out_shape=[jax.ShapeDtypeStruct((K, T), jnp.int32),
                   jax.ShapeDtypeStruct((N_EXPERTS, LANES), F32)],
        scratch_shapes=[pltpu.VMEM((tm, tm), BF16), pltpu.VMEM((N_EXPERTS, LANES), F32)],
        compiler_params=_cparams(("arbitrary",)),
        name="rank",
    )(idx_kt)


def _dest_kernel(idx_ref, rank_ref, start_ref, dest_ref):
    tm = idx_ref.shape[1]
    e_iota = lax.broadcasted_iota(jnp.int32, (N_EXPERTS, tm), 0)
    start = start_ref[:, 0:1]
    for k in range(TOP_K):
        base = jnp.sum(jnp.where(e_iota == idx_ref[k:k + 1, :], start, 0.0), axis=0, keepdims=True)
        dest_ref[k:k + 1, :] = rank_ref[k:k + 1, :] + base.astype(jnp.int32)


def _dest(idx_kt, rank_kt, start):
    K, T = idx_kt.shape
    tm = 1024
    tok_spec = pl.BlockSpec((K, tm), lambda i: (0, i))
    return pl.pallas_call(
        _dest_kernel,
        grid=(T // tm,),
        in_specs=[tok_spec, tok_spec, pl.BlockSpec((N_EXPERTS, LANES), lambda i: (0, 0))],
        out_specs=tok_spec,
        out_shape=jax.ShapeDtypeStruct((K, T), jnp.int32),
        compiler_params=_cparams(("arbitrary",)),
        name="dest",
    )(idx_kt, rank_kt, start)


def _block_plan(counts):
    blk = DISPATCH_BLOCK
    seg_blocks = (counts.astype(jnp.int32) + blk - 1) // blk
    seg_end = jnp.cumsum(seg_blocks)
    seg_first = seg_end - seg_blocks
    return seg_first * blk, seg_first, seg_blocks, seg_end[-1:]


def _combine_kernel(h_ref, x1_ref, routed_ref, mod_ref, wsg_ref, wsu_ref, wsd_ref, lng_ref, lnb_ref, o_ref):
    lo, hi = _unpack_bf16_pairs(h_ref[...])
    a = _swiglu(lo.astype(BF16), hi.astype(BF16), wsg_ref, wsu_ref)
    ffn = jnp.dot(a, wsd_ref[...], preferred_element_type=F32) + routed_ref[...]
    g2 = mod_ref[0, 5:6, :]
    o_ref[...] = _layer_norm(DEEPNORM_ALPHA * x1_ref[...] + g2 * ffn, lng_ref[...], lnb_ref[...])


def _combine_into_kernel(h_ref, x1_ref, routed_ref, mod_ref, wsg_ref, wsu_ref, wsd_ref, lng_ref, lnb_ref,
                         prev_ref, o_ref):
    del prev_ref
    _combine_kernel(h_ref, x1_ref, routed_ref, mod_ref, wsg_ref, wsu_ref, wsd_ref, lng_ref, lnb_ref, o_ref)


def _combine(h2p, x1, routed, mod, wsg, wsu, wsd, lng, lnb, seq, b_off, in_off, out_rows, out_off, out_prev):
    n, D = routed.shape
    tm = TM_COMBINE
    per_seq = seq // tm
    in_blk = in_off // tm
    blk_off = out_off // tm
    const = lambda i: (0, 0)
    in_specs = [
        pl.BlockSpec((tm, HALF), lambda i: (i + in_blk, 0)),
        pl.BlockSpec((tm, D), lambda i: (i + in_blk, 0)),
        pl.BlockSpec((tm, D), lambda i: (i, 0)),
        pl.BlockSpec((1, 6, D), lambda i: ((i + in_blk) // per_seq + b_off, 0, 0)),
        pl.BlockSpec(wsg.shape, const),
        pl.BlockSpec(wsu.shape, const),
        pl.BlockSpec(wsd.shape, const),
        pl.BlockSpec(lng.shape, const),
        pl.BlockSpec(lnb.shape, const),
    ]
    args = [h2p, x1, routed, mod, wsg, wsu, wsd, lng, lnb]
    body = _combine_kernel
    aliases = {}
    if out_prev is not None:
        in_specs.append(pl.BlockSpec(memory_space=pl.ANY))
        args.append(out_prev)
        aliases = {len(args) - 1: 0}
        body = _combine_into_kernel
    return pl.pallas_call(
        body,
        grid=(n // tm,),
        in_specs=in_specs,
        out_specs=pl.BlockSpec((tm, D), lambda i: (i + blk_off, 0)),
        out_shape=jax.ShapeDtypeStruct((out_rows, D), F32),
        input_output_aliases=aliases,
        compiler_params=_cparams(("arbitrary",)),
        name="combine",
    )(*args)


def _rope_pair(w):
    half = QK_ROPE // 2
    return w, jnp.concatenate([-w[:, half:], w[:, :half]], axis=1)


def _prep_layer(w_in, w_uq, w_ukv, w_pool, w_out, w_router):
    lat = POOL_WIDTH + Q_LORA + KV_LORA
    k_a, k_b = _rope_pair(w_in[:, lat:lat + QK_ROPE])
    w_in_ext = jnp.concatenate([w_in[:, :lat], k_a, k_b], axis=1).astype(BF16)

    qa_cols = []
    per_head = QK_NOPE + QK_ROPE
    for hd in range(N_HEADS):
        w_h = w_uq[:, hd * per_head:(hd + 1) * per_head]
        r_a, r_b = _rope_pair(w_h[:, QK_NOPE:])
        qa_cols += [w_h[:, :QK_NOPE], r_a, r_b]
    w_uqa = jnp.concatenate(qa_cols, axis=1).astype(BF16)

    wpool_bd = jax.scipy.linalg.block_diag(*[w_pool[g] for g in range(len(POOL_WINDOWS))]).astype(BF16)
    wr_hi, wr_lo = _split_hi_lo(w_router.T)
    return dict(w_in_ext=w_in_ext, w_uqa=w_uqa, w_ukv=w_ukv.astype(BF16), wpool_bd=wpool_bd,
                wout_p=w_out[:POOL_WIDTH].astype(BF16), wout_a=w_out[POOL_WIDTH:].astype(BF16),
                wr_hi=wr_hi, wr_lo=wr_lo)


def kernel(x, c, positions, w_ada, b_ada, w_in, q_norm_g, kv_norm_g, w_uq, w_ukv, w_pool, pool_scale,
           w_out, ln1_g, ln1_b, w_router, router_bias, w_exp_gate, w_exp_up, w_exp_down,
           w_sh_gate, w_sh_up, w_sh_down, ln2_g, ln2_b):
    B, S, D = x.shape
    L = w_in.shape[0]
    row = lambda v: v.reshape(1, -1)

    mod_all = _ada_mod(c, w_ada, b_ada).reshape(L, B, 6, D)
    cs_all = _rope_table(positions)
    bs = B // N_STREAMS
    T = bs * S
    n_blocks = T * TOP_K // DISPATCH_BLOCK + N_EXPERTS
    xs_streams = [x] * N_STREAMS
    result = None
    for l in range(L):
        p = _prep_layer(w_in[l], w_uq[l], w_ukv[l], w_pool[l], w_out[l], w_router[l])
        shared = (w_sh_gate[l].astype(BF16), w_sh_up[l].astype(BF16), w_sh_down[l].astype(BF16))
        mod = mod_all[l]
        for s in range(N_STREAMS):
            xl = xs_streams[s]
            b_off = s * bs
            x_off = b_off if l == 0 else 0
            u, q, k, v = _in_proj(xl, mod, cs_all, p["w_in_ext"], row(q_norm_g[l]), row(kv_norm_g[l]),
                                  p["w_uqa"], p["w_ukv"], bs, x_off, b_off)
            attn = _attention(q, k, v)
            x1, h2p, lt = _post_attn(xl, u, attn, mod, p["wpool_bd"], row(pool_scale[l]), p["wout_p"],
                                     p["wout_a"], row(ln1_g[l]), row(ln1_b[l]), p["wr_hi"], p["wr_lo"],
                                     x_off, b_off)
            idx_kt, gates_kt = _route(lt.reshape(N_EXPERTS, T // LANES, LANES), router_bias[l])
            rank_kt, counts = _rank(idx_kt)
            pad_start, seg_first, seg_blocks, n_real = _block_plan(counts[:, 0])
            start = jnp.broadcast_to(pad_start.astype(F32)[:, None], (N_EXPERTS, LANES))
            dest_kt = _dest(idx_kt, rank_kt, start)
            h2p = h2p.reshape(T, HALF)
            rows = _sc_dispatch(h2p, dest_kt, n_blocks * DISPATCH_BLOCK)
            ys = _expert_mm(seg_first, seg_blocks, n_real, rows, w_exp_gate, w_exp_up, w_exp_down, l)
            tc = T // COMBINE_CHUNKS
            out = None
            for ch in range(COMBINE_CHUNKS):
                lo = ch * tc
                routed = _sc_gather_sum(ys, dest_kt[:, lo:lo + tc], gates_kt[:, lo:lo + tc])
                if l < L - 1:
                    out = _combine(h2p, x1.reshape(T, D), routed, mod, *shared, row(ln2_g[l]), row(ln2_b[l]),
                                   S, b_off, lo, T, lo, out)
                else:
                    result = _combine(h2p, x1.reshape(T, D), routed, mod, *shared, row(ln2_g[l]),
                                      row(ln2_b[l]), S, b_off, lo, B * S, s * T + lo, result)
            if l < L - 1:
                xs_streams[s] = out.reshape(bs, S, D)
    return result.reshape(B, S, D)
```

```python
import functools

import jax
import jax.numpy as jnp
from jax import lax
from jax.experimental import pallas as pl
from jax.experimental.pallas import tpu as pltpu
from jax.experimental.pallas import tpu_sc as plsc

F32 = jnp.float32
BF16 = jnp.bfloat16

D_MODEL = 1024
DEPTH = 4
POOL_WIDTH = 256
POOL_WINDOWS = (2, 4, 8, 16)
POOL_GROUP = 64
POOL_HALO = 16
QK_NOPE = 128
QK_ROPE = 64
V_HEAD = 128
N_HEADS = 6
Q_LORA = 384
KV_LORA = 256
ROPE_THETA = 10000.0
N_EXPERTS = 64
TOP_K = 8
N_GROUPS = 8
GROUP_SIZE = N_EXPERTS // N_GROUPS
TOPK_GROUPS = 4
D_EXPERT = 256
ROUTED_SCALE = 2.5
DEEPNORM_ALPHA = (2.0 * DEPTH) ** 0.25
LN_EPS = 1e-5
RMS_EPS = 1e-6
ATTN_SCALE = (QK_NOPE + QK_ROPE) ** -0.5
LOG2_E = 1.4426950408889634
Q_SCALE = ATTN_SCALE * LOG2_E
LANES = 128
QK_PAD = 2 * LANES
MASK_VALUE = -1e30

TM_PROJ = 512
TM_POST = 1024
TQ = 512
TK = 512
ATTN_HEADS = 2
ROUTE_ROWS = 8
DISPATCH_BLOCK = 512
EXPERT_RING = 6
TM_COMBINE = 1024
SC_CORES = 2
SC_SUBCORES = 16
SC_WINDOW = 128
SC_LANES = 16
HIGH_HALF_MASK = -65536
SC_SUM_TOKENS = 16
N_STREAMS = 2
COMBINE_CHUNKS = 2
HALF = D_MODEL // 2
VMEM_LIMIT = 48 * 1024 * 1024


def _cparams(sem):
    return pltpu.CompilerParams(dimension_semantics=sem, vmem_limit_bytes=VMEM_LIMIT)


def _split_hi_lo(a):
    hi = a.astype(BF16)
    lo = (a - hi.astype(F32)).astype(BF16)
    return hi, lo


def _pack_bf16_pairs(a):
    bits = lax.bitcast_convert_type(a.astype(F32), jnp.uint32)
    half = a.shape[1] // 2
    word = (bits[:, :half] >> 16) | (bits[:, half:] & jnp.uint32(0xFFFF0000))
    return lax.bitcast_convert_type(word, jnp.int32)


def _unpack_bf16_pairs(w):
    bits = lax.bitcast_convert_type(w, jnp.uint32)
    lo = lax.bitcast_convert_type(bits << 16, F32)
    hi = lax.bitcast_convert_type(bits & jnp.uint32(0xFFFF0000), F32)
    return lo, hi


def _ada_kernel(c_ref, w_ref, b_ref, o_ref):
    c = c_ref[...]
    cond = c * jax.nn.sigmoid(c)
    c_hi, c_lo = _split_hi_lo(cond)
    w_hi, w_lo = _split_hi_lo(w_ref[0])
    acc = jnp.dot(c_hi, w_hi, preferred_element_type=F32)
    acc += jnp.dot(c_lo, w_hi, preferred_element_type=F32)
    acc += jnp.dot(c_hi, w_lo, preferred_element_type=F32)
    o_ref[0] = acc + b_ref[0]


def _ada_mod(c, w_ada, b_ada):
    L, D, N = w_ada.shape
    B = c.shape[0]
    tn = 1536
    return pl.pallas_call(
        _ada_kernel,
        grid=(L, N // tn),
        in_specs=[
            pl.BlockSpec((B, D), lambda l, j: (0, 0)),
            pl.BlockSpec((1, D, tn), lambda l, j: (l, 0, j)),
            pl.BlockSpec((1, 1, tn), lambda l, j: (l, 0, j)),
        ],
        out_specs=pl.BlockSpec((1, B, tn), lambda l, j: (l, 0, j)),
        out_shape=jax.ShapeDtypeStruct((L, B, N), F32),
        compiler_params=_cparams(("arbitrary", "arbitrary")),
        name="ada_mod",
    )(c, w_ada, b_ada.reshape(L, 1, N))


def _rope_table_kernel(pos_ref, freq_ref, o_ref):
    ang = pos_ref[0].astype(F32) * freq_ref[...]
    lane = lax.broadcasted_iota(jnp.int32, (1, LANES), 1)
    o_ref[0] = jnp.where(lane < QK_ROPE, jnp.cos(ang), jnp.sin(ang))


def _rope_table(positions):
    B, S = positions.shape
    half = QK_ROPE // 2
    inv_freq = ROPE_THETA ** (-jnp.arange(half, dtype=F32) / half)
    freq = jnp.tile(inv_freq, LANES // half).reshape(1, LANES)
    return pl.pallas_call(
        _rope_table_kernel,
        grid=(B,),
        in_specs=[
            pl.BlockSpec((1, S, 1), lambda b: (b, 0, 0)),
            pl.BlockSpec((1, LANES), lambda b: (0, 0)),
        ],
        out_specs=pl.BlockSpec((1, S, LANES), lambda b: (b, 0, 0)),
        out_shape=jax.ShapeDtypeStruct((B, S, LANES), F32),
        compiler_params=_cparams(("arbitrary",)),
        name="rope_table",
    )(positions.reshape(B, S, 1), freq)


def _rms(x, g):
    return x * lax.rsqrt(jnp.mean(x * x, axis=-1, keepdims=True) + RMS_EPS) * g


def _in_proj_kernel(x_ref, mod_ref, cs_ref, w_in_ref, qg_ref, kvg_ref, w_uqa_ref, w_ukv_ref,
                    u_ref, q_ref, k_ref, v_ref):
    x = x_ref[0]
    sh1 = mod_ref[0, 0:1, :]
    sc1 = mod_ref[0, 1:2, :]
    h = (x * (1.0 + sc1) + sh1).astype(BF16)
    proj = jnp.dot(h, w_in_ref[...], preferred_element_type=F32)
    u_ref[0] = proj[:, 0:POOL_WIDTH]
    o = POOL_WIDTH
    q_lat = proj[:, o:o + Q_LORA]
    o += Q_LORA
    kv_lat = proj[:, o:o + KV_LORA]
    o += KV_LORA
    cos_sin = cs_ref[0]

    def rotate(pair):
        t = pair * cos_sin
        return t + pltpu.roll(t, QK_ROPE, 1)

    k_rot = rotate(proj[:, o:o + LANES]).astype(BF16)
    lane = lax.broadcasted_iota(jnp.int32, (1, LANES), 1)
    q_keep = jnp.where(lane < QK_ROPE, Q_SCALE, 0.0)

    qn = _rms(q_lat, qg_ref[...]).astype(BF16)
    q_a = jnp.dot(qn, w_uqa_ref[...], preferred_element_type=F32)
    kvn = _rms(kv_lat, kvg_ref[...]).astype(BF16)
    kv = jnp.dot(kvn, w_ukv_ref[...], preferred_element_type=F32)
    for hd in range(N_HEADS):
        b0 = hd * QK_PAD
        q_ref[0, hd, :, 0:LANES] = (q_a[:, b0:b0 + LANES] * Q_SCALE).astype(BF16)
        q_ref[0, hd, :, LANES:QK_PAD] = (rotate(q_a[:, b0 + LANES:b0 + QK_PAD]) * q_keep).astype(BF16)
        k_ref[0, hd, :, 0:LANES] = kv[:, b0:b0 + QK_NOPE].astype(BF16)
        k_ref[0, hd, :, LANES:QK_PAD] = k_rot
        v_ref[0, hd] = kv[:, b0 + QK_NOPE:b0 + QK_NOPE + V_HEAD].astype(BF16)


def _in_proj(x, mod, cs, w_in_ext, qg, kvg, w_uqa, w_ukv_bf, B, x_off, b_off):
    _, S, D = x.shape
    tm = TM_PROJ
    const = lambda b, i: (0, 0)
    return pl.pallas_call(
        _in_proj_kernel,
        grid=(B, S // tm),
        in_specs=[
            pl.BlockSpec((1, tm, D), lambda b, i: (b + x_off, i, 0)),
            pl.BlockSpec((1, 6, D), lambda b, i: (b + b_off, 0, 0)),
            pl.BlockSpec((1, tm, LANES), lambda b, i: (b + b_off, i, 0)),
            pl.BlockSpec(w_in_ext.shape, const),
            pl.BlockSpec(qg.shape, const),
            pl.BlockSpec(kvg.shape, const),
            pl.BlockSpec(w_uqa.shape, const),
            pl.BlockSpec(w_ukv_bf.shape, const),
        ],
        out_specs=[
            pl.BlockSpec((1, tm, POOL_WIDTH), lambda b, i: (b, i, 0)),
            pl.BlockSpec((1, N_HEADS, tm, QK_PAD), lambda b, i: (b, 0, i, 0)),
            pl.BlockSpec((1, N_HEADS, tm, QK_PAD), lambda b, i: (b, 0, i, 0)),
            pl.BlockSpec((1, N_HEADS, tm, V_HEAD), lambda b, i: (b, 0, i, 0)),
        ],
        out_shape=[
            jax.ShapeDtypeStruct((B, S, POOL_WIDTH), F32),
            jax.ShapeDtypeStruct((B, N_HEADS, S, QK_PAD), BF16),
            jax.ShapeDtypeStruct((B, N_HEADS, S, QK_PAD), BF16),
            jax.ShapeDtypeStruct((B, N_HEADS, S, V_HEAD), BF16),
        ],
        compiler_params=_cparams(("arbitrary", "arbitrary")),
        name="in_proj",
    )(x, mod, cs, w_in_ext, qg, kvg, w_uqa, w_ukv_bf)


def _softmax_step(q, k, v, carry, mask):
    m, l, acc = carry
    s = lax.dot_general(q, k, (((1,), (1,)), ((), ())), preferred_element_type=F32)
    if mask is not None:
        s = jnp.where(mask, s, MASK_VALUE)
    m_new = jnp.maximum(m, jnp.max(s, axis=-1, keepdims=True))
    alpha = jnp.exp2(m - m_new)
    p = jnp.exp2(s - m_new)
    l_new = alpha * l + jnp.sum(p, axis=-1, keepdims=True)
    acc_new = alpha * acc + jnp.dot(p.astype(BF16), v, preferred_element_type=F32)
    return m_new, l_new, acc_new


def _attn_kernel(q_ref, k_ref, v_ref, o_ref):
    heads, seq = q_ref.shape[1], q_ref.shape[2]
    row = lax.broadcasted_iota(jnp.int32, (TQ, TK), 0)
    col = lax.broadcasted_iota(jnp.int32, (TQ, TK), 1)
    diag = row >= col
    for i in range(seq // TQ):
        qs = [q_ref[0, h, i * TQ:(i + 1) * TQ, :] for h in range(heads)]
        carries = [(jnp.full((TQ, 1), MASK_VALUE, F32), jnp.zeros((TQ, 1), F32), jnp.zeros((TQ, V_HEAD), F32))
                   for _ in range(heads)]
        for j in range(i + 1):
            for h in range(heads):
                k = k_ref[0, h, j * TK:(j + 1) * TK, :]
                v = v_ref[0, h, j * TK:(j + 1) * TK, :]
                carries[h] = _softmax_step(qs[h], k, v, carries[h], diag if j == i else None)
        for h in range(heads):
            _, l, acc = carries[h]
            o_ref[0, h, i * TQ:(i + 1) * TQ, :] = (acc / l).astype(BF16)


def _attention(q, k, v):
    B, H, S, _ = q.shape
    return pl.pallas_call(
        _attn_kernel,
        grid=(B, H // ATTN_HEADS),
        in_specs=[
            pl.BlockSpec((1, ATTN_HEADS, S, QK_PAD), lambda b, h: (b, h, 0, 0)),
            pl.BlockSpec((1, ATTN_HEADS, S, QK_PAD), lambda b, h: (b, h, 0, 0)),
            pl.BlockSpec((1, ATTN_HEADS, S, V_HEAD), lambda b, h: (b, h, 0, 0)),
        ],
        out_specs=pl.BlockSpec((1, ATTN_HEADS, S, V_HEAD), lambda b, h: (b, h, 0, 0)),
        out_shape=jax.ShapeDtypeStruct((B, H, S, V_HEAD), BF16),
        compiler_params=_cparams(("arbitrary", "arbitrary")),
        name="attention",
    )(q, k, v)


def _layer_norm(v, g, b):
    mu = jnp.mean(v, axis=-1, keepdims=True)
    d = v - mu
    var = jnp.mean(d * d, axis=-1, keepdims=True)
    return d * lax.rsqrt(var + LN_EPS) * g + b


def _post_kernel(x_ref, u_ref, halo_ref, attn_ref, mod_ref, wpool_ref, pscale_ref, wout_p_ref, wout_a_ref,
                 lng_ref, lnb_ref, wr_hi_ref, wr_lo_ref, x1_ref, h2_ref, lt_ref):
    i = pl.program_id(1)
    tm = u_ref.shape[1]
    u = u_ref[0]
    halo = jnp.where(i > 0, halo_ref[0], 0.0)
    ext = jnp.concatenate([halo, u], axis=0)
    s2 = ext + pltpu.roll(ext, 1, 0)
    s4 = s2 + pltpu.roll(s2, 2, 0)
    s8 = s4 + pltpu.roll(s4, 4, 0)
    s16 = s8 + pltpu.roll(s8, 8, 0)
    lane = lax.broadcasted_iota(jnp.int32, (1, POOL_WIDTH), 1)
    win = jnp.where(lane < POOL_GROUP, s2,
                    jnp.where(lane < 2 * POOL_GROUP, s4, jnp.where(lane < 3 * POOL_GROUP, s8, s16)))
    win = win[POOL_HALO:, :]
    width = jnp.where(lane < POOL_GROUP, POOL_WINDOWS[0],
                      jnp.where(lane < 2 * POOL_GROUP, POOL_WINDOWS[1],
                                jnp.where(lane < 3 * POOL_GROUP, POOL_WINDOWS[2], POOL_WINDOWS[3])))
    t = i * tm + lax.broadcasted_iota(jnp.int32, (tm, 1), 0)
    count = jnp.minimum(t + 1, width).astype(F32)
    token_mix = win / count - u
    pooled = jnp.dot(token_mix.astype(BF16), wpool_ref[...], preferred_element_type=F32) * pscale_ref[...]
    mixed = jnp.dot(pooled.astype(BF16), wout_p_ref[...], preferred_element_type=F32)
    attn = jnp.concatenate([attn_ref[0, hd] for hd in range(N_HEADS)], axis=1)
    mixed += jnp.dot(attn, wout_a_ref[...], preferred_element_type=F32)

    g1 = mod_ref[0, 2:3, :]
    sh2 = mod_ref[0, 3:4, :]
    sc2 = mod_ref[0, 4:5, :]
    x1 = _layer_norm(DEEPNORM_ALPHA * x_ref[0] + g1 * mixed, lng_ref[...], lnb_ref[...])
    x1_ref[0] = x1
    h2 = x1 * (1.0 + sc2) + sh2
    h_hi, h_lo = _split_hi_lo(h2)
    h2_ref[0] = _pack_bf16_pairs(h_hi)
    nt = (((1,), (1,)), ((), ()))
    lt = lax.dot_general(wr_hi_ref[...], h_hi, nt, preferred_element_type=F32)
    lt += lax.dot_general(wr_hi_ref[...], h_lo, nt, preferred_element_type=F32)
    lt += lax.dot_general(wr_lo_ref[...], h_hi, nt, preferred_element_type=F32)
    lt_ref[...] = lt


def _post_attn(x, u, attn, mod, wpool_bd, pscale, wout_p, wout_a, lng, lnb, wr_hi, wr_lo, x_off, b_off):
    B, S, _ = u.shape
    D = x.shape[2]
    tm = TM_POST
    nt = S // tm
    const = lambda b, i: (0, 0)
    halo_blocks = tm // POOL_HALO
    return pl.pallas_call(
        _post_kernel,
        grid=(B, nt),
        in_specs=[
            pl.BlockSpec((1, tm, D), lambda b, i: (b + x_off, i, 0)),
            pl.BlockSpec((1, tm, POOL_WIDTH), lambda b, i: (b, i, 0)),
            pl.BlockSpec((1, POOL_HALO, POOL_WIDTH), lambda b, i: (b, jnp.maximum(i * halo_blocks - 1, 0), 0)),
            pl.BlockSpec((1, N_HEADS, tm, V_HEAD), lambda b, i: (b, 0, i, 0)),
            pl.BlockSpec((1, 6, D), lambda b, i: (b + b_off, 0, 0)),
            pl.BlockSpec(wpool_bd.shape, const),
            pl.BlockSpec(pscale.shape, const),
            pl.BlockSpec(wout_p.shape, const),
            pl.BlockSpec(wout_a.shape, const),
            pl.BlockSpec(lng.shape, const),
            pl.BlockSpec(lnb.shape, const),
            pl.BlockSpec(wr_hi.shape, const),
            pl.BlockSpec(wr_lo.shape, const),
        ],
        out_specs=[
            pl.BlockSpec((1, tm, D), lambda b, i: (b, i, 0)),
            pl.BlockSpec((1, tm, HALF), lambda b, i: (b, i, 0)),
            pl.BlockSpec((N_EXPERTS, tm), lambda b, i: (0, b * nt + i)),
        ],
        out_shape=[
            jax.ShapeDtypeStruct((B, S, D), F32),
            jax.ShapeDtypeStruct((B, S, HALF), jnp.int32),
            jax.ShapeDtypeStruct((N_EXPERTS, B * S), F32),
        ],
        compiler_params=_cparams(("arbitrary", "arbitrary")),
        name="post_attn",
    )(x, u, u, attn, mod, wpool_bd, pscale, wout_p, wout_a, lng, lnb, wr_hi, wr_lo)


def _select_first_max(vals, n_rounds, payload=None):
    work = list(vals)
    sel = [None] * len(vals)
    rounds = []
    for _ in range(n_rounds):
        m = functools.reduce(jnp.maximum, work)
        taken = None
        win_idx = jnp.zeros(m.shape, jnp.int32)
        win_val = jnp.zeros(m.shape, F32)
        for e in range(len(work)):
            hit = work[e] == m
            first = hit if taken is None else jnp.logical_and(hit, jnp.logical_not(taken))
            taken = hit if taken is None else jnp.logical_or(taken, hit)
            sel[e] = first if sel[e] is None else jnp.logical_or(sel[e], first)
            work[e] = jnp.where(first, -jnp.inf, work[e])
            if payload is not None:
                win_idx = jnp.where(first, e, win_idx)
                win_val = jnp.where(first, payload[e], win_val)
        rounds.append((win_idx, win_val))
    return sel, rounds


def _route_kernel(bias_ref, lt_ref, idx_ref, gate_ref):
    scores = [jax.nn.sigmoid(lt_ref[e]) for e in range(N_EXPERTS)]
    choice = [scores[e] + bias_ref[e] for e in range(N_EXPERTS)]
    group_score = []
    for g in range(N_GROUPS):
        vals = choice[g * GROUP_SIZE:(g + 1) * GROUP_SIZE]
        sel2, _ = _select_first_max(vals, 2)
        group_score.append(functools.reduce(
            jnp.add, [jnp.where(sel2[j], vals[j], 0.0) for j in range(GROUP_SIZE)]))
    group_sel, _ = _select_first_max(group_score, TOPK_GROUPS)
    masked = [jnp.where(group_sel[e // GROUP_SIZE], choice[e], -jnp.inf) for e in range(N_EXPERTS)]
    _, rounds = _select_first_max(masked, TOP_K, payload=scores)
    total = functools.reduce(jnp.add, [w for _, w in rounds])
    for k, (e_k, w_k) in enumerate(rounds):
        g_k = w_k / total * ROUTED_SCALE
        for r in range(ROUTE_ROWS):
            idx_ref[k:k + 1, r * LANES:(r + 1) * LANES] = e_k[r:r + 1, :]
            gate_ref[k:k + 1, r * LANES:(r + 1) * LANES] = g_k[r:r + 1, :]


def _route(lt3, bias):
    E, R, _ = lt3.shape
    out_spec = pl.BlockSpec((TOP_K, ROUTE_ROWS * LANES), lambda r: (0, r))
    return pl.pallas_call(
        _route_kernel,
        grid=(R // ROUTE_ROWS,),
        in_specs=[
            pl.BlockSpec(memory_space=pltpu.SMEM),
            pl.BlockSpec((E, ROUTE_ROWS, LANES), lambda r: (0, r, 0)),
        ],
        out_specs=[out_spec, out_spec],
        out_shape=[jax.ShapeDtypeStruct((TOP_K, R * LANES), jnp.int32),
                   jax.ShapeDtypeStruct((TOP_K, R * LANES), F32)],
        compiler_params=_cparams(("arbitrary",)),
        name="route",
    )(bias, lt3)


def _sc_mesh():
    return plsc.VectorSubcoreMesh(core_axis_name="c", subcore_axis_name="s",
                                  num_cores=SC_CORES, num_subcores=SC_SUBCORES)


def _sc_worker_id():
    return lax.axis_index("s") * SC_CORES + lax.axis_index("c")


def _sc_gather(table, idx):
    n = idx.shape[0]
    width = table.shape[1]
    workers = SC_CORES * SC_SUBCORES
    per_worker = n // workers
    n_win = per_worker // SC_WINDOW
    assert n == workers * n_win * SC_WINDOW

    @functools.partial(
        pl.kernel, mesh=_sc_mesh(),
        out_type=jax.ShapeDtypeStruct((n, width), table.dtype),
        scratch_types=[
            pltpu.VMEM((n_win, SC_WINDOW), jnp.int32),
            pltpu.VMEM((SC_WINDOW, width), table.dtype),
            pltpu.SemaphoreType.DMA,
        ],
    )
    def gather(table_hbm, idx_hbm, out_hbm, idx_v, rows_v, sem):
        wid = _sc_worker_id()
        pltpu.sync_copy(idx_hbm.at[wid], idx_v)

        @pl.loop(0, n_win)
        def _(j):
            pltpu.async_copy(table_hbm.at[idx_v.at[j]], rows_v, sem).wait()
            pltpu.sync_copy(rows_v, out_hbm.at[pl.ds(wid * per_worker + j * SC_WINDOW, SC_WINDOW)])

    return gather(table, idx.reshape(workers, n_win, SC_WINDOW))


def _sc_gather_sum(ys, dest_kt, gates_kt):
    K, T = dest_kt.shape
    half = ys.shape[1]
    workers = SC_CORES * SC_SUBCORES
    per_worker = T // workers
    wt = SC_SUM_TOKENS
    n_win = per_worker // wt
    n_idx = K * wt
    assert T == workers * n_win * wt and n_idx <= 128 and half % SC_LANES == 0
    dest_w = dest_kt.reshape(K, workers, n_win, wt).transpose(1, 2, 0, 3).reshape(workers, n_win, n_idx)
    gates_w = gates_kt.reshape(K, workers, n_win, wt).transpose(1, 2, 0, 3).reshape(workers * n_win * n_idx)

    @functools.partial(
        pl.kernel, mesh=_sc_mesh(),
        out_type=jax.ShapeDtypeStruct((T, 2 * half), F32),
        scratch_types=[
            pltpu.VMEM((n_win, n_idx), jnp.int32),
            pltpu.VMEM((n_idx, half), jnp.int32),
            pltpu.VMEM((n_idx,), F32),
            pltpu.VMEM((wt, 2 * half), F32),
            pltpu.SemaphoreType.DMA,
        ],
        compiler_params=pltpu.CompilerParams(needs_layout_passes=False),
    )
    def gather_sum(ys_hbm, dest_hbm, gates_hbm, out_hbm, idx_v, rows_v, gate_v, out_v, sem):
        wid = _sc_worker_id()
        pltpu.sync_copy(dest_hbm.at[wid], idx_v)

        @pl.loop(0, n_win)
        def _(j):
            pltpu.async_copy(ys_hbm.at[idx_v.at[j]], rows_v, sem).wait()
            pltpu.sync_copy(gates_hbm.at[pl.ds((wid * n_win + j) * n_idx, n_idx)], gate_v)

            @pl.loop(0, wt)
            def _(t):
                g = [plsc.load_gather(gate_v, [jnp.full((SC_LANES,), k * wt + t, jnp.int32)]) for k in range(K)]

                @plsc.parallel_loop(0, half // SC_LANES, unroll=4)
                def _(c):
                    col = pl.ds(c * SC_LANES, SC_LANES)
                    acc_lo = jnp.zeros((SC_LANES,), F32)
                    acc_hi = jnp.zeros((SC_LANES,), F32)
                    for k in range(K):
                        w = rows_v[k * wt + t, col]
                        acc_lo = acc_lo + g[k] * plsc.bitcast(w << 16, F32)
                        acc_hi = acc_hi + g[k] * plsc.bitcast(w & HIGH_HALF_MASK, F32)
                    out_v[t, col] = acc_lo
                    out_v[t, pl.ds(half + c * SC_LANES, SC_LANES)] = acc_hi

            pltpu.sync_copy(out_v, out_hbm.at[pl.ds(wid * per_worker + j * wt, wt)])

    return gather_sum(ys, dest_w, gates_w)


def _sc_dispatch(rows, dest_kt, n_slots):
    T, width = rows.shape
    K = dest_kt.shape[0]
    workers = SC_CORES * SC_SUBCORES
    per_worker = T // workers
    n_win = per_worker // SC_WINDOW
    assert T == workers * n_win * SC_WINDOW
    dest_w = dest_kt.reshape(K, workers, n_win, SC_WINDOW).transpose(1, 2, 0, 3)
    dest_w = dest_w.reshape(workers, n_win * K, SC_WINDOW)

    @functools.partial(
        pl.kernel, mesh=_sc_mesh(),
        out_type=jax.ShapeDtypeStruct((n_slots, width), rows.dtype),
        scratch_types=[
            pltpu.VMEM((n_win * K, SC_WINDOW), jnp.int32),
            pltpu.VMEM((SC_WINDOW, width), rows.dtype),
        ],
    )
    def dispatch(rows_hbm, dest_hbm, out_hbm, idx_v, rows_v):
        wid = _sc_worker_id()
        pltpu.sync_copy(dest_hbm.at[wid], idx_v)

        @pl.loop(0, n_win)
        def _(j):
            pltpu.sync_copy(rows_hbm.at[pl.ds(wid * per_worker + j * SC_WINDOW, SC_WINDOW)], rows_v)
            for k in range(K):
                pltpu.sync_copy(rows_v, out_hbm.at[idx_v.at[j * K + k]])

    return dispatch(rows, dest_w)


def _swiglu(h_lo, h_hi, wg_ref, wu_ref):
    def proj(w_ref):
        return (jnp.dot(h_lo, w_ref[:HALF, :], preferred_element_type=F32)
                + jnp.dot(h_hi, w_ref[HALF:, :], preferred_element_type=F32))
    gate = proj(wg_ref)
    up = proj(wu_ref)
    return (gate * jax.nn.sigmoid(gate) * up).astype(BF16)


def _expert_kernel(seg_first_ref, seg_blocks_ref, n_real_ref, xs_hbm, wg_ref, wu_ref, wd_ref, ys_hbm,
                   xbuf, ybuf, wg_bf, wu_bf, wd_bf, in_sems, out_sems):
    e = pl.program_id(0)
    first = seg_first_ref[e]
    count = seg_blocks_ref[e]
    total = n_real_ref[0]
    blk = DISPATCH_BLOCK
    ring = xbuf.shape[0]

    def in_copy(g):
        slot = g % ring
        return pltpu.make_async_copy(xs_hbm.at[pl.ds(g * blk, blk)], xbuf.at[slot], in_sems.at[slot])

    def out_copy(g):
        slot = g % ring
        return pltpu.make_async_copy(ybuf.at[slot], ys_hbm.at[pl.ds(g * blk, blk)], out_sems.at[slot])

    @pl.when(e == 0)
    def _():
        for g in range(ring - 1):
            @pl.when(g < total)
            def _():
                in_copy(g).start()

    @pl.when(count > 0)
    def _():
        wg_bf[...] = wg_ref[0, 0].astype(BF16)
        wu_bf[...] = wu_ref[0, 0].astype(BF16)
        wd_bf[...] = wd_ref[0, 0].astype(BF16)

    def block(g, _):
        slot = g % ring
        in_copy(g).wait()

        @pl.when(g + ring - 1 < total)
        def _():
            in_copy(g + ring - 1).start()

        @pl.when(g >= ring)
        def _():
            out_copy(g - ring).wait()

        lo, hi = _unpack_bf16_pairs(xbuf[slot])
        a = _swiglu(lo.astype(BF16), hi.astype(BF16), wg_bf, wu_bf)
        y = jnp.dot(a, wd_bf[...], preferred_element_type=F32)
        ybuf[slot] = _pack_bf16_pairs(y.astype(BF16))
        out_copy(g).start()
        return 0

    lax.fori_loop(first, first + count, block, 0)

    @pl.when(e == pl.num_programs(0) - 1)
    def _():
        for back in range(ring, 0, -1):
            @pl.when(total >= back)
            def _():
                out_copy(total - back).wait()


def _expert_mm(seg_first, seg_blocks, n_real, xs, wg, wu, wd, layer):
    P = xs.shape[0]
    D = wg.shape[2]
    w_map = lambda e, sf, sb, nr: (layer, e, 0, 0)
    grid_spec = pltpu.PrefetchScalarGridSpec(
        num_scalar_prefetch=3,
        grid=(N_EXPERTS,),
        in_specs=[
            pl.BlockSpec(memory_space=pl.ANY),
            pl.BlockSpec((1, 1, D, D_EXPERT), w_map),
            pl.BlockSpec((1, 1, D, D_EXPERT), w_map),
            pl.BlockSpec((1, 1, D_EXPERT, D), w_map),
        ],
        out_specs=pl.BlockSpec(memory_space=pl.ANY),
        scratch_shapes=[
            pltpu.VMEM((EXPERT_RING, DISPATCH_BLOCK, HALF), jnp.int32),
            pltpu.VMEM((EXPERT_RING, DISPATCH_BLOCK, HALF), jnp.int32),
            pltpu.VMEM((D, D_EXPERT), BF16), pltpu.VMEM((D, D_EXPERT), BF16), pltpu.VMEM((D_EXPERT, D), BF16),
            pltpu.SemaphoreType.DMA((EXPERT_RING,)), pltpu.SemaphoreType.DMA((EXPERT_RING,)),
        ],
    )
    return pl.pallas_call(
        _expert_kernel,
        grid_spec=grid_spec,
        out_shape=jax.ShapeDtypeStruct((P, HALF), jnp.int32),
        compiler_params=_cparams(("arbitrary",)),
        name="expert_mm",
    )(seg_first, seg_blocks, n_real, xs, wg, wu, wd)


def _rank_kernel(idx_ref, rank_ref, cnt_ref, tri_ref, carry_ref):
    i = pl.program_id(0)
    tm = idx_ref.shape[1]

    @pl.when(i == 0)
    def _():
        r = lax.broadcasted_iota(jnp.int32, (tm, tm), 0)
        c = lax.broadcasted_iota(jnp.int32, (tm, tm), 1)
        tri_ref[...] = (r < c).astype(BF16)
        carry_ref[...] = jnp.zeros(carry_ref.shape, F32)

    e_iota = lax.broadcasted_iota(jnp.int32, (N_EXPERTS, tm), 0)
    hits = [e_iota == idx_ref[k:k + 1, :] for k in range(TOP_K)]
    mask = functools.reduce(jnp.logical_or, hits).astype(F32)
    rank = jnp.dot(mask.astype(BF16), tri_ref[...], preferred_element_type=F32) + carry_ref[:, 0:1]
    for k in range(TOP_K):
        rank_ref[k:k + 1, :] = jnp.sum(jnp.where(hits[k], rank, 0.0), axis=0, keepdims=True).astype(jnp.int32)
    carry_ref[...] = carry_ref[...] + jnp.sum(mask, axis=1, keepdims=True)
    cnt_ref[...] = carry_ref[...]


def _rank(idx_kt):
    K, T = idx_kt.shape
    tm = 1024
    return pl.pallas_call(
        _rank_kernel,
        grid=(T // tm,),
        in_specs=[pl.BlockSpec((K, tm), lambda i: (0, i))],
        out_specs=[pl.BlockSpec((K, tm), lambda i: (0, i)),
                   pl.BlockSpec((N_EXPERTS, LANES), lambda i: (0, 0))],
        out_shape=[jax.ShapeDtypeStruct((K, T), jnp.int32),
                   jax.ShapeDtypeStruct((N_EXPERTS, LANES), F32)],
        scratch_shapes=[pltpu.VMEM((tm, tm), BF16), pltpu.VMEM((N_EXPERTS, LANES), F32)],
        compiler_params=_cparams(("arbitrary",)),
        name="rank",
    )(idx_kt)


def _dest_kernel(idx_ref, rank_ref, start_ref, dest_ref):
    tm = idx_ref.shape[1]
    e_iota = lax.broadcasted_iota(jnp.int32, (N_EXPERTS, tm), 0)
    start = start_ref[:, 0:1]
    for k in range(TOP_K):
        base = jnp.sum(jnp.where(e_iota == idx_ref[k:k + 1, :], start, 0.0), axis=0, keepdims=True)
        dest_ref[k:k + 1, :] = rank_ref[k:k + 1, :] + base.astype(jnp.int32)


def _dest(idx_kt, rank_kt, start):
    K, T = idx_kt.shape
    tm = 1024
    tok_spec = pl.BlockSpec((K, tm), lambda i: (0, i))
    return pl.pallas_call(
        _dest_kernel,
        grid=(T // tm,),
        in_specs=[tok_spec, tok_spec, pl.BlockSpec((N_EXPERTS, LANES), lambda i: (0, 0))],
        out_specs=tok_spec,
        out_shape=jax.ShapeDtypeStruct((K, T), jnp.int32),
        compiler_params=_cparams(("arbitrary",)),
        name="dest",
    )(idx_kt, rank_kt, start)


def _block_plan(counts):
    blk = DISPATCH_BLOCK
    seg_blocks = (counts.astype(jnp.int32) + blk - 1) // blk
    seg_end = jnp.cumsum(seg_blocks)
    seg_first = seg_end - seg_blocks
    return seg_first * blk, seg_first, seg_blocks, seg_end[-1:]


def _combine_kernel(h_ref, x1_ref, routed_ref, mod_ref, wsg_ref, wsu_ref, wsd_ref, lng_ref, lnb_ref, o_ref):
    lo, hi = _unpack_bf16_pairs(h_ref[...])
    a = _swiglu(lo.astype(BF16), hi.astype(BF16), wsg_ref, wsu_ref)
    ffn = jnp.dot(a, wsd_ref[...], preferred_element_type=F32) + routed_ref[...]
    g2 = mod_ref[0, 5:6, :]
    o_ref[...] = _layer_norm(DEEPNORM_ALPHA * x1_ref[...] + g2 * ffn, lng_ref[...], lnb_ref[...])


def _combine_into_kernel(h_ref, x1_ref, routed_ref, mod_ref, wsg_ref, wsu_ref, wsd_ref, lng_ref, lnb_ref,
                         prev_ref, o_ref):
    del prev_ref
    _combine_kernel(h_ref, x1_ref, routed_ref, mod_ref, wsg_ref, wsu_ref, wsd_ref, lng_ref, lnb_ref, o_ref)


def _combine(h2p, x1, routed, mod, wsg, wsu, wsd, lng, lnb, seq, b_off, in_off, out_rows, out_off, out_prev):
    n, D = routed.shape
    tm = TM_COMBINE
    per_seq = seq // tm
    in_blk = in_off // tm
    blk_off = out_off // tm
    const = lambda i: (0, 0)
    in_specs = [
        pl.BlockSpec((tm, HALF), lambda i: (i + in_blk, 0)),
        pl.BlockSpec((tm, D), lambda i: (i + in_blk, 0)),
        pl.BlockSpec((tm, D), lambda i: (i, 0)),
        pl.BlockSpec((1, 6, D), lambda i: ((i + in_blk) // per_seq + b_off, 0, 0)),
        pl.BlockSpec(wsg.shape, const),
        pl.BlockSpec(wsu.shape, const),
        pl.BlockSpec(wsd.shape, const),
        pl.BlockSpec(lng.shape, const),
        pl.BlockSpec(lnb.shape, const),
    ]
    args = [h2p, x1, routed, mod, wsg, wsu, wsd, lng, lnb]
    body = _combine_kernel
    aliases = {}
    if out_prev is not None:
        in_specs.append(pl.BlockSpec(memory_space=pl.ANY))
        args.append(out_prev)
        aliases = {len(args) - 1: 0}
        body = _combine_into_kernel
    return pl.pallas_call(
        body,
        grid=(n // tm,),
        in_specs=in_specs,
        out_specs=pl.BlockSpec((tm, D), lambda i: (i + blk_off, 0)),
        out_shape=jax.ShapeDtypeStruct((out_rows, D), F32),
        input_output_aliases=aliases,
        compiler_params=_cparams(("arbitrary",)),
        name="combine",
    )(*args)


def _rope_pair(w):
    half = QK_ROPE // 2
    return w, jnp.concatenate([-w[:, half:], w[:, :half]], axis=1)


def _prep_layer(w_in, w_uq, w_ukv, w_pool, w_out, w_router):
    lat = POOL_WIDTH + Q_LORA + KV_LORA
    k_a, k_b = _rope_pair(w_in[:, lat:lat + QK_ROPE])
    w_in_ext = jnp.concatenate([w_in[:, :lat], k_a, k_b], axis=1).astype(BF16)

    qa_cols = []
    per_head = QK_NOPE + QK_ROPE
    for hd in range(N_HEADS):
        w_h = w_uq[:, hd * per_head:(hd + 1) * per_head]
        r_a, r_b = _rope_pair(w_h[:, QK_NOPE:])
        qa_cols += [w_h[:, :QK_NOPE], r_a, r_b]
    w_uqa = jnp.concatenate(qa_cols, axis=1).astype(BF16)

    wpool_bd = jax.scipy.linalg.block_diag(*[w_pool[g] for g in range(len(POOL_WINDOWS))]).astype(BF16)
    wr_hi, wr_lo = _split_hi_lo(w_router.T)
    return dict(w_in_ext=w_in_ext, w_uqa=w_uqa, w_ukv=w_ukv.astype(BF16), wpool_bd=wpool_bd,
                wout_p=w_out[:POOL_WIDTH].astype(BF16), wout_a=w_out[POOL_WIDTH:].astype(BF16),
                wr_hi=wr_hi, wr_lo=wr_lo)


def kernel(x, c, positions, w_ada, b_ada, w_in, q_norm_g, kv_norm_g, w_uq, w_ukv, w_pool, pool_scale,
           w_out, ln1_g, ln1_b, w_router, router_bias, w_exp_gate, w_exp_up, w_exp_down,
           w_sh_gate, w_sh_up, w_sh_down, ln2_g, ln2_b):
    B, S, D = x.shape
    L = w_in.shape[0]
    row = lambda v: v.reshape(1, -1)

    mod_all = _ada_mod(c, w_ada, b_ada).reshape(L, B, 6, D)
    cs_all = _rope_table(positions)
    bs = B // N_STREAMS
    T = bs * S
    n_blocks = T * TOP_K // DISPATCH_BLOCK + N_EXPERTS
    xs_streams = [x] * N_STREAMS
    result = None
    for l in range(L):
        p = _prep_layer(w_in[l], w_uq[l], w_ukv[l], w_pool[l], w_out[l], w_router[l])
        shared = (w_sh_gate[l].astype(BF16), w_sh_up[l].astype(BF16), w_sh_down[l].astype(BF16))
        mod = mod_all[l]
        for s in range(N_STREAMS):
            xl = xs_streams[s]
            b_off = s * bs
            x_off = b_off if l == 0 else 0
            u, q, k, v = _in_proj(xl, mod, cs_all, p["w_in_ext"], row(q_norm_g[l]), row(kv_norm_g[l]),
                                  p["w_uqa"], p["w_ukv"], bs, x_off, b_off)
            attn = _attention(q, k, v)
            x1, h2p, lt = _post_attn(xl, u, attn, mod, p["wpool_bd"], row(pool_scale[l]), p["wout_p"],
                                     p["wout_a"], row(ln1_g[l]), row(ln1_b[l]), p["wr_hi"], p["wr_lo"],
                                     x_off, b_off)
            idx_kt, gates_kt = _route(lt.reshape(N_EXPERTS, T // LANES, LANES), router_bias[l])
            rank_kt, counts = _rank(idx_kt)
            pad_start, seg_first, seg_blocks, n_real = _block_plan(counts[:, 0])
            start = jnp.broadcast_to(pad_start.astype(F32)[:, None], (N_EXPERTS, LANES))
            dest_kt = _dest(idx_kt, rank_kt, start)
            h2p = h2p.reshape(T, HALF)
            rows = _sc_dispatch(h2p, dest_kt, n_blocks * DISPATCH_BLOCK)
            ys = _expert_mm(seg_first, seg_blocks, n_real, rows, w_exp_gate, w_exp_up, w_exp_down, l)
            tc = T // COMBINE_CHUNKS
            out = None
            for ch in range(COMBINE_CHUNKS):
                lo = ch * tc
                routed = _sc_gather_sum(ys, dest_kt[:, lo:lo + tc], gates_kt[:, lo:lo + tc])
                if l < L - 1:
                    out = _combine(h2p, x1.reshape(T, D), routed, mod, *shared, row(ln2_g[l]), row(ln2_b[l]),
                                   S, b_off, lo, T, lo, out)
                else:
                    result = _combine(h2p, x1.reshape(T, D), routed, mod, *shared, row(ln2_g[l]),
                                      row(ln2_b[l]), S, b_off, lo, B * S, s * T + lo, result)
            if l < L - 1:
                xs_streams[s] = out.reshape(bs, S, D)
    return result.reshape(B, S, D)
```

```python
import functools

import jax
import jax.numpy as jnp
from jax import lax
from jax.experimental import pallas as pl
from jax.experimental.pallas import tpu as pltpu
from jax.experimental.pallas import tpu_sc as plsc

F32 = jnp.float32
BF16 = jnp.bfloat16

D_MODEL = 1024
DEPTH = 4
POOL_WIDTH = 256
POOL_WINDOWS = (2, 4, 8, 16)
POOL_GROUP = 64
POOL_HALO = 16
QK_NOPE = 128
QK_ROPE = 64
V_HEAD = 128
N_HEADS = 6
Q_LORA = 384
KV_LORA = 256
ROPE_THETA = 10000.0
N_EXPERTS = 64
TOP_K = 8
N_GROUPS = 8
GROUP_SIZE = N_EXPERTS // N_GROUPS
TOPK_GROUPS = 4
D_EXPERT = 256
ROUTED_SCALE = 2.5
DEEPNORM_ALPHA = (2.0 * DEPTH) ** 0.25
LN_EPS = 1e-5
RMS_EPS = 1e-6
ATTN_SCALE = (QK_NOPE + QK_ROPE) ** -0.5
LOG2_E = 1.4426950408889634
Q_SCALE = ATTN_SCALE * LOG2_E
LANES = 128
QK_PAD = 2 * LANES
MASK_VALUE = -1e30

TM_PROJ = 512
TM_POST = 1024
TQ = 512
TK = 512
ATTN_HEADS = 2
ROUTE_ROWS = 8
DISPATCH_BLOCK = 512
EXPERT_RING = 6
TM_COMBINE = 1024
SC_CORES = 2
SC_SUBCORES = 16
SC_WINDOW = 128
SC_LANES = 16
HIGH_HALF_MASK = -65536
SC_SUM_TOKENS = 16
N_STREAMS = 2
COMBINE_CHUNKS = 2
HALF = D_MODEL // 2
VMEM_LIMIT = 48 * 1024 * 1024


def _cparams(sem):
    return pltpu.CompilerParams(dimension_semantics=sem, vmem_limit_bytes=VMEM_LIMIT)


def _split_hi_lo(a):
    hi = a.astype(BF16)
    lo = (a - hi.astype(F32)).astype(BF16)
    return hi, lo


def _pack_bf16_pairs(a):
    bits = lax.bitcast_convert_type(a.astype(F32), jnp.uint32)
    half = a.shape[1] // 2
    word = (bits[:, :half] >> 16) | (bits[:, half:] & jnp.uint32(0xFFFF0000))
    return lax.bitcast_convert_type(word, jnp.int32)


def _unpack_bf16_pairs(w):
    bits = lax.bitcast_convert_type(w, jnp.uint32)
    lo = lax.bitcast_convert_type(bits << 16, F32)
    hi = lax.bitcast_convert_type(bits & jnp.uint32(0xFFFF0000), F32)
    return lo, hi


def _ada_kernel(c_ref, w_ref, b_ref, o_ref):
    c = c_ref[...]
    cond = c * jax.nn.sigmoid(c)
    c_hi, c_lo = _split_hi_lo(cond)
    w_hi, w_lo = _split_hi_lo(w_ref[0])
    acc = jnp.dot(c_hi, w_hi, preferred_element_type=F32)
    acc += jnp.dot(c_lo, w_hi, preferred_element_type=F32)
    acc += jnp.dot(c_hi, w_lo, preferred_element_type=F32)
    o_ref[0] = acc + b_ref[0]


def _ada_mod(c, w_ada, b_ada):
    L, D, N = w_ada.shape
    B = c.shape[0]
    tn = 1536
    return pl.pallas_call(
        _ada_kernel,
        grid=(L, N // tn),
        in_specs=[
            pl.BlockSpec((B, D), lambda l, j: (0, 0)),
            pl.BlockSpec((1, D, tn), lambda l, j: (l, 0, j)),
            pl.BlockSpec((1, 1, tn), lambda l, j: (l, 0, j)),
        ],
        out_specs=pl.BlockSpec((1, B, tn), lambda l, j: (l, 0, j)),
        out_shape=jax.ShapeDtypeStruct((L, B, N), F32),
        compiler_params=_cparams(("arbitrary", "arbitrary")),
        name="ada_mod",
    )(c, w_ada, b_ada.reshape(L, 1, N))


def _rope_table_kernel(pos_ref, freq_ref, o_ref):
    ang = pos_ref[0].astype(F32) * freq_ref[...]
    lane = lax.broadcasted_iota(jnp.int32, (1, LANES), 1)
    o_ref[0] = jnp.where(lane < QK_ROPE, jnp.cos(ang), jnp.sin(ang))


def _rope_table(positions):
    B, S = positions.shape
    half = QK_ROPE // 2
    inv_freq = ROPE_THETA ** (-jnp.arange(half, dtype=F32) / half)
    freq = jnp.tile(inv_freq, LANES // half).reshape(1, LANES)
    return pl.pallas_call(
        _rope_table_kernel,
        grid=(B,),
        in_specs=[
            pl.BlockSpec((1, S, 1), lambda b: (b, 0, 0)),
            pl.BlockSpec((1, LANES), lambda b: (0, 0)),
        ],
        out_specs=pl.BlockSpec((1, S, LANES), lambda b: (b, 0, 0)),
        out_shape=jax.ShapeDtypeStruct((B, S, LANES), F32),
        compiler_params=_cparams(("arbitrary",)),
        name="rope_table",
    )(positions.reshape(B, S, 1), freq)


def _rms(x, g):
    return x * lax.rsqrt(jnp.mean(x * x, axis=-1, keepdims=True) + RMS_EPS) * g


def _in_proj_kernel(x_ref, mod_ref, cs_ref, w_in_ref, qg_ref, kvg_ref, w_uqa_ref, w_ukv_ref,
                    u_ref, q_ref, k_ref, v_ref):
    x = x_ref[0]
    sh1 = mod_ref[0, 0:1, :]
    sc1 = mod_ref[0, 1:2, :]
    h = (x * (1.0 + sc1) + sh1).astype(BF16)
    proj = jnp.dot(h, w_in_ref[...], preferred_element_type=F32)
    u_ref[0] = proj[:, 0:POOL_WIDTH]
    o = POOL_WIDTH
    q_lat = proj[:, o:o + Q_LORA]
    o += Q_LORA
    kv_lat = proj[:, o:o + KV_LORA]
    o += KV_LORA
    cos_sin = cs_ref[0]

    def rotate(pair):
        t = pair * cos_sin
        return t + pltpu.roll(t, QK_ROPE, 1)

    k_rot = rotate(proj[:, o:o + LANES]).astype(BF16)
    lane = lax.broadcasted_iota(jnp.int32, (1, LANES), 1)
    q_keep = jnp.where(lane < QK_ROPE, Q_SCALE, 0.0)

    qn = _rms(q_lat, qg_ref[...]).astype(BF16)
    q_a = jnp.dot(qn, w_uqa_ref[...], preferred_element_type=F32)
    kvn = _rms(kv_lat, kvg_ref[...]).astype(BF16)
    kv = jnp.dot(kvn, w_ukv_ref[...], preferred_element_type=F32)
    for hd in range(N_HEADS):
        b0 = hd * QK_PAD
        q_ref[0, hd, :, 0:LANES] = (q_a[:, b0:b0 + LANES] * Q_SCALE).astype(BF16)
        q_ref[0, hd, :, LANES:QK_PAD] = (rotate(q_a[:, b0 + LANES:b0 + QK_PAD]) * q_keep).astype(BF16)
        k_ref[0, hd, :, 0:LANES] = kv[:, b0:b0 + QK_NOPE].astype(BF16)
        k_ref[0, hd, :, LANES:QK_PAD] = k_rot
        v_ref[0, hd] = kv[:, b0 + QK_NOPE:b0 + QK_NOPE + V_HEAD].astype(BF16)


def _in_proj(x, mod, cs, w_in_ext, qg, kvg, w_uqa, w_ukv_bf, B, x_off, b_off):
    _, S, D = x.shape
    tm = TM_PROJ
    const = lambda b, i: (0, 0)
    return pl.pallas_call(
        _in_proj_kernel,
        grid=(B, S // tm),
        in_specs=[
            pl.BlockSpec((1, tm, D), lambda b, i: (b + x_off, i, 0)),
            pl.BlockSpec((1, 6, D), lambda b, i: (b + b_off, 0, 0)),
            pl.BlockSpec((1, tm, LANES), lambda b, i: (b + b_off, i, 0)),
            pl.BlockSpec(w_in_ext.shape, const),
            pl.BlockSpec(qg.shape, const),
            pl.BlockSpec(kvg.shape, const),
            pl.BlockSpec(w_uqa.shape, const),
            pl.BlockSpec(w_ukv_bf.shape, const),
        ],
        out_specs=[
            pl.BlockSpec((1, tm, POOL_WIDTH), lambda b, i: (b, i, 0)),
            pl.BlockSpec((1, N_HEADS, tm, QK_PAD), lambda b, i: (b, 0, i, 0)),
            pl.BlockSpec((1, N_HEADS, tm, QK_PAD), lambda b, i: (b, 0, i, 0)),
            pl.BlockSpec((1, N_HEADS, tm, V_HEAD), lambda b, i: (b, 0, i, 0)),
        ],
        out_shape=[
            jax.ShapeDtypeStruct((B, S, POOL_WIDTH), F32),
            jax.ShapeDtypeStruct((B, N_HEADS, S, QK_PAD), BF16),
            jax.ShapeDtypeStruct((B, N_HEADS, S, QK_PAD), BF16),
            jax.ShapeDtypeStruct((B, N_HEADS, S, V_HEAD), BF16),
        ],
        compiler_params=_cparams(("arbitrary", "arbitrary")),
        name="in_proj",
    )(x, mod, cs, w_in_ext, qg, kvg, w_uqa, w_ukv_bf)


def _softmax_step(q, k, v, carry, mask):
    m, l, acc = carry
    s = lax.dot_general(q, k, (((1,), (1,)), ((), ())), preferred_element_type=F32)
    if mask is not None:
        s = jnp.where(mask, s, MASK_VALUE)
    m_new = jnp.maximum(m, jnp.max(s, axis=-1, keepdims=True))
    alpha = jnp.exp2(m - m_new)
    p = jnp.exp2(s - m_new)
    l_new = alpha * l + jnp.sum(p, axis=-1, keepdims=True)
    acc_new = alpha * acc + jnp.dot(p.astype(BF16), v, preferred_element_type=F32)
    return m_new, l_new, acc_new


def _attn_kernel(q_ref, k_ref, v_ref, o_ref):
    heads, seq = q_ref.shape[1], q_ref.shape[2]
    row = lax.broadcasted_iota(jnp.int32, (TQ, TK), 0)
    col = lax.broadcasted_iota(jnp.int32, (TQ, TK), 1)
    diag = row >= col
    for i in range(seq // TQ):
        qs = [q_ref[0, h, i * TQ:(i + 1) * TQ, :] for h in range(heads)]
        carries = [(jnp.full((TQ, 1), MASK_VALUE, F32), jnp.zeros((TQ, 1), F32), jnp.zeros((TQ, V_HEAD), F32))
                   for _ in range(heads)]
        for j in range(i + 1):
            for h in range(heads):
                k = k_ref[0, h, j * TK:(j + 1) * TK, :]
                v = v_ref[0, h, j * TK:(j + 1) * TK, :]
                carries[h] = _softmax_step(qs[h], k, v, carries[h], diag if j == i else None)
        for h in range(heads):
            _, l, acc = carries[h]
            o_ref[0, h, i * TQ:(i + 1) * TQ, :] = (acc / l).astype(BF16)


def _attention(q, k, v):
    B, H, S, _ = q.shape
    return pl.pallas_call(
        _attn_kernel,
        grid=(B, H // ATTN_HEADS),
        in_specs=[
            pl.BlockSpec((1, ATTN_HEADS, S, QK_PAD), lambda b, h: (b, h, 0, 0)),
            pl.BlockSpec((1, ATTN_HEADS, S, QK_PAD), lambda b, h: (b, h, 0, 0)),
            pl.BlockSpec((1, ATTN_HEADS, S, V_HEAD), lambda b, h: (b, h, 0, 0)),
        ],
        out_specs=pl.BlockSpec((1, ATTN_HEADS, S, V_HEAD), lambda b, h: (b, h, 0, 0)),
        out_shape=jax.ShapeDtypeStruct((B, H, S, V_HEAD), BF16),
        compiler_params=_cparams(("arbitrary", "arbitrary")),
        name="attention",
    )(q, k, v)


def _layer_norm(v, g, b):
    mu = jnp.mean(v, axis=-1, keepdims=True)
    d = v - mu
    var = jnp.mean(d * d, axis=-1, keepdims=True)
    return d * lax.rsqrt(var + LN_EPS) * g + b


def _post_kernel(x_ref, u_ref, halo_ref, attn_ref, mod_ref, wpool_ref, pscale_ref, wout_p_ref, wout_a_ref,
                 lng_ref, lnb_ref, wr_hi_ref, wr_lo_ref, x1_ref, h2_ref, lt_ref):
    i = pl.program_id(1)
    tm = u_ref.shape[1]
    u = u_ref[0]
    halo = jnp.where(i > 0, halo_ref[0], 0.0)
    ext = jnp.concatenate([halo, u], axis=0)
    s2 = ext + pltpu.roll(ext, 1, 0)
    s4 = s2 + pltpu.roll(s2, 2, 0)
    s8 = s4 + pltpu.roll(s4, 4, 0)
    s16 = s8 + pltpu.roll(s8, 8, 0)
    lane = lax.broadcasted_iota(jnp.int32, (1, POOL_WIDTH), 1)
    win = jnp.where(lane < POOL_GROUP, s2,
                    jnp.where(lane < 2 * POOL_GROUP, s4, jnp.where(lane < 3 * POOL_GROUP, s8, s16)))
    win = win[POOL_HALO:, :]
    width = jnp.where(lane < POOL_GROUP, POOL_WINDOWS[0],
                      jnp.where(lane < 2 * POOL_GROUP, POOL_WINDOWS[1],
                                jnp.where(lane < 3 * POOL_GROUP, POOL_WINDOWS[2], POOL_WINDOWS[3])))
    t = i * tm + lax.broadcasted_iota(jnp.int32, (tm, 1), 0)
    count = jnp.minimum(t + 1, width).astype(F32)
    token_mix = win / count - u
    pooled = jnp.dot(token_mix.astype(BF16), wpool_ref[...], preferred_element_type=F32) * pscale_ref[...]
    mixed = jnp.dot(pooled.astype(BF16), wout_p_ref[...], preferred_element_type=F32)
    attn = jnp.concatenate([attn_ref[0, hd] for hd in range(N_HEADS)], axis=1)
    mixed += jnp.dot(attn, wout_a_ref[...], preferred_element_type=F32)

    g1 = mod_ref[0, 2:3, :]
    sh2 = mod_ref[0, 3:4, :]
    sc2 = mod_ref[0, 4:5, :]
    x1 = _layer_norm(DEEPNORM_ALPHA * x_ref[0] + g1 * mixed, lng_ref[...], lnb_ref[...])
    x1_ref[0] = x1
    h2 = x1 * (1.0 + sc2) + sh2
    h_hi, h_lo = _split_hi_lo(h2)
    h2_ref[0] = _pack_bf16_pairs(h_hi)
    nt = (((1,), (1,)), ((), ()))
    lt = lax.dot_general(wr_hi_ref[...], h_hi, nt, preferred_element_type=F32)
    lt += lax.dot_general(wr_hi_ref[...], h_lo, nt, preferred_element_type=F32)
    lt += lax.dot_general(wr_lo_ref[...], h_hi, nt, preferred_element_type=F32)
    lt_ref[...] = lt


def _post_attn(x, u, attn, mod, wpool_bd, pscale, wout_p, wout_a, lng, lnb, wr_hi, wr_lo, x_off, b_off):
    B, S, _ = u.shape
    D = x.shape[2]
    tm = TM_POST
    nt = S // tm
    const = lambda b, i: (0, 0)
    halo_blocks = tm // POOL_HALO
    return pl.pallas_call(
        _post_kernel,
        grid=(B, nt),
        in_specs=[
            pl.BlockSpec((1, tm, D), lambda b, i: (b + x_off, i, 0)),
            pl.BlockSpec((1, tm, POOL_WIDTH), lambda b, i: (b, i, 0)),
            pl.BlockSpec((1, POOL_HALO, POOL_WIDTH), lambda b, i: (b, jnp.maximum(i * halo_blocks - 1, 0), 0)),
            pl.BlockSpec((1, N_HEADS, tm, V_HEAD), lambda b, i: (b, 0, i, 0)),
            pl.BlockSpec((1, 6, D), lambda b, i: (b + b_off, 0, 0)),
            pl.BlockSpec(wpool_bd.shape, const),
            pl.BlockSpec(pscale.shape, const),
            pl.BlockSpec(wout_p.shape, const),
            pl.BlockSpec(wout_a.shape, const),
            pl.BlockSpec(lng.shape, const),
            pl.BlockSpec(lnb.shape, const),
            pl.BlockSpec(wr_hi.shape, const),
            pl.BlockSpec(wr_lo.shape, const),
        ],
        out_specs=[
            pl.BlockSpec((1, tm, D), lambda b, i: (b, i, 0)),
            pl.BlockSpec((1, tm, HALF), lambda b, i: (b, i, 0)),
            pl.BlockSpec((N_EXPERTS, tm), lambda b, i: (0, b * nt + i)),
        ],
        out_shape=[
            jax.ShapeDtypeStruct((B, S, D), F32),
            jax.ShapeDtypeStruct((B, S, HALF), jnp.int32),
            jax.ShapeDtypeStruct((N_EXPERTS, B * S), F32),
        ],
        compiler_params=_cparams(("arbitrary", "arbitrary")),
        name="post_attn",
    )(x, u, u, attn, mod, wpool_bd, pscale, wout_p, wout_a, lng, lnb, wr_hi, wr_lo)


def _select_first_max(vals, n_rounds, payload=None):
    work = list(vals)
    sel = [None] * len(vals)
    rounds = []
    for _ in range(n_rounds):
        m = functools.reduce(jnp.maximum, work)
        taken = None
        win_idx = jnp.zeros(m.shape, jnp.int32)
        win_val = jnp.zeros(m.shape, F32)
        for e in range(len(work)):
            hit = work[e] == m
            first = hit if taken is None else jnp.logical_and(hit, jnp.logical_not(taken))
            taken = hit if taken is None else jnp.logical_or(taken, hit)
            sel[e] = first if sel[e] is None else jnp.logical_or(sel[e], first)
            work[e] = jnp.where(first, -jnp.inf, work[e])
            if payload is not None:
                win_idx = jnp.where(first, e, win_idx)
                win_val = jnp.where(first, payload[e], win_val)
        rounds.append((win_idx, win_val))
    return sel, rounds


def _route_kernel(bias_ref, lt_ref, idx_ref, gate_ref):
    scores = [jax.nn.sigmoid(lt_ref[e]) for e in range(N_EXPERTS)]
    choice = [scores[e] + bias_ref[e] for e in range(N_EXPERTS)]
    group_score = []
    for g in range(N_GROUPS):
        vals = choice[g * GROUP_SIZE:(g + 1) * GROUP_SIZE]
        sel2, _ = _select_first_max(vals, 2)
        group_score.append(functools.reduce(
            jnp.add, [jnp.where(sel2[j], vals[j], 0.0) for j in range(GROUP_SIZE)]))
    group_sel, _ = _select_first_max(group_score, TOPK_GROUPS)
    masked = [jnp.where(group_sel[e // GROUP_SIZE], choice[e], -jnp.inf) for e in range(N_EXPERTS)]
    _, rounds = _select_first_max(masked, TOP_K, payload=scores)
    total = functools.reduce(jnp.add, [w for _, w in rounds])
    for k, (e_k, w_k) in enumerate(rounds):
        g_k = w_k / total * ROUTED_SCALE
        for r in range(ROUTE_ROWS):
            idx_ref[k:k + 1, r * LANES:(r + 1) * LANES] = e_k[r:r + 1, :]
            gate_ref[k:k + 1, r * LANES:(r + 1) * LANES] = g_k[r:r + 1, :]


def _route(lt3, bias):
    E, R, _ = lt3.shape
    out_spec = pl.BlockSpec((TOP_K, ROUTE_ROWS * LANES), lambda r: (0, r))
    return pl.pallas_call(
        _route_kernel,
        grid=(R // ROUTE_ROWS,),
        in_specs=[
            pl.BlockSpec(memory_space=pltpu.SMEM),
            pl.BlockSpec((E, ROUTE_ROWS, LANES), lambda r: (0, r, 0)),
        ],
        out_specs=[out_spec, out_spec],
        out_shape=[jax.ShapeDtypeStruct((TOP_K, R * LANES), jnp.int32),
                   jax.ShapeDtypeStruct((TOP_K, R * LANES), F32)],
        compiler_params=_cparams(("arbitrary",)),
        name="route",
    )(bias, lt3)


def _sc_mesh():
    return plsc.VectorSubcoreMesh(core_axis_name="c", subcore_axis_name="s",
                                  num_cores=SC_CORES, num_subcores=SC_SUBCORES)


def _sc_worker_id():
    return lax.axis_index("s") * SC_CORES + lax.axis_index("c")


def _sc_gather_sum(ys, dest_kt, gates_kt):
    K, T = dest_kt.shape
    half = ys.shape[1]
    workers = SC_CORES * SC_SUBCORES
    per_worker = T // workers
    wt = SC_SUM_TOKENS
    n_win = per_worker // wt
    n_idx = K * wt
    assert T == workers * n_win * wt and n_idx <= SC_WINDOW and half % SC_LANES == 0
    dest_w = dest_kt.reshape(K, workers, n_win, wt).transpose(1, 2, 0, 3).reshape(workers, n_win, n_idx)
    gates_w = gates_kt.reshape(K, workers, n_win, wt).transpose(1, 2, 0, 3).reshape(workers * n_win * n_idx)

    @functools.partial(
        pl.kernel, mesh=_sc_mesh(),
        out_type=jax.ShapeDtypeStruct((T, 2 * half), F32),
        scratch_types=[
            pltpu.VMEM((n_win, n_idx), jnp.int32),
            pltpu.VMEM((n_idx, half), jnp.int32),
            pltpu.VMEM((n_idx,), F32),
            pltpu.VMEM((wt, 2 * half), F32),
            pltpu.SemaphoreType.DMA,
        ],
        compiler_params=pltpu.CompilerParams(needs_layout_passes=False),
    )
    def gather_sum(ys_hbm, dest_hbm, gates_hbm, out_hbm, idx_v, rows_v, gate_v, out_v, sem):
        wid = _sc_worker_id()
        pltpu.sync_copy(dest_hbm.at[wid], idx_v)

        @pl.loop(0, n_win)
        def _(j):
            pltpu.async_copy(ys_hbm.at[idx_v.at[j]], rows_v, sem).wait()
            pltpu.sync_copy(gates_hbm.at[pl.ds((wid * n_win + j) * n_idx, n_idx)], gate_v)

            @pl.loop(0, wt)
            def _(t):
                g = [plsc.load_gather(gate_v, [jnp.full((SC_LANES,), k * wt + t, jnp.int32)]) for k in range(K)]

                @plsc.parallel_loop(0, half // SC_LANES, unroll=4)
                def _(c):
                    col = pl.ds(c * SC_LANES, SC_LANES)
                    acc_lo = jnp.zeros((SC_LANES,), F32)
                    acc_hi = jnp.zeros((SC_LANES,), F32)
                    for k in range(K):
                        w = rows_v[k * wt + t, col]
                        acc_lo = acc_lo + g[k] * plsc.bitcast(w << 16, F32)
                        acc_hi = acc_hi + g[k] * plsc.bitcast(w & HIGH_HALF_MASK, F32)
                    out_v[t, col] = acc_lo
                    out_v[t, pl.ds(half + c * SC_LANES, SC_LANES)] = acc_hi

            pltpu.sync_copy(out_v, out_hbm.at[pl.ds(wid * per_worker + j * wt, wt)])

    return gather_sum(ys, dest_w, gates_w)


def _sc_dispatch(rows, dest_kt, n_slots):
    T, width = rows.shape
    K = dest_kt.shape[0]
    workers = SC_CORES * SC_SUBCORES
    per_worker = T // workers
    n_win = per_worker // SC_WINDOW
    assert T == workers * n_win * SC_WINDOW
    dest_w = dest_kt.reshape(K, workers, n_win, SC_WINDOW).transpose(1, 2, 0, 3)
    dest_w = dest_w.reshape(workers, n_win * K, SC_WINDOW)

    @functools.partial(
        pl.kernel, mesh=_sc_mesh(),
        out_type=jax.ShapeDtypeStruct((n_slots, width), rows.dtype),
        scratch_types=[
            pltpu.VMEM((n_win * K, SC_WINDOW), jnp.int32),
            pltpu.VMEM((SC_WINDOW, width), rows.dtype),
        ],
    )
    def dispatch(rows_hbm, dest_hbm, out_hbm, idx_v, rows_v):
        wid = _sc_worker_id()
        pltpu.sync_copy(dest_hbm.at[wid], idx_v)

        @pl.loop(0, n_win)
        def _(j):
            pltpu.sync_copy(rows_hbm.at[pl.ds(wid * per_worker + j * SC_WINDOW, SC_WINDOW)], rows_v)
            for k in range(K):
                pltpu.sync_copy(rows_v, out_hbm.at[idx_v.at[j * K + k]])

    return dispatch(rows, dest_w)


def _swiglu(h_lo, h_hi, wg_ref, wu_ref):
    def proj(w_ref):
        return (jnp.dot(h_lo, w_ref[:HALF, :], preferred_element_type=F32)
                + jnp.dot(h_hi, w_ref[HALF:, :], preferred_element_type=F32))
    gate = proj(wg_ref)
    up = proj(wu_ref)
    return (gate * jax.nn.sigmoid(gate) * up).astype(BF16)


def _expert_kernel(seg_first_ref, seg_blocks_ref, n_real_ref, xs_hbm, wg_ref, wu_ref, wd_ref, ys_hbm,
                   xbuf, ybuf, wg_bf, wu_bf, wd_bf, in_sems, out_sems):
    e = pl.program_id(0)
    first = seg_first_ref[e]
    count = seg_blocks_ref[e]
    total = n_real_ref[0]
    blk = DISPATCH_BLOCK
    ring = xbuf.shape[0]

    def in_copy(g):
        slot = g % ring
        return pltpu.make_async_copy(xs_hbm.at[pl.ds(g * blk, blk)], xbuf.at[slot], in_sems.at[slot])

    def out_copy(g):
        slot = g % ring
        return pltpu.make_async_copy(ybuf.at[slot], ys_hbm.at[pl.ds(g * blk, blk)], out_sems.at[slot])

    @pl.when(e == 0)
    def _():
        for g in range(ring - 1):
            @pl.when(g < total)
            def _():
                in_copy(g).start()

    @pl.when(count > 0)
    def _():
        wg_bf[...] = wg_ref[0, 0].astype(BF16)
        wu_bf[...] = wu_ref[0, 0].astype(BF16)
        wd_bf[...] = wd_ref[0, 0].astype(BF16)

    def block(g, _):
        slot = g % ring
        in_copy(g).wait()

        @pl.when(g + ring - 1 < total)
        def _():
            in_copy(g + ring - 1).start()

        @pl.when(g >= ring)
        def _():
            out_copy(g - ring).wait()

        lo, hi = _unpack_bf16_pairs(xbuf[slot])
        a = _swiglu(lo.astype(BF16), hi.astype(BF16), wg_bf, wu_bf)
        y = jnp.dot(a, wd_bf[...], preferred_element_type=F32)
        ybuf[slot] = _pack_bf16_pairs(y.astype(BF16))
        out_copy(g).start()
        return 0

    lax.fori_loop(first, first + count, block, 0)

    @pl.when(e == pl.num_programs(0) - 1)
    def _():
        for back in range(ring, 0, -1):
            @pl.when(total >= back)
            def _():
                out_copy(total - back).wait()


def _expert_mm(seg_first, seg_blocks, n_real, xs, wg, wu, wd, layer):
    P = xs.shape[0]
    D = wg.shape[2]
    w_map = lambda e, sf, sb, nr: (layer, e, 0, 0)
    grid_spec = pltpu.PrefetchScalarGridSpec(
        num_scalar_prefetch=3,
        grid=(N_EXPERTS,),
        in_specs=[
            pl.BlockSpec(memory_space=pl.ANY),
            pl.BlockSpec((1, 1, D, D_EXPERT), w_map),
            pl.BlockSpec((1, 1, D, D_EXPERT), w_map),
            pl.BlockSpec((1, 1, D_EXPERT, D), w_map),
        ],
        out_specs=pl.BlockSpec(memory_space=pl.ANY),
        scratch_shapes=[
            pltpu.VMEM((EXPERT_RING, DISPATCH_BLOCK, HALF), jnp.int32),
            pltpu.VMEM((EXPERT_RING, DISPATCH_BLOCK, HALF), jnp.int32),
            pltpu.VMEM((D, D_EXPERT), BF16), pltpu.VMEM((D, D_EXPERT), BF16), pltpu.VMEM((D_EXPERT, D), BF16),
            pltpu.SemaphoreType.DMA((EXPERT_RING,)), pltpu.SemaphoreType.DMA((EXPERT_RING,)),
        ],
    )
    return pl.pallas_call(
        _expert_kernel,
        grid_spec=grid_spec,
        out_shape=jax.ShapeDtypeStruct((P, HALF), jnp.int32),
        compiler_params=_cparams(("arbitrary",)),
        name="expert_mm",
    )(seg_first, seg_blocks, n_real, xs, wg, wu, wd)


def _rank_kernel(idx_ref, rank_ref, cnt_ref, tri_ref, carry_ref):
    i = pl.program_id(0)
    tm = idx_ref.shape[1]

    @pl.when(i == 0)
    def _():
        r = lax.broadcasted_iota(jnp.int32, (tm, tm), 0)
        c = lax.broadcasted_iota(jnp.int32, (tm, tm), 1)
        tri_ref[...] = (r < c).astype(BF16)
        carry_ref[...] = jnp.zeros(carry_ref.shape, F32)

    e_iota = lax.broadcasted_iota(jnp.int32, (N_EXPERTS, tm), 0)
    hits = [e_iota == idx_ref[k:k + 1, :] for k in range(TOP_K)]
    mask = functools.reduce(jnp.logical_or, hits).astype(F32)
    rank = jnp.dot(mask.astype(BF16), tri_ref[...], preferred_element_type=F32) + carry_ref[:, 0:1]
    for k in range(TOP_K):
        rank_ref[k:k + 1, :] = jnp.sum(jnp.where(hits[k], rank, 0.0), axis=0, keepdims=True).astype(jnp.int32)
    carry_ref[...] = carry_ref[...] + jnp.sum(mask, axis=1, keepdims=True)
    cnt_ref[...] = carry_ref[...]


def _rank(idx_kt):
    K, T = idx_kt.shape
    tm = 1024
    return pl.pallas_call(
        _rank_kernel,
        grid=(T // tm,),
        in_specs=[pl.BlockSpec((K, tm), lambda i: (0, i))],
        out_specs=[pl.BlockSpec((K, tm), lambda i: (0, i)),
                   pl.BlockSpec((N_EXPERTS, LANES), lambda i: (0, 0))],
        out_shape=[jax.ShapeDtypeStruct((K, T), jnp.int32),
                   jax.ShapeDtypeStruct((N_EXPERTS, LANES), F32)],
        scratch_shapes=[pltpu.VMEM((tm, tm), BF16), pltpu.VMEM((N_EXPERTS, LANES), F32)],
        compiler_params=_cparams(("arbitrary",)),
        name="rank",
    )(idx_kt)


def _dest_kernel(idx_ref, rank_ref, start_ref, dest_ref):
    tm = idx_ref.shape[1]
    e_iota = lax.broadcasted_iota(jnp.int32, (N_EXPERTS, tm), 0)
    start = start_ref[:, 0:1]
    for k in range(TOP_K):
        base = jnp.sum(jnp.where(e_iota == idx_ref[k:k + 1, :], start, 0.0), axis=0, keepdims=True)
        dest_ref[k:k + 1, :] = rank_ref[k:k + 1, :] + base.astype(jnp.int32)


def _dest(idx_kt, rank_kt, start):
    K, T = idx_kt.shape
    tm = 1024
    tok_spec = pl.BlockSpec((K, tm), lambda i: (0, i))
    return pl.pallas_call(
        _dest_kernel,
        grid=(T // tm,),
        in_specs=[tok_spec, tok_spec, pl.BlockSpec((N_EXPERTS, LANES), lambda i: (0, 0))],
        out_specs=tok_spec,
        out_shape=jax.ShapeDtypeStruct((K, T), jnp.int32),
        compiler_params=_cparams(("arbitrary",)),
        name="dest",
    )(idx_kt, rank_kt, start)


def _block_plan(counts):
    blk = DISPATCH_BLOCK
    seg_blocks = (counts.astype(jnp.int32) + blk - 1) // blk
    seg_end = jnp.cumsum(seg_blocks)
    seg_first = seg_end - seg_blocks
    return seg_first * blk, seg_first, seg_blocks, seg_end[-1:]


def _combine_kernel(h_ref, x1_ref, routed_ref, mod_ref, wsg_ref, wsu_ref, wsd_ref, lng_ref, lnb_ref, o_ref):
    lo, hi = _unpack_bf16_pairs(h_ref[...])
    a = _swiglu(lo.astype(BF16), hi.astype(BF16), wsg_ref, wsu_ref)
    ffn = jnp.dot(a, wsd_ref[...], preferred_element_type=F32) + routed_ref[...]
    g2 = mod_ref[0, 5:6, :]
    o_ref[...] = _layer_norm(DEEPNORM_ALPHA * x1_ref[...] + g2 * ffn, lng_ref[...], lnb_ref[...])


def _combine_into_kernel(h_ref, x1_ref, routed_ref, mod_ref, wsg_ref, wsu_ref, wsd_ref, lng_ref, lnb_ref,
                         prev_ref, o_ref):
    del prev_ref
    _combine_kernel(h_ref, x1_ref, routed_ref, mod_ref, wsg_ref, wsu_ref, wsd_ref, lng_ref, lnb_ref, o_ref)


def _combine(h2p, x1, routed, mod, wsg, wsu, wsd, lng, lnb, seq, b_off, in_off, out_rows, out_off, out_prev):
    n, D = routed.shape
    tm = TM_COMBINE
    per_seq = seq // tm
    in_blk = in_off // tm
    blk_off = out_off // tm
    const = lambda i: (0, 0)
    in_specs = [
        pl.BlockSpec((tm, HALF), lambda i: (i + in_blk, 0)),
        pl.BlockSpec((tm, D), lambda i: (i + in_blk, 0)),
        pl.BlockSpec((tm, D), lambda i: (i, 0)),
        pl.BlockSpec((1, 6, D), lambda i: ((i + in_blk) // per_seq + b_off, 0, 0)),
        pl.BlockSpec(wsg.shape, const),
        pl.BlockSpec(wsu.shape, const),
        pl.BlockSpec(wsd.shape, const),
        pl.BlockSpec(lng.shape, const),
        pl.BlockSpec(lnb.shape, const),
    ]
    args = [h2p, x1, routed, mod, wsg, wsu, wsd, lng, lnb]
    body = _combine_kernel
    aliases = {}
    if out_prev is not None:
        in_specs.append(pl.BlockSpec(memory_space=pl.ANY))
        args.append(out_prev)
        aliases = {len(args) - 1: 0}
        body = _combine_into_kernel
    return pl.pallas_call(
        body,
        grid=(n // tm,),
        in_specs=in_specs,
        out_specs=pl.BlockSpec((tm, D), lambda i: (i + blk_off, 0)),
        out_shape=jax.ShapeDtypeStruct((out_rows, D), F32),
        input_output_aliases=aliases,
        compiler_params=_cparams(("arbitrary",)),
        name="combine",
    )(*args)


def _rope_pair(w):
    half = QK_ROPE // 2
    return w, jnp.concatenate([-w[:, half:], w[:, :half]], axis=1)


def _prep_layer(w_in, w_uq, w_ukv, w_pool, w_out, w_router):
    lat = POOL_WIDTH + Q_LORA + KV_LORA
    k_a, k_b = _rope_pair(w_in[:, lat:lat + QK_ROPE])
    w_in_ext = jnp.concatenate([w_in[:, :lat], k_a, k_b], axis=1).astype(BF16)

    qa_cols = []
    per_head = QK_NOPE + QK_ROPE
    for hd in range(N_HEADS):
        w_h = w_uq[:, hd * per_head:(hd + 1) * per_head]
        r_a, r_b = _rope_pair(w_h[:, QK_NOPE:])
        qa_cols += [w_h[:, :QK_NOPE], r_a, r_b]
    w_uqa = jnp.concatenate(qa_cols, axis=1).astype(BF16)

    wpool_bd = jax.scipy.linalg.block_diag(*[w_pool[g] for g in range(len(POOL_WINDOWS))]).astype(BF16)
    wr_hi, wr_lo = _split_hi_lo(w_router.T)
    return dict(w_in_ext=w_in_ext, w_uqa=w_uqa, w_ukv=w_ukv.astype(BF16), wpool_bd=wpool_bd,
                wout_p=w_out[:POOL_WIDTH].astype(BF16), wout_a=w_out[POOL_WIDTH:].astype(BF16),
                wr_hi=wr_hi, wr_lo=wr_lo)


def kernel(x, c, positions, w_ada, b_ada, w_in, q_norm_g, kv_norm_g, w_uq, w_ukv, w_pool, pool_scale,
           w_out, ln1_g, ln1_b, w_router, router_bias, w_exp_gate, w_exp_up, w_exp_down,
           w_sh_gate, w_sh_up, w_sh_down, ln2_g, ln2_b):
    B, S, D = x.shape
    L = w_in.shape[0]
    row = lambda v: v.reshape(1, -1)

    mod_all = _ada_mod(c, w_ada, b_ada).reshape(L, B, 6, D)
    cs_all = _rope_table(positions)
    bs = B // N_STREAMS
    T = bs * S
    n_blocks = T * TOP_K // DISPATCH_BLOCK + N_EXPERTS
    xs_streams = [x] * N_STREAMS
    result = None
    for l in range(L):
        p = _prep_layer(w_in[l], w_uq[l], w_ukv[l], w_pool[l], w_out[l], w_router[l])
        shared = (w_sh_gate[l].astype(BF16), w_sh_up[l].astype(BF16), w_sh_down[l].astype(BF16))
        mod = mod_all[l]
        for s in range(N_STREAMS):
            xl = xs_streams[s]
            b_off = s * bs
            x_off = b_off if l == 0 else 0
            u, q, k, v = _in_proj(xl, mod, cs_all, p["w_in_ext"], row(q_norm_g[l]), row(kv_norm_g[l]),
                                  p["w_uqa"], p["w_ukv"], bs, x_off, b_off)
            attn = _attention(q, k, v)
            x1, h2p, lt = _post_attn(xl, u, attn, mod, p["wpool_bd"], row(pool_scale[l]), p["wout_p"],
                                     p["wout_a"], row(ln1_g[l]), row(ln1_b[l]), p["wr_hi"], p["wr_lo"],
                                     x_off, b_off)
            idx_kt, gates_kt = _route(lt.reshape(N_EXPERTS, T // LANES, LANES), router_bias[l])
            rank_kt, counts = _rank(idx_kt)
            pad_start, seg_first, seg_blocks, n_real = _block_plan(counts[:, 0])
            start = jnp.broadcast_to(pad_start.astype(F32)[:, None], (N_EXPERTS, LANES))
            dest_kt = _dest(idx_kt, rank_kt, start)
            h2p = h2p.reshape(T, HALF)
            rows = _sc_dispatch(h2p, dest_kt, n_blocks * DISPATCH_BLOCK)
            ys = _expert_mm(seg_first, seg_blocks, n_real, rows, w_exp_gate, w_exp_up, w_exp_down, l)
            tc = T // COMBINE_CHUNKS
            out = None
            for ch in range(COMBINE_CHUNKS):
                lo = ch * tc
                routed = _sc_gather_sum(ys, dest_kt[:, lo:lo + tc], gates_kt[:, lo:lo + tc])
                if l < L - 1:
                    out = _combine(h2p, x1.reshape(T, D), routed, mod, *shared, row(ln2_g[l]), row(ln2_b[l]),
                                   S, b_off, lo, T, lo, out)
                else:
                    result = _combine(h2p, x1.reshape(T, D), routed, mod, *shared, row(ln2_g[l]),
                                      row(ln2_b[l]), S, b_off, lo, B * S, s * T + lo, result)
            if l < L - 1:
                xs_streams[s] = out.reshape(bs, S, D)
    return result.reshape(B, S, D)
```

```python
import functools

import jax
import jax.numpy as jnp
from jax import lax
from jax.experimental import pallas as pl
from jax.experimental.pallas import tpu as pltpu
from jax.experimental.pallas import tpu_sc as plsc

F32 = jnp.float32
BF16 = jnp.bfloat16

D_MODEL = 1024
DEPTH = 4
POOL_WIDTH = 256
POOL_WINDOWS = (2, 4, 8, 16)
POOL_GROUP = 64
POOL_HALO = 16
QK_NOPE = 128
QK_ROPE = 64
V_HEAD = 128
N_HEADS = 6
Q_LORA = 384
KV_LORA = 256
ROPE_THETA = 10000.0
N_EXPERTS = 64
TOP_K = 8
N_GROUPS = 8
GROUP_SIZE = N_EXPERTS // N_GROUPS
TOPK_GROUPS = 4
D_EXPERT = 256
ROUTED_SCALE = 2.5
DEEPNORM_ALPHA = (2.0 * DEPTH) ** 0.25
LN_EPS = 1e-5
RMS_EPS = 1e-6
ATTN_SCALE = (QK_NOPE + QK_ROPE) ** -0.5
LOG2_E = 1.4426950408889634
Q_SCALE = ATTN_SCALE * LOG2_E
LANES = 128
QK_PAD = 2 * LANES
MASK_VALUE = -1e30

TM_PROJ = 512
TM_POST = 1024
TQ = 512
TK = 512
ATTN_HEADS = 2
ROUTE_ROWS = 8
DISPATCH_BLOCK = 512
EXPERT_RING = 6
TM_COMBINE = 1024
SC_CORES = 2
SC_SUBCORES = 16
SC_WINDOW = 128
SC_LANES = 16
HIGH_HALF_MASK = -65536
SC_SUM_TOKENS = 16
N_STREAMS = 2
COMBINE_CHUNKS = 2
HALF = D_MODEL // 2
VMEM_LIMIT = 48 * 1024 * 1024


def _cparams(sem):
    return pltpu.CompilerParams(dimension_semantics=sem, vmem_limit_bytes=VMEM_LIMIT)


def _split_hi_lo(a):
    hi = a.astype(BF16)
    lo = (a - hi.astype(F32)).astype(BF16)
    return hi, lo


def _pack_bf16_pairs(a):
    bits = lax.bitcast_convert_type(a.astype(F32), jnp.uint32)
    half = a.shape[1] // 2
    word = (bits[:, :half] >> 16) | (bits[:, half:] & jnp.uint32(0xFFFF0000))
    return lax.bitcast_convert_type(word, jnp.int32)


def _unpack_bf16_pairs(w):
    bits = lax.bitcast_convert_type(w, jnp.uint32)
    lo = lax.bitcast_convert_type(bits << 16, F32)
    hi = lax.bitcast_convert_type(bits & jnp.uint32(0xFFFF0000), F32)
    return lo, hi


def _ada_kernel(c_ref, w_ref, b_ref, o_ref):
    c = c_ref[...]
    cond = c * jax.nn.sigmoid(c)
    c_hi, c_lo = _split_hi_lo(cond)
    w_hi, w_lo = _split_hi_lo(w_ref[0])
    acc = jnp.dot(c_hi, w_hi, preferred_element_type=F32)
    acc += jnp.dot(c_lo, w_hi, preferred_element_type=F32)
    acc += jnp.dot(c_hi, w_lo, preferred_element_type=F32)
    o_ref[0] = acc + b_ref[0]


def _ada_mod(c, w_ada, b_ada):
    L, D, N = w_ada.shape
    B = c.shape[0]
    tn = 1536
    return pl.pallas_call(
        _ada_kernel,
        grid=(L, N // tn),
        in_specs=[
            pl.BlockSpec((B, D), lambda l, j: (0, 0)),
            pl.BlockSpec((1, D, tn), lambda l, j: (l, 0, j)),
            pl.BlockSpec((1, 1, tn), lambda l, j: (l, 0, j)),
        ],
        out_specs=pl.BlockSpec((1, B, tn), lambda l, j: (l, 0, j)),
        out_shape=jax.ShapeDtypeStruct((L, B, N), F32),
        compiler_params=_cparams(("arbitrary", "arbitrary")),
        name="ada_mod",
    )(c, w_ada, b_ada.reshape(L, 1, N))


def _rope_table_kernel(pos_ref, freq_ref, o_ref):
    ang = pos_ref[0].astype(F32) * freq_ref[...]
    lane = lax.broadcasted_iota(jnp.int32, (1, LANES), 1)
    o_ref[0] = jnp.where(lane < QK_ROPE, jnp.cos(ang), jnp.sin(ang))


def _rope_table(positions):
    B, S = positions.shape
    half = QK_ROPE // 2
    inv_freq = ROPE_THETA ** (-jnp.arange(half, dtype=F32) / half)
    freq = jnp.tile(inv_freq, LANES // half).reshape(1, LANES)
    return pl.pallas_call(
        _rope_table_kernel,
        grid=(B,),
        in_specs=[
            pl.BlockSpec((1, S, 1), lambda b: (b, 0, 0)),
            pl.BlockSpec((1, LANES), lambda b: (0, 0)),
        ],
        out_specs=pl.BlockSpec((1, S, LANES), lambda b: (b, 0, 0)),
        out_shape=jax.ShapeDtypeStruct((B, S, LANES), F32),
        compiler_params=_cparams(("arbitrary",)),
        name="rope_table",
    )(positions.reshape(B, S, 1), freq)


def _rms(x, g):
    return x * lax.rsqrt(jnp.mean(x * x, axis=-1, keepdims=True) + RMS_EPS) * g


def _in_proj_kernel(x_ref, mod_ref, cs_ref, w_in_ref, qg_ref, kvg_ref, w_uqa_ref, w_ukv_ref,
                    u_ref, q_ref, k_ref, v_ref):
    x = x_ref[0]
    sh1 = mod_ref[0, 0:1, :]
    sc1 = mod_ref[0, 1:2, :]
    h = (x * (1.0 + sc1) + sh1).astype(BF16)
    proj = jnp.dot(h, w_in_ref[...], preferred_element_type=F32)
    u_ref[0] = proj[:, 0:POOL_WIDTH]
    o = POOL_WIDTH
    q_lat = proj[:, o:o + Q_LORA]
    o += Q_LORA
    kv_lat = proj[:, o:o + KV_LORA]
    o += KV_LORA
    cos_sin = cs_ref[0]

    def rotate(pair):
        t = pair * cos_sin
        return t + pltpu.roll(t, QK_ROPE, 1)

    k_rot = rotate(proj[:, o:o + LANES]).astype(BF16)
    lane = lax.broadcasted_iota(jnp.int32, (1, LANES), 1)
    q_keep = jnp.where(lane < QK_ROPE, Q_SCALE, 0.0)

    qn = _rms(q_lat, qg_ref[...]).astype(BF16)
    q_a = jnp.dot(qn, w_uqa_ref[...], preferred_element_type=F32)
    kvn = _rms(kv_lat, kvg_ref[...]).astype(BF16)
    kv = jnp.dot(kvn, w_ukv_ref[...], preferred_element_type=F32)
    for hd in range(N_HEADS):
        b0 = hd * QK_PAD
        q_ref[0, hd, :, 0:LANES] = (q_a[:, b0:b0 + LANES] * Q_SCALE).astype(BF16)
        q_ref[0, hd, :, LANES:QK_PAD] = (rotate(q_a[:, b0 + LANES:b0 + QK_PAD]) * q_keep).astype(BF16)
        k_ref[0, hd, :, 0:LANES] = kv[:, b0:b0 + QK_NOPE].astype(BF16)
        k_ref[0, hd, :, LANES:QK_PAD] = k_rot
        v_ref[0, hd] = kv[:, b0 + QK_NOPE:b0 + QK_NOPE + V_HEAD].astype(BF16)


def _in_proj(x, mod, cs, w_in_ext, qg, kvg, w_uqa, w_ukv_bf, B, x_off, b_off):
    _, S, D = x.shape
    tm = TM_PROJ
    const = lambda b, i: (0, 0)
    return pl.pallas_call(
        _in_proj_kernel,
        grid=(B, S // tm),
        in_specs=[
            pl.BlockSpec((1, tm, D), lambda b, i: (b + x_off, i, 0)),
            pl.BlockSpec((1, 6, D), lambda b, i: (b + b_off, 0, 0)),
            pl.BlockSpec((1, tm, LANES), lambda b, i: (b + b_off, i, 0)),
            pl.BlockSpec(w_in_ext.shape, const),
            pl.BlockSpec(qg.shape, const),
            pl.BlockSpec(kvg.shape, const),
            pl.BlockSpec(w_uqa.shape, const),
            pl.BlockSpec(w_ukv_bf.shape, const),
        ],
        out_specs=[
            pl.BlockSpec((1, tm, POOL_WIDTH), lambda b, i: (b, i, 0)),
            pl.BlockSpec((1, N_HEADS, tm, QK_PAD), lambda b, i: (b, 0, i, 0)),
            pl.BlockSpec((1, N_HEADS, tm, QK_PAD), lambda b, i: (b, 0, i, 0)),
            pl.BlockSpec((1, N_HEADS, tm, V_HEAD), lambda b, i: (b, 0, i, 0)),
        ],
        out_shape=[
            jax.ShapeDtypeStruct((B, S, POOL_WIDTH), F32),
            jax.ShapeDtypeStruct((B, N_HEADS, S, QK_PAD), BF16),
            jax.ShapeDtypeStruct((B, N_HEADS, S, QK_PAD), BF16),
            jax.ShapeDtypeStruct((B, N_HEADS, S, V_HEAD), BF16),
        ],
        compiler_params=_cparams(("arbitrary", "arbitrary")),
        name="in_proj",
    )(x, mod, cs, w_in_ext, qg, kvg, w_uqa, w_ukv_bf)


def _softmax_step(q, k, v, carry, mask):
    m, l, acc = carry
    s = lax.dot_general(q, k, (((1,), (1,)), ((), ())), preferred_element_type=F32)
    if mask is not None:
        s = jnp.where(mask, s, MASK_VALUE)
    m_new = jnp.maximum(m, jnp.max(s, axis=-1, keepdims=True))
    alpha = jnp.exp2(m - m_new)
    p = jnp.exp2(s - m_new)
    l_new = alpha * l + jnp.sum(p, axis=-1, keepdims=True)
    acc_new = alpha * acc + jnp.dot(p.astype(BF16), v, preferred_element_type=F32)
    return m_new, l_new, acc_new


def _attn_kernel(q_ref, k_ref, v_ref, o_ref):
    heads, seq = q_ref.shape[1], q_ref.shape[2]
    row = lax.broadcasted_iota(jnp.int32, (TQ, TK), 0)
    col = lax.broadcasted_iota(jnp.int32, (TQ, TK), 1)
    diag = row >= col
    for i in range(seq // TQ):
        qs = [q_ref[0, h, i * TQ:(i + 1) * TQ, :] for h in range(heads)]
        carries = [(jnp.full((TQ, 1), MASK_VALUE, F32), jnp.zeros((TQ, 1), F32), jnp.zeros((TQ, V_HEAD), F32))
                   for _ in range(heads)]
        for j in range(i + 1):
            for h in range(heads):
                k = k_ref[0, h, j * TK:(j + 1) * TK, :]
                v = v_ref[0, h, j * TK:(j + 1) * TK, :]
                carries[h] = _softmax_step(qs[h], k, v, carries[h], diag if j == i else None)
        for h in range(heads):
            _, l, acc = carries[h]
            o_ref[0, h, i * TQ:(i + 1) * TQ, :] = (acc / l).astype(BF16)


def _attention(q, k, v):
    B, H, S, _ = q.shape
    return pl.pallas_call(
        _attn_kernel,
        grid=(B, H // ATTN_HEADS),
        in_specs=[
            pl.BlockSpec((1, ATTN_HEADS, S, QK_PAD), lambda b, h: (b, h, 0, 0)),
            pl.BlockSpec((1, ATTN_HEADS, S, QK_PAD), lambda b, h: (b, h, 0, 0)),
            pl.BlockSpec((1, ATTN_HEADS, S, V_HEAD), lambda b, h: (b, h, 0, 0)),
        ],
        out_specs=pl.BlockSpec((1, ATTN_HEADS, S, V_HEAD), lambda b, h: (b, h, 0, 0)),
        out_shape=jax.ShapeDtypeStruct((B, H, S, V_HEAD), BF16),
        compiler_params=_cparams(("arbitrary", "arbitrary")),
        name="attention",
    )(q, k, v)


def _layer_norm(v, g, b):
    mu = jnp.mean(v, axis=-1, keepdims=True)
    d = v - mu
    var = jnp.mean(d * d, axis=-1, keepdims=True)
    return d * lax.rsqrt(var + LN_EPS) * g + b


def _post_kernel(x_ref, u_ref, halo_ref, attn_ref, mod_ref, wpool_ref, pscale_ref, wout_p_ref, wout_a_ref,
                 lng_ref, lnb_ref, wr_hi_ref, wr_lo_ref, x1_ref, h2_ref, lt_ref):
    i = pl.program_id(1)
    tm = u_ref.shape[1]
    u = u_ref[0]
    halo = jnp.where(i > 0, halo_ref[0], 0.0)
    ext = jnp.concatenate([halo, u], axis=0)
    s2 = ext + pltpu.roll(ext, 1, 0)
    s4 = s2 + pltpu.roll(s2, 2, 0)
    s8 = s4 + pltpu.roll(s4, 4, 0)
    s16 = s8 + pltpu.roll(s8, 8, 0)
    lane = lax.broadcasted_iota(jnp.int32, (1, POOL_WIDTH), 1)
    win = jnp.where(lane < POOL_GROUP, s2,
                    jnp.where(lane < 2 * POOL_GROUP, s4, jnp.where(lane < 3 * POOL_GROUP, s8, s16)))
    win = win[POOL_HALO:, :]
    width = jnp.where(lane < POOL_GROUP, POOL_WINDOWS[0],
                      jnp.where(lane < 2 * POOL_GROUP, POOL_WINDOWS[1],
                                jnp.where(lane < 3 * POOL_GROUP, POOL_WINDOWS[2], POOL_WINDOWS[3])))
    t = i * tm + lax.broadcasted_iota(jnp.int32, (tm, 1), 0)
    count = jnp.minimum(t + 1, width).astype(F32)
    token_mix = win / count - u
    pooled = jnp.dot(token_mix.astype(BF16), wpool_ref[...], preferred_element_type=F32) * pscale_ref[...]
    mixed = jnp.dot(pooled.astype(BF16), wout_p_ref[...], preferred_element_type=F32)
    attn = jnp.concatenate([attn_ref[0, hd] for hd in range(N_HEADS)], axis=1)
    mixed += jnp.dot(attn, wout_a_ref[...], preferred_element_type=F32)

    g1 = mod_ref[0, 2:3, :]
    sh2 = mod_ref[0, 3:4, :]
    sc2 = mod_ref[0, 4:5, :]
    x1 = _layer_norm(DEEPNORM_ALPHA * x_ref[0] + g1 * mixed, lng_ref[...], lnb_ref[...])
    x1_ref[0] = x1
    h2 = x1 * (1.0 + sc2) + sh2
    h_hi, h_lo = _split_hi_lo(h2)
    h2_ref[0] = _pack_bf16_pairs(h_hi)
    nt = (((1,), (1,)), ((), ()))
    lt = lax.dot_general(wr_hi_ref[...], h_hi, nt, preferred_element_type=F32)
    lt += lax.dot_general(wr_hi_ref[...], h_lo, nt, preferred_element_type=F32)
    lt += lax.dot_general(wr_lo_ref[...], h_hi, nt, preferred_element_type=F32)
    lt_ref[...] = lt


def _post_attn(x, u, attn, mod, wpool_bd, pscale, wout_p, wout_a, lng, lnb, wr_hi, wr_lo, x_off, b_off):
    B, S, _ = u.shape
    D = x.shape[2]
    tm = TM_POST
    nt = S // tm
    const = lambda b, i: (0, 0)
    halo_blocks = tm // POOL_HALO
    return pl.pallas_call(
        _post_kernel,
        grid=(B, nt),
        in_specs=[
            pl.BlockSpec((1, tm, D), lambda b, i: (b + x_off, i, 0)),
            pl.BlockSpec((1, tm, POOL_WIDTH), lambda b, i: (b, i, 0)),
            pl.BlockSpec((1, POOL_HALO, POOL_WIDTH), lambda b, i: (b, jnp.maximum(i * halo_blocks - 1, 0), 0)),
            pl.BlockSpec((1, N_HEADS, tm, V_HEAD), lambda b, i: (b, 0, i, 0)),
            pl.BlockSpec((1, 6, D), lambda b, i: (b + b_off, 0, 0)),
            pl.BlockSpec(wpool_bd.shape, const),
            pl.BlockSpec(pscale.shape, const),
            pl.BlockSpec(wout_p.shape, const),
            pl.BlockSpec(wout_a.shape, const),
            pl.BlockSpec(lng.shape, const),
            pl.BlockSpec(lnb.shape, const),
            pl.BlockSpec(wr_hi.shape, const),
            pl.BlockSpec(wr_lo.shape, const),
        ],
        out_specs=[
            pl.BlockSpec((1, tm, D), lambda b, i: (b, i, 0)),
            pl.BlockSpec((1, tm, HALF), lambda b, i: (b, i, 0)),
            pl.BlockSpec((N_EXPERTS, tm), lambda b, i: (0, b * nt + i)),
        ],
        out_shape=[
            jax.ShapeDtypeStruct((B, S, D), F32),
            jax.ShapeDtypeStruct((B, S, HALF), jnp.int32),
            jax.ShapeDtypeStruct((N_EXPERTS, B * S), F32),
        ],
        compiler_params=_cparams(("arbitrary", "arbitrary")),
        name="post_attn",
    )(x, u, u, attn, mod, wpool_bd, pscale, wout_p, wout_a, lng, lnb, wr_hi, wr_lo)


def _select_first_max(vals, n_rounds, payload=None):
    work = list(vals)
    sel = [None] * len(vals)
    rounds = []
    for _ in range(n_rounds):
        m = functools.reduce(jnp.maximum, work)
        taken = None
        win_idx = jnp.zeros(m.shape, jnp.int32)
        win_val = jnp.zeros(m.shape, F32)
        for e in range(len(work)):
            hit = work[e] == m
            first = hit if taken is None else jnp.logical_and(hit, jnp.logical_not(taken))
            taken = hit if taken is None else jnp.logical_or(taken, hit)
            sel[e] = first if sel[e] is None else jnp.logical_or(sel[e], first)
            work[e] = jnp.where(first, -jnp.inf, work[e])
            if payload is not None:
                win_idx = jnp.where(first, e, win_idx)
                win_val = jnp.where(first, payload[e], win_val)
        rounds.append((win_idx, win_val))
    return sel, rounds


def _route_kernel(bias_ref, lt_ref, idx_ref, gate_ref):
    scores = [jax.nn.sigmoid(lt_ref[e]) for e in range(N_EXPERTS)]
    choice = [scores[e] + bias_ref[e] for e in range(N_EXPERTS)]
    group_score = []
    for g in range(N_GROUPS):
        vals = choice[g * GROUP_SIZE:(g + 1) * GROUP_SIZE]
        sel2, _ = _select_first_max(vals, 2)
        group_score.append(functools.reduce(
            jnp.add, [jnp.where(sel2[j], vals[j], 0.0) for j in range(GROUP_SIZE)]))
    group_sel, _ = _select_first_max(group_score, TOPK_GROUPS)
    masked = [jnp.where(group_sel[e // GROUP_SIZE], choice[e], -jnp.inf) for e in range(N_EXPERTS)]
    _, rounds = _select_first_max(masked, TOP_K, payload=scores)
    total = functools.reduce(jnp.add, [w for _, w in rounds])
    for k, (e_k, w_k) in enumerate(rounds):
        g_k = w_k / total * ROUTED_SCALE
        for r in range(ROUTE_ROWS):
            idx_ref[k:k + 1, r * LANES:(r + 1) * LANES] = e_k[r:r + 1, :]
            gate_ref[k:k + 1, r * LANES:(r + 1) * LANES] = g_k[r:r + 1, :]


def _route(lt3, bias):
    E, R, _ = lt3.shape
    out_spec = pl.BlockSpec((TOP_K, ROUTE_ROWS * LANES), lambda r: (0, r))
    return pl.pallas_call(
        _route_kernel,
        grid=(R // ROUTE_ROWS,),
        in_specs=[
            pl.BlockSpec(memory_space=pltpu.SMEM),
            pl.BlockSpec((E, ROUTE_ROWS, LANES), lambda r: (0, r, 0)),
        ],
        out_specs=[out_spec, out_spec],
        out_shape=[jax.ShapeDtypeStruct((TOP_K, R * LANES), jnp.int32),
                   jax.ShapeDtypeStruct((TOP_K, R * LANES), F32)],
        compiler_params=_cparams(("arbitrary",)),
        name="route",
    )(bias, lt3)


def _sc_mesh():
    return plsc.VectorSubcoreMesh(core_axis_name="c", subcore_axis_name="s",
                                  num_cores=SC_CORES, num_subcores=SC_SUBCORES)


def _sc_worker_id():
    return lax.axis_index("s") * SC_CORES + lax.axis_index("c")


def _sc_gather_sum(ys, dest_kt, gates_kt):
    K, T = dest_kt.shape
    half = ys.shape[1]
    workers = SC_CORES * SC_SUBCORES
    per_worker = T // workers
    wt = SC_SUM_TOKENS
    n_win = per_worker // wt
    n_idx = K * wt
    assert T == workers * n_win * wt and n_idx <= SC_WINDOW and half % SC_LANES == 0
    dest_w = dest_kt.reshape(K, workers, n_win, wt).transpose(1, 2, 0, 3).reshape(workers, n_win, n_idx)
    gates_w = gates_kt.reshape(K, workers, n_win, wt).transpose(1, 2, 0, 3).reshape(workers, n_win, n_idx)

    @functools.partial(
        pl.kernel, mesh=_sc_mesh(),
        out_type=jax.ShapeDtypeStruct((T, 2 * half), F32),
        scratch_types=[
            pltpu.VMEM((n_win, n_idx), jnp.int32),
            pltpu.VMEM((n_idx, half), jnp.int32),
            pltpu.VMEM((n_win, n_idx), F32),
            pltpu.VMEM((wt, 2 * half), F32),
            pltpu.SemaphoreType.DMA,
        ],
        compiler_params=pltpu.CompilerParams(needs_layout_passes=False),
    )
    def gather_sum(ys_hbm, dest_hbm, gates_hbm, out_hbm, idx_v, rows_v, gate_v, out_v, sem):
        wid = _sc_worker_id()
        pltpu.sync_copy(dest_hbm.at[wid], idx_v)
        pltpu.sync_copy(gates_hbm.at[wid], gate_v)

        @pl.loop(0, n_win)
        def _(j):
            pltpu.async_copy(ys_hbm.at[idx_v.at[j]], rows_v, sem).wait()
            win = jnp.full((SC_LANES,), j, jnp.int32)

            @pl.loop(0, wt)
            def _(t):
                g = [plsc.load_gather(gate_v, [win, jnp.full((SC_LANES,), k * wt + t, jnp.int32)])
                     for k in range(K)]

                @plsc.parallel_loop(0, half // SC_LANES, unroll=4)
                def _(c):
                    col = pl.ds(c * SC_LANES, SC_LANES)
                    acc_lo = jnp.zeros((SC_LANES,), F32)
                    acc_hi = jnp.zeros((SC_LANES,), F32)
                    for k in range(K):
                        w = rows_v[k * wt + t, col]
                        acc_lo = acc_lo + g[k] * plsc.bitcast(w << 16, F32)
                        acc_hi = acc_hi + g[k] * plsc.bitcast(w & HIGH_HALF_MASK, F32)
                    out_v[t, col] = acc_lo
                    out_v[t, pl.ds(half + c * SC_LANES, SC_LANES)] = acc_hi

            pltpu.sync_copy(out_v, out_hbm.at[pl.ds(wid * per_worker + j * wt, wt)])

    return gather_sum(ys, dest_w, gates_w)


def _sc_dispatch(rows, dest_kt, n_slots):
    T, width = rows.shape
    K = dest_kt.shape[0]
    workers = SC_CORES * SC_SUBCORES
    per_worker = T // workers
    n_win = per_worker // SC_WINDOW
    assert T == workers * n_win * SC_WINDOW
    dest_w = dest_kt.reshape(K, workers, n_win, SC_WINDOW).transpose(1, 2, 0, 3)
    dest_w = dest_w.reshape(workers, n_win * K, SC_WINDOW)

    @functools.partial(
        pl.kernel, mesh=_sc_mesh(),
        out_type=jax.ShapeDtypeStruct((n_slots, width), rows.dtype),
        scratch_types=[
            pltpu.VMEM((n_win * K, SC_WINDOW), jnp.int32),
            pltpu.VMEM((SC_WINDOW, width), rows.dtype),
        ],
    )
    def dispatch(rows_hbm, dest_hbm, out_hbm, idx_v, rows_v):
        wid = _sc_worker_id()
        pltpu.sync_copy(dest_hbm.at[wid], idx_v)

        @pl.loop(0, n_win)
        def _(j):
            pltpu.sync_copy(rows_hbm.at[pl.ds(wid * per_worker + j * SC_WINDOW, SC_WINDOW)], rows_v)
            for k in range(K):
                pltpu.sync_copy(rows_v, out_hbm.at[idx_v.at[j * K + k]])

    return dispatch(rows, dest_w)


def _swiglu(h_lo, h_hi, wg_ref, wu_ref):
    def proj(w_ref):
        return (jnp.dot(h_lo, w_ref[:HALF, :], preferred_element_type=F32)
                + jnp.dot(h_hi, w_ref[HALF:, :], preferred_element_type=F32))
    gate = proj(wg_ref)
    up = proj(wu_ref)
    return (gate * jax.nn.sigmoid(gate) * up).astype(BF16)


def _expert_kernel(seg_first_ref, seg_blocks_ref, n_real_ref, xs_hbm, wg_ref, wu_ref, wd_ref, ys_hbm,
                   xbuf, ybuf, wg_bf, wu_bf, wd_bf, in_sems, out_sems):
    e = pl.program_id(0)
    first = seg_first_ref[e]
    count = seg_blocks_ref[e]
    total = n_real_ref[0]
    blk = DISPATCH_BLOCK
    ring = xbuf.shape[0]

    def in_copy(g):
        slot = g % ring
        return pltpu.make_async_copy(xs_hbm.at[pl.ds(g * blk, blk)], xbuf.at[slot], in_sems.at[slot])

    def out_copy(g):
        slot = g % ring
        return pltpu.make_async_copy(ybuf.at[slot], ys_hbm.at[pl.ds(g * blk, blk)], out_sems.at[slot])

    @pl.when(e == 0)
    def _():
        for g in range(ring - 1):
            @pl.when(g < total)
            def _():
                in_copy(g).start()

    @pl.when(count > 0)
    def _():
        wg_bf[...] = wg_ref[0, 0].astype(BF16)
        wu_bf[...] = wu_ref[0, 0].astype(BF16)
        wd_bf[...] = wd_ref[0, 0].astype(BF16)

    def block(g, _):
        slot = g % ring
        in_copy(g).wait()

        @pl.when(g + ring - 1 < total)
        def _():
            in_copy(g + ring - 1).start()

        @pl.when(g >= ring)
        def _():
            out_copy(g - ring).wait()

        lo, hi = _unpack_bf16_pairs(xbuf[slot])
        a = _swiglu(lo.astype(BF16), hi.astype(BF16), wg_bf, wu_bf)
        y = jnp.dot(a, wd_bf[...], preferred_element_type=F32)
        ybuf[slot] = _pack_bf16_pairs(y.astype(BF16))
        out_copy(g).start()
        return 0

    lax.fori_loop(first, first + count, block, 0)

    @pl.when(e == pl.num_programs(0) - 1)
    def _():
        for back in range(ring, 0, -1):
            @pl.when(total >= back)
            def _():
                out_copy(total - back).wait()


def _expert_mm(seg_first, seg_blocks, n_real, xs, wg, wu, wd, layer):
    P = xs.shape[0]
    D = wg.shape[2]
    w_map = lambda e, sf, sb, nr: (layer, e, 0, 0)
    grid_spec = pltpu.PrefetchScalarGridSpec(
        num_scalar_prefetch=3,
        grid=(N_EXPERTS,),
        in_specs=[
            pl.BlockSpec(memory_space=pl.ANY),
            pl.BlockSpec((1, 1, D, D_EXPERT), w_map),
            pl.BlockSpec((1, 1, D, D_EXPERT), w_map),
            pl.BlockSpec((1, 1, D_EXPERT, D), w_map),
        ],
        out_specs=pl.BlockSpec(memory_space=pl.ANY),
        scratch_shapes=[
            pltpu.VMEM((EXPERT_RING, DISPATCH_BLOCK, HALF), jnp.int32),
            pltpu.VMEM((EXPERT_RING, DISPATCH_BLOCK, HALF), jnp.int32),
            pltpu.VMEM((D, D_EXPERT), BF16), pltpu.VMEM((D, D_EXPERT), BF16), pltpu.VMEM((D_EXPERT, D), BF16),
            pltpu.SemaphoreType.DMA((EXPERT_RING,)), pltpu.SemaphoreType.DMA((EXPERT_RING,)),
        ],
    )
    return pl.pallas_call(
        _expert_kernel,
        grid_spec=grid_spec,
        out_shape=jax.ShapeDtypeStruct((P, HALF), jnp.int32),
        compiler_params=_cparams(("arbitrary",)),
        name="expert_mm",
    )(seg_first, seg_blocks, n_real, xs, wg, wu, wd)


def _rank_kernel(idx_ref, rank_ref, cnt_ref, tri_ref, carry_ref):
    i = pl.program_id(0)
    tm = idx_ref.shape[1]

    @pl.when(i == 0)
    def _():
        r = lax.broadcasted_iota(jnp.int32, (tm, tm), 0)
        c = lax.broadcasted_iota(jnp.int32, (tm, tm), 1)
        tri_ref[...] = (r < c).astype(BF16)
        carry_ref[...] = jnp.zeros(carry_ref.shape, F32)

    e_iota = lax.broadcasted_iota(jnp.int32, (N_EXPERTS, tm), 0)
    hits = [e_iota == idx_ref[k:k + 1, :] for k in range(TOP_K)]
    mask = functools.reduce(jnp.logical_or, hits).astype(F32)
    rank = jnp.dot(mask.astype(BF16), tri_ref[...], preferred_element_type=F32) + carry_ref[:, 0:1]
    for k in range(TOP_K):
        rank_ref[k:k + 1, :] = jnp.sum(jnp.where(hits[k], rank, 0.0), axis=0, keepdims=True).astype(jnp.int32)
    carry_ref[...] = carry_ref[...] + jnp.sum(mask, axis=1, keepdims=True)
    cnt_ref[...] = carry_ref[...]


def _rank(idx_kt):
    K, T = idx_kt.shape
    tm = 1024
    return pl.pallas_call(
        _rank_kernel,
        grid=(T // tm,),
        in_specs=[pl.BlockSpec((K, tm), lambda i: (0, i))],
        out_specs=[pl.BlockSpec((K, tm), lambda i: (0, i)),
                   pl.BlockSpec((N_EXPERTS, LANES), lambda i: (0, 0))],
        out_shape=[jax.ShapeDtypeStruct((K, T), jnp.int32),
                   jax.ShapeDtypeStruct((N_EXPERTS, LANES), F32)],
        scratch_shapes=[pltpu.VMEM((tm, tm), BF16), pltpu.VMEM((N_EXPERTS, LANES), F32)],
        compiler_params=_cparams(("arbitrary",)),
        name="rank",
    )(idx_kt)


def _dest_kernel(idx_ref, rank_ref, start_ref, dest_ref):
    tm = idx_ref.shape[1]
    e_iota = lax.broadcasted_iota(jnp.int32, (N_EXPERTS, tm), 0)
    start = start_ref[:, 0:1]
    for k in range(TOP_K):
        base = jnp.sum(jnp.where(e_iota == idx_ref[k:k + 1, :], start, 0.0), axis=0, keepdims=True)
        dest_ref[k:k + 1, :] = rank_ref[k:k + 1, :] + base.astype(jnp.int32)


def _dest(idx_kt, rank_kt, start):
    K, T = idx_kt.shape
    tm = 1024
    tok_spec = pl.BlockSpec((K, tm), lambda i: (0, i))
    return pl.pallas_call(
        _dest_kernel,
        grid=(T // tm,),
        in_specs=[tok_spec, tok_spec, pl.BlockSpec((N_EXPERTS, LANES), lambda i: (0, 0))],
        out_specs=tok_spec,
        out_shape=jax.ShapeDtypeStruct((K, T), jnp.int32),
        compiler_params=_cparams(("arbitrary",)),
        name="dest",
    )(idx_kt, rank_kt, start)


def _block_plan(counts):
    blk = DISPATCH_BLOCK
    seg_blocks = (counts.astype(jnp.int32) + blk - 1) // blk
    seg_end = jnp.cumsum(seg_blocks)
    seg_first = seg_end - seg_blocks
    return seg_first * blk, seg_first, seg_blocks, seg_end[-1:]


def _combine_kernel(h_ref, x1_ref, routed_ref, mod_ref, wsg_ref, wsu_ref, wsd_ref, lng_ref, lnb_ref, o_ref):
    lo, hi = _unpack_bf16_pairs(h_ref[...])
    a = _swiglu(lo.astype(BF16), hi.astype(BF16), wsg_ref, wsu_ref)
    ffn = jnp.dot(a, wsd_ref[...], preferred_element_type=F32) + routed_ref[...]
    g2 = mod_ref[0, 5:6, :]
    o_ref[...] = _layer_norm(DEEPNORM_ALPHA * x1_ref[...] + g2 * ffn, lng_ref[...], lnb_ref[...])


def _combine_into_kernel(h_ref, x1_ref, routed_ref, mod_ref, wsg_ref, wsu_ref, wsd_ref, lng_ref, lnb_ref,
                         prev_ref, o_ref):
    del prev_ref
    _combine_kernel(h_ref, x1_ref, routed_ref, mod_ref, wsg_ref, wsu_ref, wsd_ref, lng_ref, lnb_ref, o_ref)


def _combine(h2p, x1, routed, mod, wsg, wsu, wsd, lng, lnb, seq, b_off, in_off, out_rows, out_off, out_prev):
    n, D = routed.shape
    tm = TM_COMBINE
    per_seq = seq // tm
    in_blk = in_off // tm
    blk_off = out_off // tm
    const = lambda i: (0, 0)
    in_specs = [
        pl.BlockSpec((tm, HALF), lambda i: (i + in_blk, 0)),
        pl.BlockSpec((tm, D), lambda i: (i + in_blk, 0)),
        pl.BlockSpec((tm, D), lambda i: (i, 0)),
        pl.BlockSpec((1, 6, D), lambda i: ((i + in_blk) // per_seq + b_off, 0, 0)),
        pl.BlockSpec(wsg.shape, const),
        pl.BlockSpec(wsu.shape, const),
        pl.BlockSpec(wsd.shape, const),
        pl.BlockSpec(lng.shape, const),
        pl.BlockSpec(lnb.shape, const),
    ]
    args = [h2p, x1, routed, mod, wsg, wsu, wsd, lng, lnb]
    body = _combine_kernel
    aliases = {}
    if out_prev is not None:
        in_specs.append(pl.BlockSpec(memory_space=pl.ANY))
        args.append(out_prev)
        aliases = {len(args) - 1: 0}
        body = _combine_into_kernel
    return pl.pallas_call(
        body,
        grid=(n // tm,),
        in_specs=in_specs,
        out_specs=pl.BlockSpec((tm, D), lambda i: (i + blk_off, 0)),
        out_shape=jax.ShapeDtypeStruct((out_rows, D), F32),
        input_output_aliases=aliases,
        compiler_params=_cparams(("arbitrary",)),
        name="combine",
    )(*args)


def _rope_pair(w):
    half = QK_ROPE // 2
    return w, jnp.concatenate([-w[:, half:], w[:, :half]], axis=1)


def _prep_layer(w_in, w_uq, w_ukv, w_pool, w_out, w_router):
    lat = POOL_WIDTH + Q_LORA + KV_LORA
    k_a, k_b = _rope_pair(w_in[:, lat:lat + QK_ROPE])
    w_in_ext = jnp.concatenate([w_in[:, :lat], k_a, k_b], axis=1).astype(BF16)

    qa_cols = []
    per_head = QK_NOPE + QK_ROPE
    for hd in range(N_HEADS):
        w_h = w_uq[:, hd * per_head:(hd + 1) * per_head]
        r_a, r_b = _rope_pair(w_h[:, QK_NOPE:])
        qa_cols += [w_h[:, :QK_NOPE], r_a, r_b]
    w_uqa = jnp.concatenate(qa_cols, axis=1).astype(BF16)

    wpool_bd = jax.scipy.linalg.block_diag(*[w_pool[g] for g in range(len(POOL_WINDOWS))]).astype(BF16)
    wr_hi, wr_lo = _split_hi_lo(w_router.T)
    return dict(w_in_ext=w_in_ext, w_uqa=w_uqa, w_ukv=w_ukv.astype(BF16), wpool_bd=wpool_bd,
                wout_p=w_out[:POOL_WIDTH].astype(BF16), wout_a=w_out[POOL_WIDTH:].astype(BF16),
                wr_hi=wr_hi, wr_lo=wr_lo)


def kernel(x, c, positions, w_ada, b_ada, w_in, q_norm_g, kv_norm_g, w_uq, w_ukv, w_pool, pool_scale,
           w_out, ln1_g, ln1_b, w_router, router_bias, w_exp_gate, w_exp_up, w_exp_down,
           w_sh_gate, w_sh_up, w_sh_down, ln2_g, ln2_b):
    B, S, D = x.shape
    L = w_in.shape[0]
    row = lambda v: v.reshape(1, -1)

    mod_all = _ada_mod(c, w_ada, b_ada).reshape(L, B, 6, D)
    cs_all = _rope_table(positions)
    bs = B // N_STREAMS
    T = bs * S
    n_blocks = T * TOP_K // DISPATCH_BLOCK + N_EXPERTS
    xs_streams = [x] * N_STREAMS
    result = None
    for l in range(L):
        p = _prep_layer(w_in[l], w_uq[l], w_ukv[l], w_pool[l], w_out[l], w_router[l])
        shared = (w_sh_gate[l].astype(BF16), w_sh_up[l].astype(BF16), w_sh_down[l].astype(BF16))
        mod = mod_all[l]
        for s in range(N_STREAMS):
            xl = xs_streams[s]
            b_off = s * bs
            x_off = b_off if l == 0 else 0
            u, q, k, v = _in_proj(xl, mod, cs_all, p["w_in_ext"], row(q_norm_g[l]), row(kv_norm_g[l]),
                                  p["w_uqa"], p["w_ukv"], bs, x_off, b_off)
            attn = _attention(q, k, v)
            x1, h2p, lt = _post_attn(xl, u, attn, mod, p["wpool_bd"], row(pool_scale[l]), p["wout_p"],
                                     p["wout_a"], row(ln1_g[l]), row(ln1_b[l]), p["wr_hi"], p["wr_lo"],
                                     x_off, b_off)
            idx_kt, gates_kt = _route(lt.reshape(N_EXPERTS, T // LANES, LANES), router_bias[l])
            rank_kt, counts = _rank(idx_kt)
            pad_start, seg_first, seg_blocks, n_real = _block_plan(counts[:, 0])
            start = jnp.broadcast_to(pad_start.astype(F32)[:, None], (N_EXPERTS, LANES))
            dest_kt = _dest(idx_kt, rank_kt, start)
            h2p = h2p.reshape(T, HALF)
            rows = _sc_dispatch(h2p, dest_kt, n_blocks * DISPATCH_BLOCK)
            ys = _expert_mm(seg_first, seg_blocks, n_real, rows, w_exp_gate, w_exp_up, w_exp_down, l)
            tc = T // COMBINE_CHUNKS
            out = None
            for ch in range(COMBINE_CHUNKS):
                lo = ch * tc
                routed = _sc_gather_sum(ys, dest_kt[:, lo:lo + tc], gates_kt[:, lo:lo + tc])
                if l < L - 1:
                    out = _combine(h2p, x1.reshape(T, D), routed, mod, *shared, row(ln2_g[l]), row(ln2_b[l]),
                                   S, b_off, lo, T, lo, out)
                else:
                    result = _combine(h2p, x1.reshape(T, D), routed, mod, *shared, row(ln2_g[l]),
                                      row(ln2_b[l]), S, b_off, lo, B * S, s * T + lo, result)
            if l < L - 1:
                xs_streams[s] = out.reshape(bs, S, D)
    return result.reshape(B, S, D)
```

```python
import functools

import jax
import jax.numpy as jnp
from jax import lax
from jax.experimental import pallas as pl
from jax.experimental.pallas import tpu as pltpu
from jax.experimental.pallas import tpu_sc as plsc

F32 = jnp.float32
BF16 = jnp.bfloat16

D_MODEL = 1024
DEPTH = 4
POOL_WIDTH = 256
POOL_WINDOWS = (2, 4, 8, 16)
POOL_GROUP = 64
POOL_HALO = 16
QK_NOPE = 128
QK_ROPE = 64
V_HEAD = 128
N_HEADS = 6
Q_LORA = 384
KV_LORA = 256
ROPE_THETA = 10000.0
N_EXPERTS = 64
TOP_K = 8
N_GROUPS = 8
GROUP_SIZE = N_EXPERTS // N_GROUPS
TOPK_GROUPS = 4
D_EXPERT = 256
ROUTED_SCALE = 2.5
DEEPNORM_ALPHA = (2.0 * DEPTH) ** 0.25
LN_EPS = 1e-5
RMS_EPS = 1e-6
ATTN_SCALE = (QK_NOPE + QK_ROPE) ** -0.5
LOG2_E = 1.4426950408889634
Q_SCALE = ATTN_SCALE * LOG2_E
LANES = 128
QK_PAD = 2 * LANES
MASK_VALUE = -1e30

TM_PROJ = 512
TM_POST = 1024
TQ = 512
TK = 512
ATTN_HEADS = 2
ROUTE_ROWS = 8
DISPATCH_BLOCK = 512
EXPERT_RING = 6
TM_COMBINE = 1024
SC_CORES = 2
SC_SUBCORES = 16
SC_WINDOW = 128
SC_LANES = 16
HIGH_HALF_MASK = -65536
SC_SUM_TOKENS = 16
N_STREAMS = 2
TAIL_CHUNKS = 2
HALF = D_MODEL // 2
VMEM_LIMIT = 48 * 1024 * 1024


def _cparams(sem):
    return pltpu.CompilerParams(dimension_semantics=sem, vmem_limit_bytes=VMEM_LIMIT)


def _split_hi_lo(a):
    hi = a.astype(BF16)
    lo = (a - hi.astype(F32)).astype(BF16)
    return hi, lo


def _pack_bf16_pairs(a):
    bits = lax.bitcast_convert_type(a.astype(F32), jnp.uint32)
    half = a.shape[1] // 2
    word = (bits[:, :half] >> 16) | (bits[:, half:] & jnp.uint32(0xFFFF0000))
    return lax.bitcast_convert_type(word, jnp.int32)


def _unpack_bf16_pairs(w):
    bits = lax.bitcast_convert_type(w, jnp.uint32)
    lo = lax.bitcast_convert_type(bits << 16, F32)
    hi = lax.bitcast_convert_type(bits & jnp.uint32(0xFFFF0000), F32)
    return lo, hi


def _ada_kernel(c_ref, w_ref, b_ref, o_ref):
    c = c_ref[...]
    cond = c * jax.nn.sigmoid(c)
    c_hi, c_lo = _split_hi_lo(cond)
    w_hi, w_lo = _split_hi_lo(w_ref[0])
    acc = jnp.dot(c_hi, w_hi, preferred_element_type=F32)
    acc += jnp.dot(c_lo, w_hi, preferred_element_type=F32)
    acc += jnp.dot(c_hi, w_lo, preferred_element_type=F32)
    o_ref[0] = acc + b_ref[0]


def _ada_mod(c, w_ada, b_ada):
    L, D, N = w_ada.shape
    B = c.shape[0]
    tn = 1536
    return pl.pallas_call(
        _ada_kernel,
        grid=(L, N // tn),
        in_specs=[
            pl.BlockSpec((B, D), lambda l, j: (0, 0)),
            pl.BlockSpec((1, D, tn), lambda l, j: (l, 0, j)),
            pl.BlockSpec((1, 1, tn), lambda l, j: (l, 0, j)),
        ],
        out_specs=pl.BlockSpec((1, B, tn), lambda l, j: (l, 0, j)),
        out_shape=jax.ShapeDtypeStruct((L, B, N), F32),
        compiler_params=_cparams(("arbitrary", "arbitrary")),
        name="ada_mod",
    )(c, w_ada, b_ada.reshape(L, 1, N))


def _rope_table_kernel(pos_ref, freq_ref, o_ref):
    ang = pos_ref[0].astype(F32) * freq_ref[...]
    lane = lax.broadcasted_iota(jnp.int32, (1, LANES), 1)
    o_ref[0] = jnp.where(lane < QK_ROPE, jnp.cos(ang), jnp.sin(ang))


def _rope_table(positions):
    B, S = positions.shape
    half = QK_ROPE // 2
    inv_freq = ROPE_THETA ** (-jnp.arange(half, dtype=F32) / half)
    freq = jnp.tile(inv_freq, LANES // half).reshape(1, LANES)
    return pl.pallas_call(
        _rope_table_kernel,
        grid=(B,),
        in_specs=[
            pl.BlockSpec((1, S, 1), lambda b: (b, 0, 0)),
            pl.BlockSpec((1, LANES), lambda b: (0, 0)),
        ],
        out_specs=pl.BlockSpec((1, S, LANES), lambda b: (b, 0, 0)),
        out_shape=jax.ShapeDtypeStruct((B, S, LANES), F32),
        compiler_params=_cparams(("arbitrary",)),
        name="rope_table",
    )(positions.reshape(B, S, 1), freq)


def _rms(x, g):
    return x * lax.rsqrt(jnp.mean(x * x, axis=-1, keepdims=True) + RMS_EPS) * g


def _in_proj_kernel(x_ref, mod_ref, cs_ref, w_in_ref, qg_ref, kvg_ref, w_uqa_ref, w_ukv_ref,
                    u_ref, q_ref, k_ref, v_ref):
    x = x_ref[0]
    sh1 = mod_ref[0, 0:1, :]
    sc1 = mod_ref[0, 1:2, :]
    h = (x * (1.0 + sc1) + sh1).astype(BF16)
    proj = jnp.dot(h, w_in_ref[...], preferred_element_type=F32)
    u_ref[0] = proj[:, 0:POOL_WIDTH]
    o = POOL_WIDTH
    q_lat = proj[:, o:o + Q_LORA]
    o += Q_LORA
    kv_lat = proj[:, o:o + KV_LORA]
    o += KV_LORA
    cos_sin = cs_ref[0]

    def rotate(pair):
        t = pair * cos_sin
        return t + pltpu.roll(t, QK_ROPE, 1)

    k_rot = rotate(proj[:, o:o + LANES]).astype(BF16)
    lane = lax.broadcasted_iota(jnp.int32, (1, LANES), 1)
    q_keep = jnp.where(lane < QK_ROPE, Q_SCALE, 0.0)

    qn = _rms(q_lat, qg_ref[...]).astype(BF16)
    q_a = jnp.dot(qn, w_uqa_ref[...], preferred_element_type=F32)
    kvn = _rms(kv_lat, kvg_ref[...]).astype(BF16)
    kv = jnp.dot(kvn, w_ukv_ref[...], preferred_element_type=F32)
    for hd in range(N_HEADS):
        b0 = hd * QK_PAD
        q_ref[0, hd, :, 0:LANES] = (q_a[:, b0:b0 + LANES] * Q_SCALE).astype(BF16)
        q_ref[0, hd, :, LANES:QK_PAD] = (rotate(q_a[:, b0 + LANES:b0 + QK_PAD]) * q_keep).astype(BF16)
        k_ref[0, hd, :, 0:LANES] = kv[:, b0:b0 + QK_NOPE].astype(BF16)
        k_ref[0, hd, :, LANES:QK_PAD] = k_rot
        v_ref[0, hd] = kv[:, b0 + QK_NOPE:b0 + QK_NOPE + V_HEAD].astype(BF16)


def _in_proj(x, mod, cs, w_in_ext, qg, kvg, w_uqa, w_ukv_bf, B, x_off, b_off):
    _, S, D = x.shape
    tm = TM_PROJ
    const = lambda b, i: (0, 0)
    return pl.pallas_call(
        _in_proj_kernel,
        grid=(B, S // tm),
        in_specs=[
            pl.BlockSpec((1, tm, D), lambda b, i: (b + x_off, i, 0)),
            pl.BlockSpec((1, 6, D), lambda b, i: (b + b_off, 0, 0)),
            pl.BlockSpec((1, tm, LANES), lambda b, i: (b + b_off, i, 0)),
            pl.BlockSpec(w_in_ext.shape, const),
            pl.BlockSpec(qg.shape, const),
            pl.BlockSpec(kvg.shape, const),
            pl.BlockSpec(w_uqa.shape, const),
            pl.BlockSpec(w_ukv_bf.shape, const),
        ],
        out_specs=[
            pl.BlockSpec((1, tm, POOL_WIDTH), lambda b, i: (b, i, 0)),
            pl.BlockSpec((1, N_HEADS, tm, QK_PAD), lambda b, i: (b, 0, i, 0)),
            pl.BlockSpec((1, N_HEADS, tm, QK_PAD), lambda b, i: (b, 0, i, 0)),
            pl.BlockSpec((1, N_HEADS, tm, V_HEAD), lambda b, i: (b, 0, i, 0)),
        ],
        out_shape=[
            jax.ShapeDtypeStruct((B, S, POOL_WIDTH), F32),
            jax.ShapeDtypeStruct((B, N_HEADS, S, QK_PAD), BF16),
            jax.ShapeDtypeStruct((B, N_HEADS, S, QK_PAD), BF16),
            jax.ShapeDtypeStruct((B, N_HEADS, S, V_HEAD), BF16),
        ],
        compiler_params=_cparams(("arbitrary", "arbitrary")),
        name="in_proj",
    )(x, mod, cs, w_in_ext, qg, kvg, w_uqa, w_ukv_bf)


def _softmax_step(q, k, v, carry, mask):
    m, l, acc = carry
    s = lax.dot_general(q, k, (((1,), (1,)), ((), ())), preferred_element_type=F32)
    if mask is not None:
        s = jnp.where(mask, s, MASK_VALUE)
    m_new = jnp.maximum(m, jnp.max(s, axis=-1, keepdims=True))
    alpha = jnp.exp2(m - m_new)
    p = jnp.exp2(s - m_new)
    l_new = alpha * l + jnp.sum(p, axis=-1, keepdims=True)
    acc_new = alpha * acc + jnp.dot(p.astype(BF16), v, preferred_element_type=F32)
    return m_new, l_new, acc_new


def _attn_kernel(q_ref, k_ref, v_ref, o_ref):
    heads, seq = q_ref.shape[1], q_ref.shape[2]
    row = lax.broadcasted_iota(jnp.int32, (TQ, TK), 0)
    col = lax.broadcasted_iota(jnp.int32, (TQ, TK), 1)
    diag = row >= col
    for i in range(seq // TQ):
        qs = [q_ref[0, h, i * TQ:(i + 1) * TQ, :] for h in range(heads)]
        carries = [(jnp.full((TQ, 1), MASK_VALUE, F32), jnp.zeros((TQ, 1), F32), jnp.zeros((TQ, V_HEAD), F32))
                   for _ in range(heads)]
        for j in range(i + 1):
            for h in range(heads):
                k = k_ref[0, h, j * TK:(j + 1) * TK, :]
                v = v_ref[0, h, j * TK:(j + 1) * TK, :]
                carries[h] = _softmax_step(qs[h], k, v, carries[h], diag if j == i else None)
        for h in range(heads):
            _, l, acc = carries[h]
            o_ref[0, h, i * TQ:(i + 1) * TQ, :] = (acc / l).astype(BF16)


def _attention(q, k, v):
    B, H, S, _ = q.shape
    return pl.pallas_call(
        _attn_kernel,
        grid=(B, H // ATTN_HEADS),
        in_specs=[
            pl.BlockSpec((1, ATTN_HEADS, S, QK_PAD), lambda b, h: (b, h, 0, 0)),
            pl.BlockSpec((1, ATTN_HEADS, S, QK_PAD), lambda b, h: (b, h, 0, 0)),
            pl.BlockSpec((1, ATTN_HEADS, S, V_HEAD), lambda b, h: (b, h, 0, 0)),
        ],
        out_specs=pl.BlockSpec((1, ATTN_HEADS, S, V_HEAD), lambda b, h: (b, h, 0, 0)),
        out_shape=jax.ShapeDtypeStruct((B, H, S, V_HEAD), BF16),
        compiler_params=_cparams(("arbitrary", "arbitrary")),
        name="attention",
    )(q, k, v)


def _layer_norm(v, g, b):
    mu = jnp.mean(v, axis=-1, keepdims=True)
    d = v - mu
    var = jnp.mean(d * d, axis=-1, keepdims=True)
    return d * lax.rsqrt(var + LN_EPS) * g + b


def _post_kernel(x_ref, u_ref, halo_ref, attn_ref, mod_ref, wpool_ref, pscale_ref, wout_p_ref, wout_a_ref,
                 lng_ref, lnb_ref, wr_hi_ref, wr_lo_ref, x1_ref, h2_ref, lt_ref):
    i = pl.program_id(1)
    tm = u_ref.shape[1]
    u = u_ref[0]
    halo = jnp.where(i > 0, halo_ref[0], 0.0)
    ext = jnp.concatenate([halo, u], axis=0)
    s2 = ext + pltpu.roll(ext, 1, 0)
    s4 = s2 + pltpu.roll(s2, 2, 0)
    s8 = s4 + pltpu.roll(s4, 4, 0)
    s16 = s8 + pltpu.roll(s8, 8, 0)
    lane = lax.broadcasted_iota(jnp.int32, (1, POOL_WIDTH), 1)
    win = jnp.where(lane < POOL_GROUP, s2,
                    jnp.where(lane < 2 * POOL_GROUP, s4, jnp.where(lane < 3 * POOL_GROUP, s8, s16)))
    win = win[POOL_HALO:, :]
    width = jnp.where(lane < POOL_GROUP, POOL_WINDOWS[0],
                      jnp.where(lane < 2 * POOL_GROUP, POOL_WINDOWS[1],
                                jnp.where(lane < 3 * POOL_GROUP, POOL_WINDOWS[2], POOL_WINDOWS[3])))
    t = i * tm + lax.broadcasted_iota(jnp.int32, (tm, 1), 0)
    count = jnp.minimum(t + 1, width).astype(F32)
    token_mix = win / count - u
    pooled = jnp.dot(token_mix.astype(BF16), wpool_ref[...], preferred_element_type=F32) * pscale_ref[...]
    mixed = jnp.dot(pooled.astype(BF16), wout_p_ref[...], preferred_element_type=F32)
    attn = jnp.concatenate([attn_ref[0, hd] for hd in range(N_HEADS)], axis=1)
    mixed += jnp.dot(attn, wout_a_ref[...], preferred_element_type=F32)

    g1 = mod_ref[0, 2:3, :]
    sh2 = mod_ref[0, 3:4, :]
    sc2 = mod_ref[0, 4:5, :]
    x1 = _layer_norm(DEEPNORM_ALPHA * x_ref[0] + g1 * mixed, lng_ref[...], lnb_ref[...])
    x1_ref[0] = x1
    h2 = x1 * (1.0 + sc2) + sh2
    h_hi, h_lo = _split_hi_lo(h2)
    h2_ref[0] = _pack_bf16_pairs(h_hi)
    nt = (((1,), (1,)), ((), ()))
    lt = lax.dot_general(wr_hi_ref[...], h_hi, nt, preferred_element_type=F32)
    lt += lax.dot_general(wr_hi_ref[...], h_lo, nt, preferred_element_type=F32)
    lt += lax.dot_general(wr_lo_ref[...], h_hi, nt, preferred_element_type=F32)
    lt_ref[...] = lt


def _post_attn(x, u, attn, mod, wpool_bd, pscale, wout_p, wout_a, lng, lnb, wr_hi, wr_lo, x_off, b_off):
    B, S, _ = u.shape
    D = x.shape[2]
    tm = TM_POST
    nt = S // tm
    const = lambda b, i: (0, 0)
    halo_blocks = tm // POOL_HALO
    return pl.pallas_call(
        _post_kernel,
        grid=(B, nt),
        in_specs=[
            pl.BlockSpec((1, tm, D), lambda b, i: (b + x_off, i, 0)),
            pl.BlockSpec((1, tm, POOL_WIDTH), lambda b, i: (b, i, 0)),
            pl.BlockSpec((1, POOL_HALO, POOL_WIDTH), lambda b, i: (b, jnp.maximum(i * halo_blocks - 1, 0), 0)),
            pl.BlockSpec((1, N_HEADS, tm, V_HEAD), lambda b, i: (b, 0, i, 0)),
            pl.BlockSpec((1, 6, D), lambda b, i: (b + b_off, 0, 0)),
            pl.BlockSpec(wpool_bd.shape, const),
            pl.BlockSpec(pscale.shape, const),
            pl.BlockSpec(wout_p.shape, const),
            pl.BlockSpec(wout_a.shape, const),
            pl.BlockSpec(lng.shape, const),
            pl.BlockSpec(lnb.shape, const),
            pl.BlockSpec(wr_hi.shape, const),
            pl.BlockSpec(wr_lo.shape, const),
        ],
        out_specs=[
            pl.BlockSpec((1, tm, D), lambda b, i: (b, i, 0)),
            pl.BlockSpec((1, tm, HALF), lambda b, i: (b, i, 0)),
            pl.BlockSpec((N_EXPERTS, tm), lambda b, i: (0, b * nt + i)),
        ],
        out_shape=[
            jax.ShapeDtypeStruct((B, S, D), F32),
            jax.ShapeDtypeStruct((B, S, HALF), jnp.int32),
            jax.ShapeDtypeStruct((N_EXPERTS, B * S), F32),
        ],
        compiler_params=_cparams(("arbitrary", "arbitrary")),
        name="post_attn",
    )(x, u, u, attn, mod, wpool_bd, pscale, wout_p, wout_a, lng, lnb, wr_hi, wr_lo)


def _select_first_max(vals, n_rounds, payload=None):
    work = list(vals)
    sel = [None] * len(vals)
    rounds = []
    for _ in range(n_rounds):
        m = functools.reduce(jnp.maximum, work)
        taken = None
        win_idx = jnp.zeros(m.shape, jnp.int32)
        win_val = jnp.zeros(m.shape, F32)
        for e in range(len(work)):
            hit = work[e] == m
            first = hit if taken is None else jnp.logical_and(hit, jnp.logical_not(taken))
            taken = hit if taken is None else jnp.logical_or(taken, hit)
            sel[e] = first if sel[e] is None else jnp.logical_or(sel[e], first)
            work[e] = jnp.where(first, -jnp.inf, work[e])
            if payload is not None:
                win_idx = jnp.where(first, e, win_idx)
                win_val = jnp.where(first, payload[e], win_val)
        rounds.append((win_idx, win_val))
    return sel, rounds


def _route_kernel(bias_ref, lt_ref, idx_ref, gate_ref):
    scores = [jax.nn.sigmoid(lt_ref[e]) for e in range(N_EXPERTS)]
    choice = [scores[e] + bias_ref[e] for e in range(N_EXPERTS)]
    group_score = []
    for g in range(N_GROUPS):
        vals = choice[g * GROUP_SIZE:(g + 1) * GROUP_SIZE]
        sel2, _ = _select_first_max(vals, 2)
        group_score.append(functools.reduce(
            jnp.add, [jnp.where(sel2[j], vals[j], 0.0) for j in range(GROUP_SIZE)]))
    group_sel, _ = _select_first_max(group_score, TOPK_GROUPS)
    masked = [jnp.where(group_sel[e // GROUP_SIZE], choice[e], -jnp.inf) for e in range(N_EXPERTS)]
    _, rounds = _select_first_max(masked, TOP_K, payload=scores)
    total = functools.reduce(jnp.add, [w for _, w in rounds])
    for k, (e_k, w_k) in enumerate(rounds):
        g_k = w_k / total * ROUTED_SCALE
        for r in range(ROUTE_ROWS):
            idx_ref[k:k + 1, r * LANES:(r + 1) * LANES] = e_k[r:r + 1, :]
            gate_ref[k:k + 1, r * LANES:(r + 1) * LANES] = g_k[r:r + 1, :]


def _route(lt3, bias):
    E, R, _ = lt3.shape
    out_spec = pl.BlockSpec((TOP_K, ROUTE_ROWS * LANES), lambda r: (0, r))
    return pl.pallas_call(
        _route_kernel,
        grid=(R // ROUTE_ROWS,),
        in_specs=[
            pl.BlockSpec(memory_space=pltpu.SMEM),
            pl.BlockSpec((E, ROUTE_ROWS, LANES), lambda r: (0, r, 0)),
        ],
        out_specs=[out_spec, out_spec],
        out_shape=[jax.ShapeDtypeStruct((TOP_K, R * LANES), jnp.int32),
                   jax.ShapeDtypeStruct((TOP_K, R * LANES), F32)],
        compiler_params=_cparams(("arbitrary",)),
        name="route",
    )(bias, lt3)


def _sc_mesh():
    return plsc.VectorSubcoreMesh(core_axis_name="c", subcore_axis_name="s",
                                  num_cores=SC_CORES, num_subcores=SC_SUBCORES)


def _sc_worker_id():
    return lax.axis_index("s") * SC_CORES + lax.axis_index("c")


def _sc_gather_sum(ys, dest_kt, gates_kt):
    K, T = dest_kt.shape
    half = ys.shape[1]
    workers = SC_CORES * SC_SUBCORES
    per_worker = T // workers
    wt = SC_SUM_TOKENS
    n_win = per_worker // wt
    n_idx = K * wt
    assert T == workers * n_win * wt and n_idx <= SC_WINDOW and half % SC_LANES == 0
    dest_w = dest_kt.reshape(K, workers, n_win, wt).transpose(1, 2, 0, 3).reshape(workers, n_win, n_idx)
    gates_w = gates_kt.reshape(K, workers, n_win, wt).transpose(1, 2, 0, 3).reshape(workers, n_win, n_idx)

    @functools.partial(
        pl.kernel, mesh=_sc_mesh(),
        out_type=jax.ShapeDtypeStruct((T, 2 * half), F32),
        scratch_types=[
            pltpu.VMEM((n_win, n_idx), jnp.int32),
            pltpu.VMEM((n_idx, half), jnp.int32),
            pltpu.VMEM((n_win, n_idx), F32),
            pltpu.VMEM((wt, 2 * half), F32),
            pltpu.SemaphoreType.DMA,
        ],
        compiler_params=pltpu.CompilerParams(needs_layout_passes=False),
    )
    def gather_sum(ys_hbm, dest_hbm, gates_hbm, out_hbm, idx_v, rows_v, gate_v, out_v, sem):
        wid = _sc_worker_id()
        pltpu.sync_copy(dest_hbm.at[wid], idx_v)
        pltpu.sync_copy(gates_hbm.at[wid], gate_v)

        @pl.loop(0, n_win)
        def _(j):
            pltpu.async_copy(ys_hbm.at[idx_v.at[j]], rows_v, sem).wait()
            win = jnp.full((SC_LANES,), j, jnp.int32)

            @pl.loop(0, wt)
            def _(t):
                g = [plsc.load_gather(gate_v, [win, jnp.full((SC_LANES,), k * wt + t, jnp.int32)])
                     for k in range(K)]

                @plsc.parallel_loop(0, half // SC_LANES, unroll=4)
                def _(c):
                    col = pl.ds(c * SC_LANES, SC_LANES)
                    acc_lo = jnp.zeros((SC_LANES,), F32)
                    acc_hi = jnp.zeros((SC_LANES,), F32)
                    for k in range(K):
                        w = rows_v[k * wt + t, col]
                        acc_lo = acc_lo + g[k] * plsc.bitcast(w << 16, F32)
                        acc_hi = acc_hi + g[k] * plsc.bitcast(w & HIGH_HALF_MASK, F32)
                    out_v[t, col] = acc_lo
                    out_v[t, pl.ds(half + c * SC_LANES, SC_LANES)] = acc_hi

            pltpu.sync_copy(out_v, out_hbm.at[pl.ds(wid * per_worker + j * wt, wt)])

    return gather_sum(ys, dest_w, gates_w)


def _sc_dispatch(rows, dest_kt, n_slots):
    T, width = rows.shape
    K = dest_kt.shape[0]
    workers = SC_CORES * SC_SUBCORES
    per_worker = T // workers
    n_win = per_worker // SC_WINDOW
    assert T == workers * n_win * SC_WINDOW
    dest_w = dest_kt.reshape(K, workers, n_win, SC_WINDOW).transpose(1, 2, 0, 3)
    dest_w = dest_w.reshape(workers, n_win * K, SC_WINDOW)

    @functools.partial(
        pl.kernel, mesh=_sc_mesh(),
        out_type=jax.ShapeDtypeStruct((n_slots, width), rows.dtype),
        scratch_types=[
            pltpu.VMEM((n_win * K, SC_WINDOW), jnp.int32),
            pltpu.VMEM((SC_WINDOW, width), rows.dtype),
        ],
    )
    def dispatch(rows_hbm, dest_hbm, out_hbm, idx_v, rows_v):
        wid = _sc_worker_id()
        pltpu.sync_copy(dest_hbm.at[wid], idx_v)

        @pl.loop(0, n_win)
        def _(j):
            pltpu.sync_copy(rows_hbm.at[pl.ds(wid * per_worker + j * SC_WINDOW, SC_WINDOW)], rows_v)
            for k in range(K):
                pltpu.sync_copy(rows_v, out_hbm.at[idx_v.at[j * K + k]])

    return dispatch(rows, dest_w)


def _swiglu(h_lo, h_hi, wg_ref, wu_ref):
    def proj(w_ref):
        return (jnp.dot(h_lo, w_ref[:HALF, :], preferred_element_type=F32)
                + jnp.dot(h_hi, w_ref[HALF:, :], preferred_element_type=F32))
    gate = proj(wg_ref)
    up = proj(wu_ref)
    return (gate * jax.nn.sigmoid(gate) * up).astype(BF16)


def _expert_kernel(seg_first_ref, seg_blocks_ref, n_real_ref, xs_hbm, wg_ref, wu_ref, wd_ref, ys_hbm,
                   xbuf, ybuf, wg_bf, wu_bf, wd_bf, in_sems, out_sems):
    e = pl.program_id(0)
    first = seg_first_ref[e]
    count = seg_blocks_ref[e]
    total = n_real_ref[0]
    blk = DISPATCH_BLOCK
    ring = xbuf.shape[0]

    def in_copy(g):
        slot = g % ring
        return pltpu.make_async_copy(xs_hbm.at[pl.ds(g * blk, blk)], xbuf.at[slot], in_sems.at[slot])

    def out_copy(g):
        slot = g % ring
        return pltpu.make_async_copy(ybuf.at[slot], ys_hbm.at[pl.ds(g * blk, blk)], out_sems.at[slot])

    @pl.when(e == 0)
    def _():
        for g in range(ring - 1):
            @pl.when(g < total)
            def _():
                in_copy(g).start()

    @pl.when(count > 0)
    def _():
        wg_bf[...] = wg_ref[0, 0].astype(BF16)
        wu_bf[...] = wu_ref[0, 0].astype(BF16)
        wd_bf[...] = wd_ref[0, 0].astype(BF16)

    def block(g, _):
        slot = g % ring
        in_copy(g).wait()

        @pl.when(g + ring - 1 < total)
        def _():
            in_copy(g + ring - 1).start()

        @pl.when(g >= ring)
        def _():
            out_copy(g - ring).wait()

        lo, hi = _unpack_bf16_pairs(xbuf[slot])
        a = _swiglu(lo.astype(BF16), hi.astype(BF16), wg_bf, wu_bf)
        y = jnp.dot(a, wd_bf[...], preferred_element_type=F32)
        ybuf[slot] = _pack_bf16_pairs(y.astype(BF16))
        out_copy(g).start()
        return 0

    lax.fori_loop(first, first + count, block, 0)

    @pl.when(e == pl.num_programs(0) - 1)
    def _():
        for back in range(ring, 0, -1):
            @pl.when(total >= back)
            def _():
                out_copy(total - back).wait()


def _expert_mm(seg_first, seg_blocks, n_real, xs, wg, wu, wd, layer):
    P = xs.shape[0]
    D = wg.shape[2]
    w_map = lambda e, sf, sb, nr: (layer, e, 0, 0)
    grid_spec = pltpu.PrefetchScalarGridSpec(
        num_scalar_prefetch=3,
        grid=(N_EXPERTS,),
        in_specs=[
            pl.BlockSpec(memory_space=pl.ANY),
            pl.BlockSpec((1, 1, D, D_EXPERT), w_map),
            pl.BlockSpec((1, 1, D, D_EXPERT), w_map),
            pl.BlockSpec((1, 1, D_EXPERT, D), w_map),
        ],
        out_specs=pl.BlockSpec(memory_space=pl.ANY),
        scratch_shapes=[
            pltpu.VMEM((EXPERT_RING, DISPATCH_BLOCK, HALF), jnp.int32),
            pltpu.VMEM((EXPERT_RING, DISPATCH_BLOCK, HALF), jnp.int32),
            pltpu.VMEM((D, D_EXPERT), BF16), pltpu.VMEM((D, D_EXPERT), BF16), pltpu.VMEM((D_EXPERT, D), BF16),
            pltpu.SemaphoreType.DMA((EXPERT_RING,)), pltpu.SemaphoreType.DMA((EXPERT_RING,)),
        ],
    )
    return pl.pallas_call(
        _expert_kernel,
        grid_spec=grid_spec,
        out_shape=jax.ShapeDtypeStruct((P, HALF), jnp.int32),
        compiler_params=_cparams(("arbitrary",)),
        name="expert_mm",
    )(seg_first, seg_blocks, n_real, xs, wg, wu, wd)


def _rank_kernel(idx_ref, rank_ref, cnt_ref, tri_ref, carry_ref):
    i = pl.program_id(0)
    tm = idx_ref.shape[1]

    @pl.when(i == 0)
    def _():
        r = lax.broadcasted_iota(jnp.int32, (tm, tm), 0)
        c = lax.broadcasted_iota(jnp.int32, (tm, tm), 1)
        tri_ref[...] = (r < c).astype(BF16)
        carry_ref[...] = jnp.zeros(carry_ref.shape, F32)

    e_iota = lax.broadcasted_iota(jnp.int32, (N_EXPERTS, tm), 0)
    hits = [e_iota == idx_ref[k:k + 1, :] for k in range(TOP_K)]
    mask = functools.reduce(jnp.logical_or, hits).astype(F32)
    rank = jnp.dot(mask.astype(BF16), tri_ref[...], preferred_element_type=F32) + carry_ref[:, 0:1]
    for k in range(TOP_K):
        rank_ref[k:k + 1, :] = jnp.sum(jnp.where(hits[k], rank, 0.0), axis=0, keepdims=True).astype(jnp.int32)
    carry_ref[...] = carry_ref[...] + jnp.sum(mask, axis=1, keepdims=True)
    cnt_ref[...] = carry_ref[...]


def _rank(idx_kt):
    K, T = idx_kt.shape
    tm = 1024
    return pl.pallas_call(
        _rank_kernel,
        grid=(T // tm,),
        in_specs=[pl.BlockSpec((K, tm), lambda i: (0, i))],
        out_specs=[pl.BlockSpec((K, tm), lambda i: (0, i)),
                   pl.BlockSpec((N_EXPERTS, LANES), lambda i: (0, 0))],
        out_shape=[jax.ShapeDtypeStruct((K, T), jnp.int32),
                   jax.ShapeDtypeStruct((N_EXPERTS, LANES), F32)],
        scratch_shapes=[pltpu.VMEM((tm, tm), BF16), pltpu.VMEM((N_EXPERTS, LANES), F32)],
        compiler_params=_cparams(("arbitrary",)),
        name="rank",
    )(idx_kt)


def _dest_kernel(idx_ref, rank_ref, start_ref, dest_ref):
    tm = idx_ref.shape[1]
    e_iota = lax.broadcasted_iota(jnp.int32, (N_EXPERTS, tm), 0)
    start = start_ref[:, 0:1]
    for k in range(TOP_K):
        base = jnp.sum(jnp.where(e_iota == idx_ref[k:k + 1, :], start, 0.0), axis=0, keepdims=True)
        dest_ref[k:k + 1, :] = rank_ref[k:k + 1, :] + base.astype(jnp.int32)


def _dest(idx_kt, rank_kt, start):
    K, T = idx_kt.shape
    tm = 1024
    tok_spec = pl.BlockSpec((K, tm), lambda i: (0, i))
    return pl.pallas_call(
        _dest_kernel,
        grid=(T // tm,),
        in_specs=[tok_spec, tok_spec, pl.BlockSpec((N_EXPERTS, LANES), lambda i: (0, 0))],
        out_specs=tok_spec,
        out_shape=jax.ShapeDtypeStruct((K, T), jnp.int32),
        compiler_params=_cparams(("arbitrary",)),
        name="dest",
    )(idx_kt, rank_kt, start)


def _block_plan(counts):
    blk = DISPATCH_BLOCK
    seg_blocks = (counts.astype(jnp.int32) + blk - 1) // blk
    seg_end = jnp.cumsum(seg_blocks)
    seg_first = seg_end - seg_blocks
    return seg_first * blk, seg_first, seg_blocks, seg_end[-1:]


def _combine_kernel(h_ref, x1_ref, routed_ref, mod_ref, wsg_ref, wsu_ref, wsd_ref, lng_ref, lnb_ref, o_ref):
    lo, hi = _unpack_bf16_pairs(h_ref[...])
    a = _swiglu(lo.astype(BF16), hi.astype(BF16), wsg_ref, wsu_ref)
    ffn = jnp.dot(a, wsd_ref[...], preferred_element_type=F32) + routed_ref[...]
    g2 = mod_ref[0, 5:6, :]
    o_ref[...] = _layer_norm(DEEPNORM_ALPHA * x1_ref[...] + g2 * ffn, lng_ref[...], lnb_ref[...])


def _combine_into_kernel(h_ref, x1_ref, routed_ref, mod_ref, wsg_ref, wsu_ref, wsd_ref, lng_ref, lnb_ref,
                         prev_ref, o_ref):
    del prev_ref
    _combine_kernel(h_ref, x1_ref, routed_ref, mod_ref, wsg_ref, wsu_ref, wsd_ref, lng_ref, lnb_ref, o_ref)


def _combine(h2p, x1, routed, mod, wsg, wsu, wsd, lng, lnb, seq, b_off, in_off, out_rows, out_off, out_prev):
    n, D = routed.shape
    tm = TM_COMBINE
    per_seq = seq // tm
    in_blk = in_off // tm
    blk_off = out_off // tm
    const = lambda i: (0, 0)
    in_specs = [
        pl.BlockSpec((tm, HALF), lambda i: (i + in_blk, 0)),
        pl.BlockSpec((tm, D), lambda i: (i + in_blk, 0)),
        pl.BlockSpec((tm, D), lambda i: (i, 0)),
        pl.BlockSpec((1, 6, D), lambda i: ((i + in_blk) // per_seq + b_off, 0, 0)),
        pl.BlockSpec(wsg.shape, const),
        pl.BlockSpec(wsu.shape, const),
        pl.BlockSpec(wsd.shape, const),
        pl.BlockSpec(lng.shape, const),
        pl.BlockSpec(lnb.shape, const),
    ]
    args = [h2p, x1, routed, mod, wsg, wsu, wsd, lng, lnb]
    body = _combine_kernel
    aliases = {}
    if out_prev is not None:
        in_specs.append(pl.BlockSpec(memory_space=pl.ANY))
        args.append(out_prev)
        aliases = {len(args) - 1: 0}
        body = _combine_into_kernel
    return pl.pallas_call(
        body,
        grid=(n // tm,),
        in_specs=in_specs,
        out_specs=pl.BlockSpec((tm, D), lambda i: (i + blk_off, 0)),
        out_shape=jax.ShapeDtypeStruct((out_rows, D), F32),
        input_output_aliases=aliases,
        compiler_params=_cparams(("arbitrary",)),
        name="combine",
    )(*args)


def _rope_pair(w):
    half = QK_ROPE // 2
    return w, jnp.concatenate([-w[:, half:], w[:, :half]], axis=1)


def _prep_layer(w_in, w_uq, w_ukv, w_pool, w_out, w_router):
    lat = POOL_WIDTH + Q_LORA + KV_LORA
    k_a, k_b = _rope_pair(w_in[:, lat:lat + QK_ROPE])
    w_in_ext = jnp.concatenate([w_in[:, :lat], k_a, k_b], axis=1).astype(BF16)

    qa_cols = []
    per_head = QK_NOPE + QK_ROPE
    for hd in range(N_HEADS):
        w_h = w_uq[:, hd * per_head:(hd + 1) * per_head]
        r_a, r_b = _rope_pair(w_h[:, QK_NOPE:])
        qa_cols += [w_h[:, :QK_NOPE], r_a, r_b]
    w_uqa = jnp.concatenate(qa_cols, axis=1).astype(BF16)

    wpool_bd = jax.scipy.linalg.block_diag(*[w_pool[g] for g in range(len(POOL_WINDOWS))]).astype(BF16)
    wr_hi, wr_lo = _split_hi_lo(w_router.T)
    return dict(w_in_ext=w_in_ext, w_uqa=w_uqa, w_ukv=w_ukv.astype(BF16), wpool_bd=wpool_bd,
                wout_p=w_out[:POOL_WIDTH].astype(BF16), wout_a=w_out[POOL_WIDTH:].astype(BF16),
                wr_hi=wr_hi, wr_lo=wr_lo)


def kernel(x, c, positions, w_ada, b_ada, w_in, q_norm_g, kv_norm_g, w_uq, w_ukv, w_pool, pool_scale,
           w_out, ln1_g, ln1_b, w_router, router_bias, w_exp_gate, w_exp_up, w_exp_down,
           w_sh_gate, w_sh_up, w_sh_down, ln2_g, ln2_b):
    B, S, D = x.shape
    L = w_in.shape[0]
    row = lambda v: v.reshape(1, -1)

    mod_all = _ada_mod(c, w_ada, b_ada).reshape(L, B, 6, D)
    cs_all = _rope_table(positions)
    bs = B // N_STREAMS
    T = bs * S
    n_blocks = T * TOP_K // DISPATCH_BLOCK + N_EXPERTS
    xs_streams = [x] * N_STREAMS
    result = None
    for l in range(L):
        p = _prep_layer(w_in[l], w_uq[l], w_ukv[l], w_pool[l], w_out[l], w_router[l])
        shared = (w_sh_gate[l].astype(BF16), w_sh_up[l].astype(BF16), w_sh_down[l].astype(BF16))
        mod = mod_all[l]
        for s in range(N_STREAMS):
            xl = xs_streams[s]
            b_off = s * bs
            x_off = b_off if l == 0 else 0
            u, q, k, v = _in_proj(xl, mod, cs_all, p["w_in_ext"], row(q_norm_g[l]), row(kv_norm_g[l]),
                                  p["w_uqa"], p["w_ukv"], bs, x_off, b_off)
            attn = _attention(q, k, v)
            x1, h2p, lt = _post_attn(xl, u, attn, mod, p["wpool_bd"], row(pool_scale[l]), p["wout_p"],
                                     p["wout_a"], row(ln1_g[l]), row(ln1_b[l]), p["wr_hi"], p["wr_lo"],
                                     x_off, b_off)
            idx_kt, gates_kt = _route(lt.reshape(N_EXPERTS, T // LANES, LANES), router_bias[l])
            rank_kt, counts = _rank(idx_kt)
            pad_start, seg_first, seg_blocks, n_real = _block_plan(counts[:, 0])
            start = jnp.broadcast_to(pad_start.astype(F32)[:, None], (N_EXPERTS, LANES))
            dest_kt = _dest(idx_kt, rank_kt, start)
            h2p = h2p.reshape(T, HALF)
            rows = _sc_dispatch(h2p, dest_kt, n_blocks * DISPATCH_BLOCK)
            ys = _expert_mm(seg_first, seg_blocks, n_real, rows, w_exp_gate, w_exp_up, w_exp_down, l)
            n_chunks = TAIL_CHUNKS if l == L - 1 else 1
            tc = T // n_chunks
            out = None
            for ch in range(n_chunks):
                lo = ch * tc
                routed = _sc_gather_sum(ys, dest_kt[:, lo:lo + tc], gates_kt[:, lo:lo + tc])
                if l < L - 1:
                    out = _combine(h2p, x1.reshape(T, D), routed, mod, *shared, row(ln2_g[l]), row(ln2_b[l]),
                                   S, b_off, lo, T, lo, out)
                else:
                    result = _combine(h2p, x1.reshape(T, D), routed, mod, *shared, row(ln2_g[l]),
                                      row(ln2_b[l]), S, b_off, lo, B * S, s * T + lo, result)
            if l < L - 1:
                xs_streams[s] = out.reshape(bs, S, D)
    return result.reshape(B, S, D)
```

```python
import functools

import jax
import jax.numpy as jnp
from jax import lax
from jax.experimental import pallas as pl
from jax.experimental.pallas import tpu as pltpu
from jax.experimental.pallas import tpu_sc as plsc

F32 = jnp.float32
BF16 = jnp.bfloat16

D_MODEL = 1024
DEPTH = 4
POOL_WIDTH = 256
POOL_WINDOWS = (2, 4, 8, 16)
POOL_GROUP = 64
POOL_HALO = 16
QK_NOPE = 128
QK_ROPE = 64
V_HEAD = 128
N_HEADS = 6
Q_LORA = 384
KV_LORA = 256
ROPE_THETA = 10000.0
N_EXPERTS = 64
TOP_K = 8
N_GROUPS = 8
GROUP_SIZE = N_EXPERTS // N_GROUPS
TOPK_GROUPS = 4
D_EXPERT = 256
ROUTED_SCALE = 2.5
DEEPNORM_ALPHA = (2.0 * DEPTH) ** 0.25
LN_EPS = 1e-5
RMS_EPS = 1e-6
ATTN_SCALE = (QK_NOPE + QK_ROPE) ** -0.5
LOG2_E = 1.4426950408889634
Q_SCALE = ATTN_SCALE * LOG2_E
LANES = 128
QK_PAD = 2 * LANES
MASK_VALUE = -1e30

TM_PROJ = 512
TM_POST = 1024
TQ = 512
TK = 512
ATTN_HEADS = 2
ROUTE_ROWS = 8
DISPATCH_BLOCK = 512
EXPERT_RING = 6
TM_COMBINE = 1024
SC_CORES = 2
SC_SUBCORES = 16
SC_WINDOW = 128
SC_LANES = 16
HIGH_HALF_MASK = -65536
SC_SUM_TOKENS = 16
N_STREAMS = 2
COMBINE_CHUNKS = 2
HALF = D_MODEL // 2
VMEM_LIMIT = 48 * 1024 * 1024


def _cparams(sem):
    return pltpu.CompilerParams(dimension_semantics=sem, vmem_limit_bytes=VMEM_LIMIT)


def _split_hi_lo(a):
    hi = a.astype(BF16)
    lo = (a - hi.astype(F32)).astype(BF16)
    return hi, lo


def _pack_bf16_pairs(a):
    bits = lax.bitcast_convert_type(a.astype(F32), jnp.uint32)
    half = a.shape[1] // 2
    word = (bits[:, :half] >> 16) | (bits[:, half:] & jnp.uint32(0xFFFF0000))
    return lax.bitcast_convert_type(word, jnp.int32)


def _unpack_bf16_pairs(w):
    bits = lax.bitcast_convert_type(w, jnp.uint32)
    lo = lax.bitcast_convert_type(bits << 16, F32)
    hi = lax.bitcast_convert_type(bits & jnp.uint32(0xFFFF0000), F32)
    return lo, hi


def _ada_kernel(c_ref, w_ref, b_ref, o_ref):
    c = c_ref[...]
    cond = c * jax.nn.sigmoid(c)
    c_hi, c_lo = _split_hi_lo(cond)
    w_hi, w_lo = _split_hi_lo(w_ref[0])
    acc = jnp.dot(c_hi, w_hi, preferred_element_type=F32)
    acc += jnp.dot(c_lo, w_hi, preferred_element_type=F32)
    acc += jnp.dot(c_hi, w_lo, preferred_element_type=F32)
    o_ref[0] = acc + b_ref[0]


def _ada_mod(c, w_ada, b_ada):
    L, D, N = w_ada.shape
    B = c.shape[0]
    tn = 1536
    return pl.pallas_call(
        _ada_kernel,
        grid=(L, N // tn),
        in_specs=[
            pl.BlockSpec((B, D), lambda l, j: (0, 0)),
            pl.BlockSpec((1, D, tn), lambda l, j: (l, 0, j)),
            pl.BlockSpec((1, 1, tn), lambda l, j: (l, 0, j)),
        ],
        out_specs=pl.BlockSpec((1, B, tn), lambda l, j: (l, 0, j)),
        out_shape=jax.ShapeDtypeStruct((L, B, N), F32),
        compiler_params=_cparams(("arbitrary", "arbitrary")),
        name="ada_mod",
    )(c, w_ada, b_ada.reshape(L, 1, N))


def _rope_table_kernel(pos_ref, freq_ref, o_ref):
    ang = pos_ref[0].astype(F32) * freq_ref[...]
    lane = lax.broadcasted_iota(jnp.int32, (1, LANES), 1)
    o_ref[0] = jnp.where(lane < QK_ROPE, jnp.cos(ang), jnp.sin(ang))


def _rope_table(positions):
    B, S = positions.shape
    half = QK_ROPE // 2
    inv_freq = ROPE_THETA ** (-jnp.arange(half, dtype=F32) / half)
    freq = jnp.tile(inv_freq, LANES // half).reshape(1, LANES)
    return pl.pallas_call(
        _rope_table_kernel,
        grid=(B,),
        in_specs=[
            pl.BlockSpec((1, S, 1), lambda b: (b, 0, 0)),
            pl.BlockSpec((1, LANES), lambda b: (0, 0)),
        ],
        out_specs=pl.BlockSpec((1, S, LANES), lambda b: (b, 0, 0)),
        out_shape=jax.ShapeDtypeStruct((B, S, LANES), F32),
        compiler_params=_cparams(("arbitrary",)),
        name="rope_table",
    )(positions.reshape(B, S, 1), freq)


def _rms(x, g):
    return x * lax.rsqrt(jnp.mean(x * x, axis=-1, keepdims=True) + RMS_EPS) * g


def _in_proj_kernel(x_ref, mod_ref, cs_ref, w_in_ref, qg_ref, kvg_ref, w_uqa_ref, w_ukv_ref,
                    u_ref, q_ref, k_ref, v_ref):
    x = x_ref[0]
    sh1 = mod_ref[0, 0:1, :]
    sc1 = mod_ref[0, 1:2, :]
    h = (x * (1.0 + sc1) + sh1).astype(BF16)
    proj = jnp.dot(h, w_in_ref[...], preferred_element_type=F32)
    u_ref[0] = proj[:, 0:POOL_WIDTH]
    o = POOL_WIDTH
    q_lat = proj[:, o:o + Q_LORA]
    o += Q_LORA
    kv_lat = proj[:, o:o + KV_LORA]
    o += KV_LORA
    cos_sin = cs_ref[0]

    def rotate(pair):
        t = pair * cos_sin
        return t + pltpu.roll(t, QK_ROPE, 1)

    k_rot = rotate(proj[:, o:o + LANES]).astype(BF16)
    lane = lax.broadcasted_iota(jnp.int32, (1, LANES), 1)
    q_keep = jnp.where(lane < QK_ROPE, Q_SCALE, 0.0)

    qn = _rms(q_lat, qg_ref[...]).astype(BF16)
    q_a = jnp.dot(qn, w_uqa_ref[...], preferred_element_type=F32)
    kvn = _rms(kv_lat, kvg_ref[...]).astype(BF16)
    kv = jnp.dot(kvn, w_ukv_ref[...], preferred_element_type=F32)
    for hd in range(N_HEADS):
        b0 = hd * QK_PAD
        q_ref[0, hd, :, 0:LANES] = (q_a[:, b0:b0 + LANES] * Q_SCALE).astype(BF16)
        q_ref[0, hd, :, LANES:QK_PAD] = (rotate(q_a[:, b0 + LANES:b0 + QK_PAD]) * q_keep).astype(BF16)
        k_ref[0, hd, :, 0:LANES] = kv[:, b0:b0 + QK_NOPE].astype(BF16)
        k_ref[0, hd, :, LANES:QK_PAD] = k_rot
        v_ref[0, hd] = kv[:, b0 + QK_NOPE:b0 + QK_NOPE + V_HEAD].astype(BF16)


def _in_proj(x, mod, cs, w_in_ext, qg, kvg, w_uqa, w_ukv_bf, B, x_off, b_off):
    _, S, D = x.shape
    tm = TM_PROJ
    const = lambda b, i: (0, 0)
    return pl.pallas_call(
        _in_proj_kernel,
        grid=(B, S // tm),
        in_specs=[
            pl.BlockSpec((1, tm, D), lambda b, i: (b + x_off, i, 0)),
            pl.BlockSpec((1, 6, D), lambda b, i: (b + b_off, 0, 0)),
            pl.BlockSpec((1, tm, LANES), lambda b, i: (b + b_off, i, 0)),
            pl.BlockSpec(w_in_ext.shape, const),
            pl.BlockSpec(qg.shape, const),
            pl.BlockSpec(kvg.shape, const),
            pl.BlockSpec(w_uqa.shape, const),
            pl.BlockSpec(w_ukv_bf.shape, const),
        ],
        out_specs=[
            pl.BlockSpec((1, tm, POOL_WIDTH), lambda b, i: (b, i, 0)),
            pl.BlockSpec((1, N_HEADS, tm, QK_PAD), lambda b, i: (b, 0, i, 0)),
            pl.BlockSpec((1, N_HEADS, tm, QK_PAD), lambda b, i: (b, 0, i, 0)),
            pl.BlockSpec((1, N_HEADS, tm, V_HEAD), lambda b, i: (b, 0, i, 0)),
        ],
        out_shape=[
            jax.ShapeDtypeStruct((B, S, POOL_WIDTH), F32),
            jax.ShapeDtypeStruct((B, N_HEADS, S, QK_PAD), BF16),
            jax.ShapeDtypeStruct((B, N_HEADS, S, QK_PAD), BF16),
            jax.ShapeDtypeStruct((B, N_HEADS, S, V_HEAD), BF16),
        ],
        compiler_params=_cparams(("arbitrary", "arbitrary")),
        name="in_proj",
    )(x, mod, cs, w_in_ext, qg, kvg, w_uqa, w_ukv_bf)


def _softmax_step(q, k, v, carry, mask):
    m, l, acc = carry
    s = lax.dot_general(q, k, (((1,), (1,)), ((), ())), preferred_element_type=F32)
    if mask is not None:
        s = jnp.where(mask, s, MASK_VALUE)
    m_new = jnp.maximum(m, jnp.max(s, axis=-1, keepdims=True))
    alpha = jnp.exp2(m - m_new)
    p = jnp.exp2(s - m_new)
    l_new = alpha * l + jnp.sum(p, axis=-1, keepdims=True)
    acc_new = alpha * acc + jnp.dot(p.astype(BF16), v, preferred_element_type=F32)
    return m_new, l_new, acc_new


def _attn_kernel(q_ref, k_ref, v_ref, o_ref):
    heads, seq = q_ref.shape[1], q_ref.shape[2]
    row = lax.broadcasted_iota(jnp.int32, (TQ, TK), 0)
    col = lax.broadcasted_iota(jnp.int32, (TQ, TK), 1)
    diag = row >= col
    for i in range(seq // TQ):
        qs = [q_ref[0, h, i * TQ:(i + 1) * TQ, :] for h in range(heads)]
        carries = [(jnp.full((TQ, 1), MASK_VALUE, F32), jnp.zeros((TQ, 1), F32), jnp.zeros((TQ, V_HEAD), F32))
                   for _ in range(heads)]
        for j in range(i + 1):
            for h in range(heads):
                k = k_ref[0, h, j * TK:(j + 1) * TK, :]
                v = v_ref[0, h, j * TK:(j + 1) * TK, :]
                carries[h] = _softmax_step(qs[h], k, v, carries[h], diag if j == i else None)
        for h in range(heads):
            _, l, acc = carries[h]
            o_ref[0, h, i * TQ:(i + 1) * TQ, :] = (acc / l).astype(BF16)


def _attention(q, k, v):
    B, H, S, _ = q.shape
    return pl.pallas_call(
        _attn_kernel,
        grid=(B, H // ATTN_HEADS),
        in_specs=[
            pl.BlockSpec((1, ATTN_HEADS, S, QK_PAD), lambda b, h: (b, h, 0, 0)),
            pl.BlockSpec((1, ATTN_HEADS, S, QK_PAD), lambda b, h: (b, h, 0, 0)),
            pl.BlockSpec((1, ATTN_HEADS, S, V_HEAD), lambda b, h: (b, h, 0, 0)),
        ],
        out_specs=pl.BlockSpec((1, ATTN_HEADS, S, V_HEAD), lambda b, h: (b, h, 0, 0)),
        out_shape=jax.ShapeDtypeStruct((B, H, S, V_HEAD), BF16),
        compiler_params=_cparams(("arbitrary", "arbitrary")),
        name="attention",
    )(q, k, v)


def _layer_norm(v, g, b):
    mu = jnp.mean(v, axis=-1, keepdims=True)
    d = v - mu
    var = jnp.mean(d * d, axis=-1, keepdims=True)
    return d * lax.rsqrt(var + LN_EPS) * g + b


def _post_kernel(x_ref, u_ref, halo_ref, attn_ref, mod_ref, wpool_ref, pscale_ref, wout_p_ref, wout_a_ref,
                 lng_ref, lnb_ref, wr_hi_ref, wr_lo_ref, x1_ref, h2_ref, lt_ref):
    i = pl.program_id(1)
    tm = u_ref.shape[1]
    u = u_ref[0]
    halo = jnp.where(i > 0, halo_ref[0], 0.0)
    ext = jnp.concatenate([halo, u], axis=0)
    s2 = ext + pltpu.roll(ext, 1, 0)
    s4 = s2 + pltpu.roll(s2, 2, 0)
    s8 = s4 + pltpu.roll(s4, 4, 0)
    s16 = s8 + pltpu.roll(s8, 8, 0)
    lane = lax.broadcasted_iota(jnp.int32, (1, POOL_WIDTH), 1)
    win = jnp.where(lane < POOL_GROUP, s2,
                    jnp.where(lane < 2 * POOL_GROUP, s4, jnp.where(lane < 3 * POOL_GROUP, s8, s16)))
    win = win[POOL_HALO:, :]
    width = jnp.where(lane < POOL_GROUP, POOL_WINDOWS[0],
                      jnp.where(lane < 2 * POOL_GROUP, POOL_WINDOWS[1],
                                jnp.where(lane < 3 * POOL_GROUP, POOL_WINDOWS[2], POOL_WINDOWS[3])))
    t = i * tm + lax.broadcasted_iota(jnp.int32, (tm, 1), 0)
    count = jnp.minimum(t + 1, width).astype(F32)
    token_mix = win / count - u
    pooled = jnp.dot(token_mix.astype(BF16), wpool_ref[...], preferred_element_type=F32) * pscale_ref[...]
    mixed = jnp.dot(pooled.astype(BF16), wout_p_ref[...], preferred_element_type=F32)
    attn = jnp.concatenate([attn_ref[0, hd] for hd in range(N_HEADS)], axis=1)
    mixed += jnp.dot(attn, wout_a_ref[...], preferred_element_type=F32)

    g1 = mod_ref[0, 2:3, :]
    sh2 = mod_ref[0, 3:4, :]
    sc2 = mod_ref[0, 4:5, :]
    x1 = _layer_norm(DEEPNORM_ALPHA * x_ref[0] + g1 * mixed, lng_ref[...], lnb_ref[...])
    x1_ref[0] = x1
    h2 = x1 * (1.0 + sc2) + sh2
    h_hi, h_lo = _split_hi_lo(h2)
    h2_ref[0] = _pack_bf16_pairs(h_hi)
    nt = (((1,), (1,)), ((), ()))
    lt = lax.dot_general(wr_hi_ref[...], h_hi, nt, preferred_element_type=F32)
    lt += lax.dot_general(wr_hi_ref[...], h_lo, nt, preferred_element_type=F32)
    lt += lax.dot_general(wr_lo_ref[...], h_hi, nt, preferred_element_type=F32)
    lt_ref[...] = lt


def _post_attn(x, u, attn, mod, wpool_bd, pscale, wout_p, wout_a, lng, lnb, wr_hi, wr_lo, x_off, b_off):
    B, S, _ = u.shape
    D = x.shape[2]
    tm = TM_POST
    nt = S // tm
    const = lambda b, i: (0, 0)
    halo_blocks = tm // POOL_HALO
    return pl.pallas_call(
        _post_kernel,
        grid=(B, nt),
        in_specs=[
            pl.BlockSpec((1, tm, D), lambda b, i: (b + x_off, i, 0)),
            pl.BlockSpec((1, tm, POOL_WIDTH), lambda b, i: (b, i, 0)),
            pl.BlockSpec((1, POOL_HALO, POOL_WIDTH), lambda b, i: (b, jnp.maximum(i * halo_blocks - 1, 0), 0)),
            pl.BlockSpec((1, N_HEADS, tm, V_HEAD), lambda b, i: (b, 0, i, 0)),
            pl.BlockSpec((1, 6, D), lambda b, i: (b + b_off, 0, 0)),
            pl.BlockSpec(wpool_bd.shape, const),
            pl.BlockSpec(pscale.shape, const),
            pl.BlockSpec(wout_p.shape, const),
            pl.BlockSpec(wout_a.shape, const),
            pl.BlockSpec(lng.shape, const),
            pl.BlockSpec(lnb.shape, const),
            pl.BlockSpec(wr_hi.shape, const),
            pl.BlockSpec(wr_lo.shape, const),
        ],
        out_specs=[
            pl.BlockSpec((1, tm, D), lambda b, i: (b, i, 0)),
            pl.BlockSpec((1, tm, HALF), lambda b, i: (b, i, 0)),
            pl.BlockSpec((N_EXPERTS, tm), lambda b, i: (0, b * nt + i)),
        ],
        out_shape=[
            jax.ShapeDtypeStruct((B, S, D), F32),
            jax.ShapeDtypeStruct((B, S, HALF), jnp.int32),
            jax.ShapeDtypeStruct((N_EXPERTS, B * S), F32),
        ],
        compiler_params=_cparams(("arbitrary", "arbitrary")),
        name="post_attn",
    )(x, u, u, attn, mod, wpool_bd, pscale, wout_p, wout_a, lng, lnb, wr_hi, wr_lo)


def _select_first_max(vals, n_rounds, payload=None):
    work = list(vals)
    sel = [None] * len(vals)
    rounds = []
    for _ in range(n_rounds):
        m = functools.reduce(jnp.maximum, work)
        taken = None
        win_idx = jnp.zeros(m.shape, jnp.int32)
        win_val = jnp.zeros(m.shape, F32)
        for e in range(len(work)):
            hit = work[e] == m
            first = hit if taken is None else jnp.logical_and(hit, jnp.logical_not(taken))
            taken = hit if taken is None else jnp.logical_or(taken, hit)
            sel[e] = first if sel[e] is None else jnp.logical_or(sel[e], first)
            work[e] = jnp.where(first, -jnp.inf, work[e])
            if payload is not None:
                win_idx = jnp.where(first, e, win_idx)
                win_val = jnp.where(first, payload[e], win_val)
        rounds.append((win_idx, win_val))
    return sel, rounds


def _route_kernel(bias_ref, lt_ref, idx_ref, gate_ref):
    scores = [jax.nn.sigmoid(lt_ref[e]) for e in range(N_EXPERTS)]
    choice = [scores[e] + bias_ref[e] for e in range(N_EXPERTS)]
    group_score = []
    for g in range(N_GROUPS):
        vals = choice[g * GROUP_SIZE:(g + 1) * GROUP_SIZE]
        sel2, _ = _select_first_max(vals, 2)
        group_score.append(functools.reduce(
            jnp.add, [jnp.where(sel2[j], vals[j], 0.0) for j in range(GROUP_SIZE)]))
    group_sel, _ = _select_first_max(group_score, TOPK_GROUPS)
    masked = [jnp.where(group_sel[e // GROUP_SIZE], choice[e], -jnp.inf) for e in range(N_EXPERTS)]
    _, rounds = _select_first_max(masked, TOP_K, payload=scores)
    total = functools.reduce(jnp.add, [w for _, w in rounds])
    for k, (e_k, w_k) in enumerate(rounds):
        g_k = w_k / total * ROUTED_SCALE
        for r in range(ROUTE_ROWS):
            idx_ref[k:k + 1, r * LANES:(r + 1) * LANES] = e_k[r:r + 1, :]
            gate_ref[k:k + 1, r * LANES:(r + 1) * LANES] = g_k[r:r + 1, :]


def _route(lt3, bias):
    E, R, _ = lt3.shape
    out_spec = pl.BlockSpec((TOP_K, ROUTE_ROWS * LANES), lambda r: (0, r))
    return pl.pallas_call(
        _route_kernel,
        grid=(R // ROUTE_ROWS,),
        in_specs=[
            pl.BlockSpec(memory_space=pltpu.SMEM),
            pl.BlockSpec((E, ROUTE_ROWS, LANES), lambda r: (0, r, 0)),
        ],
        out_specs=[out_spec, out_spec],
        out_shape=[jax.ShapeDtypeStruct((TOP_K, R * LANES), jnp.int32),
                   jax.ShapeDtypeStruct((TOP_K, R * LANES), F32)],
        compiler_params=_cparams(("arbitrary",)),
        name="route",
    )(bias, lt3)


def _sc_mesh():
    return plsc.VectorSubcoreMesh(core_axis_name="c", subcore_axis_name="s",
                                  num_cores=SC_CORES, num_subcores=SC_SUBCORES)


def _sc_worker_id():
    return lax.axis_index("s") * SC_CORES + lax.axis_index("c")


def _sc_gather_sum(ys, dest_kt, gates_kt):
    K, T = dest_kt.shape
    half = ys.shape[1]
    workers = SC_CORES * SC_SUBCORES
    per_worker = T // workers
    wt = SC_SUM_TOKENS
    n_win = per_worker // wt
    n_idx = K * wt
    assert T == workers * n_win * wt and n_idx <= SC_WINDOW and half % SC_LANES == 0
    dest_w = dest_kt.reshape(K, workers, n_win, wt).transpose(1, 2, 0, 3).reshape(workers, n_win, n_idx)
    gates_w = gates_kt.reshape(K, workers, n_win, wt).transpose(1, 2, 0, 3).reshape(workers, n_win, n_idx)

    @functools.partial(
        pl.kernel, mesh=_sc_mesh(),
        out_type=jax.ShapeDtypeStruct((T, 2 * half), F32),
        scratch_types=[
            pltpu.VMEM((n_win, n_idx), jnp.int32),
            pltpu.VMEM((n_idx, half), jnp.int32),
            pltpu.VMEM((n_win, n_idx), F32),
            pltpu.VMEM((wt, 2 * half), F32),
            pltpu.SemaphoreType.DMA,
        ],
        compiler_params=pltpu.CompilerParams(needs_layout_passes=False),
    )
    def gather_sum(ys_hbm, dest_hbm, gates_hbm, out_hbm, idx_v, rows_v, gate_v, out_v, sem):
        wid = _sc_worker_id()
        pltpu.sync_copy(dest_hbm.at[wid], idx_v)
        pltpu.sync_copy(gates_hbm.at[wid], gate_v)

        @pl.loop(0, n_win)
        def _(j):
            pltpu.async_copy(ys_hbm.at[idx_v.at[j]], rows_v, sem).wait()
            win = jnp.full((SC_LANES,), j, jnp.int32)

            @pl.loop(0, wt)
            def _(t):
                g = [plsc.load_gather(gate_v, [win, jnp.full((SC_LANES,), k * wt + t, jnp.int32)])
                     for k in range(K)]

                @plsc.parallel_loop(0, half // SC_LANES, unroll=8)
                def _(c):
                    col = pl.ds(c * SC_LANES, SC_LANES)
                    acc_lo = jnp.zeros((SC_LANES,), F32)
                    acc_hi = jnp.zeros((SC_LANES,), F32)
                    for k in range(K):
                        w = rows_v[k * wt + t, col]
                        acc_lo = acc_lo + g[k] * plsc.bitcast(w << 16, F32)
                        acc_hi = acc_hi + g[k] * plsc.bitcast(w & HIGH_HALF_MASK, F32)
                    out_v[t, col] = acc_lo
                    out_v[t, pl.ds(half + c * SC_LANES, SC_LANES)] = acc_hi

            pltpu.sync_copy(out_v, out_hbm.at[pl.ds(wid * per_worker + j * wt, wt)])

    return gather_sum(ys, dest_w, gates_w)


def _sc_dispatch(rows, dest_kt, n_slots):
    T, width = rows.shape
    K = dest_kt.shape[0]
    workers = SC_CORES * SC_SUBCORES
    per_worker = T // workers
    n_win = per_worker // SC_WINDOW
    assert T == workers * n_win * SC_WINDOW
    dest_w = dest_kt.reshape(K, workers, n_win, SC_WINDOW).transpose(1, 2, 0, 3)
    dest_w = dest_w.reshape(workers, n_win * K, SC_WINDOW)

    @functools.partial(
        pl.kernel, mesh=_sc_mesh(),
        out_type=jax.ShapeDtypeStruct((n_slots, width), rows.dtype),
        scratch_types=[
            pltpu.VMEM((n_win * K, SC_WINDOW), jnp.int32),
            pltpu.VMEM((SC_WINDOW, width), rows.dtype),
        ],
    )
    def dispatch(rows_hbm, dest_hbm, out_hbm, idx_v, rows_v):
        wid = _sc_worker_id()
        pltpu.sync_copy(dest_hbm.at[wid], idx_v)

        @pl.loop(0, n_win)
        def _(j):
            pltpu.sync_copy(rows_hbm.at[pl.ds(wid * per_worker + j * SC_WINDOW, SC_WINDOW)], rows_v)
            for k in range(K):
                pltpu.sync_copy(rows_v, out_hbm.at[idx_v.at[j * K + k]])

    return dispatch(rows, dest_w)


def _swiglu(h_lo, h_hi, wg_ref, wu_ref):
    def proj(w_ref):
        return (jnp.dot(h_lo, w_ref[:HALF, :], preferred_element_type=F32)
                + jnp.dot(h_hi, w_ref[HALF:, :], preferred_element_type=F32))
    gate = proj(wg_ref)
    up = proj(wu_ref)
    return (gate * jax.nn.sigmoid(gate) * up).astype(BF16)


def _expert_kernel(seg_first_ref, seg_blocks_ref, n_real_ref, xs_hbm, wg_ref, wu_ref, wd_ref, ys_hbm,
                   xbuf, ybuf, wg_bf, wu_bf, wd_bf, in_sems, out_sems):
    e = pl.program_id(0)
    first = seg_first_ref[e]
    count = seg_blocks_ref[e]
    total = n_real_ref[0]
    blk = DISPATCH_BLOCK
    ring = xbuf.shape[0]

    def in_copy(g):
        slot = g % ring
        return pltpu.make_async_copy(xs_hbm.at[pl.ds(g * blk, blk)], xbuf.at[slot], in_sems.at[slot])

    def out_copy(g):
        slot = g % ring
        return pltpu.make_async_copy(ybuf.at[slot], ys_hbm.at[pl.ds(g * blk, blk)], out_sems.at[slot])

    @pl.when(e == 0)
    def _():
        for g in range(ring - 1):
            @pl.when(g < total)
            def _():
                in_copy(g).start()

    @pl.when(count > 0)
    def _():
        wg_bf[...] = wg_ref[0, 0].astype(BF16)
        wu_bf[...] = wu_ref[0, 0].astype(BF16)
        wd_bf[...] = wd_ref[0, 0].astype(BF16)

    def block(g, _):
        slot = g % ring
        in_copy(g).wait()

        @pl.when(g + ring - 1 < total)
        def _():
            in_copy(g + ring - 1).start()

        @pl.when(g >= ring)
        def _():
            out_copy(g - ring).wait()

        lo, hi = _unpack_bf16_pairs(xbuf[slot])
        a = _swiglu(lo.astype(BF16), hi.astype(BF16), wg_bf, wu_bf)
        y = jnp.dot(a, wd_bf[...], preferred_element_type=F32)
        ybuf[slot] = _pack_bf16_pairs(y.astype(BF16))
        out_copy(g).start()
        return 0

    lax.fori_loop(first, first + count, block, 0)

    @pl.when(e == pl.num_programs(0) - 1)
    def _():
        for back in range(ring, 0, -1):
            @pl.when(total >= back)
            def _():
                out_copy(total - back).wait()


def _expert_mm(seg_first, seg_blocks, n_real, xs, wg, wu, wd, layer):
    P = xs.shape[0]
    D = wg.shape[2]
    w_map = lambda e, sf, sb, nr: (layer, e, 0, 0)
    grid_spec = pltpu.PrefetchScalarGridSpec(
        num_scalar_prefetch=3,
        grid=(N_EXPERTS,),
        in_specs=[
            pl.BlockSpec(memory_space=pl.ANY),
            pl.BlockSpec((1, 1, D, D_EXPERT), w_map),
            pl.BlockSpec((1, 1, D, D_EXPERT), w_map),
            pl.BlockSpec((1, 1, D_EXPERT, D), w_map),
        ],
        out_specs=pl.BlockSpec(memory_space=pl.ANY),
        scratch_shapes=[
            pltpu.VMEM((EXPERT_RING, DISPATCH_BLOCK, HALF), jnp.int32),
            pltpu.VMEM((EXPERT_RING, DISPATCH_BLOCK, HALF), jnp.int32),
            pltpu.VMEM((D, D_EXPERT), BF16), pltpu.VMEM((D, D_EXPERT), BF16), pltpu.VMEM((D_EXPERT, D), BF16),
            pltpu.SemaphoreType.DMA((EXPERT_RING,)), pltpu.SemaphoreType.DMA((EXPERT_RING,)),
        ],
    )
    return pl.pallas_call(
        _expert_kernel,
        grid_spec=grid_spec,
        out_shape=jax.ShapeDtypeStruct((P, HALF), jnp.int32),
        compiler_params=_cparams(("arbitrary",)),
        name="expert_mm",
    )(seg_first, seg_blocks, n_real, xs, wg, wu, wd)


def _rank_kernel(idx_ref, rank_ref, cnt_ref, tri_ref, carry_ref):
    i = pl.program_id(0)
    tm = idx_ref.shape[1]

    @pl.when(i == 0)
    def _():
        r = lax.broadcasted_iota(jnp.int32, (tm, tm), 0)
        c = lax.broadcasted_iota(jnp.int32, (tm, tm), 1)
        tri_ref[...] = (r < c).astype(BF16)
        carry_ref[...] = jnp.zeros(carry_ref.shape, F32)

    e_iota = lax.broadcasted_iota(jnp.int32, (N_EXPERTS, tm), 0)
    hits = [e_iota == idx_ref[k:k + 1, :] for k in range(TOP_K)]
    mask = functools.reduce(jnp.logical_or, hits).astype(F32)
    rank = jnp.dot(mask.astype(BF16), tri_ref[...], preferred_element_type=F32) + carry_ref[:, 0:1]
    for k in range(TOP_K):
        rank_ref[k:k + 1, :] = jnp.sum(jnp.where(hits[k], rank, 0.0), axis=0, keepdims=True).astype(jnp.int32)
    carry_ref[...] = carry_ref[...] + jnp.sum(mask, axis=1, keepdims=True)
    cnt_ref[...] = carry_ref[...]


def _rank(idx_kt):
    K, T = idx_kt.shape
    tm = 1024
    return pl.pallas_call(
        _rank_kernel,
        grid=(T // tm,),
        in_specs=[pl.BlockSpec((K, tm), lambda i: (0, i))],
        out_specs=[pl.BlockSpec((K, tm), lambda i: (0, i)),
                   pl.BlockSpec((N_EXPERTS, LANES), lambda i: (0, 0))],
        out_shape=[jax.ShapeDtypeStruct((K, T), jnp.int32),
                   jax.ShapeDtypeStruct((N_EXPERTS, LANES), F32)],
        scratch_shapes=[pltpu.VMEM((tm, tm), BF16), pltpu.VMEM((N_EXPERTS, LANES), F32)],
        compiler_params=_cparams(("arbitrary",)),
        name="rank",
    )(idx_kt)


def _dest_kernel(idx_ref, rank_ref, start_ref, dest_ref):
    tm = idx_ref.shape[1]
    e_iota = lax.broadcasted_iota(jnp.int32, (N_EXPERTS, tm), 0)
    start = start_ref[:, 0:1]
    for k in range(TOP_K):
        base = jnp.sum(jnp.where(e_iota == idx_ref[k:k + 1, :], start, 0.0), axis=0, keepdims=True)
        dest_ref[k:k + 1, :] = rank_ref[k:k + 1, :] + base.astype(jnp.int32)


def _dest(idx_kt, rank_kt, start):
    K, T = idx_kt.shape
    tm = 1024
    tok_spec = pl.BlockSpec((K, tm), lambda i: (0, i))
    return pl.pallas_call(
        _dest_kernel,
        grid=(T // tm,),
        in_specs=[tok_spec, tok_spec, pl.BlockSpec((N_EXPERTS, LANES), lambda i: (0, 0))],
        out_specs=tok_spec,
        out_shape=jax.ShapeDtypeStruct((K, T), jnp.int32),
        compiler_params=_cparams(("arbitrary",)),
        name="dest",
    )(idx_kt, rank_kt, start)


def _block_plan(counts):
    blk = DISPATCH_BLOCK
    seg_blocks = (counts.astype(jnp.int32) + blk - 1) // blk
    seg_end = jnp.cumsum(seg_blocks)
    seg_first = seg_end - seg_blocks
    return seg_first * blk, seg_first, seg_blocks, seg_end[-1:]


def _combine_kernel(h_ref, x1_ref, routed_ref, mod_ref, wsg_ref, wsu_ref, wsd_ref, lng_ref, lnb_ref, o_ref):
    lo, hi = _unpack_bf16_pairs(h_ref[...])
    a = _swiglu(lo.astype(BF16), hi.astype(BF16), wsg_ref, wsu_ref)
    ffn = jnp.dot(a, wsd_ref[...], preferred_element_type=F32) + routed_ref[...]
    g2 = mod_ref[0, 5:6, :]
    o_ref[...] = _layer_norm(DEEPNORM_ALPHA * x1_ref[...] + g2 * ffn, lng_ref[...], lnb_ref[...])


def _combine_into_kernel(h_ref, x1_ref, routed_ref, mod_ref, wsg_ref, wsu_ref, wsd_ref, lng_ref, lnb_ref,
                         prev_ref, o_ref):
    del prev_ref
    _combine_kernel(h_ref, x1_ref, routed_ref, mod_ref, wsg_ref, wsu_ref, wsd_ref, lng_ref, lnb_ref, o_ref)


def _combine(h2p, x1, routed, mod, wsg, wsu, wsd, lng, lnb, seq, b_off, in_off, out_rows, out_off, out_prev):
    n, D = routed.shape
    tm = TM_COMBINE
    per_seq = seq // tm
    in_blk = in_off // tm
    blk_off = out_off // tm
    const = lambda i: (0, 0)
    in_specs = [
        pl.BlockSpec((tm, HALF), lambda i: (i + in_blk, 0)),
        pl.BlockSpec((tm, D), lambda i: (i + in_blk, 0)),
        pl.BlockSpec((tm, D), lambda i: (i, 0)),
        pl.BlockSpec((1, 6, D), lambda i: ((i + in_blk) // per_seq + b_off, 0, 0)),
        pl.BlockSpec(wsg.shape, const),
        pl.BlockSpec(wsu.shape, const),
        pl.BlockSpec(wsd.shape, const),
        pl.BlockSpec(lng.shape, const),
        pl.BlockSpec(lnb.shape, const),
    ]
    args = [h2p, x1, routed, mod, wsg, wsu, wsd, lng, lnb]
    body = _combine_kernel
    aliases = {}
    if out_prev is not None:
        in_specs.append(pl.BlockSpec(memory_space=pl.ANY))
        args.append(out_prev)
        aliases = {len(args) - 1: 0}
        body = _combine_into_kernel
    return pl.pallas_call(
        body,
        grid=(n // tm,),
        in_specs=in_specs,
        out_specs=pl.BlockSpec((tm, D), lambda i: (i + blk_off, 0)),
        out_shape=jax.ShapeDtypeStruct((out_rows, D), F32),
        input_output_aliases=aliases,
        compiler_params=_cparams(("arbitrary",)),
        name="combine",
    )(*args)


def _rope_pair(w):
    half = QK_ROPE // 2
    return w, jnp.concatenate([-w[:, half:], w[:, :half]], axis=1)


def _prep_layer(w_in, w_uq, w_ukv, w_pool, w_out, w_router):
    lat = POOL_WIDTH + Q_LORA + KV_LORA
    k_a, k_b = _rope_pair(w_in[:, lat:lat + QK_ROPE])
    w_in_ext = jnp.concatenate([w_in[:, :lat], k_a, k_b], axis=1).astype(BF16)

    qa_cols = []
    per_head = QK_NOPE + QK_ROPE
    for hd in range(N_HEADS):
        w_h = w_uq[:, hd * per_head:(hd + 1) * per_head]
        r_a, r_b = _rope_pair(w_h[:, QK_NOPE:])
        qa_cols += [w_h[:, :QK_NOPE], r_a, r_b]
    w_uqa = jnp.concatenate(qa_cols, axis=1).astype(BF16)

    wpool_bd = jax.scipy.linalg.block_diag(*[w_pool[g] for g in range(len(POOL_WINDOWS))]).astype(BF16)
    wr_hi, wr_lo = _split_hi_lo(w_router.T)
    return dict(w_in_ext=w_in_ext, w_uqa=w_uqa, w_ukv=w_ukv.astype(BF16), wpool_bd=wpool_bd,
                wout_p=w_out[:POOL_WIDTH].astype(BF16), wout_a=w_out[POOL_WIDTH:].astype(BF16),
                wr_hi=wr_hi, wr_lo=wr_lo)


def kernel(x, c, positions, w_ada, b_ada, w_in, q_norm_g, kv_norm_g, w_uq, w_ukv, w_pool, pool_scale,
           w_out, ln1_g, ln1_b, w_router, router_bias, w_exp_gate, w_exp_up, w_exp_down,
           w_sh_gate, w_sh_up, w_sh_down, ln2_g, ln2_b):
    B, S, D = x.shape
    L = w_in.shape[0]
    row = lambda v: v.reshape(1, -1)

    mod_all = _ada_mod(c, w_ada, b_ada).reshape(L, B, 6, D)
    cs_all = _rope_table(positions)
    bs = B // N_STREAMS
    T = bs * S
    n_blocks = T * TOP_K // DISPATCH_BLOCK + N_EXPERTS
    xs_streams = [x] * N_STREAMS
    result = None
    for l in range(L):
        p = _prep_layer(w_in[l], w_uq[l], w_ukv[l], w_pool[l], w_out[l], w_router[l])
        shared = (w_sh_gate[l].astype(BF16), w_sh_up[l].astype(BF16), w_sh_down[l].astype(BF16))
        mod = mod_all[l]
        for s in range(N_STREAMS):
            xl = xs_streams[s]
            b_off = s * bs
            x_off = b_off if l == 0 else 0
            u, q, k, v = _in_proj(xl, mod, cs_all, p["w_in_ext"], row(q_norm_g[l]), row(kv_norm_g[l]),
                                  p["w_uqa"], p["w_ukv"], bs, x_off, b_off)
            attn = _attention(q, k, v)
            x1, h2p, lt = _post_attn(xl, u, attn, mod, p["wpool_bd"], row(pool_scale[l]), p["wout_p"],
                                     p["wout_a"], row(ln1_g[l]), row(ln1_b[l]), p["wr_hi"], p["wr_lo"],
                                     x_off, b_off)
            idx_kt, gates_kt = _route(lt.reshape(N_EXPERTS, T // LANES, LANES), router_bias[l])
            rank_kt, counts = _rank(idx_kt)
            pad_start, seg_first, seg_blocks, n_real = _block_plan(counts[:, 0])
            start = jnp.broadcast_to(pad_start.astype(F32)[:, None], (N_EXPERTS, LANES))
            dest_kt = _dest(idx_kt, rank_kt, start)
            h2p = h2p.reshape(T, HALF)
            rows = _sc_dispatch(h2p, dest_kt, n_blocks * DISPATCH_BLOCK)
            ys = _expert_mm(seg_first, seg_blocks, n_real, rows, w_exp_gate, w_exp_up, w_exp_down, l)
            tc = T // COMBINE_CHUNKS
            out = None
            for ch in range(COMBINE_CHUNKS):
                lo = ch * tc
                routed = _sc_gather_sum(ys, dest_kt[:, lo:lo + tc], gates_kt[:, lo:lo + tc])
                if l < L - 1:
                    out = _combine(h2p, x1.reshape(T, D), routed, mod, *shared, row(ln2_g[l]), row(ln2_b[l]),
                                   S, b_off, lo, T, lo, out)
                else:
                    result = _combine(h2p, x1.reshape(T, D), routed, mod, *shared, row(ln2_g[l]),
                                      row(ln2_b[l]), S, b_off, lo, B * S, s * T + lo, result)
            if l < L - 1:
                xs_streams[s] = out.reshape(bs, S, D)
    return result.reshape(B, S, D)
```
